```python
import math
import jax, jax.numpy as jnp
from jax import lax
import numpy as np

D_MODEL = 1024
BATCH = 8
SEQ = 2048
DEPTH = 1
DEC_BATCH = 32
DEC_SEQ = 8
PAST_LEN = 16384
PAGE_SIZE = 128

HEAD_DIM = 64
N_RET_HEADS = 8
N_ATT_HEADS = 8
RET_WIDTH = N_RET_HEADS * HEAD_DIM
ATT_WIDTH = N_ATT_HEADS * HEAD_DIM
MIX_WIDTH = RET_WIDTH + ATT_WIDTH
W_IN_COLS = 4 * RET_WIDTH + 3 * ATT_WIDTH
RET_CHUNK = 128
DIL_PATTERNS = ((128, 1), (512, 4), (2048, 16))
MAX_WINDOW = 2048
ATT_BLOCK = 128
N_BUCKETS = 32
MAX_DISTANCE = 2048
ROPE_BASE = 10000.0
D_FF = -(-8 * D_MODEL // (3 * 256)) * 256
NORM_EPS = 1e-6
GN_EPS = 1e-5

kernel_name = 'hymba_retention_dilated_swa_decoder_step'


def rmsnorm(x, g):
    xf = x.astype(jnp.float32)
    y = xf * lax.rsqrt(jnp.mean(xf * xf, axis=-1, keepdims=True) + NORM_EPS) * g.astype(jnp.float32)
    return y.astype(x.dtype)


def rope(x, pos):
    half = HEAD_DIM // 2
    inv = ROPE_BASE ** (-jnp.arange(0, HEAD_DIM, 2, dtype=jnp.float32) / HEAD_DIM)
    ang = pos.astype(jnp.float32)[:, None] * inv[None, :]
    cos = jnp.cos(ang)[None, :, None, :]
    sin = jnp.sin(ang)[None, :, None, :]
    xf = x.astype(jnp.float32)
    x1, x2 = xf[..., :half], xf[..., half:]
    return jnp.concatenate([x1 * cos - x2 * sin, x2 * cos + x1 * sin], axis=-1).astype(x.dtype)


def t5_bucket(dist):
    max_exact = N_BUCKETS // 2
    d_f = jnp.maximum(dist, 1).astype(jnp.float32)
    large = max_exact + (jnp.log(d_f / max_exact) / math.log(MAX_DISTANCE / max_exact)
                         * (N_BUCKETS - max_exact)).astype(jnp.int32)
    large = jnp.minimum(large, N_BUCKETS - 1)
    return jnp.where(dist < max_exact, dist, large)


def pattern_bias(rel_bias, dil, nk):
    dist = jnp.arange(nk + 1, dtype=jnp.int32) * dil
    return rel_bias[t5_bucket(dist)].astype(jnp.float32)


def retention_log_decay():
    return jnp.log1p(-jnp.exp2(-5.0 - jnp.arange(N_RET_HEADS, dtype=jnp.float32)))


def retention_chunk(S, q, k, v, log_g):
    C = q.shape[1]
    q, k, v = q.astype(jnp.float32), k.astype(jnp.float32), v.astype(jnp.float32)
    i = jnp.arange(C, dtype=jnp.float32)
    rel = i[:, None] - i[None, :]
    decay = jnp.where(rel[None] >= 0, jnp.exp(jnp.maximum(rel, 0.0)[None] * log_g[:, None, None]), 0.0)
    inner = jnp.einsum('bihd,bjhd->bhij', q, k) * decay[None]
    cross_decay = jnp.exp((i + 1.0)[:, None] * log_g[None, :])
    o = (jnp.einsum('bhij,bjhe->bihe', inner, v)
         + jnp.einsum('bihd,bhde->bihe', q, S) * cross_decay[None, :, :, None])
    k_decay = jnp.exp((C - 1.0 - i)[:, None] * log_g[None, :])
    S_new = (S * jnp.exp(C * log_g)[None, :, None, None]
             + jnp.einsum('bjhd,bjhe,jh->bhde', k, v, k_decay))
    return S_new, o


def retention_prompt(q, k, v, log_g):
    B, T, H, D = q.shape
    C = min(RET_CHUNK, T)
    nc = T // C
    to_chunks = lambda t: jnp.moveaxis(t.reshape(B, nc, C, H, D), 1, 0)
    S0 = jnp.zeros((B, H, D, D), jnp.float32)

    def step(S, inp):
        qc, kc, vc = inp
        return retention_chunk(S, qc, kc, vc, log_g)

    S_fin, outs = lax.scan(step, S0, (to_chunks(q), to_chunks(k), to_chunks(v)))
    return S_fin, jnp.moveaxis(outs, 0, 1).reshape(B, T, H, D)


def dilated_prompt_stats(q, k, v, bias_k, dil, nk):
    B, S, H, D = q.shape
    Ls = S // dil
    nb = -(-Ls // ATT_BLOCK)
    Lp = nb * ATT_BLOCK

    def sub(t):
        t = t.reshape(B, Ls, dil, H, D).transpose(0, 2, 1, 3, 4).reshape(B * dil, Ls, H, D)
        t = jnp.pad(t, ((0, 0), (0, Lp - Ls), (0, 0), (0, 0)))
        return t.reshape(B * dil, nb, ATT_BLOCK, H, D)

    def band(t):
        prev = jnp.pad(t, ((0, 0), (1, 0), (0, 0), (0, 0), (0, 0)))[:, :-1]
        return jnp.concatenate([prev, t], axis=2)

    qs = sub(q).astype(jnp.float32)
    kb = band(sub(k)).astype(jnp.float32)
    vb = band(sub(v)).astype(jnp.float32)
    qi = jnp.arange(ATT_BLOCK)[:, None]
    kj = jnp.arange(2 * ATT_BLOCK)[None, :]
    dist = ATT_BLOCK + qi - kj
    valid = (dist >= 0) & (dist <= nk)
    valid = valid[None] & ((jnp.arange(nb)[:, None, None] > 0) | (kj[None] >= ATT_BLOCK))
    bias = bias_k[jnp.minimum(jnp.maximum(dist, 0), nk)].transpose(2, 0, 1)
    s = jnp.einsum('bnqhd,bnkhd->bnhqk', qs, kb) + bias
    s = jnp.where(valid[None, :, None], s, -jnp.inf)
    m = jnp.max(s, axis=-1)
    p = jnp.exp(s - m[..., None])
    l = jnp.sum(p, axis=-1)
    acc = jnp.einsum('bnhqk,bnkhd->bnqhd', p, vb)

    def unsub(t):
        rest = t.shape[3:]
        t = t.reshape((B, dil, Lp) + rest)[:, :, :Ls]
        return jnp.swapaxes(t, 1, 2).reshape((B, S) + rest)

    return unsub(jnp.swapaxes(m, 2, 3)), unsub(jnp.swapaxes(l, 2, 3)), unsub(acc)


def dilated_sample_stats(q, k_all, v_all, bias_k, dil, nk, L):
    T = q.shape[1]
    idx = L + jnp.arange(T)[:, None] - jnp.arange(nk + 1)[None, :] * dil
    valid = idx >= 0
    idx_c = jnp.maximum(idx, 0)
    kg = jnp.take(k_all, idx_c, axis=1).astype(jnp.float32)
    vg = jnp.take(v_all, idx_c, axis=1).astype(jnp.float32)
    s = jnp.einsum('bthd,btkhd->bthk', q.astype(jnp.float32), kg) + bias_k.T[None, None]
    s = jnp.where(valid[None, :, None, :], s, -jnp.inf)
    m = jnp.max(s, axis=-1)
    p = jnp.exp(s - m[..., None])
    l = jnp.sum(p, axis=-1)
    acc = jnp.einsum('bthk,btkhd->bthd', p, vg)
    return m, l, acc


def combine_by_denominator(stats):
    m_all = jnp.max(jnp.stack([st[0] for st in stats]), axis=0)
    num = 0.0
    den = 0.0
    for m, l, acc in stats:
        w = jnp.exp(m - m_all)
        num = num + w[..., None] * acc
        den = den + w * l
    return num / den[..., None]


def decoder_layer(x, pos, ret_state, k_buf, v_buf, rel_bias, w_in, g_ret, w_out,
                  g_pre_mix, g_post_mix, g_pre_ffn, g_post_ffn, w_gu, w_down):
    B, T, _ = x.shape
    h = rmsnorm(x, g_pre_mix)
    z = h @ w_in
    R, A = RET_WIDTH, ATT_WIDTH
    rq = z[..., 0:R]
    rk = z[..., R:2 * R]
    rv = z[..., 2 * R:3 * R]
    rg = z[..., 3 * R:4 * R]
    aq = z[..., 4 * R:4 * R + A]
    ak = z[..., 4 * R + A:4 * R + 2 * A]
    av = z[..., 4 * R + 2 * A:4 * R + 3 * A]
    heads = lambda t, n: t.reshape(B, T, n, HEAD_DIM)

    log_g = retention_log_decay()
    rq = rope(heads(rq, N_RET_HEADS), pos)
    rk = rope(heads(rk, N_RET_HEADS), pos) * (HEAD_DIM ** -0.5)
    rv = heads(rv, N_RET_HEADS)
    if ret_state is None:
        S_new, ro = retention_prompt(rq, rk, rv, log_g)
    else:
        S_new, ro = retention_chunk(ret_state.astype(jnp.float32), rq, rk, rv, log_g)
    mu = jnp.mean(ro, axis=-1, keepdims=True)
    var = jnp.mean(jnp.square(ro - mu), axis=-1, keepdims=True)
    ro = (ro - mu) * lax.rsqrt(var + GN_EPS) * g_ret.astype(jnp.float32)[None, None]
    ro = ro.reshape(B, T, RET_WIDTH) * jax.nn.silu(rg.astype(jnp.float32))

    aq = heads(aq, N_ATT_HEADS) * (HEAD_DIM ** -0.5)
    ak = heads(ak, N_ATT_HEADS)
    av = heads(av, N_ATT_HEADS)
    stats = []
    if k_buf is None:
        for win, dil in DIL_PATTERNS:
            nk = win // dil
            stats.append(dilated_prompt_stats(aq, ak, av, pattern_bias(rel_bias, dil, nk), dil, nk))
        L = min(MAX_WINDOW, T)
        k_new, v_new = ak[:, T - L:], av[:, T - L:]
    else:
        L = k_buf.shape[1]
        k_all = jnp.concatenate([k_buf.astype(ak.dtype), ak], axis=1)
        v_all = jnp.concatenate([v_buf.astype(av.dtype), av], axis=1)
        for win, dil in DIL_PATTERNS:
            nk = win // dil
            stats.append(dilated_sample_stats(aq, k_all, v_all, pattern_bias(rel_bias, dil, nk), dil, nk, L))
        k_new, v_new = k_all[:, T:], v_all[:, T:]
    ao = combine_by_denominator(stats).reshape(B, T, ATT_WIDTH)

    mix = jnp.concatenate([ro.astype(x.dtype), ao.astype(x.dtype)], axis=-1) @ w_out
    x = x + rmsnorm(mix, g_post_mix)

    h = rmsnorm(x, g_pre_ffn)
    gu = h @ w_gu
    gate, up = gu[..., :D_FF], gu[..., D_FF:]
    f = (jax.nn.silu(gate) * up) @ w_down
    x = x + rmsnorm(f, g_post_ffn)
    return x, S_new.astype(x.dtype), k_new, v_new


def setup_inputs(seed: int = 0) -> dict:
    key = jax.random.key(seed)
    ks = jax.random.split(key, 20)
    f32 = jnp.float32
    L = min(MAX_WINDOW, PAST_LEN)
    nrm = lambda k, shape, s: jax.random.normal(k, shape, f32) * s
    gain = lambda k, shape: 1.0 + 0.05 * jax.random.normal(k, shape, f32)
    return {
        'x_prompt': nrm(ks[0], (BATCH, SEQ, D_MODEL), 1.0),
        'x_sample': nrm(ks[1], (DEC_BATCH, DEC_SEQ, D_MODEL), 1.0),
        'state_ret': nrm(ks[2], (DEPTH, DEC_BATCH, N_RET_HEADS, HEAD_DIM, HEAD_DIM), 1.0),
        'cache_k_win': nrm(ks[3], (DEPTH, DEC_BATCH, L, N_ATT_HEADS, HEAD_DIM), 1.0),
        'cache_v_win': nrm(ks[4], (DEPTH, DEC_BATCH, L, N_ATT_HEADS, HEAD_DIM), 1.0),
        'rel_bias': nrm(ks[5], (N_BUCKETS, N_ATT_HEADS), 0.5),
        'w_in': nrm(ks[6], (DEPTH, D_MODEL, W_IN_COLS), D_MODEL ** -0.5),
        'g_ret': gain(ks[7], (DEPTH, N_RET_HEADS, HEAD_DIM)),
        'w_out': nrm(ks[8], (DEPTH, MIX_WIDTH, D_MODEL), MIX_WIDTH ** -0.5),
        'g_pre_mix': gain(ks[9], (DEPTH, D_MODEL)),
        'g_post_mix': gain(ks[10], (DEPTH, D_MODEL)),
        'g_pre_ffn': gain(ks[11], (DEPTH, D_MODEL)),
        'g_post_ffn': gain(ks[12], (DEPTH, D_MODEL)),
        'w_gu': nrm(ks[13], (DEPTH, D_MODEL, 2 * D_FF), D_MODEL ** -0.5),
        'w_down': nrm(ks[14], (DEPTH, D_FF, D_MODEL), D_FF ** -0.5),
    }


def reference(x_prompt, x_sample, state_ret, cache_k_win, cache_v_win, rel_bias, w_in, g_ret,
              w_out, g_pre_mix, g_post_mix, g_pre_ffn, g_post_ffn, w_gu, w_down):
    pos_p = jnp.arange(x_prompt.shape[1], dtype=jnp.int32)
    pos_s = PAST_LEN + jnp.arange(x_sample.shape[1], dtype=jnp.int32)
    yp, ys = x_prompt, x_sample
    sp_l, kp_l, vp_l, ss_l, ks_l, vs_l = [], [], [], [], [], []
    for l in range(DEPTH):
        yp, sp, kp, vp = decoder_layer(yp, pos_p, None, None, None, rel_bias, w_in[l], g_ret[l],
                                       w_out[l], g_pre_mix[l], g_post_mix[l], g_pre_ffn[l],
                                       g_post_ffn[l], w_gu[l], w_down[l])
        ys, ss, ksn, vsn = decoder_layer(ys, pos_s, state_ret[l], cache_k_win[l], cache_v_win[l],
                                         rel_bias, w_in[l], g_ret[l], w_out[l], g_pre_mix[l],
                                         g_post_mix[l], g_pre_ffn[l], g_post_ffn[l], w_gu[l], w_down[l])
        sp_l.append(sp); kp_l.append(kp); vp_l.append(vp)
        ss_l.append(ss); ks_l.append(ksn); vs_l.append(vsn)
    return (yp, ys, jnp.stack(sp_l), jnp.stack(kp_l), jnp.stack(vp_l),
            jnp.stack(ss_l), jnp.stack(ks_l), jnp.stack(vs_l))
```

```python
import functools
import math

import jax
import jax.numpy as jnp
from jax import lax
from jax.experimental import pallas as pl
from jax.experimental.pallas import tpu as pltpu

F32 = jnp.float32
BF16 = jnp.bfloat16

HEAD_DIM = 64
N_RET_HEADS = 8
N_ATT_HEADS = 8
RET_WIDTH = N_RET_HEADS * HEAD_DIM
ATT_WIDTH = N_ATT_HEADS * HEAD_DIM
RET_CHUNK = 128
DIL_PATTERNS = ((128, 1), (512, 4), (2048, 16))
ATT_BLOCK = 128
N_BUCKETS = 32
MAX_DISTANCE = 2048
ROPE_BASE = 10000.0
NORM_EPS = 1e-6
GN_EPS = 1e-5
PAST_LEN = 16384
LANES = 128
NEG_INF = float("-inf")


def _cparams(sem, vmem_mb):
    return pltpu.CompilerParams(dimension_semantics=sem, vmem_limit_bytes=vmem_mb * 1024 * 1024)


def _rms(x, g):
    return x * lax.rsqrt(jnp.mean(x * x, axis=-1, keepdims=True) + NORM_EPS) * g


def _dot(a, b):
    return jnp.dot(a, b, preferred_element_type=F32)


def _dot_nt(a, b):
    return lax.dot_general(a, b, (((1,), (1,)), ((), ())), preferred_element_type=F32)


def _dot_tn(a, b):
    return lax.dot_general(a, b, (((0,), (0,)), ((), ())), preferred_element_type=F32)


def _rope_tables(pos):
    half = HEAD_DIM // 2
    inv = ROPE_BASE ** (-jnp.arange(0, HEAD_DIM, 2, dtype=F32) / HEAD_DIM)
    ang = pos.astype(F32)[:, None] * inv[None, :]
    cos, sin = jnp.cos(ang), jnp.sin(ang)
    zero = jnp.zeros_like(sin)
    rep = LANES // HEAD_DIM
    cosf = jnp.tile(jnp.concatenate([cos, cos], axis=-1), (1, rep))
    sa = jnp.tile(jnp.concatenate([-sin, zero], axis=-1), (1, rep))
    sb = jnp.tile(jnp.concatenate([zero, sin], axis=-1), (1, rep))
    return cosf, sa, sb


def _retention_consts(C):
    log_g = jnp.log1p(-jnp.exp2(-5.0 - jnp.arange(N_RET_HEADS, dtype=F32)))
    i = jnp.arange(C, dtype=F32)
    rel = i[:, None] - i[None, :]
    decay = jnp.where(rel[None] >= 0, jnp.exp(jnp.maximum(rel, 0.0)[None] * log_g[:, None, None]), 0.0)
    cross = jnp.exp((i + 1.0)[:, None] * log_g[None, :])
    kdec = jnp.exp((C - 1.0 - i)[:, None] * log_g[None, :])
    sdec = jnp.exp(C * log_g)[None, :]
    expand = lambda t: jnp.repeat(t, HEAD_DIM, axis=-1)
    return decay, expand(cross), expand(kdec), expand(sdec)


def _t5_bucket(dist):
    max_exact = N_BUCKETS // 2
    d_f = jnp.maximum(dist, 1).astype(F32)
    large = max_exact + (jnp.log(d_f / max_exact) / math.log(MAX_DISTANCE / max_exact)
                         * (N_BUCKETS - max_exact)).astype(jnp.int32)
    large = jnp.minimum(large, N_BUCKETS - 1)
    return jnp.where(dist < max_exact, dist, large)


def _pattern_bias(rel_bias, dil, nk):
    dist = jnp.arange(nk + 1, dtype=jnp.int32) * dil
    return rel_bias[_t5_bucket(dist)].astype(F32)


def _prompt_bias_tiles(rel_bias):
    qi = jnp.arange(ATT_BLOCK)[:, None]
    kj = jnp.arange(ATT_BLOCK)[None, :]
    cur, prev = [], []
    for win, dil in DIL_PATTERNS:
        nk = win // dil
        bk = _pattern_bias(rel_bias, dil, nk)
        d_c = qi - kj
        d_p = ATT_BLOCK + qi - kj
        c = jnp.where((d_c >= 0)[None], bk[jnp.clip(d_c, 0, nk)].transpose(2, 0, 1), NEG_INF)
        p = jnp.where((d_p <= nk)[None], bk[jnp.clip(d_p, 0, nk)].transpose(2, 0, 1), NEG_INF)
        cur.append(c)
        prev.append(p)
    return jnp.stack(cur), jnp.stack(prev)


def _sample_bias_tables(rel_bias, L, T, lows, R):
    t = jnp.arange(T)[:, None]
    out = []
    for (win, dil), lo in zip(DIL_PATTERNS, lows):
        nk = win // dil
        bk = _pattern_bias(rel_bias, dil, nk)
        r = lo + jnp.arange(R - lo)[None, :]
        delta = L + t - r
        valid = (delta >= 0) & (delta % dil == 0) & (delta // dil <= nk) & (r < L + T)
        b = bk[jnp.clip(delta // dil, 0, nk)]
        b = jnp.where(valid[:, :, None], b, NEG_INF).transpose(2, 0, 1)
        out.append(b.reshape(N_ATT_HEADS * T, R - lo))
    return out


def _proj_kernel(x_ref, g_ref, w_ref, cos_ref, sa_ref, sb_ref,
                 rq_ref, rk_ref, rv_ref, rg_ref, aq_ref, ak_ref, av_ref, akb_ref, avb_ref):
    h = _rms(x_ref[0], g_ref[...]).astype(BF16)
    rep = RET_WIDTH // LANES
    cosf = jnp.concatenate([cos_ref[...]] * rep, axis=-1)
    sa = jnp.concatenate([sa_ref[...]] * rep, axis=-1)
    sb = jnp.concatenate([sb_ref[...]] * rep, axis=-1)
    half = HEAD_DIM // 2

    def col(c, width):
        return _dot(h, w_ref[:, c:c + width])

    def rope(z):
        return z * cosf + pltpu.roll(z, RET_WIDTH - half, 1) * sa + pltpu.roll(z, half, 1) * sb

    R, A = RET_WIDTH, ATT_WIDTH
    scale = HEAD_DIM ** -0.5
    rq_ref[0] = rope(col(0, R)).astype(rq_ref.dtype)
    rk_ref[0] = (rope(col(R, R)) * scale).astype(rk_ref.dtype)
    rv_ref[0] = col(2 * R, R).astype(rv_ref.dtype)
    rg_ref[0] = col(3 * R, R)
    aq_ref[0] = (col(4 * R, A) * scale).astype(aq_ref.dtype)
    k = col(4 * R + A, A)
    ak_ref[0] = k
    akb_ref[0] = k.astype(akb_ref.dtype)
    v = col(4 * R + 2 * A, A)
    av_ref[0] = v
    avb_ref[0] = v.astype(avb_ref.dtype)


def _proj(x, g, w_bf, pos, tm, act_dtype):
    B, T, D = x.shape
    cosf, sa, sb = _rope_tables(pos)
    R = RET_WIDTH
    row = lambda width: pl.BlockSpec((1, tm, width), lambda j, b: (b, j, 0))
    tab = pl.BlockSpec((tm, LANES), lambda j, b: (j, 0))
    full = lambda shape: pl.BlockSpec(shape, lambda j, b: (0,) * len(shape))
    sds = lambda dt: jax.ShapeDtypeStruct((B, T, R), dt)
    return pl.pallas_call(
        _proj_kernel,
        grid=(T // tm, B),
        in_specs=[row(D), full((1, D)), full(w_bf.shape), tab, tab, tab],
        out_specs=[row(R)] * 9,
        out_shape=[sds(act_dtype), sds(act_dtype), sds(act_dtype), sds(F32), sds(act_dtype),
                   sds(F32), sds(F32), sds(act_dtype), sds(act_dtype)],
        compiler_params=_cparams(("arbitrary", "arbitrary"), 52),
        name="proj",
    )(x, g.reshape(1, D), w_bf, cosf, sa, sb)


def _ret_kernel(has_state, *refs):
    if has_state:
        (q_ref, k_ref, v_ref, gate_ref, dec_ref, cd_ref, kd_ref, sd_ref, gr_ref, s0_ref,
         o_ref, s_ref) = refs
    else:
        (q_ref, k_ref, v_ref, gate_ref, dec_ref, cd_ref, kd_ref, sd_ref, gr_ref,
         o_ref, s_ref) = refs

    @pl.when(pl.program_id(1) == 0)
    def _():
        if has_state:
            s_ref[...] = s0_ref[...]
        else:
            s_ref[...] = jnp.zeros_like(s_ref)

    q = q_ref[0].astype(BF16)
    k = k_ref[0].astype(BF16)
    v = v_ref[0].astype(BF16)
    for h in range(N_RET_HEADS):
        sl = slice(h * HEAD_DIM, (h + 1) * HEAD_DIM)
        qh, kh, vh = q[:, sl], k[:, sl], v[:, sl]
        S = s_ref[0, h]
        inner = _dot_nt(qh, kh) * dec_ref[h]
        o = _dot(inner.astype(BF16), vh) + _dot(qh, S.astype(BF16)) * cd_ref[:, sl]
        kd = (kh.astype(F32) * kd_ref[:, sl]).astype(BF16)
        s_ref[0, h] = S * sd_ref[:, sl] + _dot_tn(kd, vh)
        mu = jnp.mean(o, axis=-1, keepdims=True)
        var = jnp.mean(jnp.square(o - mu), axis=-1, keepdims=True)
        on = (o - mu) * lax.rsqrt(var + GN_EPS) * gr_ref[:, sl]
        gate = gate_ref[0, :, sl]
        o_ref[0, :, sl] = (on * (gate * jax.nn.sigmoid(gate))).astype(o_ref.dtype)


def _retention(rq, rk, rv, rg, g_ret, state, C, out_dtype):
    B, T, R = rq.shape
    H, Dh = N_RET_HEADS, HEAD_DIM
    dec, cd, kd, sd = _retention_consts(C)
    has_state = state is not None
    row = pl.BlockSpec((1, C, R), lambda b, c: (b, c, 0))
    full = lambda shape: pl.BlockSpec(shape, lambda b, c: (0,) * len(shape))
    st = pl.BlockSpec((1, H, Dh, Dh), lambda b, c: (b, 0, 0, 0))
    args = [rq, rk, rv, rg, dec, cd, kd, sd, g_ret.reshape(1, R)]
    in_specs = [row, row, row, row, full(dec.shape), full(cd.shape), full(kd.shape), full(sd.shape),
                full((1, R))]
    if has_state:
        args.append(state)
        in_specs.append(st)
    return pl.pallas_call(
        functools.partial(_ret_kernel, has_state),
        grid=(B, T // C),
        in_specs=in_specs,
        out_specs=[row, st],
        out_shape=[jax.ShapeDtypeStruct((B, T, R), out_dtype), jax.ShapeDtypeStruct((B, H, Dh, Dh), F32)],
        compiler_params=_cparams(("arbitrary", "arbitrary"), 32),
        name="retention",
    )(*args)


def _att_block(q, kc, vc, kp, vp, bias_c_ref, bias_p_ref, g, first, acc_ref, ml_ref):
    lane = lax.broadcasted_iota(jnp.int32, (ATT_BLOCK, LANES), 1)
    ml = jnp.zeros((ATT_BLOCK, LANES), F32)
    for h in range(N_ATT_HEADS):
        sl = slice(h * HEAD_DIM, (h + 1) * HEAD_DIM)
        qh = q[:, sl]
        s_c = _dot_nt(qh, kc[:, sl]) + bias_c_ref[g, h]
        m = jnp.max(s_c, axis=-1, keepdims=True)
        if kp is not None:
            s_p = _dot_nt(qh, kp[:, sl]) + bias_p_ref[g, h]
            s_p = jnp.where(first, NEG_INF, s_p)
            m = jnp.maximum(m, jnp.max(s_p, axis=-1, keepdims=True))
        p_c = jnp.exp(s_c - m)
        l = jnp.sum(p_c, axis=-1, keepdims=True)
        acc = _dot(p_c.astype(BF16), vc[:, sl])
        if kp is not None:
            p_p = jnp.exp(s_p - m)
            l = l + jnp.sum(p_p, axis=-1, keepdims=True)
            acc = acc + _dot(p_p.astype(BF16), vp[:, sl])
        acc_ref[0, :, sl] = acc
        ml = jnp.where(lane == h, m, ml)
        ml = jnp.where(lane == N_ATT_HEADS + h, l, ml)
    ml_ref[0] = ml


def _att_kernel(q1, kc1, vc1, kp1, vp1, q4, kc4, vc4, kp4, vp4, q16, kc16, vc16, bias_c, bias_p,
                acc1, ml1, acc4, ml4, acc16, ml16):
    n = pl.program_id(1)
    nb1 = pl.num_programs(1)
    nb4 = nb1 // DIL_PATTERNS[1][1]
    _att_block(q1[0], kc1[0], vc1[0], kp1[0], vp1[0], bias_c, bias_p, 0, n % nb1 == 0, acc1, ml1)
    _att_block(q4[0], kc4[0], vc4[0], kp4[0], vp4[0], bias_c, bias_p, 1, n % nb4 == 0, acc4, ml4)
    _att_block(q16[0], kc16[0], vc16[0], None, None, bias_c, bias_p, 2, None, acc16, ml16)


def _prompt_attention(aq, ak, av, rel_bias):
    B, S, A = aq.shape
    BLK = ATT_BLOCK
    nblk = S // BLK
    assert [d for _, d in DIL_PATTERNS] == [1, 4, 16] and S == 2048
    bias_c, bias_p = _prompt_bias_tiles(rel_bias)
    in_specs, args = [], []
    out_specs, out_shape, views = [], [], []
    for gi, (win, dil) in enumerate(DIL_PATTERNS):
        Ls = S // dil
        nb = Ls // BLK
        view = lambda t, Ls=Ls, dil=dil: t.reshape(B, Ls, dil * A)
        cur = pl.BlockSpec((1, BLK, A), lambda b, n, nb=nb: (b, n % nb, n // nb))
        prev = pl.BlockSpec((1, BLK, A), lambda b, n, nb=nb: (b, jnp.maximum(n % nb - 1, 0), n // nb))
        in_specs += [cur, cur, cur]
        args += [view(aq), view(ak), view(av)]
        if nb > 1:
            in_specs += [prev, prev]
            args += [view(ak), view(av)]
        out_specs += [cur, pl.BlockSpec((1, BLK, LANES), lambda b, n, nb=nb: (b, n % nb, n // nb))]
        out_shape += [jax.ShapeDtypeStruct((B, Ls, dil * A), F32),
                      jax.ShapeDtypeStruct((B, Ls, dil * LANES), F32)]
    full = lambda shape: pl.BlockSpec(shape, lambda b, n: (0,) * len(shape))
    in_specs += [full(bias_c.shape), full(bias_p.shape)]
    args += [bias_c, bias_p]
    outs = pl.pallas_call(
        _att_kernel,
        grid=(B, nblk),
        in_specs=in_specs,
        out_specs=out_specs,
        out_shape=out_shape,
        compiler_params=_cparams(("arbitrary", "arbitrary"), 32),
        name="prompt_att",
    )(*args)
    res = []
    for gi in range(3):
        res.append((outs[2 * gi].reshape(B * S, A), outs[2 * gi + 1].reshape(B * S, LANES)))
    return res


def _samp_att_kernel(lows, T, kc_ref, vc_ref, qn_ref, kn_ref, vn_ref, b1_ref, b4_ref, b16_ref,
                     ko_ref, vo_ref, ao_ref):
    L = kc_ref.shape[1]
    A = kc_ref.shape[2]
    H = N_ATT_HEADS
    kc, vc = kc_ref[0], vc_ref[0]
    kn, vn = kn_ref[0], vn_ref[0]
    ko_ref[0, :L - T, :] = kc[T:, :]
    ko_ref[0, L - T:, :] = kn
    vo_ref[0, :L - T, :] = vc[T:, :]
    vo_ref[0, L - T:, :] = vn
    pad = jnp.zeros((T, A), F32)
    kall = jnp.concatenate([kc, kn, pad], axis=0).astype(BF16)
    vall = jnp.concatenate([vc, vn, pad], axis=0).astype(BF16)
    HT = H * T
    row_head = lax.broadcasted_iota(jnp.int32, (HT, A), 0) // T
    lane_head = lax.broadcasted_iota(jnp.int32, (HT, A), 1) // HEAD_DIM
    diag = row_head == lane_head
    q_rows = jnp.where(diag, jnp.concatenate([qn_ref[0]] * H, axis=0), 0.0).astype(BF16)
    s_all = _dot_nt(q_rows, kall)
    stats = []
    for lo, b_ref in zip(lows, (b1_ref, b4_ref, b16_ref)):
        s = s_all[:, lo:] + b_ref[...]
        m = jnp.max(s, axis=-1, keepdims=True)
        p = jnp.exp(s - m)
        l = jnp.sum(p, axis=-1, keepdims=True)
        acc = _dot(p.astype(BF16), vall[lo:, :])
        stats.append((m, l, acc))
    m_all = jnp.maximum(jnp.maximum(stats[0][0], stats[1][0]), stats[2][0])
    num = jnp.zeros((HT, A), F32)
    den = jnp.zeros((HT, 1), F32)
    for m, l, acc in stats:
        w = jnp.exp(m - m_all)
        num = num + w * acc
        den = den + w * l
    comb = jnp.where(diag, num / den, 0.0)
    out = comb[0:T]
    for h in range(1, H):
        out = out + comb[h * T:(h + 1) * T]
    ao_ref[0] = out


def _sample_attention(aq, ak, av, cache_k, cache_v, rel_bias):
    B, T, A = aq.shape
    L = cache_k.shape[1]
    R = L + 2 * T
    lows = tuple(max(0, (L - win) // 16 * 16) for win, _ in DIL_PATTERNS)
    tabs = _sample_bias_tables(rel_bias, L, T, lows, R)
    big = pl.BlockSpec((1, L, A), lambda b: (b, 0, 0))
    small = pl.BlockSpec((1, T, A), lambda b: (b, 0, 0))
    full = lambda shape: pl.BlockSpec(shape, lambda b: (0,) * len(shape))
    ko, vo, ao = pl.pallas_call(
        functools.partial(_samp_att_kernel, lows, T),
        grid=(B,),
        in_specs=[big, big, small, small, small] + [full(t.shape) for t in tabs],
        out_specs=[big, big, small],
        out_shape=[jax.ShapeDtypeStruct((B, L, A), F32), jax.ShapeDtypeStruct((B, L, A), F32),
                   jax.ShapeDtypeStruct((B, T, A), F32)],
        compiler_params=_cparams(("arbitrary",), 56),
        name="sample_att",
    )(cache_k, cache_v, aq, ak, av, *tabs)
    return ao, ko, vo


def _split3_dot(c, e):
    c1 = c.astype(BF16)
    r1 = c - c1.astype(F32)
    c2 = r1.astype(BF16)
    c3 = (r1 - c2.astype(F32)).astype(BF16)
    return _dot(c1, e) + _dot(c2, e) + _dot(c3, e)


def _out_kernel(merge, d_ff, ff_chunk, *refs):
    if merge:
        (x_ref, ro_ref, a1, m1, a4, m4, a16, m16, e_ref, wo_ref, wgu_ref, wd_ref,
         g_pm, g_pf, g_of, y_ref, act_ref) = refs
        H = N_ATT_HEADS
        mls = [m1[...], m4[...], m16[...]]
        lane = lax.broadcasted_iota(jnp.int32, mls[0].shape, 1)
        m_all = jnp.maximum(jnp.maximum(mls[0], mls[1]), mls[2])
        ws = [jnp.exp(ml - m_all) for ml in mls]
        den = sum(w * pltpu.roll(ml, LANES - H, 1) for w, ml in zip(ws, mls))
        ao = jnp.zeros(a1.shape, F32)
        for w, a in zip(ws, (a1, a4, a16)):
            coef = jnp.where(lane < H, w / den, 0.0)
            ao = ao + _split3_dot(coef, e_ref[...]) * a[...]
    else:
        (x_ref, ro_ref, ao_ref, wo_ref, wgu_ref, wd_ref, g_pm, g_pf, g_of, y_ref, act_ref) = refs
        ao = ao_ref[...]
    R = RET_WIDTH
    mix = _dot(ro_ref[...].astype(BF16), wo_ref[:R, :]) + _dot(ao.astype(BF16), wo_ref[R:, :])
    x1 = x_ref[...] + _rms(mix, g_pm[...])
    h = _rms(x1, g_pf[...]).astype(BF16)
    for c in range(0, d_ff, ff_chunk):
        gate = _dot(h, wgu_ref[:, c:c + ff_chunk])
        up = _dot(h, wgu_ref[:, d_ff + c:d_ff + c + ff_chunk])
        act_ref[:, c:c + ff_chunk] = (gate * jax.nn.sigmoid(gate) * up).astype(BF16)
    f = _dot(act_ref[...], wd_ref[...])
    y_ref[...] = x1 + _rms(f, g_of[...])


def _out_block(x, ro, att, wo_bf, wgu_bf, wd_bf, g_post_mix, g_pre_ffn, g_post_ffn, tm):
    N, D = x.shape
    d_ff = wd_bf.shape[0]
    merge = isinstance(att, list)
    row = lambda width: pl.BlockSpec((tm, width), lambda i: (i, 0))
    const = lambda shape: pl.BlockSpec(shape, lambda i: (0,) * len(shape), pipeline_mode=pl.Buffered(1))
    args = [x, ro]
    in_specs = [row(D), row(RET_WIDTH)]
    if merge:
        for acc, ml in att:
            args += [acc, ml]
            in_specs += [row(ATT_WIDTH), row(LANES)]
        head_of_lane = jnp.arange(ATT_WIDTH)[None, :] // HEAD_DIM
        expand = (jnp.arange(LANES)[:, None] == head_of_lane).astype(BF16)
        args.append(expand)
        in_specs.append(const(expand.shape))
    else:
        args.append(att)
        in_specs.append(row(ATT_WIDTH))
    args += [wo_bf, wgu_bf, wd_bf, g_post_mix.reshape(1, D), g_pre_ffn.reshape(1, D), g_post_ffn.reshape(1, D)]
    in_specs += [const(wo_bf.shape), const(wgu_bf.shape), const(wd_bf.shape),
                 const((1, D)), const((1, D)), const((1, D))]
    return pl.pallas_call(
        functools.partial(_out_kernel, merge, d_ff, 256),
        grid=(N // tm,),
        in_specs=in_specs,
        out_specs=row(D),
        out_shape=jax.ShapeDtypeStruct((N, D), F32),
        scratch_shapes=[pltpu.VMEM((tm, d_ff), BF16)],
        compiler_params=_cparams(("arbitrary",), 56),
        name="out_ffn",
    )(*args)


def kernel(x_prompt, x_sample, state_ret, cache_k_win, cache_v_win, rel_bias, w_in, g_ret, w_out,
           g_pre_mix, g_post_mix, g_pre_ffn, g_post_ffn, w_gu, w_down):
    depth = w_in.shape[0]
    assert depth == 1
    B, S, D = x_prompt.shape
    Bs, Ts, _ = x_sample.shape
    H, Dh = N_ATT_HEADS, HEAD_DIM
    l = 0
    w_in_bf = w_in[l].astype(BF16)
    wo_bf = w_out[l].astype(BF16)
    wgu_bf = w_gu[l].astype(BF16)
    wd_bf = w_down[l].astype(BF16)

    pos_p = jnp.arange(S, dtype=jnp.int32)
    rq, rk, rv, rg, aq, ak, av, akb, avb = _proj(x_prompt, g_pre_mix[l], w_in_bf, pos_p, 512, BF16)
    ro, s_p = _retention(rq, rk, rv, rg, g_ret[l], None, RET_CHUNK, BF16)
    att = _prompt_attention(aq, akb, avb, rel_bias)
    y_p = _out_block(x_prompt.reshape(B * S, D), ro.reshape(B * S, RET_WIDTH), att, wo_bf, wgu_bf, wd_bf,
                     g_post_mix[l], g_pre_ffn[l], g_post_ffn[l], 256)

    pos_s = PAST_LEN + jnp.arange(Ts, dtype=jnp.int32)
    N_s = Bs * Ts
    outs = _proj(x_sample.reshape(1, N_s, D), g_pre_mix[l], w_in_bf, jnp.tile(pos_s, Bs), N_s, F32)
    srq, srk, srv, srg, saq, sak, sav, _, _ = [t.reshape(Bs, Ts, RET_WIDTH) for t in outs]
    sro, s_s = _retention(srq, srk, srv, srg, g_ret[l], state_ret[l], Ts, F32)
    L = cache_k_win.shape[2]
    sao, k_s, v_s = _sample_attention(saq, sak, sav, cache_k_win[l].reshape(Bs, L, ATT_WIDTH),
                                      cache_v_win[l].reshape(Bs, L, ATT_WIDTH), rel_bias)
    y_s = _out_block(x_sample.reshape(N_s, D), sro.reshape(N_s, RET_WIDTH), sao.reshape(N_s, ATT_WIDTH),
                     wo_bf, wgu_bf, wd_bf, g_post_mix[l], g_pre_ffn[l], g_post_ffn[l], N_s)

    return (y_p.reshape(B, S, D), y_s.reshape(Bs, Ts, D),
            s_p[None], ak.reshape(1, B, S, H, Dh), av.reshape(1, B, S, H, Dh),
            s_s[None], k_s.reshape(1, Bs, L, H, Dh), v_s.reshape(1, Bs, L, H, Dh))
```

```python
import functools
import math

import jax
import jax.numpy as jnp
from jax import lax
from jax.experimental import pallas as pl
from jax.experimental.pallas import tpu as pltpu

F32 = jnp.float32
BF16 = jnp.bfloat16

HEAD_DIM = 64
N_RET_HEADS = 8
N_ATT_HEADS = 8
RET_WIDTH = N_RET_HEADS * HEAD_DIM
ATT_WIDTH = N_ATT_HEADS * HEAD_DIM
RET_CHUNK = 128
DIL_PATTERNS = ((128, 1), (512, 4), (2048, 16))
ATT_BLOCK = 128
N_BUCKETS = 32
MAX_DISTANCE = 2048
ROPE_BASE = 10000.0
NORM_EPS = 1e-6
GN_EPS = 1e-5
PAST_LEN = 16384
LANES = 128
ATT_CHUNKS = ATT_WIDTH // LANES
NEG_INF = float("-inf")


def _cparams(sem, vmem_mb):
    return pltpu.CompilerParams(dimension_semantics=sem, vmem_limit_bytes=vmem_mb * 1024 * 1024)


def _rms(x, g):
    return x * lax.rsqrt(jnp.mean(x * x, axis=-1, keepdims=True) + NORM_EPS) * g


def _dot(a, b):
    return jnp.dot(a, b, preferred_element_type=F32)


def _dot_nt(a, b):
    return lax.dot_general(a, b, (((1,), (1,)), ((), ())), preferred_element_type=F32)


def _dot_tn(a, b):
    return lax.dot_general(a, b, (((0,), (0,)), ((), ())), preferred_element_type=F32)


def _rope_tables(pos):
    inv = ROPE_BASE ** (-jnp.arange(0, HEAD_DIM, 2, dtype=F32) / HEAD_DIM)
    ang = pos.astype(F32)[:, None] * inv[None, :]
    cos, sin = jnp.cos(ang), jnp.sin(ang)
    zero = jnp.zeros_like(sin)
    rep = LANES // HEAD_DIM
    cosf = jnp.tile(jnp.concatenate([cos, cos], axis=-1), (1, rep))
    sa = jnp.tile(jnp.concatenate([-sin, zero], axis=-1), (1, rep))
    sb = jnp.tile(jnp.concatenate([zero, sin], axis=-1), (1, rep))
    return cosf, sa, sb


def _retention_consts(C):
    log_g = jnp.log1p(-jnp.exp2(-5.0 - jnp.arange(N_RET_HEADS, dtype=F32)))
    i = jnp.arange(C, dtype=F32)
    rel = i[:, None] - i[None, :]
    decay = jnp.where(rel[None] >= 0, jnp.exp(jnp.maximum(rel, 0.0)[None] * log_g[:, None, None]), 0.0)
    cross = jnp.exp((i + 1.0)[:, None] * log_g[None, :])
    kdec = jnp.exp((C - 1.0 - i)[:, None] * log_g[None, :])
    sdec = jnp.exp(C * log_g)[None, :]
    expand = lambda t: jnp.repeat(t, HEAD_DIM, axis=-1)
    return decay, expand(cross), expand(kdec), expand(sdec)


def _t5_bucket(dist):
    max_exact = N_BUCKETS // 2
    d_f = jnp.maximum(dist, 1).astype(F32)
    large = max_exact + (jnp.log(d_f / max_exact) / math.log(MAX_DISTANCE / max_exact)
                         * (N_BUCKETS - max_exact)).astype(jnp.int32)
    large = jnp.minimum(large, N_BUCKETS - 1)
    return jnp.where(dist < max_exact, dist, large)


def _pattern_bias_rev(rel_bias, dil, nk):
    dist = jnp.arange(nk, -1, -1, dtype=jnp.int32) * dil
    onehot = (_t5_bucket(dist)[None, :] == jnp.arange(N_BUCKETS)[:, None]).astype(F32)
    return jnp.dot(rel_bias.astype(F32).T, onehot, precision=lax.Precision.HIGHEST)


def _prompt_bias_tables(rel_bias):
    BLK, H = ATT_BLOCK, N_ATT_HEADS
    out = []
    for win, dil in DIL_PATTERNS:
        nk = win // dil
        assert nk == BLK
        period = 3 * BLK
        v = jnp.concatenate([_pattern_bias_rev(rel_bias, dil, nk),
                             jnp.full((H, period - nk - 1), NEG_INF, F32)], axis=1)
        flat = jnp.tile(v, (1, BLK))[:, :BLK * (period - 1)]
        skew = flat.reshape(H, BLK, period - 1)[:, :, :2 * BLK]
        out.append(skew.reshape(H * BLK, 2 * BLK))
    return out


def _sample_bias_tables(rel_bias, L, T):
    H = N_ATT_HEADS
    out = []
    for win, dil in DIL_PATTERNS:
        nk = win // dil
        rev = _pattern_bias_rev(rel_bias, dil, nk)
        if dil > 1:
            gaps = jnp.full((H, nk + 1, dil - 1), NEG_INF, F32)
            rev = jnp.concatenate([rev[:, :, None], gaps], axis=2).reshape(H, (nk + 1) * dil)[:, :nk * dil + 1]
        pv = jnp.concatenate([jnp.full((H, T), NEG_INF, F32), rev, jnp.full((H, 2 * T), NEG_INF, F32)], axis=1)
        width = win + 2 * T
        rows = jnp.stack([pv[:, T - t:T - t + width] for t in range(T)], axis=1)
        out.append(rows.reshape(H * T, width))
    return out


def _proj_kernel(chunked, x_ref, g_ref, w_ref, cos_ref, sa_ref, sb_ref, *outs):
    h = _rms(x_ref[0], g_ref[...]).astype(BF16)
    rep = RET_WIDTH // LANES
    cosf = jnp.concatenate([cos_ref[...]] * rep, axis=-1)
    sa = jnp.concatenate([sa_ref[...]] * rep, axis=-1)
    sb = jnp.concatenate([sb_ref[...]] * rep, axis=-1)
    half = HEAD_DIM // 2

    def col(c, width):
        return _dot(h, w_ref[:, c:c + width])

    def rope(z):
        return z * cosf + pltpu.roll(z, RET_WIDTH - half, 1) * sa + pltpu.roll(z, half, 1) * sb

    def put_chunks(ref, z):
        for c in range(ATT_CHUNKS):
            ref[0, c] = z[:, c * LANES:(c + 1) * LANES]

    R, A = RET_WIDTH, ATT_WIDTH
    scale = HEAD_DIM ** -0.5
    rq_ref, rk_ref, rv_ref, rg_ref = outs[:4]
    rq_ref[0] = rope(col(0, R)).astype(rq_ref.dtype)
    rk_ref[0] = (rope(col(R, R)) * scale).astype(rk_ref.dtype)
    rv_ref[0] = col(2 * R, R).astype(rv_ref.dtype)
    rg_ref[0] = col(3 * R, R)
    q = col(4 * R, A) * scale
    k = col(4 * R + A, A)
    v = col(4 * R + 2 * A, A)
    if chunked:
        aqc_ref, ak_ref, akc_ref, av_ref, avc_ref = outs[4:]
        put_chunks(aqc_ref, q)
        put_chunks(akc_ref, k)
        put_chunks(avc_ref, v)
    else:
        aq_ref, ak_ref, av_ref = outs[4:]
        aq_ref[0] = q
    ak_ref[0] = k
    av_ref[0] = v


def _proj(x, g, w_bf, pos, tm, act_dtype, chunked):
    B, T, D = x.shape
    cosf, sa, sb = _rope_tables(pos)
    R = RET_WIDTH
    row = pl.BlockSpec((1, tm, R), lambda j, b: (b, j, 0))
    chk = pl.BlockSpec((1, ATT_CHUNKS, tm, LANES), lambda j, b: (b, 0, j, 0))
    tab = pl.BlockSpec((tm, LANES), lambda j, b: (j, 0))
    full = lambda shape: pl.BlockSpec(shape, lambda j, b: (0,) * len(shape))
    sds = lambda dt: jax.ShapeDtypeStruct((B, T, R), dt)
    csds = jax.ShapeDtypeStruct((B, ATT_CHUNKS, T, LANES), F32)
    out_specs = [row] * 4
    out_shape = [sds(act_dtype), sds(act_dtype), sds(act_dtype), sds(F32)]
    if chunked:
        out_specs += [chk, row, chk, row, chk]
        out_shape += [csds, sds(F32), csds, sds(F32), csds]
    else:
        out_specs += [row] * 3
        out_shape += [sds(F32)] * 3
    return pl.pallas_call(
        functools.partial(_proj_kernel, chunked),
        grid=(T // tm, B),
        in_specs=[pl.BlockSpec((1, tm, D), lambda j, b: (b, j, 0)), full((1, D)), full(w_bf.shape), tab, tab, tab],
        out_specs=out_specs,
        out_shape=out_shape,
        compiler_params=_cparams(("arbitrary", "arbitrary"), 52),
        name="proj",
    )(x, g.reshape(1, D), w_bf, cosf, sa, sb)


def _ret_kernel(has_state, *refs):
    if has_state:
        (q_ref, k_ref, v_ref, gate_ref, dec_ref, cd_ref, kd_ref, sd_ref, gr_ref, s0_ref,
         o_ref, s_ref) = refs
    else:
        (q_ref, k_ref, v_ref, gate_ref, dec_ref, cd_ref, kd_ref, sd_ref, gr_ref,
         o_ref, s_ref) = refs

    @pl.when(pl.program_id(1) == 0)
    def _():
        if has_state:
            s_ref[...] = s0_ref[...]
        else:
            s_ref[...] = jnp.zeros_like(s_ref)

    q = q_ref[0].astype(BF16)
    k = k_ref[0].astype(BF16)
    v = v_ref[0].astype(BF16)
    for h in range(N_RET_HEADS):
        sl = slice(h * HEAD_DIM, (h + 1) * HEAD_DIM)
        qh, kh, vh = q[:, sl], k[:, sl], v[:, sl]
        S = s_ref[0, h]
        inner = _dot_nt(qh, kh) * dec_ref[h]
        o = _dot(inner.astype(BF16), vh) + _dot(qh, S.astype(BF16)) * cd_ref[:, sl]
        kd = (kh.astype(F32) * kd_ref[:, sl]).astype(BF16)
        s_ref[0, h] = S * sd_ref[:, sl] + _dot_tn(kd, vh)
        mu = jnp.mean(o, axis=-1, keepdims=True)
        var = jnp.mean(jnp.square(o - mu), axis=-1, keepdims=True)
        on = (o - mu) * lax.rsqrt(var + GN_EPS) * gr_ref[:, sl]
        gate = gate_ref[0, :, sl]
        o_ref[0, :, sl] = (on * (gate * jax.nn.sigmoid(gate))).astype(o_ref.dtype)


def _retention(rq, rk, rv, rg, g_ret, state, C, out_dtype):
    B, T, R = rq.shape
    H, Dh = N_RET_HEADS, HEAD_DIM
    dec, cd, kd, sd = _retention_consts(C)
    has_state = state is not None
    row = pl.BlockSpec((1, C, R), lambda b, c: (b, c, 0))
    full = lambda shape: pl.BlockSpec(shape, lambda b, c: (0,) * len(shape))
    st = pl.BlockSpec((1, H, Dh, Dh), lambda b, c: (b, 0, 0, 0))
    args = [rq, rk, rv, rg, dec, cd, kd, sd, g_ret.reshape(1, R)]
    in_specs = [row, row, row, row, full(dec.shape), full(cd.shape), full(kd.shape), full(sd.shape),
                full((1, R))]
    if has_state:
        args.append(state)
        in_specs.append(st)
    return pl.pallas_call(
        functools.partial(_ret_kernel, has_state),
        grid=(B, T // C),
        in_specs=in_specs,
        out_specs=[row, st],
        out_shape=[jax.ShapeDtypeStruct((B, T, R), out_dtype), jax.ShapeDtypeStruct((B, H, Dh, Dh), F32)],
        compiler_params=_cparams(("arbitrary", "arbitrary"), 32),
        name="retention",
    )(*args)


def _split3_dot(c, e):
    c1 = c.astype(BF16)
    r1 = c - c1.astype(F32)
    c2 = r1.astype(BF16)
    c3 = (r1 - c2.astype(F32)).astype(BF16)
    return _dot(c1, e) + _dot(c2, e) + _dot(c3, e)


def _rows(ref, start, dil, c):
    idx = pl.ds(start, ATT_BLOCK) if dil == 1 else pl.ds(start, ATT_BLOCK, stride=dil)
    return ref[0, c, idx, :]


def _att_block(q_ref, k_ref, v_ref, bias_ref, dil, cur, prev, first):
    BLK, H = ATT_BLOCK, N_ATT_HEADS
    half = lax.broadcasted_iota(jnp.int32, (BLK, LANES), 1) < HEAD_DIM
    s_parts = []
    for c in range(ATT_CHUNKS):
        q = _rows(q_ref, cur, dil, c)
        q_pair = jnp.concatenate([jnp.where(half, q, 0.0), jnp.where(half, 0.0, q)], axis=0).astype(BF16)
        k = _rows(k_ref, cur, dil, c)
        if prev is not None:
            k = jnp.concatenate([_rows(k_ref, prev, dil, c), k], axis=0)
        s_parts.append(_dot_nt(q_pair, k.astype(BF16)))
    s = jnp.concatenate(s_parts, axis=0)
    if prev is not None:
        s = s + bias_ref[...]
        col = lax.broadcasted_iota(jnp.int32, s.shape, 1)
        s = jnp.where(jnp.logical_and(first, col < BLK), NEG_INF, s)
    else:
        s = s + bias_ref[:, BLK:]
    m = jnp.max(s, axis=-1, keepdims=True)
    p = jnp.exp(s - m)
    l = jnp.sum(p, axis=-1, keepdims=True)
    pb = p.astype(BF16)
    accs = []
    for c in range(ATT_CHUNKS):
        v = _rows(v_ref, cur, dil, c)
        if prev is not None:
            v = jnp.concatenate([_rows(v_ref, prev, dil, c), v], axis=0)
        o = _dot(pb[2 * c * BLK:(2 * c + 2) * BLK], v.astype(BF16))
        accs.append(jnp.where(half, o[:BLK], o[BLK:]))
    lane = lax.broadcasted_iota(jnp.int32, (BLK, LANES), 1)
    ml = jnp.zeros((BLK, LANES), F32)
    for h in range(H):
        ml = jnp.where(lane == h, m[h * BLK:(h + 1) * BLK], ml)
        ml = jnp.where(lane == H + h, l[h * BLK:(h + 1) * BLK], ml)
    return accs, ml


def _att_kernel(q_ref, k_ref, v_ref, b1_ref, b4_ref, b16_ref, e_ref, ao_ref, acc4, ml4, acc16, ml16):
    BLK, H = ATT_BLOCK, N_ATT_HEADS
    n = pl.program_id(1)
    nblk = pl.num_programs(1) // 2
    d4, d16 = DIL_PATTERNS[1][1], DIL_PATTERNS[2][1]

    @pl.when(n < nblk)
    def _():
        nb4 = nblk // d4
        r, j = n // nb4, n % nb4
        cur = r + j * (BLK * d4)
        prev = r + jnp.maximum(j - 1, 0) * (BLK * d4)
        accs, ml = _att_block(q_ref, k_ref, v_ref, b4_ref, d4, cur, prev, j == 0)
        for c in range(ATT_CHUNKS):
            acc4[c, pl.ds(cur, BLK, stride=d4), :] = accs[c]
        ml4[pl.ds(cur, BLK, stride=d4), :] = ml
        accs, ml = _att_block(q_ref, k_ref, v_ref, b16_ref, d16, n, None, None)
        for c in range(ATT_CHUNKS):
            acc16[c, pl.ds(n, BLK, stride=d16), :] = accs[c]
        ml16[pl.ds(n, BLK, stride=d16), :] = ml

    @pl.when(n >= nblk)
    def _():
        j = n - nblk
        cur = pl.multiple_of(j * BLK, BLK)
        prev = pl.multiple_of(jnp.maximum(j - 1, 0) * BLK, BLK)
        accs1, ml1 = _att_block(q_ref, k_ref, v_ref, b1_ref, 1, cur, prev, j == 0)
        rows = pl.ds(cur, BLK)
        mls = [ml1, ml4[rows, :], ml16[rows, :]]
        accs = [accs1, [acc4[c, rows, :] for c in range(ATT_CHUNKS)], [acc16[c, rows, :] for c in range(ATT_CHUNKS)]]
        lane = lax.broadcasted_iota(jnp.int32, (BLK, LANES), 1)
        m_all = jnp.maximum(jnp.maximum(mls[0], mls[1]), mls[2])
        ws = [jnp.exp(ml - m_all) for ml in mls]
        den = sum(w * pltpu.roll(ml, LANES - H, 1) for w, ml in zip(ws, mls))
        ao = [jnp.zeros((BLK, LANES), F32) for _ in range(ATT_CHUNKS)]
        for w, acc in zip(ws, accs):
            coef = _split3_dot(jnp.where(lane < H, w / den, 0.0), e_ref[...])
            for c in range(ATT_CHUNKS):
                ao[c] = ao[c] + coef[:, c * LANES:(c + 1) * LANES] * acc[c]
        ao_ref[0] = jnp.concatenate(ao, axis=-1).astype(ao_ref.dtype)


def _prompt_attention(aqc, akc, avc, rel_bias):
    B, CH, S, _ = aqc.shape
    A = ATT_WIDTH
    BLK = ATT_BLOCK
    nblk = S // BLK
    assert [d for _, d in DIL_PATTERNS] == [1, 4, 16] and nblk == 16
    b1, b4, b16 = _prompt_bias_tables(rel_bias)
    head_of_lane = jnp.arange(A)[None, :] // HEAD_DIM
    expand = (jnp.arange(LANES)[:, None] == head_of_lane).astype(BF16)
    seq = pl.BlockSpec((1, CH, S, LANES), lambda b, n: (b, 0, 0, 0))
    const = lambda shape: pl.BlockSpec(shape, lambda b, n: (0,) * len(shape), pipeline_mode=pl.Buffered(1))
    return pl.pallas_call(
        _att_kernel,
        grid=(B, 2 * nblk),
        in_specs=[seq, seq, seq, const(b1.shape), const(b4.shape), const(b16.shape), const(expand.shape)],
        out_specs=pl.BlockSpec((1, BLK, A), lambda b, n: (b, jnp.maximum(n - nblk, 0), 0)),
        out_shape=jax.ShapeDtypeStruct((B, S, A), BF16),
        scratch_shapes=[pltpu.VMEM((CH, S, LANES), F32), pltpu.VMEM((S, LANES), F32),
                        pltpu.VMEM((CH, S, LANES), F32), pltpu.VMEM((S, LANES), F32)],
        compiler_params=_cparams(("arbitrary", "arbitrary"), 56),
        name="prompt_att",
    )(aqc, akc, avc, b1, b4, b16, expand)


def _samp_att_kernel(lows, T, kc_ref, vc_ref, qn_ref, kn_ref, vn_ref, b1_ref, b4_ref, b16_ref,
                     ko_ref, vo_ref, ao_ref):
    L = kc_ref.shape[1]
    A = kc_ref.shape[2]
    H = N_ATT_HEADS
    kc, vc = kc_ref[0], vc_ref[0]
    kn, vn = kn_ref[0], vn_ref[0]
    ko_ref[0, :L - T, :] = kc[T:, :]
    ko_ref[0, L - T:, :] = kn
    vo_ref[0, :L - T, :] = vc[T:, :]
    vo_ref[0, L - T:, :] = vn
    pad = jnp.zeros((T, A), F32)
    kall = jnp.concatenate([kc, kn, pad], axis=0).astype(BF16)
    vall = jnp.concatenate([vc, vn, pad], axis=0).astype(BF16)
    HT = H * T
    row_head = lax.broadcasted_iota(jnp.int32, (HT, A), 0) // T
    lane_head = lax.broadcasted_iota(jnp.int32, (HT, A), 1) // HEAD_DIM
    diag = row_head == lane_head
    q_rows = jnp.where(diag, jnp.concatenate([qn_ref[0]] * H, axis=0), 0.0).astype(BF16)
    s_all = _dot_nt(q_rows, kall)
    stats = []
    for lo, b_ref in zip(lows, (b1_ref, b4_ref, b16_ref)):
        s = s_all[:, lo:] + b_ref[...]
        m = jnp.max(s, axis=-1, keepdims=True)
        p = jnp.exp(s - m)
        l = jnp.sum(p, axis=-1, keepdims=True)
        acc = _dot(p.astype(BF16), vall[lo:, :])
        stats.append((m, l, acc))
    m_all = jnp.maximum(jnp.maximum(stats[0][0], stats[1][0]), stats[2][0])
    num = jnp.zeros((HT, A), F32)
    den = jnp.zeros((HT, 1), F32)
    for m, l, acc in stats:
        w = jnp.exp(m - m_all)
        num = num + w * acc
        den = den + w * l
    comb = jnp.where(diag, num / den, 0.0)
    out = comb[0:T]
    for h in range(1, H):
        out = out + comb[h * T:(h + 1) * T]
    ao_ref[0] = out


def _sample_attention(aq, ak, av, cache_k, cache_v, rel_bias):
    B, T, A = aq.shape
    L = cache_k.shape[1]
    lows = tuple(L - win for win, _ in DIL_PATTERNS)
    assert all(lo >= 0 and lo % LANES == 0 for lo in lows)
    tabs = _sample_bias_tables(rel_bias, L, T)
    big = pl.BlockSpec((1, L, A), lambda b: (b, 0, 0))
    small = pl.BlockSpec((1, T, A), lambda b: (b, 0, 0))
    full = lambda shape: pl.BlockSpec(shape, lambda b: (0,) * len(shape))
    ko, vo, ao = pl.pallas_call(
        functools.partial(_samp_att_kernel, lows, T),
        grid=(B,),
        in_specs=[big, big, small, small, small] + [full(t.shape) for t in tabs],
        out_specs=[big, big, small],
        out_shape=[jax.ShapeDtypeStruct((B, L, A), F32), jax.ShapeDtypeStruct((B, L, A), F32),
                   jax.ShapeDtypeStruct((B, T, A), F32)],
        compiler_params=_cparams(("arbitrary",), 56),
        name="sample_att",
    )(cache_k, cache_v, aq, ak, av, *tabs)
    return ao, ko, vo


def _out_kernel(d_ff, ff_chunk, x_ref, ro_ref, ao_ref, wo_ref, wgu_ref, wd_ref, g_pm, g_pf, g_of, y_ref, act_ref):
    R = RET_WIDTH
    mix = _dot(ro_ref[...].astype(BF16), wo_ref[:R, :]) + _dot(ao_ref[...].astype(BF16), wo_ref[R:, :])
    x1 = x_ref[...] + _rms(mix, g_pm[...])
    h = _rms(x1, g_pf[...]).astype(BF16)
    for c in range(0, d_ff, ff_chunk):
        gate = _dot(h, wgu_ref[:, c:c + ff_chunk])
        up = _dot(h, wgu_ref[:, d_ff + c:d_ff + c + ff_chunk])
        act_ref[:, c:c + ff_chunk] = (gate * jax.nn.sigmoid(gate) * up).astype(BF16)
    f = _dot(act_ref[...], wd_ref[...])
    y_ref[...] = x1 + _rms(f, g_of[...])


def _out_block(x, ro, ao, wo_bf, wgu_bf, wd_bf, g_post_mix, g_pre_ffn, g_post_ffn, tm):
    N, D = x.shape
    d_ff = wd_bf.shape[0]
    row = lambda width: pl.BlockSpec((tm, width), lambda i: (i, 0))
    const = lambda shape: pl.BlockSpec(shape, lambda i: (0,) * len(shape), pipeline_mode=pl.Buffered(1))
    return pl.pallas_call(
        functools.partial(_out_kernel, d_ff, 256),
        grid=(N // tm,),
        in_specs=[row(D), row(RET_WIDTH), row(ATT_WIDTH), const(wo_bf.shape), const(wgu_bf.shape),
                  const(wd_bf.shape), const((1, D)), const((1, D)), const((1, D))],
        out_specs=row(D),
        out_shape=jax.ShapeDtypeStruct((N, D), F32),
        scratch_shapes=[pltpu.VMEM((tm, d_ff), BF16)],
        compiler_params=_cparams(("arbitrary",), 56),
        name="out_ffn",
    )(x, ro, ao, wo_bf, wgu_bf, wd_bf, g_post_mix.reshape(1, D), g_pre_ffn.reshape(1, D),
      g_post_ffn.reshape(1, D))


def kernel(x_prompt, x_sample, state_ret, cache_k_win, cache_v_win, rel_bias, w_in, g_ret, w_out,
           g_pre_mix, g_post_mix, g_pre_ffn, g_post_ffn, w_gu, w_down):
    depth = w_in.shape[0]
    assert depth == 1
    B, S, D = x_prompt.shape
    Bs, Ts, _ = x_sample.shape
    H, Dh = N_ATT_HEADS, HEAD_DIM
    l = 0
    w_in_bf = w_in[l].astype(BF16)
    wo_bf = w_out[l].astype(BF16)
    wgu_bf = w_gu[l].astype(BF16)
    wd_bf = w_down[l].astype(BF16)

    pos_p = jnp.arange(S, dtype=jnp.int32)
    rq, rk, rv, rg, aqc, ak, akc, av, avc = _proj(x_prompt, g_pre_mix[l], w_in_bf, pos_p, 512, BF16, True)
    ro, s_p = _retention(rq, rk, rv, rg, g_ret[l], None, RET_CHUNK, BF16)
    ao = _prompt_attention(aqc, akc, avc, rel_bias)
    y_p = _out_block(x_prompt.reshape(B * S, D), ro.reshape(B * S, RET_WIDTH), ao.reshape(B * S, ATT_WIDTH),
                     wo_bf, wgu_bf, wd_bf, g_post_mix[l], g_pre_ffn[l], g_post_ffn[l], 256)

    pos_s = PAST_LEN + jnp.arange(Ts, dtype=jnp.int32)
    N_s = Bs * Ts
    outs = _proj(x_sample.reshape(1, N_s, D), g_pre_mix[l], w_in_bf, jnp.tile(pos_s, Bs), N_s, F32, False)
    srq, srk, srv, srg, saq, sak, sav = [t.reshape(Bs, Ts, RET_WIDTH) for t in outs]
    sro, s_s = _retention(srq, srk, srv, srg, g_ret[l], state_ret[l], Ts, F32)
    L = cache_k_win.shape[2]
    sao, k_s, v_s = _sample_attention(saq, sak, sav, cache_k_win[l].reshape(Bs, L, ATT_WIDTH),
                                      cache_v_win[l].reshape(Bs, L, ATT_WIDTH), rel_bias)
    y_s = _out_block(x_sample.reshape(N_s, D), sro.reshape(N_s, RET_WIDTH), sao.reshape(N_s, ATT_WIDTH),
                     wo_bf, wgu_bf, wd_bf, g_post_mix[l], g_pre_ffn[l], g_post_ffn[l], N_s)

    return (y_p.reshape(B, S, D), y_s.reshape(Bs, Ts, D),
            s_p[None], ak.reshape(1, B, S, H, Dh), av.reshape(1, B, S, H, Dh),
            s_s[None], k_s.reshape(1, Bs, L, H, Dh), v_s.reshape(1, Bs, L, H, Dh))
```

```python
import functools
import math

import jax
import jax.numpy as jnp
from jax import lax
from jax.experimental import pallas as pl
from jax.experimental.pallas import tpu as pltpu

F32 = jnp.float32
BF16 = jnp.bfloat16

HEAD_DIM = 64
N_RET_HEADS = 8
N_ATT_HEADS = 8
RET_WIDTH = N_RET_HEADS * HEAD_DIM
ATT_WIDTH = N_ATT_HEADS * HEAD_DIM
RET_CHUNK = 128
DIL_PATTERNS = ((128, 1), (512, 4), (2048, 16))
ATT_BLOCK = 128
N_BUCKETS = 32
MAX_DISTANCE = 2048
ROPE_BASE = 10000.0
NORM_EPS = 1e-6
GN_EPS = 1e-5
PAST_LEN = 16384
LANES = 128
ATT_CHUNKS = ATT_WIDTH // LANES
NEG_INF = float("-inf")


def _cparams(sem, vmem_mb):
    return pltpu.CompilerParams(dimension_semantics=sem, vmem_limit_bytes=vmem_mb * 1024 * 1024)


def _rms(x, g):
    return x * lax.rsqrt(jnp.mean(x * x, axis=-1, keepdims=True) + NORM_EPS) * g


def _dot(a, b):
    return jnp.dot(a, b, preferred_element_type=F32)


def _dot_nt(a, b):
    return lax.dot_general(a, b, (((1,), (1,)), ((), ())), preferred_element_type=F32)


def _dot_tn(a, b):
    return lax.dot_general(a, b, (((0,), (0,)), ((), ())), preferred_element_type=F32)


def _rope_tables(pos):
    inv = ROPE_BASE ** (-jnp.arange(0, HEAD_DIM, 2, dtype=F32) / HEAD_DIM)
    ang = pos.astype(F32)[:, None] * inv[None, :]
    cos, sin = jnp.cos(ang), jnp.sin(ang)
    zero = jnp.zeros_like(sin)
    rep = LANES // HEAD_DIM
    cosf = jnp.tile(jnp.concatenate([cos, cos], axis=-1), (1, rep))
    sa = jnp.tile(jnp.concatenate([-sin, zero], axis=-1), (1, rep))
    sb = jnp.tile(jnp.concatenate([zero, sin], axis=-1), (1, rep))
    return cosf, sa, sb


def _retention_consts(C):
    log_g = jnp.log1p(-jnp.exp2(-5.0 - jnp.arange(N_RET_HEADS, dtype=F32)))
    i = jnp.arange(C, dtype=F32)
    rel = i[:, None] - i[None, :]
    decay = jnp.where(rel[None] >= 0, jnp.exp(jnp.maximum(rel, 0.0)[None] * log_g[:, None, None]), 0.0)
    cross = jnp.exp((i + 1.0)[:, None] * log_g[None, :])
    kdec = jnp.exp((C - 1.0 - i)[:, None] * log_g[None, :])
    sdec = jnp.exp(C * log_g)[None, :]
    expand = lambda t: jnp.repeat(t, HEAD_DIM, axis=-1)
    return decay, expand(cross), expand(kdec), expand(sdec)


def _t5_bucket(dist):
    max_exact = N_BUCKETS // 2
    d_f = jnp.maximum(dist, 1).astype(F32)
    large = max_exact + (jnp.log(d_f / max_exact) / math.log(MAX_DISTANCE / max_exact)
                         * (N_BUCKETS - max_exact)).astype(jnp.int32)
    large = jnp.minimum(large, N_BUCKETS - 1)
    return jnp.where(dist < max_exact, dist, large)


def _pattern_bias_rev(rel_bias, dil, nk):
    dist = jnp.arange(nk, -1, -1, dtype=jnp.int32) * dil
    onehot = (_t5_bucket(dist)[None, :] == jnp.arange(N_BUCKETS)[:, None]).astype(F32)
    return jnp.dot(rel_bias.astype(F32).T, onehot, precision=lax.Precision.HIGHEST)


def _prompt_bias_tables(rel_bias):
    BLK, H = ATT_BLOCK, N_ATT_HEADS
    out = []
    for win, dil in DIL_PATTERNS:
        nk = win // dil
        assert nk == BLK
        period = 3 * BLK
        v = jnp.concatenate([_pattern_bias_rev(rel_bias, dil, nk),
                             jnp.full((H, period - nk - 1), NEG_INF, F32)], axis=1)
        flat = jnp.tile(v, (1, BLK))[:, :BLK * (period - 1)]
        skew = flat.reshape(H, BLK, period - 1)[:, :, :2 * BLK]
        out.append(skew.reshape(H * BLK, 2 * BLK))
    return out


def _sample_bias_tables(rel_bias, L, T):
    H = N_ATT_HEADS
    out = []
    for win, dil in DIL_PATTERNS:
        nk = win // dil
        rev = _pattern_bias_rev(rel_bias, dil, nk)
        if dil > 1:
            gaps = jnp.full((H, nk + 1, dil - 1), NEG_INF, F32)
            rev = jnp.concatenate([rev[:, :, None], gaps], axis=2).reshape(H, (nk + 1) * dil)[:, :nk * dil + 1]
        pv = jnp.concatenate([jnp.full((H, T), NEG_INF, F32), rev, jnp.full((H, 2 * T), NEG_INF, F32)], axis=1)
        width = win + 2 * T
        rows = jnp.stack([pv[:, T - t:T - t + width] for t in range(T)], axis=1)
        out.append(rows.reshape(H * T, width))
    return out


def _proj_cols(x, g_ref, w_ref, cos, sa, sb):
    h = _rms(x, g_ref[...]).astype(BF16)
    rep = RET_WIDTH // LANES
    cosf = jnp.concatenate([cos] * rep, axis=-1)
    saf = jnp.concatenate([sa] * rep, axis=-1)
    sbf = jnp.concatenate([sb] * rep, axis=-1)
    half = HEAD_DIM // 2

    def col(c, width):
        return _dot(h, w_ref[:, c:c + width])

    def rope(z):
        return z * cosf + pltpu.roll(z, RET_WIDTH - half, 1) * saf + pltpu.roll(z, half, 1) * sbf

    R, A = RET_WIDTH, ATT_WIDTH
    scale = HEAD_DIM ** -0.5
    rq = rope(col(0, R))
    rk = rope(col(R, R)) * scale
    rv = col(2 * R, R)
    rg = col(3 * R, R)
    aq = col(4 * R, A) * scale
    ak = col(4 * R + A, A)
    av = col(4 * R + 2 * A, A)
    return rq, rk, rv, rg, aq, ak, av


def _proj_kernel(x_ref, g_ref, w_ref, cos_ref, sa_ref, sb_ref, *outs):
    cols = _proj_cols(x_ref[...], g_ref, w_ref, cos_ref[...], sa_ref[...], sb_ref[...])
    for ref, z in zip(outs, cols):
        ref[...] = z


def _proj(x, g, w_bf, pos):
    N, D = x.shape
    tabs = _rope_tables(pos)
    full = lambda shape: pl.BlockSpec(shape, lambda i: (0,) * len(shape))
    return pl.pallas_call(
        _proj_kernel,
        grid=(1,),
        in_specs=[full((N, D)), full((1, D)), full(w_bf.shape)] + [full(t.shape) for t in tabs],
        out_specs=[full((N, RET_WIDTH))] * 7,
        out_shape=[jax.ShapeDtypeStruct((N, RET_WIDTH), F32)] * 7,
        compiler_params=_cparams(("arbitrary",), 40),
        name="proj",
    )(x, g.reshape(1, D), w_bf, *tabs)


def _proj_ret_kernel(x_ref, g_ref, w_ref, cos_ref, sa_ref, sb_ref, dec_ref, cd_ref, kd_ref, sd_ref, gr_ref,
                     ro_ref, s_out_ref, aqc_ref, ak_ref, akc_ref, av_ref, avc_ref, sbd_ref):
    j = pl.program_id(1)
    tm = x_ref.shape[1]
    C = RET_CHUNK

    @pl.when(j == 0)
    def _():
        sbd_ref[...] = jnp.zeros_like(sbd_ref)

    rows = pl.ds(pl.multiple_of(j * tm, tm), tm)
    rq, rk, rv, rg, aq, ak, av = _proj_cols(x_ref[0], g_ref, w_ref, cos_ref[rows, :], sa_ref[rows, :],
                                            sb_ref[rows, :])
    for c in range(ATT_CHUNKS):
        ls = slice(c * LANES, (c + 1) * LANES)
        aqc_ref[0, c] = aq[:, ls]
        akc_ref[0, c] = ak[:, ls]
        avc_ref[0, c] = av[:, ls]
    ak_ref[0] = ak
    av_ref[0] = av

    lane = lax.broadcasted_iota(jnp.int32, (C, LANES), 1)
    half = lane < HEAD_DIM
    same_head = (lax.broadcasted_iota(jnp.int32, (LANES, LANES), 0) // HEAD_DIM
                 == lax.broadcasted_iota(jnp.int32, (LANES, LANES), 1) // HEAD_DIM)
    inv_d = 1.0 / HEAD_DIM

    def half_mean(t):
        lo = jnp.sum(jnp.where(half, t, 0.0), axis=-1, keepdims=True)
        hi = jnp.sum(jnp.where(half, 0.0, t), axis=-1, keepdims=True)
        return jnp.where(half, lo, hi) * inv_d

    for c in range(RET_WIDTH // LANES):
        ls = slice(c * LANES, (c + 1) * LANES)
        S = sbd_ref[c]
        for cc in range(tm // C):
            rs = slice(cc * C, (cc + 1) * C)
            q = rq[rs, ls]
            qb = q.astype(BF16)
            kb = rk[rs, ls].astype(BF16)
            vb = rv[rs, ls].astype(BF16)
            q_pair = jnp.concatenate([jnp.where(half, q, 0.0), jnp.where(half, 0.0, q)], axis=0).astype(BF16)
            inner = _dot_nt(q_pair, kb) * dec_ref[c]
            o2 = _dot(inner.astype(BF16), vb)
            o = jnp.where(half, o2[:C], o2[C:]) + _dot(qb, S.astype(BF16)) * cd_ref[:, ls]
            kd = (kb.astype(F32) * kd_ref[:, ls]).astype(BF16)
            S = S * sd_ref[:, ls] + jnp.where(same_head, _dot_tn(kd, vb), 0.0)
            mu = half_mean(o)
            d = o - mu
            var = half_mean(d * d)
            on = d * lax.rsqrt(var + GN_EPS) * gr_ref[:, ls]
            gate = rg[rs, ls]
            ro_ref[0, rs, ls] = (on * (gate * jax.nn.sigmoid(gate))).astype(ro_ref.dtype)
        sbd_ref[c] = S
        s_out_ref[0, 2 * c] = S[:HEAD_DIM, :HEAD_DIM]
        s_out_ref[0, 2 * c + 1] = S[HEAD_DIM:, HEAD_DIM:]


def _proj_retention(x, g, w_bf, g_ret, tm):
    B, T, D = x.shape
    R, A, H, Dh = RET_WIDTH, ATT_WIDTH, N_RET_HEADS, HEAD_DIM
    tabs = _rope_tables(jnp.arange(T, dtype=jnp.int32))
    dec, cd, kd, sd = _retention_consts(RET_CHUNK)
    dec_pair = dec.reshape(H // 2, 2 * RET_CHUNK, RET_CHUNK)
    row = lambda width: pl.BlockSpec((1, tm, width), lambda b, j: (b, j, 0))
    chk = pl.BlockSpec((1, ATT_CHUNKS, tm, LANES), lambda b, j: (b, 0, j, 0))
    full = lambda shape: pl.BlockSpec(shape, lambda b, j: (0,) * len(shape))
    csds = jax.ShapeDtypeStruct((B, ATT_CHUNKS, T, LANES), F32)
    consts = [g.reshape(1, D), w_bf, *tabs, dec_pair, cd, kd, sd, g_ret.reshape(1, R)]
    return pl.pallas_call(
        _proj_ret_kernel,
        grid=(B, T // tm),
        in_specs=[row(D)] + [full(t.shape) for t in consts],
        out_specs=[row(R), pl.BlockSpec((1, H, Dh, Dh), lambda b, j: (b, 0, 0, 0)), chk, row(A), chk, row(A), chk],
        out_shape=[jax.ShapeDtypeStruct((B, T, R), BF16), jax.ShapeDtypeStruct((B, H, Dh, Dh), F32),
                   csds, jax.ShapeDtypeStruct((B, T, A), F32), csds, jax.ShapeDtypeStruct((B, T, A), F32), csds],
        scratch_shapes=[pltpu.VMEM((R // LANES, LANES, LANES), F32)],
        compiler_params=_cparams(("arbitrary", "arbitrary"), 56),
        name="proj_retention",
    )(x, *consts)


def _ret_kernel(q_ref, k_ref, v_ref, gate_ref, dec_ref, cd_ref, kd_ref, sd_ref, gr_ref, s0_ref, o_ref, s_ref):
    q = q_ref[0].astype(BF16)
    k = k_ref[0].astype(BF16)
    v = v_ref[0].astype(BF16)
    for h in range(N_RET_HEADS):
        sl = slice(h * HEAD_DIM, (h + 1) * HEAD_DIM)
        qh, kh, vh = q[:, sl], k[:, sl], v[:, sl]
        S = s0_ref[0, h]
        inner = _dot_nt(qh, kh) * dec_ref[h]
        o = _dot(inner.astype(BF16), vh) + _dot(qh, S.astype(BF16)) * cd_ref[:, sl]
        kd = (kh.astype(F32) * kd_ref[:, sl]).astype(BF16)
        s_ref[0, h] = S * sd_ref[:, sl] + _dot_tn(kd, vh)
        mu = jnp.mean(o, axis=-1, keepdims=True)
        var = jnp.mean(jnp.square(o - mu), axis=-1, keepdims=True)
        on = (o - mu) * lax.rsqrt(var + GN_EPS) * gr_ref[:, sl]
        gate = gate_ref[0, :, sl]
        o_ref[0, :, sl] = on * (gate * jax.nn.sigmoid(gate))


def _retention_step(rq, rk, rv, rg, g_ret, state):
    B, T, R = rq.shape
    H, Dh = N_RET_HEADS, HEAD_DIM
    consts = [*_retention_consts(T), g_ret.reshape(1, R)]
    row = pl.BlockSpec((1, T, R), lambda b: (b, 0, 0))
    full = lambda shape: pl.BlockSpec(shape, lambda b: (0,) * len(shape))
    st = pl.BlockSpec((1, H, Dh, Dh), lambda b: (b, 0, 0, 0))
    return pl.pallas_call(
        _ret_kernel,
        grid=(B,),
        in_specs=[row, row, row, row] + [full(t.shape) for t in consts] + [st],
        out_specs=[row, st],
        out_shape=[jax.ShapeDtypeStruct((B, T, R), F32), jax.ShapeDtypeStruct((B, H, Dh, Dh), F32)],
        compiler_params=_cparams(("arbitrary",), 32),
        name="retention",
    )(rq, rk, rv, rg, *consts, state)


def _split3_dot(c, e):
    c1 = c.astype(BF16)
    r1 = c - c1.astype(F32)
    c2 = r1.astype(BF16)
    c3 = (r1 - c2.astype(F32)).astype(BF16)
    return _dot(c1, e) + _dot(c2, e) + _dot(c3, e)


def _rows(ref, start, dil, c):
    idx = pl.ds(start, ATT_BLOCK) if dil == 1 else pl.ds(start, ATT_BLOCK, stride=dil)
    return ref[0, c, idx, :]


def _att_block(q_ref, k_ref, v_ref, bias_ref, dil, cur, prev, first):
    BLK, H = ATT_BLOCK, N_ATT_HEADS
    half = lax.broadcasted_iota(jnp.int32, (BLK, LANES), 1) < HEAD_DIM
    s_parts = []
    for c in range(ATT_CHUNKS):
        q = _rows(q_ref, cur, dil, c)
        q_pair = jnp.concatenate([jnp.where(half, q, 0.0), jnp.where(half, 0.0, q)], axis=0).astype(BF16)
        k = _rows(k_ref, cur, dil, c)
        if prev is not None:
            k = jnp.concatenate([_rows(k_ref, prev, dil, c), k], axis=0)
        s_parts.append(_dot_nt(q_pair, k.astype(BF16)))
    s = jnp.concatenate(s_parts, axis=0)
    if prev is not None:
        s = s + bias_ref[...]
        col = lax.broadcasted_iota(jnp.int32, s.shape, 1)
        s = jnp.where(jnp.logical_and(first, col < BLK), NEG_INF, s)
    else:
        s = s + bias_ref[:, BLK:]
    m = jnp.max(s, axis=-1, keepdims=True)
    p = jnp.exp(s - m)
    l = jnp.sum(p, axis=-1, keepdims=True)
    pb = p.astype(BF16)
    accs = []
    for c in range(ATT_CHUNKS):
        v = _rows(v_ref, cur, dil, c)
        if prev is not None:
            v = jnp.concatenate([_rows(v_ref, prev, dil, c), v], axis=0)
        o = _dot(pb[2 * c * BLK:(2 * c + 2) * BLK], v.astype(BF16))
        accs.append(jnp.where(half, o[:BLK], o[BLK:]))
    lane = lax.broadcasted_iota(jnp.int32, (BLK, LANES), 1)
    ml = jnp.zeros((BLK, LANES), F32)
    for h in range(H):
        ml = jnp.where(lane == h, m[h * BLK:(h + 1) * BLK], ml)
        ml = jnp.where(lane == H + h, l[h * BLK:(h + 1) * BLK], ml)
    return accs, ml


def _att_kernel(q_ref, k_ref, v_ref, b1_ref, b4_ref, b16_ref, e_ref, ao_ref, acc4, ml4, acc16, ml16):
    BLK, H = ATT_BLOCK, N_ATT_HEADS
    n = pl.program_id(1)
    nblk = pl.num_programs(1) // 2
    d4, d16 = DIL_PATTERNS[1][1], DIL_PATTERNS[2][1]

    @pl.when(n < nblk)
    def _():
        nb4 = nblk // d4
        r, j = n // nb4, n % nb4
        cur = r + j * (BLK * d4)
        prev = r + jnp.maximum(j - 1, 0) * (BLK * d4)
        accs, ml = _att_block(q_ref, k_ref, v_ref, b4_ref, d4, cur, prev, j == 0)
        for c in range(ATT_CHUNKS):
            acc4[c, pl.ds(cur, BLK, stride=d4), :] = accs[c]
        ml4[pl.ds(cur, BLK, stride=d4), :] = ml
        accs, ml = _att_block(q_ref, k_ref, v_ref, b16_ref, d16, n, None, None)
        for c in range(ATT_CHUNKS):
            acc16[c, pl.ds(n, BLK, stride=d16), :] = accs[c]
        ml16[pl.ds(n, BLK, stride=d16), :] = ml

    @pl.when(n >= nblk)
    def _():
        j = n - nblk
        cur = pl.multiple_of(j * BLK, BLK)
        prev = pl.multiple_of(jnp.maximum(j - 1, 0) * BLK, BLK)
        accs1, ml1 = _att_block(q_ref, k_ref, v_ref, b1_ref, 1, cur, prev, j == 0)
        rows = pl.ds(cur, BLK)
        mls = [ml1, ml4[rows, :], ml16[rows, :]]
        accs = [accs1, [acc4[c, rows, :] for c in range(ATT_CHUNKS)], [acc16[c, rows, :] for c in range(ATT_CHUNKS)]]
        lane = lax.broadcasted_iota(jnp.int32, (BLK, LANES), 1)
        m_all = jnp.maximum(jnp.maximum(mls[0], mls[1]), mls[2])
        ws = [jnp.exp(ml - m_all) for ml in mls]
        den = sum(w * pltpu.roll(ml, LANES - H, 1) for w, ml in zip(ws, mls))
        ao = [jnp.zeros((BLK, LANES), F32) for _ in range(ATT_CHUNKS)]
        for w, acc in zip(ws, accs):
            coef = _split3_dot(jnp.where(lane < H, w / den, 0.0), e_ref[...])
            for c in range(ATT_CHUNKS):
                ao[c] = ao[c] + coef[:, c * LANES:(c + 1) * LANES] * acc[c]
        ao_ref[0] = jnp.concatenate(ao, axis=-1).astype(ao_ref.dtype)


def _prompt_attention(aqc, akc, avc, rel_bias):
    B, CH, S, _ = aqc.shape
    A = ATT_WIDTH
    BLK = ATT_BLOCK
    nblk = S // BLK
    assert [d for _, d in DIL_PATTERNS] == [1, 4, 16] and nblk == 16
    b1, b4, b16 = _prompt_bias_tables(rel_bias)
    head_of_lane = jnp.arange(A)[None, :] // HEAD_DIM
    expand = (jnp.arange(LANES)[:, None] == head_of_lane).astype(BF16)
    seq = pl.BlockSpec((1, CH, S, LANES), lambda b, n: (b, 0, 0, 0))
    const = lambda shape: pl.BlockSpec(shape, lambda b, n: (0,) * len(shape), pipeline_mode=pl.Buffered(1))
    return pl.pallas_call(
        _att_kernel,
        grid=(B, 2 * nblk),
        in_specs=[seq, seq, seq, const(b1.shape), const(b4.shape), const(b16.shape), const(expand.shape)],
        out_specs=pl.BlockSpec((1, BLK, A), lambda b, n: (b, jnp.maximum(n - nblk, 0), 0)),
        out_shape=jax.ShapeDtypeStruct((B, S, A), BF16),
        scratch_shapes=[pltpu.VMEM((CH, S, LANES), F32), pltpu.VMEM((S, LANES), F32),
                        pltpu.VMEM((CH, S, LANES), F32), pltpu.VMEM((S, LANES), F32)],
        compiler_params=_cparams(("arbitrary", "arbitrary"), 56),
        name="prompt_att",
    )(aqc, akc, avc, b1, b4, b16, expand)


def _samp_att_kernel(lows, T, kc_ref, vc_ref, qn_ref, kn_ref, vn_ref, b1_ref, b4_ref, b16_ref,
                     ko_ref, vo_ref, ao_ref):
    L = kc_ref.shape[1]
    A = kc_ref.shape[2]
    H = N_ATT_HEADS
    kc, vc = kc_ref[0], vc_ref[0]
    kn, vn = kn_ref[0], vn_ref[0]
    ko_ref[0, :L - T, :] = kc[T:, :]
    ko_ref[0, L - T:, :] = kn
    vo_ref[0, :L - T, :] = vc[T:, :]
    vo_ref[0, L - T:, :] = vn
    pad = jnp.zeros((T, A), F32)
    kall = jnp.concatenate([kc, kn, pad], axis=0).astype(BF16)
    vall = jnp.concatenate([vc, vn, pad], axis=0).astype(BF16)
    HT = H * T
    row_head = lax.broadcasted_iota(jnp.int32, (HT, A), 0) // T
    lane_head = lax.broadcasted_iota(jnp.int32, (HT, A), 1) // HEAD_DIM
    diag = row_head == lane_head
    q_rows = jnp.where(diag, jnp.concatenate([qn_ref[0]] * H, axis=0), 0.0).astype(BF16)
    s_all = _dot_nt(q_rows, kall)
    stats = []
    for lo, b_ref in zip(lows, (b1_ref, b4_ref, b16_ref)):
        s = s_all[:, lo:] + b_ref[...]
        m = jnp.max(s, axis=-1, keepdims=True)
        p = jnp.exp(s - m)
        l = jnp.sum(p, axis=-1, keepdims=True)
        acc = _dot(p.astype(BF16), vall[lo:, :])
        stats.append((m, l, acc))
    m_all = jnp.maximum(jnp.maximum(stats[0][0], stats[1][0]), stats[2][0])
    num = jnp.zeros((HT, A), F32)
    den = jnp.zeros((HT, 1), F32)
    for m, l, acc in stats:
        w = jnp.exp(m - m_all)
        num = num + w * acc
        den = den + w * l
    comb = jnp.where(diag, num / den, 0.0)
    out = comb[0:T]
    for h in range(1, H):
        out = out + comb[h * T:(h + 1) * T]
    ao_ref[0] = out


def _sample_attention(aq, ak, av, cache_k, cache_v, rel_bias):
    B, T, A = aq.shape
    L = cache_k.shape[1]
    lows = tuple(L - win for win, _ in DIL_PATTERNS)
    assert all(lo >= 0 and lo % LANES == 0 for lo in lows)
    tabs = _sample_bias_tables(rel_bias, L, T)
    big = pl.BlockSpec((1, L, A), lambda b: (b, 0, 0))
    small = pl.BlockSpec((1, T, A), lambda b: (b, 0, 0))
    full = lambda shape: pl.BlockSpec(shape, lambda b: (0,) * len(shape))
    ko, vo, ao = pl.pallas_call(
        functools.partial(_samp_att_kernel, lows, T),
        grid=(B,),
        in_specs=[big, big, small, small, small] + [full(t.shape) for t in tabs],
        out_specs=[big, big, small],
        out_shape=[jax.ShapeDtypeStruct((B, L, A), F32), jax.ShapeDtypeStruct((B, L, A), F32),
                   jax.ShapeDtypeStruct((B, T, A), F32)],
        compiler_params=_cparams(("arbitrary",), 56),
        name="sample_att",
    )(cache_k, cache_v, aq, ak, av, *tabs)
    return ao, ko, vo


def _out_kernel(d_ff, ff_chunk, x_ref, ro_ref, ao_ref, wo_ref, wgu_ref, wd_ref, g_pm, g_pf, g_of, y_ref, act_ref):
    R = RET_WIDTH
    mix = _dot(ro_ref[...].astype(BF16), wo_ref[:R, :]) + _dot(ao_ref[...].astype(BF16), wo_ref[R:, :])
    x1 = x_ref[...] + _rms(mix, g_pm[...])
    h = _rms(x1, g_pf[...]).astype(BF16)
    for c in range(0, d_ff, ff_chunk):
        gate = _dot(h, wgu_ref[:, c:c + ff_chunk])
        up = _dot(h, wgu_ref[:, d_ff + c:d_ff + c + ff_chunk])
        act_ref[:, c:c + ff_chunk] = (gate * jax.nn.sigmoid(gate) * up).astype(BF16)
    f = _dot(act_ref[...], wd_ref[...])
    y_ref[...] = x1 + _rms(f, g_of[...])


def _out_block(x, ro, ao, wo_bf, wgu_bf, wd_bf, g_post_mix, g_pre_ffn, g_post_ffn, tm):
    N, D = x.shape
    d_ff = wd_bf.shape[0]
    row = lambda width: pl.BlockSpec((tm, width), lambda i: (i, 0))
    const = lambda shape: pl.BlockSpec(shape, lambda i: (0,) * len(shape), pipeline_mode=pl.Buffered(1))
    return pl.pallas_call(
        functools.partial(_out_kernel, d_ff, 256),
        grid=(N // tm,),
        in_specs=[row(D), row(RET_WIDTH), row(ATT_WIDTH), const(wo_bf.shape), const(wgu_bf.shape),
                  const(wd_bf.shape), const((1, D)), const((1, D)), const((1, D))],
        out_specs=row(D),
        out_shape=jax.ShapeDtypeStruct((N, D), F32),
        scratch_shapes=[pltpu.VMEM((tm, d_ff), BF16)],
        compiler_params=_cparams(("arbitrary",), 56),
        name="out_ffn",
    )(x, ro, ao, wo_bf, wgu_bf, wd_bf, g_post_mix.reshape(1, D), g_pre_ffn.reshape(1, D),
      g_post_ffn.reshape(1, D))


def kernel(x_prompt, x_sample, state_ret, cache_k_win, cache_v_win, rel_bias, w_in, g_ret, w_out,
           g_pre_mix, g_post_mix, g_pre_ffn, g_post_ffn, w_gu, w_down):
    depth = w_in.shape[0]
    assert depth == 1
    B, S, D = x_prompt.shape
    Bs, Ts, _ = x_sample.shape
    H, Dh = N_ATT_HEADS, HEAD_DIM
    l = 0
    w_in_bf = w_in[l].astype(BF16)
    wo_bf = w_out[l].astype(BF16)
    wgu_bf = w_gu[l].astype(BF16)
    wd_bf = w_down[l].astype(BF16)

    ro, s_p, aqc, ak, akc, av, avc = _proj_retention(x_prompt, g_pre_mix[l], w_in_bf, g_ret[l], 512)
    ao = _prompt_attention(aqc, akc, avc, rel_bias)
    y_p = _out_block(x_prompt.reshape(B * S, D), ro.reshape(B * S, RET_WIDTH), ao.reshape(B * S, ATT_WIDTH),
                     wo_bf, wgu_bf, wd_bf, g_post_mix[l], g_pre_ffn[l], g_post_ffn[l], 256)

    pos_s = PAST_LEN + jnp.arange(Ts, dtype=jnp.int32)
    N_s = Bs * Ts
    outs = _proj(x_sample.reshape(N_s, D), g_pre_mix[l], w_in_bf, jnp.tile(pos_s, Bs))
    srq, srk, srv, srg, saq, sak, sav = [t.reshape(Bs, Ts, RET_WIDTH) for t in outs]
    sro, s_s = _retention_step(srq, srk, srv, srg, g_ret[l], state_ret[l])
    L = cache_k_win.shape[2]
    sao, k_s, v_s = _sample_attention(saq, sak, sav, cache_k_win[l].reshape(Bs, L, ATT_WIDTH),
                                      cache_v_win[l].reshape(Bs, L, ATT_WIDTH), rel_bias)
    y_s = _out_block(x_sample.reshape(N_s, D), sro.reshape(N_s, RET_WIDTH), sao.reshape(N_s, ATT_WIDTH),
                     wo_bf, wgu_bf, wd_bf, g_post_mix[l], g_pre_ffn[l], g_post_ffn[l], N_s)

    return (y_p.reshape(B, S, D), y_s.reshape(Bs, Ts, D),
            s_p[None], ak.reshape(1, B, S, H, Dh), av.reshape(1, B, S, H, Dh),
            s_s[None], k_s.reshape(1, Bs, L, H, Dh), v_s.reshape(1, Bs, L, H, Dh))
```

```python
import functools
import math

import jax
import jax.numpy as jnp
from jax import lax
from jax.experimental import pallas as pl
from jax.experimental.pallas import tpu as pltpu

F32 = jnp.float32
BF16 = jnp.bfloat16

HEAD_DIM = 64
N_RET_HEADS = 8
N_ATT_HEADS = 8
RET_WIDTH = N_RET_HEADS * HEAD_DIM
ATT_WIDTH = N_ATT_HEADS * HEAD_DIM
RET_CHUNK = 128
DIL_PATTERNS = ((128, 1), (512, 4), (2048, 16))
ATT_BLOCK = 128
N_BUCKETS = 32
MAX_DISTANCE = 2048
ROPE_BASE = 10000.0
NORM_EPS = 1e-6
GN_EPS = 1e-5
PAST_LEN = 16384
LANES = 128
ATT_CHUNKS = ATT_WIDTH // LANES
NEG_INF = float("-inf")


def _cparams(sem, vmem_mb):
    return pltpu.CompilerParams(dimension_semantics=sem, vmem_limit_bytes=vmem_mb * 1024 * 1024)


def _rms(x, g):
    return x * lax.rsqrt(jnp.mean(x * x, axis=-1, keepdims=True) + NORM_EPS) * g


def _dot(a, b):
    return jnp.dot(a, b, preferred_element_type=F32)


def _dot_nt(a, b):
    return lax.dot_general(a, b, (((1,), (1,)), ((), ())), preferred_element_type=F32)


def _dot_tn(a, b):
    return lax.dot_general(a, b, (((0,), (0,)), ((), ())), preferred_element_type=F32)


def _rope_tables(pos):
    inv = ROPE_BASE ** (-jnp.arange(0, HEAD_DIM, 2, dtype=F32) / HEAD_DIM)
    ang = pos.astype(F32)[:, None] * inv[None, :]
    cos, sin = jnp.cos(ang), jnp.sin(ang)
    zero = jnp.zeros_like(sin)
    rep = LANES // HEAD_DIM
    cosf = jnp.tile(jnp.concatenate([cos, cos], axis=-1), (1, rep))
    sa = jnp.tile(jnp.concatenate([-sin, zero], axis=-1), (1, rep))
    sb = jnp.tile(jnp.concatenate([zero, sin], axis=-1), (1, rep))
    return cosf, sa, sb


def _retention_consts(C):
    log_g = jnp.log1p(-jnp.exp2(-5.0 - jnp.arange(N_RET_HEADS, dtype=F32)))
    i = jnp.arange(C, dtype=F32)
    rel = i[:, None] - i[None, :]
    decay = jnp.where(rel[None] >= 0, jnp.exp(jnp.maximum(rel, 0.0)[None] * log_g[:, None, None]), 0.0)
    cross = jnp.exp((i + 1.0)[:, None] * log_g[None, :])
    kdec = jnp.exp((C - 1.0 - i)[:, None] * log_g[None, :])
    sdec = jnp.exp(C * log_g)[None, :]
    expand = lambda t: jnp.repeat(t, HEAD_DIM, axis=-1)
    return decay, expand(cross), expand(kdec), expand(sdec)


def _t5_bucket(dist):
    max_exact = N_BUCKETS // 2
    d_f = jnp.maximum(dist, 1).astype(F32)
    large = max_exact + (jnp.log(d_f / max_exact) / math.log(MAX_DISTANCE / max_exact)
                         * (N_BUCKETS - max_exact)).astype(jnp.int32)
    large = jnp.minimum(large, N_BUCKETS - 1)
    return jnp.where(dist < max_exact, dist, large)


def _pattern_bias_rev(rel_bias, dil, nk):
    dist = jnp.arange(nk, -1, -1, dtype=jnp.int32) * dil
    return rel_bias[_t5_bucket(dist)].astype(F32).T


def _prompt_bias_tables(rel_bias):
    BLK, H = ATT_BLOCK, N_ATT_HEADS
    out = []
    for win, dil in DIL_PATTERNS:
        nk = win // dil
        assert nk == BLK
        period = 3 * BLK
        v = jnp.concatenate([_pattern_bias_rev(rel_bias, dil, nk),
                             jnp.full((H, period - nk - 1), NEG_INF, F32)], axis=1)
        flat = jnp.tile(v, (1, BLK))[:, :BLK * (period - 1)]
        skew = flat.reshape(H, BLK, period - 1)[:, :, :2 * BLK]
        out.append(skew.reshape(H * BLK, 2 * BLK))
    return out


def _sample_bias_tables(rel_bias, L, T):
    H = N_ATT_HEADS
    cache_part, new_part = [], []
    for win, dil in DIL_PATTERNS:
        nk = win // dil
        rev = _pattern_bias_rev(rel_bias, dil, nk)
        if dil > 1:
            gaps = jnp.full((H, nk + 1, dil - 1), NEG_INF, F32)
            rev = jnp.concatenate([rev[:, :, None], gaps], axis=2).reshape(H, (nk + 1) * dil)[:, :nk * dil + 1]
        pv = jnp.concatenate([jnp.full((H, T), NEG_INF, F32), rev, jnp.full((H, 2 * T), NEG_INF, F32)], axis=1)
        width = win + T
        rows = jnp.stack([pv[:, T - t:T - t + width] for t in range(T)], axis=1)
        cache_part.append(rows[:, :, :win].reshape(H * T, win))
        new = jnp.concatenate([jnp.full((H, T, LANES - T), NEG_INF, F32), rows[:, :, win:]], axis=2)
        new_part.append(new.reshape(H * T, LANES))
    return cache_part + new_part


def _proj_cols(x, g_ref, w_ref, cos, sa, sb):
    h = _rms(x, g_ref[...]).astype(BF16)
    rep = RET_WIDTH // LANES
    cosf = jnp.concatenate([cos] * rep, axis=-1)
    saf = jnp.concatenate([sa] * rep, axis=-1)
    sbf = jnp.concatenate([sb] * rep, axis=-1)
    half = HEAD_DIM // 2

    def col(c, width):
        return _dot(h, w_ref[:, c:c + width])

    def rope(z):
        return z * cosf + pltpu.roll(z, RET_WIDTH - half, 1) * saf + pltpu.roll(z, half, 1) * sbf

    R, A = RET_WIDTH, ATT_WIDTH
    scale = HEAD_DIM ** -0.5
    rq = rope(col(0, R))
    rk = rope(col(R, R)) * scale
    rv = col(2 * R, R)
    rg = col(3 * R, R)
    aq = col(4 * R, A) * scale
    ak = col(4 * R + A, A)
    av = col(4 * R + 2 * A, A)
    return rq, rk, rv, rg, aq, ak, av


def _proj_kernel(x_ref, g_ref, w_ref, cos_ref, sa_ref, sb_ref, *outs):
    cols = _proj_cols(x_ref[...], g_ref, w_ref, cos_ref[...], sa_ref[...], sb_ref[...])
    for ref, z in zip(outs, cols):
        ref[...] = z


def _proj(x, g, w_bf, pos):
    N, D = x.shape
    tabs = _rope_tables(pos)
    full = lambda shape: pl.BlockSpec(shape, lambda i: (0,) * len(shape))
    return pl.pallas_call(
        _proj_kernel,
        grid=(1,),
        in_specs=[full((N, D)), full((1, D)), full(w_bf.shape)] + [full(t.shape) for t in tabs],
        out_specs=[full((N, RET_WIDTH))] * 7,
        out_shape=[jax.ShapeDtypeStruct((N, RET_WIDTH), F32)] * 7,
        compiler_params=_cparams(("arbitrary",), 40),
        name="proj",
    )(x, g.reshape(1, D), w_bf, *tabs)


def _proj_ret_kernel(x_ref, g_ref, w_ref, cos_ref, sa_ref, sb_ref, dec_ref, cd_ref, kd_ref, sd_ref, gr_ref,
                     ro_ref, s_out_ref, aqc_ref, ak_ref, akc_ref, av_ref, avc_ref, sbd_ref):
    j = pl.program_id(1)
    tm = x_ref.shape[1]
    C = RET_CHUNK

    @pl.when(j == 0)
    def _():
        sbd_ref[...] = jnp.zeros_like(sbd_ref)

    rows = pl.ds(pl.multiple_of(j * tm, tm), tm)
    rq, rk, rv, rg, aq, ak, av = _proj_cols(x_ref[0], g_ref, w_ref, cos_ref[rows, :], sa_ref[rows, :],
                                            sb_ref[rows, :])
    for c in range(ATT_CHUNKS):
        ls = slice(c * LANES, (c + 1) * LANES)
        aqc_ref[0, c] = aq[:, ls]
        akc_ref[0, c] = ak[:, ls]
        avc_ref[0, c] = av[:, ls]
    ak_ref[0] = ak
    av_ref[0] = av

    lane = lax.broadcasted_iota(jnp.int32, (C, LANES), 1)
    half = lane < HEAD_DIM
    same_head = (lax.broadcasted_iota(jnp.int32, (LANES, LANES), 0) // HEAD_DIM
                 == lax.broadcasted_iota(jnp.int32, (LANES, LANES), 1) // HEAD_DIM)
    inv_d = 1.0 / HEAD_DIM

    def half_mean(t):
        lo = jnp.sum(jnp.where(half, t, 0.0), axis=-1, keepdims=True)
        hi = jnp.sum(jnp.where(half, 0.0, t), axis=-1, keepdims=True)
        return jnp.where(half, lo, hi) * inv_d

    for c in range(RET_WIDTH // LANES):
        ls = slice(c * LANES, (c + 1) * LANES)
        S = sbd_ref[c]
        for cc in range(tm // C):
            rs = slice(cc * C, (cc + 1) * C)
            q = rq[rs, ls]
            qb = q.astype(BF16)
            kb = rk[rs, ls].astype(BF16)
            vb = rv[rs, ls].astype(BF16)
            q_pair = jnp.concatenate([jnp.where(half, q, 0.0), jnp.where(half, 0.0, q)], axis=0).astype(BF16)
            inner = _dot_nt(q_pair, kb) * dec_ref[c]
            o2 = _dot(inner.astype(BF16), vb)
            o = jnp.where(half, o2[:C], o2[C:]) + _dot(qb, S.astype(BF16)) * cd_ref[:, ls]
            kd = (kb.astype(F32) * kd_ref[:, ls]).astype(BF16)
            S = S * sd_ref[:, ls] + jnp.where(same_head, _dot_tn(kd, vb), 0.0)
            mu = half_mean(o)
            d = o - mu
            var = half_mean(d * d)
            on = d * lax.rsqrt(var + GN_EPS) * gr_ref[:, ls]
            gate = rg[rs, ls]
            ro_ref[0, rs, ls] = (on * (gate * jax.nn.sigmoid(gate))).astype(ro_ref.dtype)
        sbd_ref[c] = S
        s_out_ref[0, 2 * c] = S[:HEAD_DIM, :HEAD_DIM]
        s_out_ref[0, 2 * c + 1] = S[HEAD_DIM:, HEAD_DIM:]


def _proj_retention(x, g, w_bf, g_ret, tm):
    B, T, D = x.shape
    R, A, H, Dh = RET_WIDTH, ATT_WIDTH, N_RET_HEADS, HEAD_DIM
    tabs = _rope_tables(jnp.arange(T, dtype=jnp.int32))
    dec, cd, kd, sd = _retention_consts(RET_CHUNK)
    dec_pair = dec.reshape(H // 2, 2 * RET_CHUNK, RET_CHUNK)
    row = lambda width: pl.BlockSpec((1, tm, width), lambda b, j: (b, j, 0))
    chk = pl.BlockSpec((1, ATT_CHUNKS, tm, LANES), lambda b, j: (b, 0, j, 0))
    full = lambda shape: pl.BlockSpec(shape, lambda b, j: (0,) * len(shape))
    csds = jax.ShapeDtypeStruct((B, ATT_CHUNKS, T, LANES), F32)
    consts = [g.reshape(1, D), w_bf, *tabs, dec_pair, cd, kd, sd, g_ret.reshape(1, R)]
    return pl.pallas_call(
        _proj_ret_kernel,
        grid=(B, T // tm),
        in_specs=[row(D)] + [full(t.shape) for t in consts],
        out_specs=[row(R), pl.BlockSpec((1, H, Dh, Dh), lambda b, j: (b, 0, 0, 0)), chk, row(A), chk, row(A), chk],
        out_shape=[jax.ShapeDtypeStruct((B, T, R), BF16), jax.ShapeDtypeStruct((B, H, Dh, Dh), F32),
                   csds, jax.ShapeDtypeStruct((B, T, A), F32), csds, jax.ShapeDtypeStruct((B, T, A), F32), csds],
        scratch_shapes=[pltpu.VMEM((R // LANES, LANES, LANES), F32)],
        compiler_params=_cparams(("arbitrary", "arbitrary"), 56),
        name="proj_retention",
    )(x, *consts)


def _ret_kernel(q_ref, k_ref, v_ref, gate_ref, dec_ref, cd_ref, kd_ref, sd_ref, gr_ref, s0_ref, o_ref, s_ref):
    q = q_ref[0].astype(BF16)
    k = k_ref[0].astype(BF16)
    v = v_ref[0].astype(BF16)
    for h in range(N_RET_HEADS):
        sl = slice(h * HEAD_DIM, (h + 1) * HEAD_DIM)
        qh, kh, vh = q[:, sl], k[:, sl], v[:, sl]
        S = s0_ref[0, h]
        inner = _dot_nt(qh, kh) * dec_ref[h]
        o = _dot(inner.astype(BF16), vh) + _dot(qh, S.astype(BF16)) * cd_ref[:, sl]
        kd = (kh.astype(F32) * kd_ref[:, sl]).astype(BF16)
        s_ref[0, h] = S * sd_ref[:, sl] + _dot_tn(kd, vh)
        mu = jnp.mean(o, axis=-1, keepdims=True)
        var = jnp.mean(jnp.square(o - mu), axis=-1, keepdims=True)
        on = (o - mu) * lax.rsqrt(var + GN_EPS) * gr_ref[:, sl]
        gate = gate_ref[0, :, sl]
        o_ref[0, :, sl] = on * (gate * jax.nn.sigmoid(gate))


def _retention_step(rq, rk, rv, rg, g_ret, state):
    B, T, R = rq.shape
    H, Dh = N_RET_HEADS, HEAD_DIM
    consts = [*_retention_consts(T), g_ret.reshape(1, R)]
    row = pl.BlockSpec((1, T, R), lambda b: (b, 0, 0))
    full = lambda shape: pl.BlockSpec(shape, lambda b: (0,) * len(shape))
    st = pl.BlockSpec((1, H, Dh, Dh), lambda b: (b, 0, 0, 0))
    return pl.pallas_call(
        _ret_kernel,
        grid=(B,),
        in_specs=[row, row, row, row] + [full(t.shape) for t in consts] + [st],
        out_specs=[row, st],
        out_shape=[jax.ShapeDtypeStruct((B, T, R), F32), jax.ShapeDtypeStruct((B, H, Dh, Dh), F32)],
        compiler_params=_cparams(("arbitrary",), 32),
        name="retention",
    )(rq, rk, rv, rg, *consts, state)


def _split3_dot(c, e):
    c1 = c.astype(BF16)
    r1 = c - c1.astype(F32)
    c2 = r1.astype(BF16)
    c3 = (r1 - c2.astype(F32)).astype(BF16)
    return _dot(c1, e) + _dot(c2, e) + _dot(c3, e)


def _rows(ref, start, dil, c):
    idx = pl.ds(start, ATT_BLOCK) if dil == 1 else pl.ds(start, ATT_BLOCK, stride=dil)
    return ref[0, c, idx, :]


def _att_block(q_ref, k_ref, v_ref, bias_ref, dil, cur, prev, first):
    BLK, H = ATT_BLOCK, N_ATT_HEADS
    half = lax.broadcasted_iota(jnp.int32, (BLK, LANES), 1) < HEAD_DIM
    s_parts = []
    for c in range(ATT_CHUNKS):
        q = _rows(q_ref, cur, dil, c)
        q_pair = jnp.concatenate([jnp.where(half, q, 0.0), jnp.where(half, 0.0, q)], axis=0).astype(BF16)
        k = _rows(k_ref, cur, dil, c)
        if prev is not None:
            k = jnp.concatenate([_rows(k_ref, prev, dil, c), k], axis=0)
        s_parts.append(_dot_nt(q_pair, k.astype(BF16)))
    s = jnp.concatenate(s_parts, axis=0)
    if prev is not None:
        s = s + bias_ref[...]
        col = lax.broadcasted_iota(jnp.int32, s.shape, 1)
        s = jnp.where(jnp.logical_and(first, col < BLK), NEG_INF, s)
    else:
        s = s + bias_ref[:, BLK:]
    m = jnp.max(s, axis=-1, keepdims=True)
    p = jnp.exp(s - m)
    l = jnp.sum(p, axis=-1, keepdims=True)
    pb = p.astype(BF16)
    accs = []
    for c in range(ATT_CHUNKS):
        v = _rows(v_ref, cur, dil, c)
        if prev is not None:
            v = jnp.concatenate([_rows(v_ref, prev, dil, c), v], axis=0)
        o = _dot(pb[2 * c * BLK:(2 * c + 2) * BLK], v.astype(BF16))
        accs.append(jnp.where(half, o[:BLK], o[BLK:]))
    lane = lax.broadcasted_iota(jnp.int32, (BLK, LANES), 1)
    ml = jnp.zeros((BLK, LANES), F32)
    for h in range(H):
        ml = jnp.where(lane == h, m[h * BLK:(h + 1) * BLK], ml)
        ml = jnp.where(lane == H + h, l[h * BLK:(h + 1) * BLK], ml)
    return accs, ml


def _att_kernel(q_ref, k_ref, v_ref, b1_ref, b4_ref, b16_ref, e_ref, ao_ref, acc4, ml4, acc16, ml16):
    BLK, H = ATT_BLOCK, N_ATT_HEADS
    n = pl.program_id(1)
    nblk = pl.num_programs(1) // 2
    d4, d16 = DIL_PATTERNS[1][1], DIL_PATTERNS[2][1]

    @pl.when(n < nblk)
    def _():
        nb4 = nblk // d4
        r, j = n // nb4, n % nb4
        cur = r + j * (BLK * d4)
        prev = r + jnp.maximum(j - 1, 0) * (BLK * d4)
        accs, ml = _att_block(q_ref, k_ref, v_ref, b4_ref, d4, cur, prev, j == 0)
        for c in range(ATT_CHUNKS):
            acc4[c, pl.ds(cur, BLK, stride=d4), :] = accs[c]
        ml4[pl.ds(cur, BLK, stride=d4), :] = ml
        accs, ml = _att_block(q_ref, k_ref, v_ref, b16_ref, d16, n, None, None)
        for c in range(ATT_CHUNKS):
            acc16[c, pl.ds(n, BLK, stride=d16), :] = accs[c]
        ml16[pl.ds(n, BLK, stride=d16), :] = ml

    @pl.when(n >= nblk)
    def _():
        j = n - nblk
        cur = pl.multiple_of(j * BLK, BLK)
        prev = pl.multiple_of(jnp.maximum(j - 1, 0) * BLK, BLK)
        accs1, ml1 = _att_block(q_ref, k_ref, v_ref, b1_ref, 1, cur, prev, j == 0)
        rows = pl.ds(cur, BLK)
        mls = [ml1, ml4[rows, :], ml16[rows, :]]
        accs = [accs1, [acc4[c, rows, :] for c in range(ATT_CHUNKS)], [acc16[c, rows, :] for c in range(ATT_CHUNKS)]]
        lane = lax.broadcasted_iota(jnp.int32, (BLK, LANES), 1)
        m_all = jnp.maximum(jnp.maximum(mls[0], mls[1]), mls[2])
        ws = [jnp.exp(ml - m_all) for ml in mls]
        den = sum(w * pltpu.roll(ml, LANES - H, 1) for w, ml in zip(ws, mls))
        ao = [jnp.zeros((BLK, LANES), F32) for _ in range(ATT_CHUNKS)]
        for w, acc in zip(ws, accs):
            coef = _split3_dot(jnp.where(lane < H, w / den, 0.0), e_ref[...])
            for c in range(ATT_CHUNKS):
                ao[c] = ao[c] + coef[:, c * LANES:(c + 1) * LANES] * acc[c]
        ao_ref[0] = jnp.concatenate(ao, axis=-1).astype(ao_ref.dtype)


def _prompt_attention(aqc, akc, avc, rel_bias):
    B, CH, S, _ = aqc.shape
    A = ATT_WIDTH
    BLK = ATT_BLOCK
    nblk = S // BLK
    assert [d for _, d in DIL_PATTERNS] == [1, 4, 16] and nblk == 16
    b1, b4, b16 = _prompt_bias_tables(rel_bias)
    head_of_lane = jnp.arange(A)[None, :] // HEAD_DIM
    expand = (jnp.arange(LANES)[:, None] == head_of_lane).astype(BF16)
    seq = pl.BlockSpec((1, CH, S, LANES), lambda b, n: (b, 0, 0, 0))
    const = lambda shape: pl.BlockSpec(shape, lambda b, n: (0,) * len(shape), pipeline_mode=pl.Buffered(1))
    return pl.pallas_call(
        _att_kernel,
        grid=(B, 2 * nblk),
        in_specs=[seq, seq, seq, const(b1.shape), const(b4.shape), const(b16.shape), const(expand.shape)],
        out_specs=pl.BlockSpec((1, BLK, A), lambda b, n: (b, jnp.maximum(n - nblk, 0), 0)),
        out_shape=jax.ShapeDtypeStruct((B, S, A), BF16),
        scratch_shapes=[pltpu.VMEM((CH, S, LANES), F32), pltpu.VMEM((S, LANES), F32),
                        pltpu.VMEM((CH, S, LANES), F32), pltpu.VMEM((S, LANES), F32)],
        compiler_params=_cparams(("arbitrary", "arbitrary"), 56),
        name="prompt_att",
    )(aqc, akc, avc, b1, b4, b16, expand)


def _samp_att_kernel(lows, T, kT_ref, vT_ref, qn_ref, kn_ref, vn_ref, c1_ref, c4_ref, c16_ref,
                     n1_ref, n4_ref, n16_ref, koT_ref, voT_ref, ao_ref):
    H, Dh, L = kT_ref.shape[1:]
    A = H * Dh
    zpad = jnp.zeros((LANES - T, A), F32)
    tail_lane = lax.broadcasted_iota(jnp.int32, (A, LANES), 1) >= LANES - T

    def shift_in(xT_ref, new_ref, out_ref):
        xT = xT_ref[0].reshape(A, L)
        new_p = jnp.concatenate([zpad, new_ref[0]], axis=0)
        rolled = pltpu.roll(xT, L - T, 1)
        tail = jnp.where(tail_lane, new_p.T, rolled[:, L - LANES:])
        out_ref[0] = jnp.concatenate([rolled[:, :L - LANES], tail], axis=1).reshape(H, Dh, L)
        return xT.astype(BF16), new_p.astype(BF16)

    kT, kn = shift_in(kT_ref, kn_ref, koT_ref)
    vT, vn = shift_in(vT_ref, vn_ref, voT_ref)
    HT = H * T
    row_head = lax.broadcasted_iota(jnp.int32, (HT, A), 0) // T
    lane_head = lax.broadcasted_iota(jnp.int32, (HT, A), 1) // HEAD_DIM
    diag = row_head == lane_head
    q_rows = jnp.where(diag, jnp.concatenate([qn_ref[0]] * H, axis=0), 0.0).astype(BF16)
    s_cache = _dot(q_rows, kT)
    s_new = _dot_nt(q_rows, kn)
    stats = []
    for lo, c_ref, n_ref in zip(lows, (c1_ref, c4_ref, c16_ref), (n1_ref, n4_ref, n16_ref)):
        sc = s_cache[:, lo:] + c_ref[...]
        sn = s_new + n_ref[...]
        m = jnp.maximum(jnp.max(sc, axis=-1, keepdims=True), jnp.max(sn, axis=-1, keepdims=True))
        pc = jnp.exp(sc - m)
        pn = jnp.exp(sn - m)
        l = jnp.sum(pc, axis=-1, keepdims=True) + jnp.sum(pn, axis=-1, keepdims=True)
        acc = _dot_nt(pc.astype(BF16), vT[:, lo:]) + _dot(pn.astype(BF16), vn)
        stats.append((m, l, acc))
    m_all = jnp.maximum(jnp.maximum(stats[0][0], stats[1][0]), stats[2][0])
    num = jnp.zeros((HT, A), F32)
    den = jnp.zeros((HT, 1), F32)
    for m, l, acc in stats:
        w = jnp.exp(m - m_all)
        num = num + w * acc
        den = den + w * l
    comb = jnp.where(diag, num / den, 0.0)
    out = comb[0:T]
    for h in range(1, H):
        out = out + comb[h * T:(h + 1) * T]
    ao_ref[0] = out


def _sample_attention(aq, ak, av, cache_kT, cache_vT, rel_bias):
    B, T, A = aq.shape
    _, H, Dh, L = cache_kT.shape
    lows = tuple(L - win for win, _ in DIL_PATTERNS)
    assert all(lo >= 0 and lo % LANES == 0 for lo in lows) and T <= LANES
    tabs = _sample_bias_tables(rel_bias, L, T)
    big = pl.BlockSpec((1, H, Dh, L), lambda b: (b, 0, 0, 0))
    small = pl.BlockSpec((1, T, A), lambda b: (b, 0, 0))
    full = lambda shape: pl.BlockSpec(shape, lambda b: (0,) * len(shape))
    win_sds = jax.ShapeDtypeStruct((B, H, Dh, L), F32)
    ko, vo, ao = pl.pallas_call(
        functools.partial(_samp_att_kernel, lows, T),
        grid=(B,),
        in_specs=[big, big, small, small, small] + [full(t.shape) for t in tabs],
        out_specs=[big, big, small],
        out_shape=[win_sds, win_sds, jax.ShapeDtypeStruct((B, T, A), F32)],
        compiler_params=_cparams(("arbitrary",), 56),
        name="sample_att",
    )(cache_kT, cache_vT, aq, ak, av, *tabs)
    return ao, ko, vo


def _out_kernel(d_ff, ff_chunk, x_ref, ro_ref, ao_ref, wo_ref, wgu_ref, wd_ref, g_pm, g_pf, g_of, y_ref, act_ref):
    R = RET_WIDTH
    mix = _dot(ro_ref[...].astype(BF16), wo_ref[:R, :]) + _dot(ao_ref[...].astype(BF16), wo_ref[R:, :])
    x1 = x_ref[...] + _rms(mix, g_pm[...])
    h = _rms(x1, g_pf[...]).astype(BF16)
    for c in range(0, d_ff, ff_chunk):
        gate = _dot(h, wgu_ref[:, c:c + ff_chunk])
        up = _dot(h, wgu_ref[:, d_ff + c:d_ff + c + ff_chunk])
        act_ref[:, c:c + ff_chunk] = (gate * jax.nn.sigmoid(gate) * up).astype(BF16)
    f = _dot(act_ref[...], wd_ref[...])
    y_ref[...] = x1 + _rms(f, g_of[...])


def _out_block(x, ro, ao, wo_bf, wgu_bf, wd_bf, g_post_mix, g_pre_ffn, g_post_ffn, tm):
    N, D = x.shape
    d_ff = wd_bf.shape[0]
    row = lambda width: pl.BlockSpec((tm, width), lambda i: (i, 0))
    const = lambda shape: pl.BlockSpec(shape, lambda i: (0,) * len(shape), pipeline_mode=pl.Buffered(1))
    return pl.pallas_call(
        functools.partial(_out_kernel, d_ff, 256),
        grid=(N // tm,),
        in_specs=[row(D), row(RET_WIDTH), row(ATT_WIDTH), const(wo_bf.shape), const(wgu_bf.shape),
                  const(wd_bf.shape), const((1, D)), const((1, D)), const((1, D))],
        out_specs=row(D),
        out_shape=jax.ShapeDtypeStruct((N, D), F32),
        scratch_shapes=[pltpu.VMEM((tm, d_ff), BF16)],
        compiler_params=_cparams(("arbitrary",), 56),
        name="out_ffn",
    )(x, ro, ao, wo_bf, wgu_bf, wd_bf, g_post_mix.reshape(1, D), g_pre_ffn.reshape(1, D),
      g_post_ffn.reshape(1, D))


def kernel(x_prompt, x_sample, state_ret, cache_k_win, cache_v_win, rel_bias, w_in, g_ret, w_out,
           g_pre_mix, g_post_mix, g_pre_ffn, g_post_ffn, w_gu, w_down):
    depth = w_in.shape[0]
    assert depth == 1
    B, S, D = x_prompt.shape
    Bs, Ts, _ = x_sample.shape
    H, Dh = N_ATT_HEADS, HEAD_DIM
    l = 0
    w_in_bf = w_in[l].astype(BF16)
    wo_bf = w_out[l].astype(BF16)
    wgu_bf = w_gu[l].astype(BF16)
    wd_bf = w_down[l].astype(BF16)

    ro, s_p, aqc, ak, akc, av, avc = _proj_retention(x_prompt, g_pre_mix[l], w_in_bf, g_ret[l], 512)
    ao = _prompt_attention(aqc, akc, avc, rel_bias)
    y_p = _out_block(x_prompt.reshape(B * S, D), ro.reshape(B * S, RET_WIDTH), ao.reshape(B * S, ATT_WIDTH),
                     wo_bf, wgu_bf, wd_bf, g_post_mix[l], g_pre_ffn[l], g_post_ffn[l], 256)

    pos_s = PAST_LEN + jnp.arange(Ts, dtype=jnp.int32)
    N_s = Bs * Ts
    outs = _proj(x_sample.reshape(N_s, D), g_pre_mix[l], w_in_bf, jnp.tile(pos_s, Bs))
    srq, srk, srv, srg, saq, sak, sav = [t.reshape(Bs, Ts, RET_WIDTH) for t in outs]
    sro, s_s = _retention_step(srq, srk, srv, srg, g_ret[l], state_ret[l])
    to_t = lambda t: jnp.transpose(t, (0, 2, 3, 1))
    from_t = lambda t: jnp.transpose(t, (0, 3, 1, 2))
    sao, k_sT, v_sT = _sample_attention(saq, sak, sav, to_t(cache_k_win[l]), to_t(cache_v_win[l]), rel_bias)
    k_s, v_s = from_t(k_sT), from_t(v_sT)
    y_s = _out_block(x_sample.reshape(N_s, D), sro.reshape(N_s, RET_WIDTH), sao.reshape(N_s, ATT_WIDTH),
                     wo_bf, wgu_bf, wd_bf, g_post_mix[l], g_pre_ffn[l], g_post_ffn[l], N_s)

    return (y_p.reshape(B, S, D), y_s.reshape(Bs, Ts, D),
            s_p[None], ak.reshape(1, B, S, H, Dh), av.reshape(1, B, S, H, Dh),
            s_s[None], k_s[None], v_s[None])
```

```python
import functools
import math

import jax
import jax.numpy as jnp
from jax import lax
from jax.experimental import pallas as pl
from jax.experimental.pallas import tpu as pltpu

F32 = jnp.float32
BF16 = jnp.bfloat16

HEAD_DIM = 64
N_RET_HEADS = 8
N_ATT_HEADS = 8
RET_WIDTH = N_RET_HEADS * HEAD_DIM
ATT_WIDTH = N_ATT_HEADS * HEAD_DIM
RET_CHUNK = 128
DIL_PATTERNS = ((128, 1), (512, 4), (2048, 16))
ATT_BLOCK = 128
N_BUCKETS = 32
MAX_DISTANCE = 2048
ROPE_BASE = 10000.0
NORM_EPS = 1e-6
GN_EPS = 1e-5
PAST_LEN = 16384
LANES = 128
ATT_CHUNKS = ATT_WIDTH // LANES
NEG_INF = float("-inf")


def _cparams(sem, vmem_mb):
    return pltpu.CompilerParams(dimension_semantics=sem, vmem_limit_bytes=vmem_mb * 1024 * 1024)


def _rms(x, g):
    return x * lax.rsqrt(jnp.mean(x * x, axis=-1, keepdims=True) + NORM_EPS) * g


def _dot(a, b):
    return jnp.dot(a, b, preferred_element_type=F32)


def _dot_nt(a, b):
    return lax.dot_general(a, b, (((1,), (1,)), ((), ())), preferred_element_type=F32)


def _dot_tn(a, b):
    return lax.dot_general(a, b, (((0,), (0,)), ((), ())), preferred_element_type=F32)


def _rope_tables(pos):
    inv = ROPE_BASE ** (-jnp.arange(0, HEAD_DIM, 2, dtype=F32) / HEAD_DIM)
    ang = pos.astype(F32)[:, None] * inv[None, :]
    cos, sin = jnp.cos(ang), jnp.sin(ang)
    zero = jnp.zeros_like(sin)
    rep = LANES // HEAD_DIM
    cosf = jnp.tile(jnp.concatenate([cos, cos], axis=-1), (1, rep))
    sa = jnp.tile(jnp.concatenate([-sin, zero], axis=-1), (1, rep))
    sb = jnp.tile(jnp.concatenate([zero, sin], axis=-1), (1, rep))
    return cosf, sa, sb


def _retention_consts(C):
    log_g = jnp.log1p(-jnp.exp2(-5.0 - jnp.arange(N_RET_HEADS, dtype=F32)))
    i = jnp.arange(C, dtype=F32)
    rel = i[:, None] - i[None, :]
    decay = jnp.where(rel[None] >= 0, jnp.exp(jnp.maximum(rel, 0.0)[None] * log_g[:, None, None]), 0.0)
    cross = jnp.exp((i + 1.0)[:, None] * log_g[None, :])
    kdec = jnp.exp((C - 1.0 - i)[:, None] * log_g[None, :])
    sdec = jnp.exp(C * log_g)[None, :]
    expand = lambda t: jnp.repeat(t, HEAD_DIM, axis=-1)
    return decay, expand(cross), expand(kdec), expand(sdec)


def _t5_bucket(dist):
    max_exact = N_BUCKETS // 2
    d_f = jnp.maximum(dist, 1).astype(F32)
    large = max_exact + (jnp.log(d_f / max_exact) / math.log(MAX_DISTANCE / max_exact)
                         * (N_BUCKETS - max_exact)).astype(jnp.int32)
    large = jnp.minimum(large, N_BUCKETS - 1)
    return jnp.where(dist < max_exact, dist, large)


def _pattern_bias_rev(rel_bias, dil, nk):
    dist = jnp.arange(nk, -1, -1, dtype=jnp.int32) * dil
    return rel_bias[_t5_bucket(dist)].astype(F32).T


def _prompt_bias_tables(rel_bias):
    BLK, H = ATT_BLOCK, N_ATT_HEADS
    out = []
    for win, dil in DIL_PATTERNS:
        nk = win // dil
        assert nk == BLK
        period = 3 * BLK
        v = jnp.concatenate([_pattern_bias_rev(rel_bias, dil, nk),
                             jnp.full((H, period - nk - 1), NEG_INF, F32)], axis=1)
        flat = jnp.tile(v, (1, BLK))[:, :BLK * (period - 1)]
        skew = flat.reshape(H, BLK, period - 1)[:, :, :2 * BLK]
        out.append(skew.reshape(H * BLK, 2 * BLK))
    return out


def _sample_bias_tables(rel_bias, L, T):
    H = N_ATT_HEADS
    cache_part, new_part = [], []
    for win, dil in DIL_PATTERNS:
        nk = win // dil
        rev = _pattern_bias_rev(rel_bias, dil, nk)
        if dil > 1:
            gaps = jnp.full((H, nk + 1, dil - 1), NEG_INF, F32)
            rev = jnp.concatenate([rev[:, :, None], gaps], axis=2).reshape(H, (nk + 1) * dil)[:, :nk * dil + 1]
        pv = jnp.concatenate([jnp.full((H, T), NEG_INF, F32), rev, jnp.full((H, 2 * T), NEG_INF, F32)], axis=1)
        width = win + T
        rows = jnp.stack([pv[:, T - t:T - t + width] for t in range(T)], axis=1)
        cache_part.append(rows[:, :, :win].reshape(H * T, win))
        new = jnp.concatenate([jnp.full((H, T, LANES - T), NEG_INF, F32), rows[:, :, win:]], axis=2)
        new_part.append(new.reshape(H * T, LANES))
    return cache_part + new_part


def _proj_cols(x, g_ref, w_ref, cos, sa, sb):
    h = _rms(x, g_ref[...]).astype(BF16)
    rep = RET_WIDTH // LANES
    cosf = jnp.concatenate([cos] * rep, axis=-1)
    saf = jnp.concatenate([sa] * rep, axis=-1)
    sbf = jnp.concatenate([sb] * rep, axis=-1)
    half = HEAD_DIM // 2

    def col(c, width):
        return _dot(h, w_ref[:, c:c + width])

    def rope(z):
        return z * cosf + pltpu.roll(z, RET_WIDTH - half, 1) * saf + pltpu.roll(z, half, 1) * sbf

    R, A = RET_WIDTH, ATT_WIDTH
    scale = HEAD_DIM ** -0.5
    rq = rope(col(0, R))
    rk = rope(col(R, R)) * scale
    rv = col(2 * R, R)
    rg = col(3 * R, R)
    aq = col(4 * R, A) * scale
    ak = col(4 * R + A, A)
    av = col(4 * R + 2 * A, A)
    return rq, rk, rv, rg, aq, ak, av


def _proj_kernel(x_ref, g_ref, w_ref, cos_ref, sa_ref, sb_ref, *outs):
    cols = _proj_cols(x_ref[...], g_ref, w_ref, cos_ref[...], sa_ref[...], sb_ref[...])
    for ref, z in zip(outs, cols):
        ref[...] = z


def _proj(x, g, w_bf, pos):
    N, D = x.shape
    tabs = _rope_tables(pos)
    full = lambda shape: pl.BlockSpec(shape, lambda i: (0,) * len(shape))
    return pl.pallas_call(
        _proj_kernel,
        grid=(1,),
        in_specs=[full((N, D)), full((1, D)), full(w_bf.shape)] + [full(t.shape) for t in tabs],
        out_specs=[full((N, RET_WIDTH))] * 7,
        out_shape=[jax.ShapeDtypeStruct((N, RET_WIDTH), F32)] * 7,
        compiler_params=_cparams(("arbitrary",), 40),
        name="proj",
    )(x, g.reshape(1, D), w_bf, *tabs)


def _proj_ret_kernel(x_ref, g_ref, w_ref, cos_ref, sa_ref, sb_ref, dec_ref, cd_ref, kd_ref, sd_ref, gr_ref,
                     ro_ref, s_out_ref, aq_ref, ak_ref, av_ref, sbd_ref):
    j = pl.program_id(1)
    tm = x_ref.shape[1]
    C = RET_CHUNK

    @pl.when(j == 0)
    def _():
        sbd_ref[...] = jnp.zeros_like(sbd_ref)

    rows = pl.ds(pl.multiple_of(j * tm, tm), tm)
    rq, rk, rv, rg, aq, ak, av = _proj_cols(x_ref[0], g_ref, w_ref, cos_ref[rows, :], sa_ref[rows, :],
                                            sb_ref[rows, :])
    aq_ref[0] = aq.astype(aq_ref.dtype)
    ak_ref[0] = ak
    av_ref[0] = av

    lane = lax.broadcasted_iota(jnp.int32, (C, LANES), 1)
    half = lane < HEAD_DIM
    same_head = (lax.broadcasted_iota(jnp.int32, (LANES, LANES), 0) // HEAD_DIM
                 == lax.broadcasted_iota(jnp.int32, (LANES, LANES), 1) // HEAD_DIM)
    inv_d = 1.0 / HEAD_DIM

    def half_mean(t):
        lo = jnp.sum(jnp.where(half, t, 0.0), axis=-1, keepdims=True)
        hi = jnp.sum(jnp.where(half, 0.0, t), axis=-1, keepdims=True)
        return jnp.where(half, lo, hi) * inv_d

    for c in range(RET_WIDTH // LANES):
        ls = slice(c * LANES, (c + 1) * LANES)
        S = sbd_ref[c]
        for cc in range(tm // C):
            rs = slice(cc * C, (cc + 1) * C)
            q = rq[rs, ls]
            qb = q.astype(BF16)
            kb = rk[rs, ls].astype(BF16)
            vb = rv[rs, ls].astype(BF16)
            q_pair = jnp.concatenate([jnp.where(half, q, 0.0), jnp.where(half, 0.0, q)], axis=0).astype(BF16)
            inner = _dot_nt(q_pair, kb) * dec_ref[c]
            o2 = _dot(inner.astype(BF16), vb)
            o = jnp.where(half, o2[:C], o2[C:]) + _dot(qb, S.astype(BF16)) * cd_ref[:, ls]
            kd = (kb.astype(F32) * kd_ref[:, ls]).astype(BF16)
            S = S * sd_ref[:, ls] + jnp.where(same_head, _dot_tn(kd, vb), 0.0)
            mu = half_mean(o)
            d = o - mu
            var = half_mean(d * d)
            on = d * lax.rsqrt(var + GN_EPS) * gr_ref[:, ls]
            gate = rg[rs, ls]
            ro_ref[0, rs, ls] = (on * (gate * jax.nn.sigmoid(gate))).astype(ro_ref.dtype)
        sbd_ref[c] = S
        s_out_ref[0, 2 * c] = S[:HEAD_DIM, :HEAD_DIM]
        s_out_ref[0, 2 * c + 1] = S[HEAD_DIM:, HEAD_DIM:]


def _proj_retention(x, g, w_bf, g_ret, tm):
    B, T, D = x.shape
    R, A, H, Dh = RET_WIDTH, ATT_WIDTH, N_RET_HEADS, HEAD_DIM
    tabs = _rope_tables(jnp.arange(T, dtype=jnp.int32))
    dec, cd, kd, sd = _retention_consts(RET_CHUNK)
    dec_pair = dec.reshape(H // 2, 2 * RET_CHUNK, RET_CHUNK)
    row = lambda width: pl.BlockSpec((1, tm, width), lambda b, j: (b, j, 0))
    full = lambda shape: pl.BlockSpec(shape, lambda b, j: (0,) * len(shape))
    consts = [g.reshape(1, D), w_bf, *tabs, dec_pair, cd, kd, sd, g_ret.reshape(1, R)]
    return pl.pallas_call(
        _proj_ret_kernel,
        grid=(B, T // tm),
        in_specs=[row(D)] + [full(t.shape) for t in consts],
        out_specs=[row(R), pl.BlockSpec((1, H, Dh, Dh), lambda b, j: (b, 0, 0, 0)), row(A), row(A), row(A)],
        out_shape=[jax.ShapeDtypeStruct((B, T, R), BF16), jax.ShapeDtypeStruct((B, H, Dh, Dh), F32),
                   jax.ShapeDtypeStruct((B, T, A), BF16), jax.ShapeDtypeStruct((B, T, A), F32),
                   jax.ShapeDtypeStruct((B, T, A), F32)],
        scratch_shapes=[pltpu.VMEM((R // LANES, LANES, LANES), F32)],
        compiler_params=_cparams(("arbitrary", "arbitrary"), 56),
        name="proj_retention",
    )(x, *consts)


def _ret_kernel(q_ref, k_ref, v_ref, gate_ref, dec_ref, cd_ref, kd_ref, sd_ref, gr_ref, s0_ref, o_ref, s_ref):
    q = q_ref[0].astype(BF16)
    k = k_ref[0].astype(BF16)
    v = v_ref[0].astype(BF16)
    for h in range(N_RET_HEADS):
        sl = slice(h * HEAD_DIM, (h + 1) * HEAD_DIM)
        qh, kh, vh = q[:, sl], k[:, sl], v[:, sl]
        S = s0_ref[0, h]
        inner = _dot_nt(qh, kh) * dec_ref[h]
        o = _dot(inner.astype(BF16), vh) + _dot(qh, S.astype(BF16)) * cd_ref[:, sl]
        kd = (kh.astype(F32) * kd_ref[:, sl]).astype(BF16)
        s_ref[0, h] = S * sd_ref[:, sl] + _dot_tn(kd, vh)
        mu = jnp.mean(o, axis=-1, keepdims=True)
        var = jnp.mean(jnp.square(o - mu), axis=-1, keepdims=True)
        on = (o - mu) * lax.rsqrt(var + GN_EPS) * gr_ref[:, sl]
        gate = gate_ref[0, :, sl]
        o_ref[0, :, sl] = on * (gate * jax.nn.sigmoid(gate))


def _retention_step(rq, rk, rv, rg, g_ret, state):
    B, T, R = rq.shape
    H, Dh = N_RET_HEADS, HEAD_DIM
    consts = [*_retention_consts(T), g_ret.reshape(1, R)]
    row = pl.BlockSpec((1, T, R), lambda b: (b, 0, 0))
    full = lambda shape: pl.BlockSpec(shape, lambda b: (0,) * len(shape))
    st = pl.BlockSpec((1, H, Dh, Dh), lambda b: (b, 0, 0, 0))
    return pl.pallas_call(
        _ret_kernel,
        grid=(B,),
        in_specs=[row, row, row, row] + [full(t.shape) for t in consts] + [st],
        out_specs=[row, st],
        out_shape=[jax.ShapeDtypeStruct((B, T, R), F32), jax.ShapeDtypeStruct((B, H, Dh, Dh), F32)],
        compiler_params=_cparams(("arbitrary",), 32),
        name="retention",
    )(rq, rk, rv, rg, *consts, state)


def _split3_dot(c, e):
    c1 = c.astype(BF16)
    r1 = c - c1.astype(F32)
    c2 = r1.astype(BF16)
    c3 = (r1 - c2.astype(F32)).astype(BF16)
    return _dot(c1, e) + _dot(c2, e) + _dot(c3, e)


def _rows(ref, start, dil, c):
    idx = pl.ds(start, ATT_BLOCK) if dil == 1 else pl.ds(start, ATT_BLOCK, stride=dil)
    return ref[c, idx, :]


def _att_block(q_ref, k_ref, v_ref, bias_ref, dil, cur, prev, first):
    BLK, H = ATT_BLOCK, N_ATT_HEADS
    half = lax.broadcasted_iota(jnp.int32, (BLK, LANES), 1) < HEAD_DIM
    s_parts = []
    for c in range(ATT_CHUNKS):
        q = _rows(q_ref, cur, dil, c)
        q_pair = jnp.concatenate([jnp.where(half, q, 0.0), jnp.where(half, 0.0, q)], axis=0).astype(BF16)
        k = _rows(k_ref, cur, dil, c)
        if prev is not None:
            k = jnp.concatenate([_rows(k_ref, prev, dil, c), k], axis=0)
        s_parts.append(_dot_nt(q_pair, k.astype(BF16)))
    s = jnp.concatenate(s_parts, axis=0)
    if prev is not None:
        s = s + bias_ref[...]
        col = lax.broadcasted_iota(jnp.int32, s.shape, 1)
        s = jnp.where(jnp.logical_and(first, col < BLK), NEG_INF, s)
    else:
        s = s + bias_ref[:, BLK:]
    m = jnp.max(s, axis=-1, keepdims=True)
    p = jnp.exp(s - m)
    l = jnp.sum(p, axis=-1, keepdims=True)
    pb = p.astype(BF16)
    accs = []
    for c in range(ATT_CHUNKS):
        v = _rows(v_ref, cur, dil, c)
        if prev is not None:
            v = jnp.concatenate([_rows(v_ref, prev, dil, c), v], axis=0)
        o = _dot(pb[2 * c * BLK:(2 * c + 2) * BLK], v.astype(BF16))
        accs.append(jnp.where(half, o[:BLK], o[BLK:]))
    lane = lax.broadcasted_iota(jnp.int32, (BLK, LANES), 1)
    ml = jnp.zeros((BLK, LANES), F32)
    for h in range(H):
        ml = jnp.where(lane == h, m[h * BLK:(h + 1) * BLK], ml)
        ml = jnp.where(lane == H + h, l[h * BLK:(h + 1) * BLK], ml)
    return accs, ml


def _att_kernel(q_ref, k_ref, v_ref, b1_ref, b4_ref, b16_ref, e_ref, ao_ref, qc, kc, vc, acc4, ml4, acc16, ml16):
    BLK, H = ATT_BLOCK, N_ATT_HEADS
    S = q_ref.shape[1]
    nblk = S // BLK
    d4, d16 = DIL_PATTERNS[1][1], DIL_PATTERNS[2][1]
    nb4 = nblk // d4
    for c in range(ATT_CHUNKS):
        ls = slice(c * LANES, (c + 1) * LANES)
        qc[c] = q_ref[0, :, ls].astype(F32)
        kc[c] = k_ref[0, :, ls]
        vc[c] = v_ref[0, :, ls]

    def dilated(n, carry):
        r, j = n // nb4, n % nb4
        cur = r + j * (BLK * d4)
        prev = r + jnp.maximum(j - 1, 0) * (BLK * d4)
        accs, ml = _att_block(qc, kc, vc, b4_ref, d4, cur, prev, j == 0)
        for c in range(ATT_CHUNKS):
            acc4[c, pl.ds(cur, BLK, stride=d4), :] = accs[c]
        ml4[pl.ds(cur, BLK, stride=d4), :] = ml
        accs, ml = _att_block(qc, kc, vc, b16_ref, d16, n, None, None)
        for c in range(ATT_CHUNKS):
            acc16[c, pl.ds(n, BLK, stride=d16), :] = accs[c]
        ml16[pl.ds(n, BLK, stride=d16), :] = ml
        return carry

    def dense_and_merge(j, carry):
        cur = pl.multiple_of(j * BLK, BLK)
        prev = pl.multiple_of(jnp.maximum(j - 1, 0) * BLK, BLK)
        accs1, ml1 = _att_block(qc, kc, vc, b1_ref, 1, cur, prev, j == 0)
        rows = pl.ds(cur, BLK)
        mls = [ml1, ml4[rows, :], ml16[rows, :]]
        accs = [accs1, [acc4[c, rows, :] for c in range(ATT_CHUNKS)], [acc16[c, rows, :] for c in range(ATT_CHUNKS)]]
        lane = lax.broadcasted_iota(jnp.int32, (BLK, LANES), 1)
        m_all = jnp.maximum(jnp.maximum(mls[0], mls[1]), mls[2])
        ws = [jnp.exp(ml - m_all) for ml in mls]
        den = sum(w * pltpu.roll(ml, LANES - H, 1) for w, ml in zip(ws, mls))
        ao = [jnp.zeros((BLK, LANES), F32) for _ in range(ATT_CHUNKS)]
        for w, acc in zip(ws, accs):
            coef = _split3_dot(jnp.where(lane < H, w / den, 0.0), e_ref[...])
            for c in range(ATT_CHUNKS):
                ao[c] = ao[c] + coef[:, c * LANES:(c + 1) * LANES] * acc[c]
        ao_ref[0, rows, :] = jnp.concatenate(ao, axis=-1).astype(ao_ref.dtype)
        return carry

    lax.fori_loop(0, nblk, dilated, 0)
    lax.fori_loop(0, nblk, dense_and_merge, 0)


def _prompt_attention(aq, ak, av, rel_bias):
    B, S, A = aq.shape
    CH = ATT_CHUNKS
    assert [d for _, d in DIL_PATTERNS] == [1, 4, 16] and S // ATT_BLOCK == 16
    b1, b4, b16 = _prompt_bias_tables(rel_bias)
    head_of_lane = jnp.arange(A)[None, :] // HEAD_DIM
    expand = (jnp.arange(LANES)[:, None] == head_of_lane).astype(BF16)
    seq = pl.BlockSpec((1, S, A), lambda b: (b, 0, 0))
    const = lambda shape: pl.BlockSpec(shape, lambda b: (0,) * len(shape), pipeline_mode=pl.Buffered(1))
    chunked = pltpu.VMEM((CH, S, LANES), F32)
    stats = pltpu.VMEM((S, LANES), F32)
    return pl.pallas_call(
        _att_kernel,
        grid=(B,),
        in_specs=[seq, seq, seq, const(b1.shape), const(b4.shape), const(b16.shape), const(expand.shape)],
        out_specs=seq,
        out_shape=jax.ShapeDtypeStruct((B, S, A), BF16),
        scratch_shapes=[chunked, chunked, chunked, chunked, stats, chunked, stats],
        compiler_params=_cparams(("arbitrary",), 58),
        name="prompt_att",
    )(aq, ak, av, b1, b4, b16, expand)


def _samp_att_kernel(lows, T, kT_ref, vT_ref, qn_ref, kn_ref, vn_ref, c1_ref, c4_ref, c16_ref,
                     n1_ref, n4_ref, n16_ref, koT_ref, voT_ref, ao_ref):
    H, Dh, L = kT_ref.shape[1:]
    A = H * Dh
    zpad = jnp.zeros((LANES - T, A), F32)
    tail_lane = lax.broadcasted_iota(jnp.int32, (A, LANES), 1) >= LANES - T

    def shift_in(xT_ref, new_ref, out_ref):
        xT = xT_ref[0].reshape(A, L)
        new_p = jnp.concatenate([zpad, new_ref[0]], axis=0)
        rolled = pltpu.roll(xT, L - T, 1)
        tail = jnp.where(tail_lane, new_p.T, rolled[:, L - LANES:])
        out_ref[0] = jnp.concatenate([rolled[:, :L - LANES], tail], axis=1).reshape(H, Dh, L)
        return xT.astype(BF16), new_p.astype(BF16)

    kT, kn = shift_in(kT_ref, kn_ref, koT_ref)
    vT, vn = shift_in(vT_ref, vn_ref, voT_ref)
    HT = H * T
    row_head = lax.broadcasted_iota(jnp.int32, (HT, A), 0) // T
    lane_head = lax.broadcasted_iota(jnp.int32, (HT, A), 1) // HEAD_DIM
    diag = row_head == lane_head
    q_rows = jnp.where(diag, jnp.concatenate([qn_ref[0]] * H, axis=0), 0.0).astype(BF16)
    s_cache = _dot(q_rows, kT)
    s_new = _dot_nt(q_rows, kn)
    stats = []
    for lo, c_ref, n_ref in zip(lows, (c1_ref, c4_ref, c16_ref), (n1_ref, n4_ref, n16_ref)):
        sc = s_cache[:, lo:] + c_ref[...]
        sn = s_new + n_ref[...]
        m = jnp.maximum(jnp.max(sc, axis=-1, keepdims=True), jnp.max(sn, axis=-1, keepdims=True))
        pc = jnp.exp(sc - m)
        pn = jnp.exp(sn - m)
        l = jnp.sum(pc, axis=-1, keepdims=True) + jnp.sum(pn, axis=-1, keepdims=True)
        acc = _dot_nt(pc.astype(BF16), vT[:, lo:]) + _dot(pn.astype(BF16), vn)
        stats.append((m, l, acc))
    m_all = jnp.maximum(jnp.maximum(stats[0][0], stats[1][0]), stats[2][0])
    num = jnp.zeros((HT, A), F32)
    den = jnp.zeros((HT, 1), F32)
    for m, l, acc in stats:
        w = jnp.exp(m - m_all)
        num = num + w * acc
        den = den + w * l
    comb = jnp.where(diag, num / den, 0.0)
    out = comb[0:T]
    for h in range(1, H):
        out = out + comb[h * T:(h + 1) * T]
    ao_ref[0] = out


def _sample_attention(aq, ak, av, cache_kT, cache_vT, rel_bias):
    B, T, A = aq.shape
    _, H, Dh, L = cache_kT.shape
    lows = tuple(L - win for win, _ in DIL_PATTERNS)
    assert all(lo >= 0 and lo % LANES == 0 for lo in lows) and T <= LANES
    tabs = _sample_bias_tables(rel_bias, L, T)
    big = pl.BlockSpec((1, H, Dh, L), lambda b: (b, 0, 0, 0))
    small = pl.BlockSpec((1, T, A), lambda b: (b, 0, 0))
    full = lambda shape: pl.BlockSpec(shape, lambda b: (0,) * len(shape))
    win_sds = jax.ShapeDtypeStruct((B, H, Dh, L), F32)
    ko, vo, ao = pl.pallas_call(
        functools.partial(_samp_att_kernel, lows, T),
        grid=(B,),
        in_specs=[big, big, small, small, small] + [full(t.shape) for t in tabs],
        out_specs=[big, big, small],
        out_shape=[win_sds, win_sds, jax.ShapeDtypeStruct((B, T, A), F32)],
        compiler_params=_cparams(("arbitrary",), 56),
        name="sample_att",
    )(cache_kT, cache_vT, aq, ak, av, *tabs)
    return ao, ko, vo


def _out_kernel(d_ff, ff_chunk, x_ref, ro_ref, ao_ref, wo_ref, wgu_ref, wd_ref, g_pm, g_pf, g_of, y_ref, act_ref):
    R = RET_WIDTH
    mix = _dot(ro_ref[...].astype(BF16), wo_ref[:R, :]) + _dot(ao_ref[...].astype(BF16), wo_ref[R:, :])
    x1 = x_ref[...] + _rms(mix, g_pm[...])
    h = _rms(x1, g_pf[...]).astype(BF16)
    for c in range(0, d_ff, ff_chunk):
        gate = _dot(h, wgu_ref[:, c:c + ff_chunk])
        up = _dot(h, wgu_ref[:, d_ff + c:d_ff + c + ff_chunk])
        act_ref[:, c:c + ff_chunk] = (gate * jax.nn.sigmoid(gate) * up).astype(BF16)
    f = _dot(act_ref[...], wd_ref[...])
    y_ref[...] = x1 + _rms(f, g_of[...])


def _out_block(x, ro, ao, wo_bf, wgu_bf, wd_bf, g_post_mix, g_pre_ffn, g_post_ffn, tm):
    N, D = x.shape
    d_ff = wd_bf.shape[0]
    row = lambda width: pl.BlockSpec((tm, width), lambda i: (i, 0))
    const = lambda shape: pl.BlockSpec(shape, lambda i: (0,) * len(shape), pipeline_mode=pl.Buffered(1))
    return pl.pallas_call(
        functools.partial(_out_kernel, d_ff, 256),
        grid=(N // tm,),
        in_specs=[row(D), row(RET_WIDTH), row(ATT_WIDTH), const(wo_bf.shape), const(wgu_bf.shape),
                  const(wd_bf.shape), const((1, D)), const((1, D)), const((1, D))],
        out_specs=row(D),
        out_shape=jax.ShapeDtypeStruct((N, D), F32),
        scratch_shapes=[pltpu.VMEM((tm, d_ff), BF16)],
        compiler_params=_cparams(("arbitrary",), 56),
        name="out_ffn",
    )(x, ro, ao, wo_bf, wgu_bf, wd_bf, g_post_mix.reshape(1, D), g_pre_ffn.reshape(1, D),
      g_post_ffn.reshape(1, D))


def kernel(x_prompt, x_sample, state_ret, cache_k_win, cache_v_win, rel_bias, w_in, g_ret, w_out,
           g_pre_mix, g_post_mix, g_pre_ffn, g_post_ffn, w_gu, w_down):
    depth = w_in.shape[0]
    assert depth == 1
    B, S, D = x_prompt.shape
    Bs, Ts, _ = x_sample.shape
    H, Dh = N_ATT_HEADS, HEAD_DIM
    l = 0
    w_in_bf = w_in[l].astype(BF16)
    wo_bf = w_out[l].astype(BF16)
    wgu_bf = w_gu[l].astype(BF16)
    wd_bf = w_down[l].astype(BF16)

    ro, s_p, aq, ak, av = _proj_retention(x_prompt, g_pre_mix[l], w_in_bf, g_ret[l], 512)
    ao = _prompt_attention(aq, ak, av, rel_bias)
    y_p = _out_block(x_prompt.reshape(B * S, D), ro.reshape(B * S, RET_WIDTH), ao.reshape(B * S, ATT_WIDTH),
                     wo_bf, wgu_bf, wd_bf, g_post_mix[l], g_pre_ffn[l], g_post_ffn[l], 512)

    pos_s = PAST_LEN + jnp.arange(Ts, dtype=jnp.int32)
    N_s = Bs * Ts
    outs = _proj(x_sample.reshape(N_s, D), g_pre_mix[l], w_in_bf, jnp.tile(pos_s, Bs))
    srq, srk, srv, srg, saq, sak, sav = [t.reshape(Bs, Ts, RET_WIDTH) for t in outs]
    sro, s_s = _retention_step(srq, srk, srv, srg, g_ret[l], state_ret[l])
    to_t = lambda t: jnp.transpose(t, (0, 2, 3, 1))
    from_t = lambda t: jnp.transpose(t, (0, 3, 1, 2))
    sao, k_sT, v_sT = _sample_attention(saq, sak, sav, to_t(cache_k_win[l]), to_t(cache_v_win[l]), rel_bias)
    k_s, v_s = from_t(k_sT), from_t(v_sT)
    y_s = _out_block(x_sample.reshape(N_s, D), sro.reshape(N_s, RET_WIDTH), sao.reshape(N_s, ATT_WIDTH),
                     wo_bf, wgu_bf, wd_bf, g_post_mix[l], g_pre_ffn[l], g_post_ffn[l], N_s)

    return (y_p.reshape(B, S, D), y_s.reshape(Bs, Ts, D),
            s_p[None], ak.reshape(1, B, S, H, Dh), av.reshape(1, B, S, H, Dh),
            s_s[None], k_s[None], v_s[None])
```

```python
import functools
import math

import jax
import jax.numpy as jnp
from jax import lax
from jax.experimental import pallas as pl
from jax.experimental.pallas import tpu as pltpu

F32 = jnp.float32
BF16 = jnp.bfloat16

HEAD_DIM = 64
N_RET_HEADS = 8
N_ATT_HEADS = 8
RET_WIDTH = N_RET_HEADS * HEAD_DIM
ATT_WIDTH = N_ATT_HEADS * HEAD_DIM
RET_CHUNK = 128
DIL_PATTERNS = ((128, 1), (512, 4), (2048, 16))
ATT_BLOCK = 128
N_BUCKETS = 32
MAX_DISTANCE = 2048
ROPE_BASE = 10000.0
NORM_EPS = 1e-6
GN_EPS = 1e-5
PAST_LEN = 16384
LANES = 128
ATT_CHUNKS = ATT_WIDTH // LANES
NEG_INF = float("-inf")


def _cparams(sem, vmem_mb):
    return pltpu.CompilerParams(dimension_semantics=sem, vmem_limit_bytes=vmem_mb * 1024 * 1024)


def _rms(x, g):
    return x * lax.rsqrt(jnp.mean(x * x, axis=-1, keepdims=True) + NORM_EPS) * g


def _dot(a, b):
    return jnp.dot(a, b, preferred_element_type=F32)


def _dot_nt(a, b):
    return lax.dot_general(a, b, (((1,), (1,)), ((), ())), preferred_element_type=F32)


def _dot_tn(a, b):
    return lax.dot_general(a, b, (((0,), (0,)), ((), ())), preferred_element_type=F32)


def _rope_tables(pos):
    inv = ROPE_BASE ** (-jnp.arange(0, HEAD_DIM, 2, dtype=F32) / HEAD_DIM)
    ang = pos.astype(F32)[:, None] * inv[None, :]
    cos, sin = jnp.cos(ang), jnp.sin(ang)
    zero = jnp.zeros_like(sin)
    rep = LANES // HEAD_DIM
    cosf = jnp.tile(jnp.concatenate([cos, cos], axis=-1), (1, rep))
    sa = jnp.tile(jnp.concatenate([-sin, zero], axis=-1), (1, rep))
    sb = jnp.tile(jnp.concatenate([zero, sin], axis=-1), (1, rep))
    return cosf, sa, sb


def _retention_consts(C):
    log_g = jnp.log1p(-jnp.exp2(-5.0 - jnp.arange(N_RET_HEADS, dtype=F32)))
    i = jnp.arange(C, dtype=F32)
    rel = i[:, None] - i[None, :]
    decay = jnp.where(rel[None] >= 0, jnp.exp(jnp.maximum(rel, 0.0)[None] * log_g[:, None, None]), 0.0)
    cross = jnp.exp((i + 1.0)[:, None] * log_g[None, :])
    kdec = jnp.exp((C - 1.0 - i)[:, None] * log_g[None, :])
    sdec = jnp.exp(C * log_g)[None, :]
    expand = lambda t: jnp.repeat(t, HEAD_DIM, axis=-1)
    return decay, expand(cross), expand(kdec), expand(sdec)


def _t5_bucket(dist):
    max_exact = N_BUCKETS // 2
    d_f = jnp.maximum(dist, 1).astype(F32)
    large = max_exact + (jnp.log(d_f / max_exact) / math.log(MAX_DISTANCE / max_exact)
                         * (N_BUCKETS - max_exact)).astype(jnp.int32)
    large = jnp.minimum(large, N_BUCKETS - 1)
    return jnp.where(dist < max_exact, dist, large)


def _pattern_bias_rev(rel_bias, dil, nk):
    dist = jnp.arange(nk, -1, -1, dtype=jnp.int32) * dil
    return rel_bias[_t5_bucket(dist)].astype(F32).T


def _prompt_bias_tables(rel_bias):
    BLK, H = ATT_BLOCK, N_ATT_HEADS
    out = []
    for win, dil in DIL_PATTERNS:
        nk = win // dil
        assert nk == BLK
        period = 3 * BLK
        v = jnp.concatenate([_pattern_bias_rev(rel_bias, dil, nk),
                             jnp.full((H, period - nk - 1), NEG_INF, F32)], axis=1)
        flat = jnp.tile(v, (1, BLK))[:, :BLK * (period - 1)]
        skew = flat.reshape(H, BLK, period - 1)[:, :, :2 * BLK]
        out.append(skew.reshape(H * BLK, 2 * BLK))
    return out


def _sample_bias_tables(rel_bias, L, T):
    H = N_ATT_HEADS
    cache_part, new_part = [], []
    for win, dil in DIL_PATTERNS:
        nk = win // dil
        rev = _pattern_bias_rev(rel_bias, dil, nk)
        if dil > 1:
            gaps = jnp.full((H, nk + 1, dil - 1), NEG_INF, F32)
            rev = jnp.concatenate([rev[:, :, None], gaps], axis=2).reshape(H, (nk + 1) * dil)[:, :nk * dil + 1]
        pv = jnp.concatenate([jnp.full((H, T), NEG_INF, F32), rev, jnp.full((H, 2 * T), NEG_INF, F32)], axis=1)
        width = win + T
        rows = jnp.stack([pv[:, T - t:T - t + width] for t in range(T)], axis=1)
        cache_part.append(rows[:, :, :win].reshape(H * T, win))
        new = jnp.concatenate([jnp.full((H, T, LANES - T), NEG_INF, F32), rows[:, :, win:]], axis=2)
        new_part.append(new.reshape(H * T, LANES))
    return cache_part + new_part


def _proj_steps(x, g_ref, w_ref, cos, sa, sb):
    h = _rms(x, g_ref[...]).astype(BF16)
    W = 2 * LANES
    rep = W // LANES
    cosf = jnp.concatenate([cos] * rep, axis=-1)
    saf = jnp.concatenate([sa] * rep, axis=-1)
    sbf = jnp.concatenate([sb] * rep, axis=-1)
    half = HEAD_DIM // 2

    def col(c):
        return _dot(h, w_ref[:, c:c + W])

    def rope(z):
        return z * cosf + pltpu.roll(z, W - half, 1) * saf + pltpu.roll(z, half, 1) * sbf

    R, A = RET_WIDTH, ATT_WIDTH
    scale = HEAD_DIM ** -0.5
    steps = []
    for c in range(0, R, W):
        steps.append(lambda c=c: rope(col(c)))
    for c in range(R, 2 * R, W):
        steps.append(lambda c=c: rope(col(c)) * scale)
    for c in range(2 * R, 4 * R, W):
        steps.append(lambda c=c: col(c))
    for c in range(4 * R, 4 * R + A, W):
        steps.append(lambda c=c: col(c) * scale)
    for c in range(4 * R + A, 4 * R + 3 * A, W):
        steps.append(lambda c=c: col(c))
    return steps


def _join_groups(pieces):
    return [jnp.concatenate(pieces[i:i + 2], axis=-1) for i in range(0, len(pieces), 2)]


def _proj_kernel(x_ref, g_ref, w_ref, cos_ref, sa_ref, sb_ref, *outs):
    steps = _proj_steps(x_ref[...], g_ref, w_ref, cos_ref[...], sa_ref[...], sb_ref[...])
    for ref, z in zip(outs, _join_groups([step() for step in steps])):
        ref[...] = z


def _proj(x, g, w_bf, pos):
    N, D = x.shape
    tabs = _rope_tables(pos)
    full = lambda shape: pl.BlockSpec(shape, lambda i: (0,) * len(shape))
    return pl.pallas_call(
        _proj_kernel,
        grid=(1,),
        in_specs=[full((N, D)), full((1, D)), full(w_bf.shape)] + [full(t.shape) for t in tabs],
        out_specs=[full((N, RET_WIDTH))] * 7,
        out_shape=[jax.ShapeDtypeStruct((N, RET_WIDTH), F32)] * 7,
        compiler_params=_cparams(("arbitrary",), 40),
        name="proj",
    )(x, g.reshape(1, D), w_bf, *tabs)


def _proj_ret_kernel(tiles_per_seq, x_ref, g_ref, w_ref, cos_ref, sa_ref, sb_ref, dec_ref, cd_ref, kd_ref,
                     sd_ref, gr_ref, ro_ref, s_out_ref, aq_ref, ak_ref, av_ref, sbd_ref, qkv_ref, gate_ref):
    s = pl.program_id(0)
    n_tiles = pl.num_programs(0) - 1
    tm = x_ref.shape[1]
    C = RET_CHUNK
    cur = s % 2
    prv = 1 - cur

    @pl.when(s == 0)
    def _():
        sbd_ref[...] = jnp.zeros_like(sbd_ref)
        qkv_ref[1] = jnp.zeros(qkv_ref.shape[1:], qkv_ref.dtype)
        gate_ref[1] = jnp.zeros(gate_ref.shape[1:], gate_ref.dtype)

    lane = lax.broadcasted_iota(jnp.int32, (C, LANES), 1)
    half = lane < HEAD_DIM
    same_head = (lax.broadcasted_iota(jnp.int32, (LANES, LANES), 0) // HEAD_DIM
                 == lax.broadcasted_iota(jnp.int32, (LANES, LANES), 1) // HEAD_DIM)
    inv_d = 1.0 / HEAD_DIM
    starts_seq = (s - 1) % tiles_per_seq == 0

    def half_mean(t):
        lo = jnp.sum(jnp.where(half, t, 0.0), axis=-1, keepdims=True)
        hi = jnp.sum(jnp.where(half, 0.0, t), axis=-1, keepdims=True)
        return jnp.where(half, lo, hi) * inv_d

    n_pairs = RET_WIDTH // LANES
    n_chunks = tm // C
    state = [jnp.where(starts_seq, 0.0, sbd_ref[c]) for c in range(n_pairs)]

    def retention_unit(c, cc):
        ls = slice(c * LANES, (c + 1) * LANES)
        rs = slice(cc * C, (cc + 1) * C)
        S = state[c]
        qb = qkv_ref[prv, 0, rs, ls]
        kb = qkv_ref[prv, 1, rs, ls]
        vb = qkv_ref[prv, 2, rs, ls]
        q = qb.astype(F32)
        q_pair = jnp.concatenate([jnp.where(half, q, 0.0), jnp.where(half, 0.0, q)], axis=0).astype(BF16)
        inner = _dot_nt(q_pair, kb) * dec_ref[c]
        o2 = _dot(inner.astype(BF16), vb)
        o = jnp.where(half, o2[:C], o2[C:]) + _dot(qb, S.astype(BF16)) * cd_ref[:, ls]
        kd = (kb.astype(F32) * kd_ref[:, ls]).astype(BF16)
        state[c] = S * sd_ref[:, ls] + jnp.where(same_head, _dot_tn(kd, vb), 0.0)
        mu = half_mean(o)
        d = o - mu
        var = half_mean(d * d)
        on = d * lax.rsqrt(var + GN_EPS) * gr_ref[:, ls]
        gate = gate_ref[prv, rs, ls]
        ro_ref[0, rs, ls] = (on * (gate * jax.nn.sigmoid(gate))).astype(ro_ref.dtype)

    j = jnp.minimum(s, n_tiles - 1) % tiles_per_seq
    rows = pl.ds(pl.multiple_of(j * tm, tm), tm)
    steps = _proj_steps(x_ref[0], g_ref, w_ref, cos_ref[rows, :], sa_ref[rows, :], sb_ref[rows, :])
    units = [(c, cc) for cc in range(n_chunks) for c in range(n_pairs)]
    cols = []
    for i, step in enumerate(steps):
        cols.append(step())
        for c, cc in units[i * len(units) // len(steps):(i + 1) * len(units) // len(steps)]:
            retention_unit(c, cc)
    for c in range(n_pairs):
        S = state[c]
        sbd_ref[c] = S
        s_out_ref[0, 2 * c] = S[:HEAD_DIM, :HEAD_DIM]
        s_out_ref[0, 2 * c + 1] = S[HEAD_DIM:, HEAD_DIM:]
    rq, rk, rv, rg, aq, ak, av = _join_groups(cols)
    aq_ref[0] = aq.astype(aq_ref.dtype)
    ak_ref[0] = ak
    av_ref[0] = av
    qkv_ref[cur, 0] = rq.astype(BF16)
    qkv_ref[cur, 1] = rk.astype(BF16)
    qkv_ref[cur, 2] = rv.astype(BF16)
    gate_ref[cur] = rg


def _proj_retention(x, g, w_bf, g_ret, tm):
    B, T, D = x.shape
    R, A, H, Dh = RET_WIDTH, ATT_WIDTH, N_RET_HEADS, HEAD_DIM
    tps = T // tm
    n_tiles = B * tps
    tabs = _rope_tables(jnp.arange(T, dtype=jnp.int32))
    dec, cd, kd, sd = _retention_consts(RET_CHUNK)
    dec_pair = dec.reshape(H // 2, 2 * RET_CHUNK, RET_CHUNK)

    def this_tile(width):
        def index(s):
            t = jnp.minimum(s, n_tiles - 1)
            return (t // tps, t % tps, 0)
        return pl.BlockSpec((1, tm, width), index)

    def prev_tile(s):
        t = jnp.maximum(s - 1, 0)
        return (t // tps, t % tps, 0)

    full = lambda shape: pl.BlockSpec(shape, lambda s: (0,) * len(shape))
    consts = [g.reshape(1, D), w_bf, *tabs, dec_pair, cd, kd, sd, g_ret.reshape(1, R)]
    return pl.pallas_call(
        functools.partial(_proj_ret_kernel, tps),
        grid=(n_tiles + 1,),
        in_specs=[this_tile(D)] + [full(t.shape) for t in consts],
        out_specs=[pl.BlockSpec((1, tm, R), prev_tile),
                   pl.BlockSpec((1, H, Dh, Dh), lambda s: (jnp.maximum(s - 1, 0) // tps, 0, 0, 0)),
                   this_tile(A), this_tile(A), this_tile(A)],
        out_shape=[jax.ShapeDtypeStruct((B, T, R), BF16), jax.ShapeDtypeStruct((B, H, Dh, Dh), F32),
                   jax.ShapeDtypeStruct((B, T, A), BF16), jax.ShapeDtypeStruct((B, T, A), F32),
                   jax.ShapeDtypeStruct((B, T, A), F32)],
        scratch_shapes=[pltpu.VMEM((R // LANES, LANES, LANES), F32), pltpu.VMEM((2, 3, tm, R), BF16),
                        pltpu.VMEM((2, tm, R), F32)],
        compiler_params=_cparams(("arbitrary",), 56),
        name="proj_retention",
    )(x, *consts)


def _ret_kernel(q_ref, k_ref, v_ref, gate_ref, dec_ref, cd_ref, kd_ref, sd_ref, gr_ref, s0_ref, o_ref, s_ref):
    for b in range(q_ref.shape[0]):
        q = q_ref[b].astype(BF16)
        k = k_ref[b].astype(BF16)
        v = v_ref[b].astype(BF16)
        for h in range(N_RET_HEADS):
            sl = slice(h * HEAD_DIM, (h + 1) * HEAD_DIM)
            qh, kh, vh = q[:, sl], k[:, sl], v[:, sl]
            S = s0_ref[b, h]
            inner = _dot_nt(qh, kh) * dec_ref[h]
            o = _dot(inner.astype(BF16), vh) + _dot(qh, S.astype(BF16)) * cd_ref[:, sl]
            kd = (kh.astype(F32) * kd_ref[:, sl]).astype(BF16)
            s_ref[b, h] = S * sd_ref[:, sl] + _dot_tn(kd, vh)
            mu = jnp.mean(o, axis=-1, keepdims=True)
            var = jnp.mean(jnp.square(o - mu), axis=-1, keepdims=True)
            on = (o - mu) * lax.rsqrt(var + GN_EPS) * gr_ref[:, sl]
            gate = gate_ref[b, :, sl]
            o_ref[b, :, sl] = on * (gate * jax.nn.sigmoid(gate))


def _retention_step(rq, rk, rv, rg, g_ret, state):
    B, T, R = rq.shape
    H, Dh = N_RET_HEADS, HEAD_DIM
    consts = [*_retention_consts(T), g_ret.reshape(1, R)]
    bb = math.gcd(B, 8)
    row = pl.BlockSpec((bb, T, R), lambda b: (b, 0, 0))
    full = lambda shape: pl.BlockSpec(shape, lambda b: (0,) * len(shape))
    st = pl.BlockSpec((bb, H, Dh, Dh), lambda b: (b, 0, 0, 0))
    return pl.pallas_call(
        _ret_kernel,
        grid=(B // bb,),
        in_specs=[row, row, row, row] + [full(t.shape) for t in consts] + [st],
        out_specs=[row, st],
        out_shape=[jax.ShapeDtypeStruct((B, T, R), F32), jax.ShapeDtypeStruct((B, H, Dh, Dh), F32)],
        compiler_params=_cparams(("arbitrary",), 32),
        name="retention",
    )(rq, rk, rv, rg, *consts, state)


def _split3_dot(c, e):
    c1 = c.astype(BF16)
    r1 = c - c1.astype(F32)
    c2 = r1.astype(BF16)
    c3 = (r1 - c2.astype(F32)).astype(BF16)
    return _dot(c1, e) + _dot(c2, e) + _dot(c3, e)


def _rows(ref, start, dil, c):
    idx = pl.ds(start, ATT_BLOCK) if dil == 1 else pl.ds(start, ATT_BLOCK, stride=dil)
    return ref[c, idx, :]


def _att_block(q_ref, k_ref, v_ref, bias_ref, dil, cur, prev, first):
    BLK, H = ATT_BLOCK, N_ATT_HEADS
    half = lax.broadcasted_iota(jnp.int32, (BLK, LANES), 1) < HEAD_DIM
    s_parts = []
    for c in range(ATT_CHUNKS):
        q = _rows(q_ref, cur, dil, c)
        q_pair = jnp.concatenate([jnp.where(half, q, 0.0), jnp.where(half, 0.0, q)], axis=0).astype(BF16)
        k = _rows(k_ref, cur, dil, c)
        if prev is not None:
            k = jnp.concatenate([_rows(k_ref, prev, dil, c), k], axis=0)
        s_parts.append(_dot_nt(q_pair, k.astype(BF16)))
    s = jnp.concatenate(s_parts, axis=0)
    if prev is not None:
        s = s + bias_ref[...]
        col = lax.broadcasted_iota(jnp.int32, s.shape, 1)
        s = jnp.where(jnp.logical_and(first, col < BLK), NEG_INF, s)
    else:
        s = s + bias_ref[:, BLK:]
    m = jnp.max(s, axis=-1, keepdims=True)
    p = jnp.exp(s - m)
    l = jnp.sum(p, axis=-1, keepdims=True)
    pb = p.astype(BF16)
    accs = []
    for c in range(ATT_CHUNKS):
        v = _rows(v_ref, cur, dil, c)
        if prev is not None:
            v = jnp.concatenate([_rows(v_ref, prev, dil, c), v], axis=0)
        o = _dot(pb[2 * c * BLK:(2 * c + 2) * BLK], v.astype(BF16))
        accs.append(jnp.where(half, o[:BLK], o[BLK:]))
    lane = lax.broadcasted_iota(jnp.int32, (BLK, LANES), 1)
    ml = jnp.zeros((BLK, LANES), F32)
    for h in range(H):
        ml = jnp.where(lane == h, m[h * BLK:(h + 1) * BLK], ml)
        ml = jnp.where(lane == H + h, l[h * BLK:(h + 1) * BLK], ml)
    return accs, ml


def _att_kernel(q_ref, k_ref, v_ref, b1_ref, b4_ref, b16_ref, e_ref, ao_ref, qc, kc, vc, acc4, ml4, acc16, ml16):
    BLK, H = ATT_BLOCK, N_ATT_HEADS
    S = q_ref.shape[1]
    nblk = S // BLK
    d4, d16 = DIL_PATTERNS[1][1], DIL_PATTERNS[2][1]
    nb4 = nblk // d4
    for c in range(ATT_CHUNKS):
        ls = slice(c * LANES, (c + 1) * LANES)
        qc[c] = q_ref[0, :, ls].astype(F32)
        kc[c] = k_ref[0, :, ls]
        vc[c] = v_ref[0, :, ls]

    def dilated(n, carry):
        r, j = n // nb4, n % nb4
        cur = r + j * (BLK * d4)
        prev = r + jnp.maximum(j - 1, 0) * (BLK * d4)
        accs, ml = _att_block(qc, kc, vc, b4_ref, d4, cur, prev, j == 0)
        for c in range(ATT_CHUNKS):
            acc4[c, pl.ds(cur, BLK, stride=d4), :] = accs[c]
        ml4[pl.ds(cur, BLK, stride=d4), :] = ml
        accs, ml = _att_block(qc, kc, vc, b16_ref, d16, n, None, None)
        for c in range(ATT_CHUNKS):
            acc16[c, pl.ds(n, BLK, stride=d16), :] = accs[c]
        ml16[pl.ds(n, BLK, stride=d16), :] = ml
        return carry

    def dense_and_merge(j, carry):
        cur = pl.multiple_of(j * BLK, BLK)
        prev = pl.multiple_of(jnp.maximum(j - 1, 0) * BLK, BLK)
        accs1, ml1 = _att_block(qc, kc, vc, b1_ref, 1, cur, prev, j == 0)
        rows = pl.ds(cur, BLK)
        mls = [ml1, ml4[rows, :], ml16[rows, :]]
        accs = [accs1, [acc4[c, rows, :] for c in range(ATT_CHUNKS)], [acc16[c, rows, :] for c in range(ATT_CHUNKS)]]
        lane = lax.broadcasted_iota(jnp.int32, (BLK, LANES), 1)
        m_all = jnp.maximum(jnp.maximum(mls[0], mls[1]), mls[2])
        ws = [jnp.exp(ml - m_all) for ml in mls]
        den = sum(w * pltpu.roll(ml, LANES - H, 1) for w, ml in zip(ws, mls))
        ao = [jnp.zeros((BLK, LANES), F32) for _ in range(ATT_CHUNKS)]
        for w, acc in zip(ws, accs):
            coef = _split3_dot(jnp.where(lane < H, w / den, 0.0), e_ref[...])
            for c in range(ATT_CHUNKS):
                ao[c] = ao[c] + coef[:, c * LANES:(c + 1) * LANES] * acc[c]
        ao_ref[0, rows, :] = jnp.concatenate(ao, axis=-1).astype(ao_ref.dtype)
        return carry

    lax.fori_loop(0, nblk, dilated, 0)
    lax.fori_loop(0, nblk, dense_and_merge, 0)


def _prompt_attention(aq, ak, av, rel_bias):
    B, S, A = aq.shape
    CH = ATT_CHUNKS
    assert [d for _, d in DIL_PATTERNS] == [1, 4, 16] and S // ATT_BLOCK == 16
    b1, b4, b16 = _prompt_bias_tables(rel_bias)
    head_of_lane = jnp.arange(A)[None, :] // HEAD_DIM
    expand = (jnp.arange(LANES)[:, None] == head_of_lane).astype(BF16)
    seq = pl.BlockSpec((1, S, A), lambda b: (b, 0, 0))
    const = lambda shape: pl.BlockSpec(shape, lambda b: (0,) * len(shape), pipeline_mode=pl.Buffered(1))
    chunked = pltpu.VMEM((CH, S, LANES), F32)
    stats = pltpu.VMEM((S, LANES), F32)
    return pl.pallas_call(
        _att_kernel,
        grid=(B,),
        in_specs=[seq, seq, seq, const(b1.shape), const(b4.shape), const(b16.shape), const(expand.shape)],
        out_specs=seq,
        out_shape=jax.ShapeDtypeStruct((B, S, A), BF16),
        scratch_shapes=[chunked, chunked, chunked, chunked, stats, chunked, stats],
        compiler_params=_cparams(("arbitrary",), 58),
        name="prompt_att",
    )(aq, ak, av, b1, b4, b16, expand)


def _samp_att_kernel(lows, T, kT_ref, vT_ref, qn_ref, kn_ref, vn_ref, c1_ref, c4_ref, c16_ref,
                     n1_ref, n4_ref, n16_ref, koT_ref, voT_ref, ao_ref):
    H, Dh, L = kT_ref.shape[1:]
    A = H * Dh
    zpad = jnp.zeros((LANES - T, A), F32)
    tail_lane = lax.broadcasted_iota(jnp.int32, (A, LANES), 1) >= LANES - T

    def shift_in(xT_ref, new_ref, out_ref):
        xT = xT_ref[0].reshape(A, L)
        new_p = jnp.concatenate([zpad, new_ref[0]], axis=0)
        rolled = pltpu.roll(xT, L - T, 1)
        tail = jnp.where(tail_lane, new_p.T, rolled[:, L - LANES:])
        out_ref[0] = jnp.concatenate([rolled[:, :L - LANES], tail], axis=1).reshape(H, Dh, L)
        return xT.astype(BF16), new_p.astype(BF16)

    kT, kn = shift_in(kT_ref, kn_ref, koT_ref)
    vT, vn = shift_in(vT_ref, vn_ref, voT_ref)
    HT = H * T
    row_head = lax.broadcasted_iota(jnp.int32, (HT, A), 0) // T
    lane_head = lax.broadcasted_iota(jnp.int32, (HT, A), 1) // HEAD_DIM
    diag = row_head == lane_head
    q_rows = jnp.where(diag, jnp.concatenate([qn_ref[0]] * H, axis=0), 0.0).astype(BF16)
    s_cache = _dot(q_rows, kT)
    s_new = _dot_nt(q_rows, kn)
    stats = []
    for lo, c_ref, n_ref in zip(lows, (c1_ref, c4_ref, c16_ref), (n1_ref, n4_ref, n16_ref)):
        sc = s_cache[:, lo:] + c_ref[...]
        sn = s_new + n_ref[...]
        m = jnp.maximum(jnp.max(sc, axis=-1, keepdims=True), jnp.max(sn, axis=-1, keepdims=True))
        pc = jnp.exp(sc - m)
        pn = jnp.exp(sn - m)
        l = jnp.sum(pc, axis=-1, keepdims=True) + jnp.sum(pn, axis=-1, keepdims=True)
        acc = _dot_nt(pc.astype(BF16), vT[:, lo:]) + _dot(pn.astype(BF16), vn)
        stats.append((m, l, acc))
    m_all = jnp.maximum(jnp.maximum(stats[0][0], stats[1][0]), stats[2][0])
    num = jnp.zeros((HT, A), F32)
    den = jnp.zeros((HT, 1), F32)
    for m, l, acc in stats:
        w = jnp.exp(m - m_all)
        num = num + w * acc
        den = den + w * l
    comb = jnp.where(diag, num / den, 0.0)
    out = comb[0:T]
    for h in range(1, H):
        out = out + comb[h * T:(h + 1) * T]
    ao_ref[0] = out


def _sample_attention(aq, ak, av, cache_kT, cache_vT, rel_bias):
    B, T, A = aq.shape
    _, H, Dh, L = cache_kT.shape
    lows = tuple(L - win for win, _ in DIL_PATTERNS)
    assert all(lo >= 0 and lo % LANES == 0 for lo in lows) and T <= LANES
    tabs = _sample_bias_tables(rel_bias, L, T)
    big = pl.BlockSpec((1, H, Dh, L), lambda b: (b, 0, 0, 0))
    small = pl.BlockSpec((1, T, A), lambda b: (b, 0, 0))
    full = lambda shape: pl.BlockSpec(shape, lambda b: (0,) * len(shape))
    win_sds = jax.ShapeDtypeStruct((B, H, Dh, L), F32)
    ko, vo, ao = pl.pallas_call(
        functools.partial(_samp_att_kernel, lows, T),
        grid=(B,),
        in_specs=[big, big, small, small, small] + [full(t.shape) for t in tabs],
        out_specs=[big, big, small],
        out_shape=[win_sds, win_sds, jax.ShapeDtypeStruct((B, T, A), F32)],
        compiler_params=_cparams(("arbitrary",), 56),
        name="sample_att",
    )(cache_kT, cache_vT, aq, ak, av, *tabs)
    return ao, ko, vo


def _out_kernel(d_ff, ff_chunk, x_ref, ro_ref, ao_ref, wo_ref, wgu_ref, wd_ref, g_pm, g_pf, g_of, y_ref, act_ref):
    R = RET_WIDTH
    mix = _dot(ro_ref[...].astype(BF16), wo_ref[:R, :]) + _dot(ao_ref[...].astype(BF16), wo_ref[R:, :])
    x1 = x_ref[...] + _rms(mix, g_pm[...])
    h = _rms(x1, g_pf[...]).astype(BF16)
    for c in range(0, d_ff, ff_chunk):
        gate = _dot(h, wgu_ref[:, c:c + ff_chunk])
        up = _dot(h, wgu_ref[:, d_ff + c:d_ff + c + ff_chunk])
        act_ref[:, c:c + ff_chunk] = (gate * jax.nn.sigmoid(gate) * up).astype(BF16)
    f = _dot(act_ref[...], wd_ref[...])
    y_ref[...] = x1 + _rms(f, g_of[...])


def _out_block(x, ro, ao, wo_bf, wgu_bf, wd_bf, g_post_mix, g_pre_ffn, g_post_ffn, tm):
    N, D = x.shape
    d_ff = wd_bf.shape[0]
    row = lambda width: pl.BlockSpec((tm, width), lambda i: (i, 0))
    const = lambda shape: pl.BlockSpec(shape, lambda i: (0,) * len(shape), pipeline_mode=pl.Buffered(1))
    return pl.pallas_call(
        functools.partial(_out_kernel, d_ff, 256),
        grid=(N // tm,),
        in_specs=[row(D), row(RET_WIDTH), row(ATT_WIDTH), const(wo_bf.shape), const(wgu_bf.shape),
                  const(wd_bf.shape), const((1, D)), const((1, D)), const((1, D))],
        out_specs=row(D),
        out_shape=jax.ShapeDtypeStruct((N, D), F32),
        scratch_shapes=[pltpu.VMEM((tm, d_ff), BF16)],
        compiler_params=_cparams(("arbitrary",), 56),
        name="out_ffn",
    )(x, ro, ao, wo_bf, wgu_bf, wd_bf, g_post_mix.reshape(1, D), g_pre_ffn.reshape(1, D),
      g_post_ffn.reshape(1, D))


def kernel(x_prompt, x_sample, state_ret, cache_k_win, cache_v_win, rel_bias, w_in, g_ret, w_out,
           g_pre_mix, g_post_mix, g_pre_ffn, g_post_ffn, w_gu, w_down):
    depth = w_in.shape[0]
    assert depth == 1
    B, S, D = x_prompt.shape
    Bs, Ts, _ = x_sample.shape
    H, Dh = N_ATT_HEADS, HEAD_DIM
    l = 0
    w_in_bf = w_in[l].astype(BF16)
    wo_bf = w_out[l].astype(BF16)
    wgu_bf = w_gu[l].astype(BF16)
    wd_bf = w_down[l].astype(BF16)

    ro, s_p, aq, ak, av = _proj_retention(x_prompt, g_pre_mix[l], w_in_bf, g_ret[l], 512)
    ao = _prompt_attention(aq, ak, av, rel_bias)
    y_p = _out_block(x_prompt.reshape(B * S, D), ro.reshape(B * S, RET_WIDTH), ao.reshape(B * S, ATT_WIDTH),
                     wo_bf, wgu_bf, wd_bf, g_post_mix[l], g_pre_ffn[l], g_post_ffn[l], 512)

    pos_s = PAST_LEN + jnp.arange(Ts, dtype=jnp.int32)
    N_s = Bs * Ts
    outs = _proj(x_sample.reshape(N_s, D), g_pre_mix[l], w_in_bf, jnp.tile(pos_s, Bs))
    srq, srk, srv, srg, saq, sak, sav = [t.reshape(Bs, Ts, RET_WIDTH) for t in outs]
    sro, s_s = _retention_step(srq, srk, srv, srg, g_ret[l], state_ret[l])
    to_t = lambda t: jnp.transpose(t, (0, 2, 3, 1))
    from_t = lambda t: jnp.transpose(t, (0, 3, 1, 2))
    sao, k_sT, v_sT = _sample_attention(saq, sak, sav, to_t(cache_k_win[l]), to_t(cache_v_win[l]), rel_bias)
    k_s, v_s = from_t(k_sT), from_t(v_sT)
    y_s = _out_block(x_sample.reshape(N_s, D), sro.reshape(N_s, RET_WIDTH), sao.reshape(N_s, ATT_WIDTH),
                     wo_bf, wgu_bf, wd_bf, g_post_mix[l], g_pre_ffn[l], g_post_ffn[l], N_s)

    return (y_p.reshape(B, S, D), y_s.reshape(Bs, Ts, D),
            s_p[None], ak.reshape(1, B, S, H, Dh), av.reshape(1, B, S, H, Dh),
            s_s[None], k_s[None], v_s[None])
```

```python
import functools
import math

import jax
import jax.numpy as jnp
from jax import lax
from jax.experimental import pallas as pl
from jax.experimental.pallas import tpu as pltpu

F32 = jnp.float32
BF16 = jnp.bfloat16

HEAD_DIM = 64
N_RET_HEADS = 8
N_ATT_HEADS = 8
RET_WIDTH = N_RET_HEADS * HEAD_DIM
ATT_WIDTH = N_ATT_HEADS * HEAD_DIM
RET_CHUNK = 128
DIL_PATTERNS = ((128, 1), (512, 4), (2048, 16))
ATT_BLOCK = 128
N_BUCKETS = 32
MAX_DISTANCE = 2048
ROPE_BASE = 10000.0
NORM_EPS = 1e-6
GN_EPS = 1e-5
PAST_LEN = 16384
LANES = 128
ATT_CHUNKS = ATT_WIDTH // LANES
NEG_INF = float("-inf")


def _cparams(sem, vmem_mb):
    return pltpu.CompilerParams(dimension_semantics=sem, vmem_limit_bytes=vmem_mb * 1024 * 1024)


def _rms(x, g):
    return x * lax.rsqrt(jnp.mean(x * x, axis=-1, keepdims=True) + NORM_EPS) * g


def _dot(a, b):
    return jnp.dot(a, b, preferred_element_type=F32)


def _dot_nt(a, b):
    return lax.dot_general(a, b, (((1,), (1,)), ((), ())), preferred_element_type=F32)


def _dot_tn(a, b):
    return lax.dot_general(a, b, (((0,), (0,)), ((), ())), preferred_element_type=F32)


def _rope_tables(pos):
    inv = ROPE_BASE ** (-jnp.arange(0, HEAD_DIM, 2, dtype=F32) / HEAD_DIM)
    ang = pos.astype(F32)[:, None] * inv[None, :]
    cos, sin = jnp.cos(ang), jnp.sin(ang)
    zero = jnp.zeros_like(sin)
    rep = LANES // HEAD_DIM
    cosf = jnp.tile(jnp.concatenate([cos, cos], axis=-1), (1, rep))
    sa = jnp.tile(jnp.concatenate([-sin, zero], axis=-1), (1, rep))
    sb = jnp.tile(jnp.concatenate([zero, sin], axis=-1), (1, rep))
    return cosf, sa, sb


def _retention_consts(C):
    log_g = jnp.log1p(-jnp.exp2(-5.0 - jnp.arange(N_RET_HEADS, dtype=F32)))
    i = jnp.arange(C, dtype=F32)
    rel = i[:, None] - i[None, :]
    decay = jnp.where(rel[None] >= 0, jnp.exp(jnp.maximum(rel, 0.0)[None] * log_g[:, None, None]), 0.0)
    cross = jnp.exp((i + 1.0)[:, None] * log_g[None, :])
    kdec = jnp.exp((C - 1.0 - i)[:, None] * log_g[None, :])
    sdec = jnp.exp(C * log_g)[None, :]
    expand = lambda t: jnp.repeat(t, HEAD_DIM, axis=-1)
    return decay, expand(cross), expand(kdec), expand(sdec)


def _t5_bucket(dist):
    max_exact = N_BUCKETS // 2
    d_f = jnp.maximum(dist, 1).astype(F32)
    large = max_exact + (jnp.log(d_f / max_exact) / math.log(MAX_DISTANCE / max_exact)
                         * (N_BUCKETS - max_exact)).astype(jnp.int32)
    large = jnp.minimum(large, N_BUCKETS - 1)
    return jnp.where(dist < max_exact, dist, large)


def _pattern_bias_rev(rel_bias, dil, nk):
    dist = jnp.arange(nk, -1, -1, dtype=jnp.int32) * dil
    return rel_bias[_t5_bucket(dist)].astype(F32).T


def _prompt_bias_tables(rel_bias):
    BLK, H = ATT_BLOCK, N_ATT_HEADS
    out = []
    for win, dil in DIL_PATTERNS:
        nk = win // dil
        assert nk == BLK
        period = 3 * BLK
        v = jnp.concatenate([_pattern_bias_rev(rel_bias, dil, nk),
                             jnp.full((H, period - nk - 1), NEG_INF, F32)], axis=1)
        flat = jnp.tile(v, (1, BLK))[:, :BLK * (period - 1)]
        skew = flat.reshape(H, BLK, period - 1)[:, :, :2 * BLK]
        out.append(skew.reshape(H * BLK, 2 * BLK))
    return out


def _sample_bias_tables(rel_bias, L, T):
    H = N_ATT_HEADS
    cache_part, new_part = [], []
    for win, dil in DIL_PATTERNS:
        nk = win // dil
        rev = _pattern_bias_rev(rel_bias, dil, nk)
        if dil > 1:
            gaps = jnp.full((H, nk + 1, dil - 1), NEG_INF, F32)
            rev = jnp.concatenate([rev[:, :, None], gaps], axis=2).reshape(H, (nk + 1) * dil)[:, :nk * dil + 1]
        pv = jnp.concatenate([jnp.full((H, T), NEG_INF, F32), rev, jnp.full((H, 2 * T), NEG_INF, F32)], axis=1)
        width = win + T
        rows = jnp.stack([pv[:, T - t:T - t + width] for t in range(T)], axis=1)
        cache_part.append(rows[:, :, :win].reshape(H * T, win))
        new = jnp.concatenate([jnp.full((H, T, LANES - T), NEG_INF, F32), rows[:, :, win:]], axis=2)
        new_part.append(new.reshape(H * T, LANES))
    return cache_part + new_part


def _proj_steps(x, g_ref, w_ref, cos, sa, sb):
    h = _rms(x, g_ref[...]).astype(BF16)
    W = 2 * LANES
    rep = W // LANES
    cosf = jnp.concatenate([cos] * rep, axis=-1)
    saf = jnp.concatenate([sa] * rep, axis=-1)
    sbf = jnp.concatenate([sb] * rep, axis=-1)
    half = HEAD_DIM // 2

    def col(c):
        return _dot(h, w_ref[:, c:c + W])

    def rope(z):
        return z * cosf + pltpu.roll(z, W - half, 1) * saf + pltpu.roll(z, half, 1) * sbf

    R, A = RET_WIDTH, ATT_WIDTH
    scale = HEAD_DIM ** -0.5
    steps = []
    for c in range(0, R, W):
        steps.append(lambda c=c: rope(col(c)))
    for c in range(R, 2 * R, W):
        steps.append(lambda c=c: rope(col(c)) * scale)
    for c in range(2 * R, 4 * R, W):
        steps.append(lambda c=c: col(c))
    for c in range(4 * R, 4 * R + A, W):
        steps.append(lambda c=c: col(c) * scale)
    for c in range(4 * R + A, 4 * R + 3 * A, W):
        steps.append(lambda c=c: col(c))
    return steps


def _join_groups(pieces):
    return [jnp.concatenate(pieces[i:i + 2], axis=-1) for i in range(0, len(pieces), 2)]


def _proj_kernel(x_ref, g_ref, w_ref, cos_ref, sa_ref, sb_ref, *outs):
    steps = _proj_steps(x_ref[...], g_ref, w_ref, cos_ref[...], sa_ref[...], sb_ref[...])
    for ref, z in zip(outs, _join_groups([step() for step in steps])):
        ref[...] = z


def _proj(x, g, w_bf, pos):
    N, D = x.shape
    tabs = _rope_tables(pos)
    full = lambda shape: pl.BlockSpec(shape, lambda i: (0,) * len(shape))
    return pl.pallas_call(
        _proj_kernel,
        grid=(1,),
        in_specs=[full((N, D)), full((1, D)), full(w_bf.shape)] + [full(t.shape) for t in tabs],
        out_specs=[full((N, RET_WIDTH))] * 7,
        out_shape=[jax.ShapeDtypeStruct((N, RET_WIDTH), F32)] * 7,
        compiler_params=_cparams(("arbitrary",), 40),
        name="proj",
    )(x, g.reshape(1, D), w_bf, *tabs)


def _proj_ret_kernel(tiles_per_seq, x_ref, g_ref, w_ref, cos_ref, sa_ref, sb_ref, dec_ref, cd_ref, kd_ref,
                     sd_ref, gr_ref, ro_ref, s_out_ref, aq_ref, ak_ref, av_ref, sbd_ref, qkv_ref, gate_ref):
    s = pl.program_id(0)
    n_tiles = pl.num_programs(0) - 1
    tm = x_ref.shape[1]
    C = RET_CHUNK
    cur = s % 2
    prv = 1 - cur

    @pl.when(s == 0)
    def _():
        sbd_ref[...] = jnp.zeros_like(sbd_ref)
        qkv_ref[1] = jnp.zeros(qkv_ref.shape[1:], qkv_ref.dtype)
        gate_ref[1] = jnp.zeros(gate_ref.shape[1:], gate_ref.dtype)

    lane = lax.broadcasted_iota(jnp.int32, (C, LANES), 1)
    half = lane < HEAD_DIM
    same_head = (lax.broadcasted_iota(jnp.int32, (LANES, LANES), 0) // HEAD_DIM
                 == lax.broadcasted_iota(jnp.int32, (LANES, LANES), 1) // HEAD_DIM)
    inv_d = 1.0 / HEAD_DIM
    starts_seq = (s - 1) % tiles_per_seq == 0

    def half_mean(t):
        lo = jnp.sum(jnp.where(half, t, 0.0), axis=-1, keepdims=True)
        hi = jnp.sum(jnp.where(half, 0.0, t), axis=-1, keepdims=True)
        return jnp.where(half, lo, hi) * inv_d

    n_pairs = RET_WIDTH // LANES
    n_chunks = tm // C
    state = [jnp.where(starts_seq, 0.0, sbd_ref[c]) for c in range(n_pairs)]

    def retention_unit(c, cc):
        ls = slice(c * LANES, (c + 1) * LANES)
        rs = slice(cc * C, (cc + 1) * C)
        S = state[c]
        qb = qkv_ref[prv, 0, rs, ls]
        kb = qkv_ref[prv, 1, rs, ls]
        vb = qkv_ref[prv, 2, rs, ls]
        q = qb.astype(F32)
        q_pair = jnp.concatenate([jnp.where(half, q, 0.0), jnp.where(half, 0.0, q)], axis=0).astype(BF16)
        inner = _dot_nt(q_pair, kb) * dec_ref[c]
        o2 = _dot(inner.astype(BF16), vb)
        o = jnp.where(half, o2[:C], o2[C:]) + _dot(qb, S.astype(BF16)) * cd_ref[:, ls]
        kd = (kb.astype(F32) * kd_ref[:, ls]).astype(BF16)
        state[c] = S * sd_ref[:, ls] + jnp.where(same_head, _dot_tn(kd, vb), 0.0)
        mu = half_mean(o)
        d = o - mu
        var = half_mean(d * d)
        on = d * lax.rsqrt(var + GN_EPS) * gr_ref[:, ls]
        gate = gate_ref[prv, rs, ls]
        ro_ref[0, rs, ls] = (on * (gate * jax.nn.sigmoid(gate))).astype(ro_ref.dtype)

    j = jnp.minimum(s, n_tiles - 1) % tiles_per_seq
    rows = pl.ds(pl.multiple_of(j * tm, tm), tm)
    steps = _proj_steps(x_ref[0], g_ref, w_ref, cos_ref[rows, :], sa_ref[rows, :], sb_ref[rows, :])
    units = [(c, cc) for cc in range(n_chunks) for c in range(n_pairs)]
    cols = []
    for i, step in enumerate(steps):
        cols.append(step())
        for c, cc in units[i * len(units) // len(steps):(i + 1) * len(units) // len(steps)]:
            retention_unit(c, cc)
    for c in range(n_pairs):
        S = state[c]
        sbd_ref[c] = S
        s_out_ref[0, 2 * c] = S[:HEAD_DIM, :HEAD_DIM]
        s_out_ref[0, 2 * c + 1] = S[HEAD_DIM:, HEAD_DIM:]
    rq, rk, rv, rg, aq, ak, av = _join_groups(cols)
    aq_ref[0] = aq.astype(aq_ref.dtype)
    ak_ref[0] = ak
    av_ref[0] = av
    qkv_ref[cur, 0] = rq.astype(BF16)
    qkv_ref[cur, 1] = rk.astype(BF16)
    qkv_ref[cur, 2] = rv.astype(BF16)
    gate_ref[cur] = rg


def _proj_retention(x, g, w_bf, g_ret, tm):
    B, T, D = x.shape
    R, A, H, Dh = RET_WIDTH, ATT_WIDTH, N_RET_HEADS, HEAD_DIM
    tps = T // tm
    n_tiles = B * tps
    tabs = _rope_tables(jnp.arange(T, dtype=jnp.int32))
    dec, cd, kd, sd = _retention_consts(RET_CHUNK)
    dec_pair = dec.reshape(H // 2, 2 * RET_CHUNK, RET_CHUNK)

    def this_tile(width):
        def index(s):
            t = jnp.minimum(s, n_tiles - 1)
            return (t // tps, t % tps, 0)
        return pl.BlockSpec((1, tm, width), index)

    def prev_tile(s):
        t = jnp.maximum(s - 1, 0)
        return (t // tps, t % tps, 0)

    full = lambda shape: pl.BlockSpec(shape, lambda s: (0,) * len(shape))
    consts = [g.reshape(1, D), w_bf, *tabs, dec_pair, cd, kd, sd, g_ret.reshape(1, R)]
    return pl.pallas_call(
        functools.partial(_proj_ret_kernel, tps),
        grid=(n_tiles + 1,),
        in_specs=[this_tile(D)] + [full(t.shape) for t in consts],
        out_specs=[pl.BlockSpec((1, tm, R), prev_tile),
                   pl.BlockSpec((1, H, Dh, Dh), lambda s: (jnp.maximum(s - 1, 0) // tps, 0, 0, 0)),
                   this_tile(A), this_tile(A), this_tile(A)],
        out_shape=[jax.ShapeDtypeStruct((B, T, R), BF16), jax.ShapeDtypeStruct((B, H, Dh, Dh), F32),
                   jax.ShapeDtypeStruct((B, T, A), BF16), jax.ShapeDtypeStruct((B, T, A), F32),
                   jax.ShapeDtypeStruct((B, T, A), F32)],
        scratch_shapes=[pltpu.VMEM((R // LANES, LANES, LANES), F32), pltpu.VMEM((2, 3, tm, R), BF16),
                        pltpu.VMEM((2, tm, R), F32)],
        compiler_params=_cparams(("arbitrary",), 56),
        name="proj_retention",
    )(x, *consts)


def _ret_kernel(q_ref, k_ref, v_ref, gate_ref, dec_ref, cd_ref, kd_ref, sd_ref, gr_ref, s0_ref, o_ref, s_ref):
    for b in range(q_ref.shape[0]):
        q = q_ref[b].astype(BF16)
        k = k_ref[b].astype(BF16)
        v = v_ref[b].astype(BF16)
        for h in range(N_RET_HEADS):
            sl = slice(h * HEAD_DIM, (h + 1) * HEAD_DIM)
            qh, kh, vh = q[:, sl], k[:, sl], v[:, sl]
            S = s0_ref[b, h]
            inner = _dot_nt(qh, kh) * dec_ref[h]
            o = _dot(inner.astype(BF16), vh) + _dot(qh, S.astype(BF16)) * cd_ref[:, sl]
            kd = (kh.astype(F32) * kd_ref[:, sl]).astype(BF16)
            s_ref[b, h] = S * sd_ref[:, sl] + _dot_tn(kd, vh)
            mu = jnp.mean(o, axis=-1, keepdims=True)
            var = jnp.mean(jnp.square(o - mu), axis=-1, keepdims=True)
            on = (o - mu) * lax.rsqrt(var + GN_EPS) * gr_ref[:, sl]
            gate = gate_ref[b, :, sl]
            o_ref[b, :, sl] = on * (gate * jax.nn.sigmoid(gate))


def _retention_step(rq, rk, rv, rg, g_ret, state):
    B, T, R = rq.shape
    H, Dh = N_RET_HEADS, HEAD_DIM
    consts = [*_retention_consts(T), g_ret.reshape(1, R)]
    bb = math.gcd(B, 8)
    row = pl.BlockSpec((bb, T, R), lambda b: (b, 0, 0))
    full = lambda shape: pl.BlockSpec(shape, lambda b: (0,) * len(shape))
    st = pl.BlockSpec((bb, H, Dh, Dh), lambda b: (b, 0, 0, 0))
    return pl.pallas_call(
        _ret_kernel,
        grid=(B // bb,),
        in_specs=[row, row, row, row] + [full(t.shape) for t in consts] + [st],
        out_specs=[row, st],
        out_shape=[jax.ShapeDtypeStruct((B, T, R), F32), jax.ShapeDtypeStruct((B, H, Dh, Dh), F32)],
        compiler_params=_cparams(("arbitrary",), 32),
        name="retention",
    )(rq, rk, rv, rg, *consts, state)


def _split3_dot(c, e):
    c1 = c.astype(BF16)
    r1 = c - c1.astype(F32)
    c2 = r1.astype(BF16)
    c3 = (r1 - c2.astype(F32)).astype(BF16)
    return _dot(c1, e) + _dot(c2, e) + _dot(c3, e)


def _rows(ref, start, dil, c):
    idx = pl.ds(start, ATT_BLOCK) if dil == 1 else pl.ds(start, ATT_BLOCK, stride=dil)
    return ref[c, idx, :]


def _att_block(q_ref, k_ref, v_ref, bias_ref, dil, cur, prev, first):
    BLK, H = ATT_BLOCK, N_ATT_HEADS
    half = lax.broadcasted_iota(jnp.int32, (BLK, LANES), 1) < HEAD_DIM
    s_parts = []
    for c in range(ATT_CHUNKS):
        q = _rows(q_ref, cur, dil, c)
        q_pair = jnp.concatenate([jnp.where(half, q, 0.0), jnp.where(half, 0.0, q)], axis=0).astype(BF16)
        k = _rows(k_ref, cur, dil, c)
        if prev is not None:
            k = jnp.concatenate([_rows(k_ref, prev, dil, c), k], axis=0)
        s_parts.append(_dot_nt(q_pair, k.astype(BF16)))
    s = jnp.concatenate(s_parts, axis=0)
    if prev is not None:
        s = s + bias_ref[...]
        col = lax.broadcasted_iota(jnp.int32, s.shape, 1)
        s = jnp.where(jnp.logical_and(first, col < BLK), NEG_INF, s)
    else:
        s = s + bias_ref[:, BLK:]
    m = jnp.max(s, axis=-1, keepdims=True)
    p = jnp.exp(s - m)
    l = jnp.sum(p, axis=-1, keepdims=True)
    pb = p.astype(BF16)
    accs = []
    for c in range(ATT_CHUNKS):
        v = _rows(v_ref, cur, dil, c)
        if prev is not None:
            v = jnp.concatenate([_rows(v_ref, prev, dil, c), v], axis=0)
        o = _dot(pb[2 * c * BLK:(2 * c + 2) * BLK], v.astype(BF16))
        accs.append(jnp.where(half, o[:BLK], o[BLK:]))
    lane = lax.broadcasted_iota(jnp.int32, (BLK, LANES), 1)
    ml = jnp.zeros((BLK, LANES), F32)
    for h in range(H):
        ml = jnp.where(lane == h, m[h * BLK:(h + 1) * BLK], ml)
        ml = jnp.where(lane == H + h, l[h * BLK:(h + 1) * BLK], ml)
    return accs, ml


def _att_kernel(q_ref, k_ref, v_ref, b1_ref, b4_ref, b16_ref, e_ref, ao_ref, qc, kc, vc, acc4, ml4, acc16, ml16):
    BLK, H = ATT_BLOCK, N_ATT_HEADS
    S = q_ref.shape[1]
    nblk = S // BLK
    d4, d16 = DIL_PATTERNS[1][1], DIL_PATTERNS[2][1]
    nb4 = nblk // d4
    for c in range(ATT_CHUNKS):
        ls = slice(c * LANES, (c + 1) * LANES)
        qc[c] = q_ref[0, :, ls].astype(F32)
        kc[c] = k_ref[0, :, ls]
        vc[c] = v_ref[0, :, ls]

    def dilated(n, carry):
        r, j = n // nb4, n % nb4
        cur = r + j * (BLK * d4)
        prev = r + jnp.maximum(j - 1, 0) * (BLK * d4)
        accs, ml = _att_block(qc, kc, vc, b4_ref, d4, cur, prev, j == 0)
        for c in range(ATT_CHUNKS):
            acc4[c, pl.ds(cur, BLK, stride=d4), :] = accs[c]
        ml4[pl.ds(cur, BLK, stride=d4), :] = ml
        accs, ml = _att_block(qc, kc, vc, b16_ref, d16, n, None, None)
        for c in range(ATT_CHUNKS):
            acc16[c, pl.ds(n, BLK, stride=d16), :] = accs[c]
        ml16[pl.ds(n, BLK, stride=d16), :] = ml
        return carry

    def dense_and_merge(j, carry):
        cur = pl.multiple_of(j * BLK, BLK)
        prev = pl.multiple_of(jnp.maximum(j - 1, 0) * BLK, BLK)
        accs1, ml1 = _att_block(qc, kc, vc, b1_ref, 1, cur, prev, j == 0)
        rows = pl.ds(cur, BLK)
        mls = [ml1, ml4[rows, :], ml16[rows, :]]
        accs = [accs1, [acc4[c, rows, :] for c in range(ATT_CHUNKS)], [acc16[c, rows, :] for c in range(ATT_CHUNKS)]]
        lane = lax.broadcasted_iota(jnp.int32, (BLK, LANES), 1)
        m_all = jnp.maximum(jnp.maximum(mls[0], mls[1]), mls[2])
        ws = [jnp.exp(ml - m_all) for ml in mls]
        den = sum(w * pltpu.roll(ml, LANES - H, 1) for w, ml in zip(ws, mls))
        ao = [jnp.zeros((BLK, LANES), F32) for _ in range(ATT_CHUNKS)]
        for w, acc in zip(ws, accs):
            coef = _split3_dot(jnp.where(lane < H, w / den, 0.0), e_ref[...])
            for c in range(ATT_CHUNKS):
                ao[c] = ao[c] + coef[:, c * LANES:(c + 1) * LANES] * acc[c]
        ao_ref[0, rows, :] = jnp.concatenate(ao, axis=-1).astype(ao_ref.dtype)
        return carry

    lax.fori_loop(0, nblk, dilated, 0)
    lax.fori_loop(0, nblk, dense_and_merge, 0)


def _prompt_attention(aq, ak, av, rel_bias):
    B, S, A = aq.shape
    CH = ATT_CHUNKS
    assert [d for _, d in DIL_PATTERNS] == [1, 4, 16] and S // ATT_BLOCK == 16
    b1, b4, b16 = _prompt_bias_tables(rel_bias)
    head_of_lane = jnp.arange(A)[None, :] // HEAD_DIM
    expand = (jnp.arange(LANES)[:, None] == head_of_lane).astype(BF16)
    seq = pl.BlockSpec((1, S, A), lambda b: (b, 0, 0))
    const = lambda shape: pl.BlockSpec(shape, lambda b: (0,) * len(shape), pipeline_mode=pl.Buffered(1))
    chunked = pltpu.VMEM((CH, S, LANES), F32)
    stats = pltpu.VMEM((S, LANES), F32)
    return pl.pallas_call(
        _att_kernel,
        grid=(B,),
        in_specs=[seq, seq, seq, const(b1.shape), const(b4.shape), const(b16.shape), const(expand.shape)],
        out_specs=seq,
        out_shape=jax.ShapeDtypeStruct((B, S, A), BF16),
        scratch_shapes=[chunked, chunked, chunked, chunked, stats, chunked, stats],
        compiler_params=_cparams(("arbitrary",), 58),
        name="prompt_att",
    )(aq, ak, av, b1, b4, b16, expand)


def _samp_att_kernel(lows, T, kT_ref, vT_ref, qn_ref, kn_ref, vn_ref, c1_ref, c4_ref, c16_ref,
                     n1_ref, n4_ref, n16_ref, koT_ref, voT_ref, ao_ref):
    H, Dh, L = kT_ref.shape[1:]
    A = H * Dh
    zpad = jnp.zeros((LANES - T, A), F32)
    tail_lane = lax.broadcasted_iota(jnp.int32, (A, LANES), 1) >= LANES - T

    def shift_in(xT_ref, new_ref, out_ref):
        xT = xT_ref[0].reshape(A, L)
        new_p = jnp.concatenate([zpad, new_ref[0]], axis=0)
        rolled = pltpu.roll(xT, L - T, 1)
        tail = jnp.where(tail_lane, new_p.T, rolled[:, L - LANES:])
        out_ref[0] = jnp.concatenate([rolled[:, :L - LANES], tail], axis=1).reshape(H, Dh, L)
        return xT.astype(BF16), new_p.astype(BF16)

    kT, kn = shift_in(kT_ref, kn_ref, koT_ref)
    vT, vn = shift_in(vT_ref, vn_ref, voT_ref)
    HT = H * T
    row_head = lax.broadcasted_iota(jnp.int32, (HT, A), 0) // T
    lane_head = lax.broadcasted_iota(jnp.int32, (HT, A), 1) // HEAD_DIM
    diag = row_head == lane_head
    q_rows = jnp.where(diag, jnp.concatenate([qn_ref[0]] * H, axis=0), 0.0).astype(BF16)
    s_cache = _dot(q_rows, kT)
    s_new = _dot_nt(q_rows, kn)
    stats = []
    for lo, c_ref, n_ref in zip(lows, (c1_ref, c4_ref, c16_ref), (n1_ref, n4_ref, n16_ref)):
        sc = s_cache[:, lo:] + c_ref[...]
        sn = s_new + n_ref[...]
        m = jnp.maximum(jnp.max(sc, axis=-1, keepdims=True), jnp.max(sn, axis=-1, keepdims=True))
        pc = jnp.exp(sc - m)
        pn = jnp.exp(sn - m)
        l = jnp.sum(pc, axis=-1, keepdims=True) + jnp.sum(pn, axis=-1, keepdims=True)
        acc = _dot_nt(pc.astype(BF16), vT[:, lo:]) + _dot(pn.astype(BF16), vn)
        stats.append((m, l, acc))
    m_all = jnp.maximum(jnp.maximum(stats[0][0], stats[1][0]), stats[2][0])
    num = jnp.zeros((HT, A), F32)
    den = jnp.zeros((HT, 1), F32)
    for m, l, acc in stats:
        w = jnp.exp(m - m_all)
        num = num + w * acc
        den = den + w * l
    comb = jnp.where(diag, num / den, 0.0)
    out = comb[0:T]
    for h in range(1, H):
        out = out + comb[h * T:(h + 1) * T]
    ao_ref[0] = out


def _sample_window_operands(aq, ak, av, cache_kT, cache_vT, rel_bias, groups):
    B, T, A = aq.shape
    _, H, Dh, L = cache_kT.shape
    Hg, Ag = H // groups, A // groups
    lows = tuple(L - win for win, _ in DIL_PATTERNS)
    assert all(lo >= 0 and lo % LANES == 0 for lo in lows) and T <= LANES and Ag % LANES == 0
    tabs = _sample_bias_tables(rel_bias, L, T)
    big = pl.BlockSpec((1, Hg, Dh, L), lambda i: (i // groups, i % groups, 0, 0))
    small = pl.BlockSpec((1, T, Ag), lambda i: (i // groups, 0, i % groups))
    tab = lambda t: pl.BlockSpec((Hg * T, t.shape[1]), lambda i: (i % groups, 0))
    win_sds = jax.ShapeDtypeStruct((B, H, Dh, L), F32)
    args = [cache_kT, cache_vT, aq, ak, av, *tabs]
    in_specs = [big, big, small, small, small] + [tab(t) for t in tabs]
    out_specs = [big, big, small]
    out_shape = [win_sds, win_sds, jax.ShapeDtypeStruct((B, T, A), F32)]
    return functools.partial(_samp_att_kernel, lows, T), args, in_specs, out_specs, out_shape


def _out_kernel(d_ff, ff_chunk, x_ref, ro_ref, ao_ref, wo_ref, wgu_ref, wd_ref, g_pm, g_pf, g_of, y_ref, act_ref):
    R = RET_WIDTH
    mix = _dot(ro_ref[...].astype(BF16), wo_ref[:R, :]) + _dot(ao_ref[...].astype(BF16), wo_ref[R:, :])
    x1 = x_ref[...] + _rms(mix, g_pm[...])
    h = _rms(x1, g_pf[...]).astype(BF16)
    for c in range(0, d_ff, ff_chunk):
        gate = _dot(h, wgu_ref[:, c:c + ff_chunk])
        up = _dot(h, wgu_ref[:, d_ff + c:d_ff + c + ff_chunk])
        act_ref[:, c:c + ff_chunk] = (gate * jax.nn.sigmoid(gate) * up).astype(BF16)
    f = _dot(act_ref[...], wd_ref[...])
    y_ref[...] = x1 + _rms(f, g_of[...])


def _out_window_kernel(ffn, window, n_ffn_in, n_win_in, *refs):
    ffn_in = refs[:n_ffn_in]
    win_in = refs[n_ffn_in:n_ffn_in + n_win_in]
    y_ref, koT_ref, voT_ref, sao_ref, act_ref = refs[n_ffn_in + n_win_in:]
    window(*win_in, koT_ref, voT_ref, sao_ref)
    ffn(*ffn_in, y_ref, act_ref)


def _out_block(x, ro, ao, wo_bf, wgu_bf, wd_bf, g_post_mix, g_pre_ffn, g_post_ffn, tm, window=None):
    N, D = x.shape
    d_ff = wd_bf.shape[0]
    row = lambda width: pl.BlockSpec((tm, width), lambda i: (i, 0))
    const = lambda shape: pl.BlockSpec(shape, lambda i: (0,) * len(shape), pipeline_mode=pl.Buffered(1))
    args = [x, ro, ao, wo_bf, wgu_bf, wd_bf, g_post_mix.reshape(1, D), g_pre_ffn.reshape(1, D),
            g_post_ffn.reshape(1, D)]
    in_specs = [row(D), row(RET_WIDTH), row(ATT_WIDTH), const(wo_bf.shape), const(wgu_bf.shape),
                const(wd_bf.shape), const((1, D)), const((1, D)), const((1, D))]
    body = functools.partial(_out_kernel, d_ff, 256)
    out_specs, out_shape = [row(D)], [jax.ShapeDtypeStruct((N, D), F32)]
    if window is not None:
        win_body, win_args, win_in_specs, win_out_specs, win_out_shape = window
        assert N // tm == win_out_shape[0].shape[0] * (win_out_shape[0].shape[1] // win_out_specs[0].block_shape[1])
        body = functools.partial(_out_window_kernel, body, win_body, len(args), len(win_args))
        args, in_specs = args + win_args, in_specs + win_in_specs
        out_specs, out_shape = out_specs + win_out_specs, out_shape + win_out_shape
    outs = pl.pallas_call(
        body,
        grid=(N // tm,),
        in_specs=in_specs,
        out_specs=out_specs,
        out_shape=out_shape,
        scratch_shapes=[pltpu.VMEM((tm, d_ff), BF16)],
        compiler_params=_cparams(("arbitrary",), 56),
        name="out_ffn",
    )(*args)
    return outs[0] if window is None else outs


def kernel(x_prompt, x_sample, state_ret, cache_k_win, cache_v_win, rel_bias, w_in, g_ret, w_out,
           g_pre_mix, g_post_mix, g_pre_ffn, g_post_ffn, w_gu, w_down):
    depth = w_in.shape[0]
    assert depth == 1
    B, S, D = x_prompt.shape
    Bs, Ts, _ = x_sample.shape
    H, Dh = N_ATT_HEADS, HEAD_DIM
    l = 0
    w_in_bf = w_in[l].astype(BF16)
    wo_bf = w_out[l].astype(BF16)
    wgu_bf = w_gu[l].astype(BF16)
    wd_bf = w_down[l].astype(BF16)

    pos_s = PAST_LEN + jnp.arange(Ts, dtype=jnp.int32)
    N_s = Bs * Ts
    outs = _proj(x_sample.reshape(N_s, D), g_pre_mix[l], w_in_bf, jnp.tile(pos_s, Bs))
    srq, srk, srv, srg, saq, sak, sav = [t.reshape(Bs, Ts, RET_WIDTH) for t in outs]
    sro, s_s = _retention_step(srq, srk, srv, srg, g_ret[l], state_ret[l])
    to_t = lambda t: jnp.transpose(t, (0, 2, 3, 1))
    from_t = lambda t: jnp.transpose(t, (0, 3, 1, 2))
    window = _sample_window_operands(saq, sak, sav, to_t(cache_k_win[l]), to_t(cache_v_win[l]), rel_bias, 2)

    ro, s_p, aq, ak, av = _proj_retention(x_prompt, g_pre_mix[l], w_in_bf, g_ret[l], 512)
    ao = _prompt_attention(aq, ak, av, rel_bias)
    y_p, k_sT, v_sT, sao = _out_block(x_prompt.reshape(B * S, D), ro.reshape(B * S, RET_WIDTH),
                                      ao.reshape(B * S, ATT_WIDTH), wo_bf, wgu_bf, wd_bf, g_post_mix[l],
                                      g_pre_ffn[l], g_post_ffn[l], 256, window)
    k_s, v_s = from_t(k_sT), from_t(v_sT)

    y_s = _out_block(x_sample.reshape(N_s, D), sro.reshape(N_s, RET_WIDTH), sao.reshape(N_s, ATT_WIDTH),
                     wo_bf, wgu_bf, wd_bf, g_post_mix[l], g_pre_ffn[l], g_post_ffn[l], N_s)

    return (y_p.reshape(B, S, D), y_s.reshape(Bs, Ts, D),
            s_p[None], ak.reshape(1, B, S, H, Dh), av.reshape(1, B, S, H, Dh),
            s_s[None], k_s[None], v_s[None])
```

```python
import functools
import math

import jax
import jax.numpy as jnp
from jax import lax
from jax.experimental import pallas as pl
from jax.experimental.pallas import tpu as pltpu

F32 = jnp.float32
BF16 = jnp.bfloat16

HEAD_DIM = 64
N_RET_HEADS = 8
N_ATT_HEADS = 8
RET_WIDTH = N_RET_HEADS * HEAD_DIM
ATT_WIDTH = N_ATT_HEADS * HEAD_DIM
RET_CHUNK = 128
DIL_PATTERNS = ((128, 1), (512, 4), (2048, 16))
ATT_BLOCK = 128
N_BUCKETS = 32
MAX_DISTANCE = 2048
ROPE_BASE = 10000.0
NORM_EPS = 1e-6
GN_EPS = 1e-5
PAST_LEN = 16384
LANES = 128
ATT_CHUNKS = ATT_WIDTH // LANES
NEG_INF = float("-inf")


def _cparams(sem, vmem_mb):
    return pltpu.CompilerParams(dimension_semantics=sem, vmem_limit_bytes=vmem_mb * 1024 * 1024)


def _rms(x, g):
    return x * lax.rsqrt(jnp.mean(x * x, axis=-1, keepdims=True) + NORM_EPS) * g


def _dot(a, b):
    return jnp.dot(a, b, preferred_element_type=F32)


def _dot_nt(a, b):
    return lax.dot_general(a, b, (((1,), (1,)), ((), ())), preferred_element_type=F32)


def _dot_tn(a, b):
    return lax.dot_general(a, b, (((0,), (0,)), ((), ())), preferred_element_type=F32)


def _rope_tables(pos):
    inv = ROPE_BASE ** (-jnp.arange(0, HEAD_DIM, 2, dtype=F32) / HEAD_DIM)
    ang = pos.astype(F32)[:, None] * inv[None, :]
    cos, sin = jnp.cos(ang), jnp.sin(ang)
    zero = jnp.zeros_like(sin)
    rep = LANES // HEAD_DIM
    cosf = jnp.tile(jnp.concatenate([cos, cos], axis=-1), (1, rep))
    sa = jnp.tile(jnp.concatenate([-sin, zero], axis=-1), (1, rep))
    sb = jnp.tile(jnp.concatenate([zero, sin], axis=-1), (1, rep))
    return cosf, sa, sb


def _retention_consts(C):
    log_g = jnp.log1p(-jnp.exp2(-5.0 - jnp.arange(N_RET_HEADS, dtype=F32)))
    i = jnp.arange(C, dtype=F32)
    rel = i[:, None] - i[None, :]
    decay = jnp.where(rel[None] >= 0, jnp.exp(jnp.maximum(rel, 0.0)[None] * log_g[:, None, None]), 0.0)
    cross = jnp.exp((i + 1.0)[:, None] * log_g[None, :])
    kdec = jnp.exp((C - 1.0 - i)[:, None] * log_g[None, :])
    sdec = jnp.exp(C * log_g)[None, :]
    expand = lambda t: jnp.repeat(t, HEAD_DIM, axis=-1)
    return decay, expand(cross), expand(kdec), expand(sdec)


def _t5_bucket(dist):
    max_exact = N_BUCKETS // 2
    d_f = jnp.maximum(dist, 1).astype(F32)
    large = max_exact + (jnp.log(d_f / max_exact) / math.log(MAX_DISTANCE / max_exact)
                         * (N_BUCKETS - max_exact)).astype(jnp.int32)
    large = jnp.minimum(large, N_BUCKETS - 1)
    return jnp.where(dist < max_exact, dist, large)


def _pattern_bias_rev(rel_bias, dil, nk):
    dist = jnp.arange(nk, -1, -1, dtype=jnp.int32) * dil
    return rel_bias[_t5_bucket(dist)].astype(F32).T


def _prompt_bias_tables(rel_bias):
    BLK, H = ATT_BLOCK, N_ATT_HEADS
    out = []
    for win, dil in DIL_PATTERNS:
        nk = win // dil
        assert nk == BLK
        period = 3 * BLK
        v = jnp.concatenate([_pattern_bias_rev(rel_bias, dil, nk),
                             jnp.full((H, period - nk - 1), NEG_INF, F32)], axis=1)
        flat = jnp.tile(v, (1, BLK))[:, :BLK * (period - 1)]
        skew = flat.reshape(H, BLK, period - 1)[:, :, :2 * BLK]
        out.append(skew.reshape(H * BLK, 2 * BLK))
    return out


def _sample_bias_tables(rel_bias, L, T):
    H = N_ATT_HEADS
    cache_part, new_part = [], []
    for win, dil in DIL_PATTERNS:
        nk = win // dil
        rev = _pattern_bias_rev(rel_bias, dil, nk)
        if dil > 1:
            gaps = jnp.full((H, nk + 1, dil - 1), NEG_INF, F32)
            rev = jnp.concatenate([rev[:, :, None], gaps], axis=2).reshape(H, (nk + 1) * dil)[:, :nk * dil + 1]
        pv = jnp.concatenate([jnp.full((H, T), NEG_INF, F32), rev, jnp.full((H, 2 * T), NEG_INF, F32)], axis=1)
        width = win + T
        rows = jnp.stack([pv[:, T - t:T - t + width] for t in range(T)], axis=1)
        cache_part.append(rows[:, :, :win].reshape(H * T, win))
        new = jnp.concatenate([jnp.full((H, T, LANES - T), NEG_INF, F32), rows[:, :, win:]], axis=2)
        new_part.append(new.reshape(H * T, LANES))
    return cache_part + new_part


def _proj_steps(x, g_ref, w_ref, cos, sa, sb):
    h = _rms(x, g_ref[...]).astype(BF16)
    W = 2 * LANES
    rep = W // LANES
    cosf = jnp.concatenate([cos] * rep, axis=-1)
    saf = jnp.concatenate([sa] * rep, axis=-1)
    sbf = jnp.concatenate([sb] * rep, axis=-1)
    half = HEAD_DIM // 2

    def col(c):
        return _dot(h, w_ref[:, c:c + W])

    def rope(z):
        return z * cosf + pltpu.roll(z, W - half, 1) * saf + pltpu.roll(z, half, 1) * sbf

    R, A = RET_WIDTH, ATT_WIDTH
    scale = HEAD_DIM ** -0.5
    steps = []
    for c in range(0, R, W):
        steps.append(lambda c=c: rope(col(c)))
    for c in range(R, 2 * R, W):
        steps.append(lambda c=c: rope(col(c)) * scale)
    for c in range(2 * R, 4 * R, W):
        steps.append(lambda c=c: col(c))
    for c in range(4 * R, 4 * R + A, W):
        steps.append(lambda c=c: col(c) * scale)
    for c in range(4 * R + A, 4 * R + 3 * A, W):
        steps.append(lambda c=c: col(c))
    return steps


def _join_groups(pieces):
    return [jnp.concatenate(pieces[i:i + 2], axis=-1) for i in range(0, len(pieces), 2)]


def _proj_kernel(x_ref, g_ref, w_ref, cos_ref, sa_ref, sb_ref, *outs):
    steps = _proj_steps(x_ref[...], g_ref, w_ref, cos_ref[...], sa_ref[...], sb_ref[...])
    for ref, z in zip(outs, _join_groups([step() for step in steps])):
        ref[...] = z


def _proj(x, g, w_bf, pos):
    N, D = x.shape
    tabs = _rope_tables(pos)
    full = lambda shape: pl.BlockSpec(shape, lambda i: (0,) * len(shape))
    return pl.pallas_call(
        _proj_kernel,
        grid=(1,),
        in_specs=[full((N, D)), full((1, D)), full(w_bf.shape)] + [full(t.shape) for t in tabs],
        out_specs=[full((N, RET_WIDTH))] * 7,
        out_shape=[jax.ShapeDtypeStruct((N, RET_WIDTH), F32)] * 7,
        compiler_params=_cparams(("arbitrary",), 40),
        name="proj",
    )(x, g.reshape(1, D), w_bf, *tabs)


def _proj_ret_kernel(tiles_per_seq, x_ref, g_ref, w_ref, cos_ref, sa_ref, sb_ref, dec_ref, cd_ref, kd_ref,
                     sd_ref, gr_ref, ro_ref, s_out_ref, aq_ref, ak_ref, av_ref, sbd_ref, qkv_ref, gate_ref):
    s = pl.program_id(0)
    n_tiles = pl.num_programs(0) - 1
    tm = x_ref.shape[1]
    C = RET_CHUNK
    cur = s % 2
    prv = 1 - cur

    @pl.when(s == 0)
    def _():
        sbd_ref[...] = jnp.zeros_like(sbd_ref)
        qkv_ref[1] = jnp.zeros(qkv_ref.shape[1:], qkv_ref.dtype)
        gate_ref[1] = jnp.zeros(gate_ref.shape[1:], gate_ref.dtype)

    lane = lax.broadcasted_iota(jnp.int32, (C, LANES), 1)
    half = lane < HEAD_DIM
    same_head = (lax.broadcasted_iota(jnp.int32, (LANES, LANES), 0) // HEAD_DIM
                 == lax.broadcasted_iota(jnp.int32, (LANES, LANES), 1) // HEAD_DIM)
    inv_d = 1.0 / HEAD_DIM
    starts_seq = (s - 1) % tiles_per_seq == 0

    def half_mean(t):
        lo = jnp.sum(jnp.where(half, t, 0.0), axis=-1, keepdims=True)
        hi = jnp.sum(jnp.where(half, 0.0, t), axis=-1, keepdims=True)
        return jnp.where(half, lo, hi) * inv_d

    n_pairs = RET_WIDTH // LANES
    n_chunks = tm // C
    state = [jnp.where(starts_seq, 0.0, sbd_ref[c]) for c in range(n_pairs)]

    def retention_unit(c, cc):
        ls = slice(c * LANES, (c + 1) * LANES)
        rs = slice(cc * C, (cc + 1) * C)
        S = state[c]
        qb = qkv_ref[prv, 0, rs, ls]
        kb = qkv_ref[prv, 1, rs, ls]
        vb = qkv_ref[prv, 2, rs, ls]
        q = qb.astype(F32)
        q_pair = jnp.concatenate([jnp.where(half, q, 0.0), jnp.where(half, 0.0, q)], axis=0).astype(BF16)
        inner = _dot_nt(q_pair, kb) * dec_ref[c]
        o2 = _dot(inner.astype(BF16), vb)
        o = jnp.where(half, o2[:C], o2[C:]) + _dot(qb, S.astype(BF16)) * cd_ref[:, ls]
        kd = (kb.astype(F32) * kd_ref[:, ls]).astype(BF16)
        state[c] = S * sd_ref[:, ls] + jnp.where(same_head, _dot_tn(kd, vb), 0.0)
        mu = half_mean(o)
        d = o - mu
        var = half_mean(d * d)
        on = d * lax.rsqrt(var + GN_EPS) * gr_ref[:, ls]
        gate = gate_ref[prv, rs, ls]
        ro_ref[0, rs, ls] = (on * (gate * jax.nn.sigmoid(gate))).astype(ro_ref.dtype)

    j = jnp.minimum(s, n_tiles - 1) % tiles_per_seq
    rows = pl.ds(pl.multiple_of(j * tm, tm), tm)
    steps = _proj_steps(x_ref[0], g_ref, w_ref, cos_ref[rows, :], sa_ref[rows, :], sb_ref[rows, :])
    units = [(c, cc) for cc in range(n_chunks) for c in range(n_pairs)]
    cols = []
    for i, step in enumerate(steps):
        cols.append(step())
        for c, cc in units[i * len(units) // len(steps):(i + 1) * len(units) // len(steps)]:
            retention_unit(c, cc)
    for c in range(n_pairs):
        S = state[c]
        sbd_ref[c] = S
        s_out_ref[0, 2 * c] = S[:HEAD_DIM, :HEAD_DIM]
        s_out_ref[0, 2 * c + 1] = S[HEAD_DIM:, HEAD_DIM:]
    rq, rk, rv, rg, aq, ak, av = _join_groups(cols)
    aq_ref[0] = aq.astype(aq_ref.dtype)
    ak_ref[0] = ak
    av_ref[0] = av
    qkv_ref[cur, 0] = rq.astype(BF16)
    qkv_ref[cur, 1] = rk.astype(BF16)
    qkv_ref[cur, 2] = rv.astype(BF16)
    gate_ref[cur] = rg


def _proj_retention(x, g, w_bf, g_ret, tm):
    B, T, D = x.shape
    R, A, H, Dh = RET_WIDTH, ATT_WIDTH, N_RET_HEADS, HEAD_DIM
    tps = T // tm
    n_tiles = B * tps
    tabs = _rope_tables(jnp.arange(T, dtype=jnp.int32))
    dec, cd, kd, sd = _retention_consts(RET_CHUNK)
    dec_pair = dec.reshape(H // 2, 2 * RET_CHUNK, RET_CHUNK)

    def this_tile(width):
        def index(s):
            t = jnp.minimum(s, n_tiles - 1)
            return (t // tps, t % tps, 0)
        return pl.BlockSpec((1, tm, width), index)

    def prev_tile(s):
        t = jnp.maximum(s - 1, 0)
        return (t // tps, t % tps, 0)

    full = lambda shape: pl.BlockSpec(shape, lambda s: (0,) * len(shape))
    consts = [g.reshape(1, D), w_bf, *tabs, dec_pair, cd, kd, sd, g_ret.reshape(1, R)]
    return pl.pallas_call(
        functools.partial(_proj_ret_kernel, tps),
        grid=(n_tiles + 1,),
        in_specs=[this_tile(D)] + [full(t.shape) for t in consts],
        out_specs=[pl.BlockSpec((1, tm, R), prev_tile),
                   pl.BlockSpec((1, H, Dh, Dh), lambda s: (jnp.maximum(s - 1, 0) // tps, 0, 0, 0)),
                   this_tile(A), this_tile(A), this_tile(A)],
        out_shape=[jax.ShapeDtypeStruct((B, T, R), BF16), jax.ShapeDtypeStruct((B, H, Dh, Dh), F32),
                   jax.ShapeDtypeStruct((B, T, A), BF16), jax.ShapeDtypeStruct((B, T, A), F32),
                   jax.ShapeDtypeStruct((B, T, A), F32)],
        scratch_shapes=[pltpu.VMEM((R // LANES, LANES, LANES), F32), pltpu.VMEM((2, 3, tm, R), BF16),
                        pltpu.VMEM((2, tm, R), F32)],
        compiler_params=_cparams(("arbitrary",), 56),
        name="proj_retention",
    )(x, *consts)


def _ret_kernel(q_ref, k_ref, v_ref, gate_ref, dec_ref, cd_ref, kd_ref, sd_ref, gr_ref, s0_ref, o_ref, s_ref):
    for b in range(q_ref.shape[0]):
        q = q_ref[b].astype(BF16)
        k = k_ref[b].astype(BF16)
        v = v_ref[b].astype(BF16)
        for h in range(N_RET_HEADS):
            sl = slice(h * HEAD_DIM, (h + 1) * HEAD_DIM)
            qh, kh, vh = q[:, sl], k[:, sl], v[:, sl]
            S = s0_ref[b, h]
            inner = _dot_nt(qh, kh) * dec_ref[h]
            o = _dot(inner.astype(BF16), vh) + _dot(qh, S.astype(BF16)) * cd_ref[:, sl]
            kd = (kh.astype(F32) * kd_ref[:, sl]).astype(BF16)
            s_ref[b, h] = S * sd_ref[:, sl] + _dot_tn(kd, vh)
            mu = jnp.mean(o, axis=-1, keepdims=True)
            var = jnp.mean(jnp.square(o - mu), axis=-1, keepdims=True)
            on = (o - mu) * lax.rsqrt(var + GN_EPS) * gr_ref[:, sl]
            gate = gate_ref[b, :, sl]
            o_ref[b, :, sl] = on * (gate * jax.nn.sigmoid(gate))


def _retention_step(rq, rk, rv, rg, g_ret, state):
    B, T, R = rq.shape
    H, Dh = N_RET_HEADS, HEAD_DIM
    consts = [*_retention_consts(T), g_ret.reshape(1, R)]
    bb = math.gcd(B, 8)
    row = pl.BlockSpec((bb, T, R), lambda b: (b, 0, 0))
    full = lambda shape: pl.BlockSpec(shape, lambda b: (0,) * len(shape))
    st = pl.BlockSpec((bb, H, Dh, Dh), lambda b: (b, 0, 0, 0))
    return pl.pallas_call(
        _ret_kernel,
        grid=(B // bb,),
        in_specs=[row, row, row, row] + [full(t.shape) for t in consts] + [st],
        out_specs=[row, st],
        out_shape=[jax.ShapeDtypeStruct((B, T, R), F32), jax.ShapeDtypeStruct((B, H, Dh, Dh), F32)],
        compiler_params=_cparams(("arbitrary",), 32),
        name="retention",
    )(rq, rk, rv, rg, *consts, state)


def _split3_dot(c, e):
    c1 = c.astype(BF16)
    r1 = c - c1.astype(F32)
    c2 = r1.astype(BF16)
    c3 = (r1 - c2.astype(F32)).astype(BF16)
    return _dot(c1, e) + _dot(c2, e) + _dot(c3, e)


def _rows(ref, start, dil, c):
    idx = pl.ds(start, ATT_BLOCK) if dil == 1 else pl.ds(start, ATT_BLOCK, stride=dil)
    return ref[c, idx, :]


def _att_block(q_ref, k_ref, v_ref, bias_ref, dil, cur, prev, first):
    BLK, H = ATT_BLOCK, N_ATT_HEADS
    half = lax.broadcasted_iota(jnp.int32, (BLK, LANES), 1) < HEAD_DIM
    s_parts = []
    for c in range(ATT_CHUNKS):
        q = _rows(q_ref, cur, dil, c)
        q_pair = jnp.concatenate([jnp.where(half, q, 0.0), jnp.where(half, 0.0, q)], axis=0).astype(BF16)
        k = _rows(k_ref, cur, dil, c)
        if prev is not None:
            k = jnp.concatenate([_rows(k_ref, prev, dil, c), k], axis=0)
        s_parts.append(_dot_nt(q_pair, k.astype(BF16)))
    s = jnp.concatenate(s_parts, axis=0)
    if prev is not None:
        s = s + bias_ref[...]
        col = lax.broadcasted_iota(jnp.int32, s.shape, 1)
        s = jnp.where(jnp.logical_and(first, col < BLK), NEG_INF, s)
    else:
        s = s + bias_ref[:, BLK:]
    m = jnp.max(s, axis=-1, keepdims=True)
    p = jnp.exp(s - m)
    l = jnp.sum(p, axis=-1, keepdims=True)
    pb = p.astype(BF16)
    accs = []
    for c in range(ATT_CHUNKS):
        v = _rows(v_ref, cur, dil, c)
        if prev is not None:
            v = jnp.concatenate([_rows(v_ref, prev, dil, c), v], axis=0)
        o = _dot(pb[2 * c * BLK:(2 * c + 2) * BLK], v.astype(BF16))
        accs.append(jnp.where(half, o[:BLK], o[BLK:]))
    lane = lax.broadcasted_iota(jnp.int32, (BLK, LANES), 1)
    ml = jnp.zeros((BLK, LANES), F32)
    for h in range(H):
        ml = jnp.where(lane == h, m[h * BLK:(h + 1) * BLK], ml)
        ml = jnp.where(lane == H + h, l[h * BLK:(h + 1) * BLK], ml)
    return accs, ml


def _att_kernel(q_ref, k_ref, v_ref, b1_ref, b4_ref, b16_ref, e_ref, ao_ref, qc, kc, vc, acc4, ml4, acc16, ml16):
    BLK, H = ATT_BLOCK, N_ATT_HEADS
    S = q_ref.shape[1]
    nblk = S // BLK
    d4, d16 = DIL_PATTERNS[1][1], DIL_PATTERNS[2][1]
    nb4 = nblk // d4
    for c in range(ATT_CHUNKS):
        ls = slice(c * LANES, (c + 1) * LANES)
        qc[c] = q_ref[0, :, ls].astype(F32)
        kc[c] = k_ref[0, :, ls]
        vc[c] = v_ref[0, :, ls]

    def dilated(n, carry):
        r, j = n // nb4, n % nb4
        cur = r + j * (BLK * d4)
        prev = r + jnp.maximum(j - 1, 0) * (BLK * d4)
        accs, ml = _att_block(qc, kc, vc, b4_ref, d4, cur, prev, j == 0)
        for c in range(ATT_CHUNKS):
            acc4[c, pl.ds(cur, BLK, stride=d4), :] = accs[c]
        ml4[pl.ds(cur, BLK, stride=d4), :] = ml
        accs, ml = _att_block(qc, kc, vc, b16_ref, d16, n, None, None)
        for c in range(ATT_CHUNKS):
            acc16[c, pl.ds(n, BLK, stride=d16), :] = accs[c]
        ml16[pl.ds(n, BLK, stride=d16), :] = ml
        return carry

    def dense_and_merge(j, carry):
        cur = pl.multiple_of(j * BLK, BLK)
        prev = pl.multiple_of(jnp.maximum(j - 1, 0) * BLK, BLK)
        accs1, ml1 = _att_block(qc, kc, vc, b1_ref, 1, cur, prev, j == 0)
        rows = pl.ds(cur, BLK)
        mls = [ml1, ml4[rows, :], ml16[rows, :]]
        accs = [accs1, [acc4[c, rows, :] for c in range(ATT_CHUNKS)], [acc16[c, rows, :] for c in range(ATT_CHUNKS)]]
        lane = lax.broadcasted_iota(jnp.int32, (BLK, LANES), 1)
        m_all = jnp.maximum(jnp.maximum(mls[0], mls[1]), mls[2])
        ws = [jnp.exp(ml - m_all) for ml in mls]
        den = sum(w * pltpu.roll(ml, LANES - H, 1) for w, ml in zip(ws, mls))
        ao = [jnp.zeros((BLK, LANES), F32) for _ in range(ATT_CHUNKS)]
        for w, acc in zip(ws, accs):
            coef = _split3_dot(jnp.where(lane < H, w / den, 0.0), e_ref[...])
            for c in range(ATT_CHUNKS):
                ao[c] = ao[c] + coef[:, c * LANES:(c + 1) * LANES] * acc[c]
        ao_ref[0, rows, :] = jnp.concatenate(ao, axis=-1).astype(ao_ref.dtype)
        return carry

    lax.fori_loop(0, nblk, dilated, 0)
    lax.fori_loop(0, nblk, dense_and_merge, 0)


def _prompt_attention(aq, ak, av, rel_bias):
    B, S, A = aq.shape
    CH = ATT_CHUNKS
    assert [d for _, d in DIL_PATTERNS] == [1, 4, 16] and S // ATT_BLOCK == 16
    b1, b4, b16 = _prompt_bias_tables(rel_bias)
    head_of_lane = jnp.arange(A)[None, :] // HEAD_DIM
    expand = (jnp.arange(LANES)[:, None] == head_of_lane).astype(BF16)
    seq = pl.BlockSpec((1, S, A), lambda b: (b, 0, 0))
    const = lambda shape: pl.BlockSpec(shape, lambda b: (0,) * len(shape), pipeline_mode=pl.Buffered(1))
    chunked = pltpu.VMEM((CH, S, LANES), F32)
    stats = pltpu.VMEM((S, LANES), F32)
    return pl.pallas_call(
        _att_kernel,
        grid=(B,),
        in_specs=[seq, seq, seq, const(b1.shape), const(b4.shape), const(b16.shape), const(expand.shape)],
        out_specs=seq,
        out_shape=jax.ShapeDtypeStruct((B, S, A), BF16),
        scratch_shapes=[chunked, chunked, chunked, chunked, stats, chunked, stats],
        compiler_params=_cparams(("arbitrary",), 58),
        name="prompt_att",
    )(aq, ak, av, b1, b4, b16, expand)


def _window_pieces(lows, T, kT_ref, vT_ref, qn_ref, kn_ref, vn_ref, c1_ref, c4_ref, c16_ref,
                   n1_ref, n4_ref, n16_ref, koT_ref, voT_ref, ao_ref):
    H, Dh, L = kT_ref.shape[1:]
    A = H * Dh
    zpad = jnp.zeros((LANES - T, A), F32)
    tail_lane = lax.broadcasted_iota(jnp.int32, (Dh, LANES), 1) >= LANES - T
    shifted = {}

    def shift_in(name, xT_ref, new_ref, out_ref):
        new_p = jnp.concatenate([zpad, new_ref[0]], axis=0)
        new_t = new_p.T
        heads = []
        for h in range(H):
            x = xT_ref[0, h]
            rolled = pltpu.roll(x, L - T, 1)
            tail = jnp.where(tail_lane, new_t[h * Dh:(h + 1) * Dh], rolled[:, L - LANES:])
            out_ref[0, h] = jnp.concatenate([rolled[:, :L - LANES], tail], axis=1)
            heads.append(x.astype(BF16))
            yield
        shifted[name] = (jnp.concatenate(heads, axis=0), new_p.astype(BF16))

    yield from shift_in("k", kT_ref, kn_ref, koT_ref)
    yield from shift_in("v", vT_ref, vn_ref, voT_ref)
    (kT, kn), (vT, vn) = shifted["k"], shifted["v"]
    HT = H * T
    row_head = lax.broadcasted_iota(jnp.int32, (HT, A), 0) // T
    lane_head = lax.broadcasted_iota(jnp.int32, (HT, A), 1) // HEAD_DIM
    diag = row_head == lane_head
    q_rows = jnp.where(diag, jnp.concatenate([qn_ref[0]] * H, axis=0), 0.0).astype(BF16)
    s_cache = _dot(q_rows, kT)
    s_new = _dot_nt(q_rows, kn)
    yield
    stats = []
    for lo, c_ref, n_ref in zip(lows, (c1_ref, c4_ref, c16_ref), (n1_ref, n4_ref, n16_ref)):
        sc = s_cache[:, lo:] + c_ref[...]
        sn = s_new + n_ref[...]
        m = jnp.maximum(jnp.max(sc, axis=-1, keepdims=True), jnp.max(sn, axis=-1, keepdims=True))
        pc = jnp.exp(sc - m)
        pn = jnp.exp(sn - m)
        l = jnp.sum(pc, axis=-1, keepdims=True) + jnp.sum(pn, axis=-1, keepdims=True)
        acc = _dot_nt(pc.astype(BF16), vT[:, lo:]) + _dot(pn.astype(BF16), vn)
        stats.append((m, l, acc))
        yield
    m_all = jnp.maximum(jnp.maximum(stats[0][0], stats[1][0]), stats[2][0])
    num = jnp.zeros((HT, A), F32)
    den = jnp.zeros((HT, 1), F32)
    for m, l, acc in stats:
        w = jnp.exp(m - m_all)
        num = num + w * acc
        den = den + w * l
    comb = jnp.where(diag, num / den, 0.0)
    out = comb[0:T]
    for h in range(1, H):
        out = out + comb[h * T:(h + 1) * T]
    ao_ref[0] = out
    yield


def _sample_window_operands(aq, ak, av, cache_kT, cache_vT, rel_bias, groups):
    B, T, A = aq.shape
    _, H, Dh, L = cache_kT.shape
    Hg, Ag = H // groups, A // groups
    lows = tuple(L - win for win, _ in DIL_PATTERNS)
    assert all(lo >= 0 and lo % LANES == 0 for lo in lows) and T <= LANES and Ag % LANES == 0
    tabs = _sample_bias_tables(rel_bias, L, T)
    big = pl.BlockSpec((1, Hg, Dh, L), lambda i: (i // groups, i % groups, 0, 0))
    small = pl.BlockSpec((1, T, Ag), lambda i: (i // groups, 0, i % groups))
    tab = lambda t: pl.BlockSpec((Hg * T, t.shape[1]), lambda i: (i % groups, 0))
    win_sds = jax.ShapeDtypeStruct((B, H, Dh, L), F32)
    args = [cache_kT, cache_vT, aq, ak, av, *tabs]
    in_specs = [big, big, small, small, small] + [tab(t) for t in tabs]
    out_specs = [big, big, small]
    out_shape = [win_sds, win_sds, jax.ShapeDtypeStruct((B, T, A), F32)]
    return functools.partial(_window_pieces, lows, T), args, in_specs, out_specs, out_shape


def _ffn_pieces(d_ff, ff_chunk, x_ref, ro_ref, ao_ref, wo_ref, wgu_ref, wd_ref, g_pm, g_pf, g_of, y_ref, act_ref):
    R = RET_WIDTH
    mix = _dot(ro_ref[...].astype(BF16), wo_ref[:R, :]) + _dot(ao_ref[...].astype(BF16), wo_ref[R:, :])
    x1 = x_ref[...] + _rms(mix, g_pm[...])
    h = _rms(x1, g_pf[...]).astype(BF16)
    yield
    for c in range(0, d_ff, ff_chunk):
        gate = _dot(h, wgu_ref[:, c:c + ff_chunk])
        up = _dot(h, wgu_ref[:, d_ff + c:d_ff + c + ff_chunk])
        act_ref[:, c:c + ff_chunk] = (gate * jax.nn.sigmoid(gate) * up).astype(BF16)
        yield
    f = _dot(act_ref[...], wd_ref[...])
    y_ref[...] = x1 + _rms(f, g_of[...])
    yield


def _out_kernel(d_ff, ff_chunk, *refs):
    for _ in _ffn_pieces(d_ff, ff_chunk, *refs):
        pass


def _out_window_kernel(d_ff, ff_chunk, window, n_ffn_in, n_win_in, *refs):
    ffn_in = refs[:n_ffn_in]
    win_in = refs[n_ffn_in:n_ffn_in + n_win_in]
    y_ref, koT_ref, voT_ref, sao_ref, act_ref = refs[n_ffn_in + n_win_in:]
    win = window(*win_in, koT_ref, voT_ref, sao_ref)
    for _ in _ffn_pieces(d_ff, ff_chunk, *ffn_in, y_ref, act_ref):
        next(win, None)
    for _ in win:
        pass


def _out_block(x, ro, ao, wo_bf, wgu_bf, wd_bf, g_post_mix, g_pre_ffn, g_post_ffn, tm, window=None):
    N, D = x.shape
    d_ff = wd_bf.shape[0]
    row = lambda width: pl.BlockSpec((tm, width), lambda i: (i, 0))
    const = lambda shape: pl.BlockSpec(shape, lambda i: (0,) * len(shape), pipeline_mode=pl.Buffered(1))
    args = [x, ro, ao, wo_bf, wgu_bf, wd_bf, g_post_mix.reshape(1, D), g_pre_ffn.reshape(1, D),
            g_post_ffn.reshape(1, D)]
    in_specs = [row(D), row(RET_WIDTH), row(ATT_WIDTH), const(wo_bf.shape), const(wgu_bf.shape),
                const(wd_bf.shape), const((1, D)), const((1, D)), const((1, D))]
    body = functools.partial(_out_kernel, d_ff, 256)
    out_specs, out_shape = [row(D)], [jax.ShapeDtypeStruct((N, D), F32)]
    if window is not None:
        win_body, win_args, win_in_specs, win_out_specs, win_out_shape = window
        assert N // tm == win_out_shape[0].shape[0] * (win_out_shape[0].shape[1] // win_out_specs[0].block_shape[1])
        body = functools.partial(_out_window_kernel, d_ff, 256, win_body, len(args), len(win_args))
        args, in_specs = args + win_args, in_specs + win_in_specs
        out_specs, out_shape = out_specs + win_out_specs, out_shape + win_out_shape
    outs = pl.pallas_call(
        body,
        grid=(N // tm,),
        in_specs=in_specs,
        out_specs=out_specs,
        out_shape=out_shape,
        scratch_shapes=[pltpu.VMEM((tm, d_ff), BF16)],
        compiler_params=_cparams(("arbitrary",), 56),
        name="out_ffn",
    )(*args)
    return outs[0] if window is None else outs


def kernel(x_prompt, x_sample, state_ret, cache_k_win, cache_v_win, rel_bias, w_in, g_ret, w_out,
           g_pre_mix, g_post_mix, g_pre_ffn, g_post_ffn, w_gu, w_down):
    depth = w_in.shape[0]
    assert depth == 1
    B, S, D = x_prompt.shape
    Bs, Ts, _ = x_sample.shape
    H, Dh = N_ATT_HEADS, HEAD_DIM
    l = 0
    w_in_bf = w_in[l].astype(BF16)
    wo_bf = w_out[l].astype(BF16)
    wgu_bf = w_gu[l].astype(BF16)
    wd_bf = w_down[l].astype(BF16)

    pos_s = PAST_LEN + jnp.arange(Ts, dtype=jnp.int32)
    N_s = Bs * Ts
    outs = _proj(x_sample.reshape(N_s, D), g_pre_mix[l], w_in_bf, jnp.tile(pos_s, Bs))
    srq, srk, srv, srg, saq, sak, sav = [t.reshape(Bs, Ts, RET_WIDTH) for t in outs]
    sro, s_s = _retention_step(srq, srk, srv, srg, g_ret[l], state_ret[l])
    to_t = lambda t: jnp.transpose(t, (0, 2, 3, 1))
    from_t = lambda t: jnp.transpose(t, (0, 3, 1, 2))
    window = _sample_window_operands(saq, sak, sav, to_t(cache_k_win[l]), to_t(cache_v_win[l]), rel_bias, 2)

    ro, s_p, aq, ak, av = _proj_retention(x_prompt, g_pre_mix[l], w_in_bf, g_ret[l], 512)
    ao = _prompt_attention(aq, ak, av, rel_bias)
    y_p, k_sT, v_sT, sao = _out_block(x_prompt.reshape(B * S, D), ro.reshape(B * S, RET_WIDTH),
                                      ao.reshape(B * S, ATT_WIDTH), wo_bf, wgu_bf, wd_bf, g_post_mix[l],
                                      g_pre_ffn[l], g_post_ffn[l], 256, window)
    k_s, v_s = from_t(k_sT), from_t(v_sT)

    y_s = _out_block(x_sample.reshape(N_s, D), sro.reshape(N_s, RET_WIDTH), sao.reshape(N_s, ATT_WIDTH),
                     wo_bf, wgu_bf, wd_bf, g_post_mix[l], g_pre_ffn[l], g_post_ffn[l], N_s)

    return (y_p.reshape(B, S, D), y_s.reshape(Bs, Ts, D),
            s_p[None], ak.reshape(1, B, S, H, Dh), av.reshape(1, B, S, H, Dh),
            s_s[None], k_s[None], v_s[None])
```

```python
import functools
import math

import jax
import jax.numpy as jnp
from jax import lax
from jax.experimental import pallas as pl
from jax.experimental.pallas import tpu as pltpu

F32 = jnp.float32
BF16 = jnp.bfloat16

HEAD_DIM = 64
N_RET_HEADS = 8
N_ATT_HEADS = 8
RET_WIDTH = N_RET_HEADS * HEAD_DIM
ATT_WIDTH = N_ATT_HEADS * HEAD_DIM
RET_CHUNK = 128
DIL_PATTERNS = ((128, 1), (512, 4), (2048, 16))
ATT_BLOCK = 128
N_BUCKETS = 32
MAX_DISTANCE = 2048
ROPE_BASE = 10000.0
NORM_EPS = 1e-6
GN_EPS = 1e-5
PAST_LEN = 16384
LANES = 128
ATT_CHUNKS = ATT_WIDTH // LANES
NEG_INF = float("-inf")


def _cparams(sem, vmem_mb):
    return pltpu.CompilerParams(dimension_semantics=sem, vmem_limit_bytes=vmem_mb * 1024 * 1024)


def _rms(x, g):
    return x * lax.rsqrt(jnp.mean(x * x, axis=-1, keepdims=True) + NORM_EPS) * g


def _dot(a, b):
    return jnp.dot(a, b, preferred_element_type=F32)


def _dot_nt(a, b):
    return lax.dot_general(a, b, (((1,), (1,)), ((), ())), preferred_element_type=F32)


def _dot_tn(a, b):
    return lax.dot_general(a, b, (((0,), (0,)), ((), ())), preferred_element_type=F32)


def _rope_tables(pos):
    inv = ROPE_BASE ** (-jnp.arange(0, HEAD_DIM, 2, dtype=F32) / HEAD_DIM)
    ang = pos.astype(F32)[:, None] * inv[None, :]
    cos, sin = jnp.cos(ang), jnp.sin(ang)
    zero = jnp.zeros_like(sin)
    rep = LANES // HEAD_DIM
    cosf = jnp.tile(jnp.concatenate([cos, cos], axis=-1), (1, rep))
    sa = jnp.tile(jnp.concatenate([-sin, zero], axis=-1), (1, rep))
    sb = jnp.tile(jnp.concatenate([zero, sin], axis=-1), (1, rep))
    return cosf, sa, sb


def _retention_consts(C):
    log_g = jnp.log1p(-jnp.exp2(-5.0 - jnp.arange(N_RET_HEADS, dtype=F32)))
    i = jnp.arange(C, dtype=F32)
    rel = i[:, None] - i[None, :]
    decay = jnp.where(rel[None] >= 0, jnp.exp(jnp.maximum(rel, 0.0)[None] * log_g[:, None, None]), 0.0)
    cross = jnp.exp((i + 1.0)[:, None] * log_g[None, :])
    kdec = jnp.exp((C - 1.0 - i)[:, None] * log_g[None, :])
    sdec = jnp.exp(C * log_g)[None, :]
    expand = lambda t: jnp.repeat(t, HEAD_DIM, axis=-1)
    return decay, expand(cross), expand(kdec), expand(sdec)


def _t5_bucket(dist):
    max_exact = N_BUCKETS // 2
    d_f = jnp.maximum(dist, 1).astype(F32)
    large = max_exact + (jnp.log(d_f / max_exact) / math.log(MAX_DISTANCE / max_exact)
                         * (N_BUCKETS - max_exact)).astype(jnp.int32)
    large = jnp.minimum(large, N_BUCKETS - 1)
    return jnp.where(dist < max_exact, dist, large)


def _pattern_bias_rev(rel_bias, dil, nk):
    dist = jnp.arange(nk, -1, -1, dtype=jnp.int32) * dil
    return rel_bias[_t5_bucket(dist)].astype(F32).T


def _prompt_bias_tables(rel_bias):
    BLK, H = ATT_BLOCK, N_ATT_HEADS
    out = []
    for win, dil in DIL_PATTERNS:
        nk = win // dil
        assert nk == BLK
        period = 3 * BLK
        v = jnp.concatenate([_pattern_bias_rev(rel_bias, dil, nk),
                             jnp.full((H, period - nk - 1), NEG_INF, F32)], axis=1)
        flat = jnp.tile(v, (1, BLK))[:, :BLK * (period - 1)]
        skew = flat.reshape(H, BLK, period - 1)[:, :, :2 * BLK]
        out.append(skew.reshape(H * BLK, 2 * BLK))
    return out


def _sample_bias_tables(rel_bias, L, T):
    H = N_ATT_HEADS
    cache_part, new_part = [], []
    for win, dil in DIL_PATTERNS:
        nk = win // dil
        rev = _pattern_bias_rev(rel_bias, dil, nk)
        if dil > 1:
            gaps = jnp.full((H, nk + 1, dil - 1), NEG_INF, F32)
            rev = jnp.concatenate([rev[:, :, None], gaps], axis=2).reshape(H, (nk + 1) * dil)[:, :nk * dil + 1]
        pv = jnp.concatenate([jnp.full((H, T), NEG_INF, F32), rev, jnp.full((H, 2 * T), NEG_INF, F32)], axis=1)
        width = win + T
        rows = jnp.stack([pv[:, T - t:T - t + width] for t in range(T)], axis=1)
        cache_part.append(rows[:, :, :win].reshape(H * T, win))
        new = jnp.concatenate([jnp.full((H, T, LANES - T), NEG_INF, F32), rows[:, :, win:]], axis=2)
        new_part.append(new.reshape(H * T, LANES))
    return cache_part + new_part


def _proj_steps(x, g_ref, w_ref, cos, sa, sb):
    h = _rms(x, g_ref[...]).astype(BF16)
    W = 2 * LANES
    rep = W // LANES
    cosf = jnp.concatenate([cos] * rep, axis=-1)
    saf = jnp.concatenate([sa] * rep, axis=-1)
    sbf = jnp.concatenate([sb] * rep, axis=-1)
    half = HEAD_DIM // 2

    def col(c):
        return _dot(h, w_ref[:, c:c + W])

    def rope(z):
        return z * cosf + pltpu.roll(z, W - half, 1) * saf + pltpu.roll(z, half, 1) * sbf

    R, A = RET_WIDTH, ATT_WIDTH
    scale = HEAD_DIM ** -0.5
    steps = []
    for c in range(0, R, W):
        steps.append(lambda c=c: rope(col(c)))
    for c in range(R, 2 * R, W):
        steps.append(lambda c=c: rope(col(c)) * scale)
    for c in range(2 * R, 4 * R, W):
        steps.append(lambda c=c: col(c))
    for c in range(4 * R, 4 * R + A, W):
        steps.append(lambda c=c: col(c) * scale)
    for c in range(4 * R + A, 4 * R + 3 * A, W):
        steps.append(lambda c=c: col(c))
    return steps


def _join_groups(pieces):
    return [jnp.concatenate(pieces[i:i + 2], axis=-1) for i in range(0, len(pieces), 2)]


def _proj_kernel(x_ref, g_ref, w_ref, cos_ref, sa_ref, sb_ref, *outs):
    steps = _proj_steps(x_ref[...], g_ref, w_ref, cos_ref[...], sa_ref[...], sb_ref[...])
    for ref, z in zip(outs, _join_groups([step() for step in steps])):
        ref[...] = z


def _proj(x, g, w_bf, pos):
    N, D = x.shape
    tabs = _rope_tables(pos)
    full = lambda shape: pl.BlockSpec(shape, lambda i: (0,) * len(shape))
    return pl.pallas_call(
        _proj_kernel,
        grid=(1,),
        in_specs=[full((N, D)), full((1, D)), full(w_bf.shape)] + [full(t.shape) for t in tabs],
        out_specs=[full((N, RET_WIDTH))] * 7,
        out_shape=[jax.ShapeDtypeStruct((N, RET_WIDTH), F32)] * 7,
        compiler_params=_cparams(("arbitrary",), 40),
        name="proj",
    )(x, g.reshape(1, D), w_bf, *tabs)


def _proj_ret_kernel(tiles_per_seq, x_ref, g_ref, w_ref, cos_ref, sa_ref, sb_ref, dec_ref, cd_ref, kd_ref,
                     sd_ref, gr_ref, ro_ref, s_out_ref, aq_ref, ak_ref, av_ref, sbd_ref, qkv_ref, gate_ref):
    s = pl.program_id(0)
    n_tiles = pl.num_programs(0) - 1
    tm = x_ref.shape[1]
    C = RET_CHUNK
    cur = s % 2
    prv = 1 - cur

    @pl.when(s == 0)
    def _():
        sbd_ref[...] = jnp.zeros_like(sbd_ref)
        qkv_ref[1] = jnp.zeros(qkv_ref.shape[1:], qkv_ref.dtype)
        gate_ref[1] = jnp.zeros(gate_ref.shape[1:], gate_ref.dtype)

    lane = lax.broadcasted_iota(jnp.int32, (C, LANES), 1)
    half = lane < HEAD_DIM
    same_head = (lax.broadcasted_iota(jnp.int32, (LANES, LANES), 0) // HEAD_DIM
                 == lax.broadcasted_iota(jnp.int32, (LANES, LANES), 1) // HEAD_DIM)
    inv_d = 1.0 / HEAD_DIM
    starts_seq = (s - 1) % tiles_per_seq == 0

    def half_mean(t):
        lo = jnp.sum(jnp.where(half, t, 0.0), axis=-1, keepdims=True)
        hi = jnp.sum(jnp.where(half, 0.0, t), axis=-1, keepdims=True)
        return jnp.where(half, lo, hi) * inv_d

    n_pairs = RET_WIDTH // LANES
    n_chunks = tm // C
    state = [jnp.where(starts_seq, 0.0, sbd_ref[c]) for c in range(n_pairs)]

    def retention_unit(c, cc):
        ls = slice(c * LANES, (c + 1) * LANES)
        rs = slice(cc * C, (cc + 1) * C)
        S = state[c]
        qb = qkv_ref[prv, 0, rs, ls]
        kb = qkv_ref[prv, 1, rs, ls]
        vb = qkv_ref[prv, 2, rs, ls]
        q = qb.astype(F32)
        q_pair = jnp.concatenate([jnp.where(half, q, 0.0), jnp.where(half, 0.0, q)], axis=0).astype(BF16)
        inner = _dot_nt(q_pair, kb) * dec_ref[c]
        o2 = _dot(inner.astype(BF16), vb)
        o = jnp.where(half, o2[:C], o2[C:]) + _dot(qb, S.astype(BF16)) * cd_ref[:, ls]
        kd = (kb.astype(F32) * kd_ref[:, ls]).astype(BF16)
        state[c] = S * sd_ref[:, ls] + jnp.where(same_head, _dot_tn(kd, vb), 0.0)
        mu = half_mean(o)
        d = o - mu
        var = half_mean(d * d)
        on = d * lax.rsqrt(var + GN_EPS) * gr_ref[:, ls]
        gate = gate_ref[prv, rs, ls]
        ro_ref[0, rs, ls] = (on * (gate * jax.nn.sigmoid(gate))).astype(ro_ref.dtype)

    j = jnp.minimum(s, n_tiles - 1) % tiles_per_seq
    rows = pl.ds(pl.multiple_of(j * tm, tm), tm)
    steps = _proj_steps(x_ref[0], g_ref, w_ref, cos_ref[rows, :], sa_ref[rows, :], sb_ref[rows, :])
    units = [(c, cc) for cc in range(n_chunks) for c in range(n_pairs)]
    cols = []
    for i, step in enumerate(steps):
        cols.append(step())
        for c, cc in units[i * len(units) // len(steps):(i + 1) * len(units) // len(steps)]:
            retention_unit(c, cc)
    for c in range(n_pairs):
        S = state[c]
        sbd_ref[c] = S
        s_out_ref[0, 2 * c] = S[:HEAD_DIM, :HEAD_DIM]
        s_out_ref[0, 2 * c + 1] = S[HEAD_DIM:, HEAD_DIM:]
    rq, rk, rv, rg, aq, ak, av = _join_groups(cols)
    aq_ref[0] = aq.astype(aq_ref.dtype)
    ak_ref[0] = ak
    av_ref[0] = av
    qkv_ref[cur, 0] = rq.astype(BF16)
    qkv_ref[cur, 1] = rk.astype(BF16)
    qkv_ref[cur, 2] = rv.astype(BF16)
    gate_ref[cur] = rg


def _proj_retention(x, g, w_bf, g_ret, tm):
    B, T, D = x.shape
    R, A, H, Dh = RET_WIDTH, ATT_WIDTH, N_RET_HEADS, HEAD_DIM
    tps = T // tm
    n_tiles = B * tps
    tabs = _rope_tables(jnp.arange(T, dtype=jnp.int32))
    dec, cd, kd, sd = _retention_consts(RET_CHUNK)
    dec_pair = dec.reshape(H // 2, 2 * RET_CHUNK, RET_CHUNK)

    def this_tile(width):
        def index(s):
            t = jnp.minimum(s, n_tiles - 1)
            return (t // tps, t % tps, 0)
        return pl.BlockSpec((1, tm, width), index)

    def prev_tile(s):
        t = jnp.maximum(s - 1, 0)
        return (t // tps, t % tps, 0)

    full = lambda shape: pl.BlockSpec(shape, lambda s: (0,) * len(shape))
    consts = [g.reshape(1, D), w_bf, *tabs, dec_pair, cd, kd, sd, g_ret.reshape(1, R)]
    return pl.pallas_call(
        functools.partial(_proj_ret_kernel, tps),
        grid=(n_tiles + 1,),
        in_specs=[this_tile(D)] + [full(t.shape) for t in consts],
        out_specs=[pl.BlockSpec((1, tm, R), prev_tile),
                   pl.BlockSpec((1, H, Dh, Dh), lambda s: (jnp.maximum(s - 1, 0) // tps, 0, 0, 0)),
                   this_tile(A), this_tile(A), this_tile(A)],
        out_shape=[jax.ShapeDtypeStruct((B, T, R), BF16), jax.ShapeDtypeStruct((B, H, Dh, Dh), F32),
                   jax.ShapeDtypeStruct((B, T, A), BF16), jax.ShapeDtypeStruct((B, T, A), F32),
                   jax.ShapeDtypeStruct((B, T, A), F32)],
        scratch_shapes=[pltpu.VMEM((R // LANES, LANES, LANES), F32), pltpu.VMEM((2, 3, tm, R), BF16),
                        pltpu.VMEM((2, tm, R), F32)],
        compiler_params=_cparams(("arbitrary",), 56),
        name="proj_retention",
    )(x, *consts)


def _ret_kernel(q_ref, k_ref, v_ref, gate_ref, dec_ref, cd_ref, kd_ref, sd_ref, gr_ref, s0_ref, o_ref, s_ref):
    for b in range(q_ref.shape[0]):
        q = q_ref[b].astype(BF16)
        k = k_ref[b].astype(BF16)
        v = v_ref[b].astype(BF16)
        for h in range(N_RET_HEADS):
            sl = slice(h * HEAD_DIM, (h + 1) * HEAD_DIM)
            qh, kh, vh = q[:, sl], k[:, sl], v[:, sl]
            S = s0_ref[b, h]
            inner = _dot_nt(qh, kh) * dec_ref[h]
            o = _dot(inner.astype(BF16), vh) + _dot(qh, S.astype(BF16)) * cd_ref[:, sl]
            kd = (kh.astype(F32) * kd_ref[:, sl]).astype(BF16)
            s_ref[b, h] = S * sd_ref[:, sl] + _dot_tn(kd, vh)
            mu = jnp.mean(o, axis=-1, keepdims=True)
            var = jnp.mean(jnp.square(o - mu), axis=-1, keepdims=True)
            on = (o - mu) * lax.rsqrt(var + GN_EPS) * gr_ref[:, sl]
            gate = gate_ref[b, :, sl]
            o_ref[b, :, sl] = on * (gate * jax.nn.sigmoid(gate))


def _retention_step(rq, rk, rv, rg, g_ret, state):
    B, T, R = rq.shape
    H, Dh = N_RET_HEADS, HEAD_DIM
    consts = [*_retention_consts(T), g_ret.reshape(1, R)]
    bb = math.gcd(B, 8)
    row = pl.BlockSpec((bb, T, R), lambda b: (b, 0, 0))
    full = lambda shape: pl.BlockSpec(shape, lambda b: (0,) * len(shape))
    st = pl.BlockSpec((bb, H, Dh, Dh), lambda b: (b, 0, 0, 0))
    return pl.pallas_call(
        _ret_kernel,
        grid=(B // bb,),
        in_specs=[row, row, row, row] + [full(t.shape) for t in consts] + [st],
        out_specs=[row, st],
        out_shape=[jax.ShapeDtypeStruct((B, T, R), F32), jax.ShapeDtypeStruct((B, H, Dh, Dh), F32)],
        compiler_params=_cparams(("arbitrary",), 32),
        name="retention",
    )(rq, rk, rv, rg, *consts, state)


def _split3_dot(c, e):
    c1 = c.astype(BF16)
    r1 = c - c1.astype(F32)
    c2 = r1.astype(BF16)
    c3 = (r1 - c2.astype(F32)).astype(BF16)
    return _dot(c1, e) + _dot(c2, e) + _dot(c3, e)


def _chunk_rows(ref, stride):
    def load(start, c):
        idx = pl.ds(start, ATT_BLOCK) if stride == 1 else pl.ds(start, ATT_BLOCK, stride=stride)
        return ref[c, idx, :]
    return load


def _token_rows(ref):
    def load(start, c):
        return ref[0, pl.ds(start, ATT_BLOCK), c * LANES:(c + 1) * LANES].astype(F32)
    return load


def _att_block(load_q, load_k, load_v, bias_ref, cur, prev, first):
    BLK, H = ATT_BLOCK, N_ATT_HEADS
    half = lax.broadcasted_iota(jnp.int32, (BLK, LANES), 1) < HEAD_DIM
    s_parts = []
    for c in range(ATT_CHUNKS):
        q = load_q(cur, c)
        q_pair = jnp.concatenate([jnp.where(half, q, 0.0), jnp.where(half, 0.0, q)], axis=0).astype(BF16)
        k = load_k(cur, c)
        if prev is not None:
            k = jnp.concatenate([load_k(prev, c), k], axis=0)
        s_parts.append(_dot_nt(q_pair, k.astype(BF16)))
    s = jnp.concatenate(s_parts, axis=0)
    if prev is not None:
        s = s + bias_ref[...]
        col = lax.broadcasted_iota(jnp.int32, s.shape, 1)
        s = jnp.where(jnp.logical_and(first, col < BLK), NEG_INF, s)
    else:
        s = s + bias_ref[:, BLK:]
    m = jnp.max(s, axis=-1, keepdims=True)
    p = jnp.exp(s - m)
    l = jnp.sum(p, axis=-1, keepdims=True)
    pb = p.astype(BF16)
    accs = []
    for c in range(ATT_CHUNKS):
        v = load_v(cur, c)
        if prev is not None:
            v = jnp.concatenate([load_v(prev, c), v], axis=0)
        o = _dot(pb[2 * c * BLK:(2 * c + 2) * BLK], v.astype(BF16))
        accs.append(jnp.where(half, o[:BLK], o[BLK:]))
    lane = lax.broadcasted_iota(jnp.int32, (BLK, LANES), 1)
    ml = jnp.zeros((BLK, LANES), F32)
    for h in range(H):
        ml = jnp.where(lane == h, m[h * BLK:(h + 1) * BLK], ml)
        ml = jnp.where(lane == H + h, l[h * BLK:(h + 1) * BLK], ml)
    return accs, ml


def _att_kernel(q_ref, k_ref, v_ref, b1_ref, b4_ref, b16_ref, e_ref, ao_ref, q4, k4, v4, stage, acc4, ml4,
                acc16, ml16):
    BLK, H = ATT_BLOCK, N_ATT_HEADS
    S = q_ref.shape[1]
    nblk = S // BLK
    d4, d16 = DIL_PATTERNS[1][1], DIL_PATTERNS[2][1]
    nb4 = nblk // d4
    per_res = S // d4
    for src, dst in ((q_ref, q4), (k_ref, k4), (v_ref, v4)):
        for c in range(ATT_CHUNKS):
            stage[...] = src[0, :, c * LANES:(c + 1) * LANES].astype(F32)
            for r in range(d4):
                dst[c, r * per_res:(r + 1) * per_res, :] = stage[pl.ds(r, per_res, stride=d4), :]
    contiguous = [_chunk_rows(t, 1) for t in (q4, k4, v4)]
    every_4th = [_chunk_rows(t, d16 // d4) for t in (q4, k4, v4)]
    tokens = [_token_rows(t) for t in (q_ref, k_ref, v_ref)]

    def dilated(n, carry):
        r, j = n // nb4, n % nb4
        cur = r * per_res + j * BLK
        prev = r * per_res + jnp.maximum(j - 1, 0) * BLK
        accs, ml = _att_block(*contiguous, b4_ref, cur, prev, j == 0)
        tok = r + j * (BLK * d4)
        for c in range(ATT_CHUNKS):
            acc4[c, pl.ds(tok, BLK, stride=d4), :] = accs[c]
        ml4[pl.ds(tok, BLK, stride=d4), :] = ml
        accs, ml = _att_block(*every_4th, b16_ref, (n % d4) * per_res + n // d4, None, None)
        for c in range(ATT_CHUNKS):
            acc16[c, pl.ds(n, BLK, stride=d16), :] = accs[c]
        ml16[pl.ds(n, BLK, stride=d16), :] = ml
        return carry

    def dense_and_merge(j, carry):
        cur = pl.multiple_of(j * BLK, BLK)
        prev = pl.multiple_of(jnp.maximum(j - 1, 0) * BLK, BLK)
        accs1, ml1 = _att_block(*tokens, b1_ref, cur, prev, j == 0)
        rows = pl.ds(cur, BLK)
        mls = [ml1, ml4[rows, :], ml16[rows, :]]
        accs = [accs1, [acc4[c, rows, :] for c in range(ATT_CHUNKS)], [acc16[c, rows, :] for c in range(ATT_CHUNKS)]]
        lane = lax.broadcasted_iota(jnp.int32, (BLK, LANES), 1)
        m_all = jnp.maximum(jnp.maximum(mls[0], mls[1]), mls[2])
        ws = [jnp.exp(ml - m_all) for ml in mls]
        den = sum(w * pltpu.roll(ml, LANES - H, 1) for w, ml in zip(ws, mls))
        ao = [jnp.zeros((BLK, LANES), F32) for _ in range(ATT_CHUNKS)]
        for w, acc in zip(ws, accs):
            coef = _split3_dot(jnp.where(lane < H, w / den, 0.0), e_ref[...])
            for c in range(ATT_CHUNKS):
                ao[c] = ao[c] + coef[:, c * LANES:(c + 1) * LANES] * acc[c]
        ao_ref[0, rows, :] = jnp.concatenate(ao, axis=-1).astype(ao_ref.dtype)
        return carry

    lax.fori_loop(0, nblk, dilated, 0)
    lax.fori_loop(0, nblk, dense_and_merge, 0)


def _prompt_attention(aq, ak, av, rel_bias):
    B, S, A = aq.shape
    CH = ATT_CHUNKS
    assert [d for _, d in DIL_PATTERNS] == [1, 4, 16] and S // ATT_BLOCK == 16
    b1, b4, b16 = _prompt_bias_tables(rel_bias)
    head_of_lane = jnp.arange(A)[None, :] // HEAD_DIM
    expand = (jnp.arange(LANES)[:, None] == head_of_lane).astype(BF16)
    seq = pl.BlockSpec((1, S, A), lambda b: (b, 0, 0))
    const = lambda shape: pl.BlockSpec(shape, lambda b: (0,) * len(shape), pipeline_mode=pl.Buffered(1))
    chunked = pltpu.VMEM((CH, S, LANES), F32)
    stats = pltpu.VMEM((S, LANES), F32)
    return pl.pallas_call(
        _att_kernel,
        grid=(B,),
        in_specs=[seq, seq, seq, const(b1.shape), const(b4.shape), const(b16.shape), const(expand.shape)],
        out_specs=seq,
        out_shape=jax.ShapeDtypeStruct((B, S, A), BF16),
        scratch_shapes=[chunked, chunked, chunked, stats, chunked, stats, chunked, stats],
        compiler_params=_cparams(("arbitrary",), 58),
        name="prompt_att",
    )(aq, ak, av, b1, b4, b16, expand)


def _window_pieces(lows, T, kT_ref, vT_ref, qn_ref, kn_ref, vn_ref, c1_ref, c4_ref, c16_ref,
                   n1_ref, n4_ref, n16_ref, koT_ref, voT_ref, ao_ref):
    H, Dh, L = kT_ref.shape[1:]
    A = H * Dh
    zpad = jnp.zeros((LANES - T, A), F32)
    tail_lane = lax.broadcasted_iota(jnp.int32, (Dh, LANES), 1) >= LANES - T
    shifted = {}

    def shift_in(name, xT_ref, new_ref, out_ref):
        new_p = jnp.concatenate([zpad, new_ref[0]], axis=0)
        new_t = new_p.T
        heads = []
        for h in range(H):
            x = xT_ref[0, h]
            rolled = pltpu.roll(x, L - T, 1)
            tail = jnp.where(tail_lane, new_t[h * Dh:(h + 1) * Dh], rolled[:, L - LANES:])
            out_ref[0, h] = jnp.concatenate([rolled[:, :L - LANES], tail], axis=1)
            heads.append(x.astype(BF16))
            yield
        shifted[name] = (jnp.concatenate(heads, axis=0), new_p.astype(BF16))

    yield from shift_in("k", kT_ref, kn_ref, koT_ref)
    yield from shift_in("v", vT_ref, vn_ref, voT_ref)
    (kT, kn), (vT, vn) = shifted["k"], shifted["v"]
    HT = H * T
    row_head = lax.broadcasted_iota(jnp.int32, (HT, A), 0) // T
    lane_head = lax.broadcasted_iota(jnp.int32, (HT, A), 1) // HEAD_DIM
    diag = row_head == lane_head
    q_rows = jnp.where(diag, jnp.concatenate([qn_ref[0]] * H, axis=0), 0.0).astype(BF16)
    s_cache = _dot(q_rows, kT)
    s_new = _dot_nt(q_rows, kn)
    yield
    stats = []
    for lo, c_ref, n_ref in zip(lows, (c1_ref, c4_ref, c16_ref), (n1_ref, n4_ref, n16_ref)):
        sc = s_cache[:, lo:] + c_ref[...]
        sn = s_new + n_ref[...]
        m = jnp.maximum(jnp.max(sc, axis=-1, keepdims=True), jnp.max(sn, axis=-1, keepdims=True))
        pc = jnp.exp(sc - m)
        pn = jnp.exp(sn - m)
        l = jnp.sum(pc, axis=-1, keepdims=True) + jnp.sum(pn, axis=-1, keepdims=True)
        acc = _dot_nt(pc.astype(BF16), vT[:, lo:]) + _dot(pn.astype(BF16), vn)
        stats.append((m, l, acc))
        yield
    m_all = jnp.maximum(jnp.maximum(stats[0][0], stats[1][0]), stats[2][0])
    num = jnp.zeros((HT, A), F32)
    den = jnp.zeros((HT, 1), F32)
    for m, l, acc in stats:
        w = jnp.exp(m - m_all)
        num = num + w * acc
        den = den + w * l
    comb = jnp.where(diag, num / den, 0.0)
    out = comb[0:T]
    for h in range(1, H):
        out = out + comb[h * T:(h + 1) * T]
    ao_ref[0] = out
    yield


def _sample_window_operands(aq, ak, av, cache_kT, cache_vT, rel_bias, groups):
    B, T, A = aq.shape
    _, H, Dh, L = cache_kT.shape
    Hg, Ag = H // groups, A // groups
    lows = tuple(L - win for win, _ in DIL_PATTERNS)
    assert all(lo >= 0 and lo % LANES == 0 for lo in lows) and T <= LANES and Ag % LANES == 0
    tabs = _sample_bias_tables(rel_bias, L, T)
    big = pl.BlockSpec((1, Hg, Dh, L), lambda i: (i // groups, i % groups, 0, 0))
    small = pl.BlockSpec((1, T, Ag), lambda i: (i // groups, 0, i % groups))
    tab = lambda t: pl.BlockSpec((Hg * T, t.shape[1]), lambda i: (i % groups, 0))
    win_sds = jax.ShapeDtypeStruct((B, H, Dh, L), F32)
    args = [cache_kT, cache_vT, aq, ak, av, *tabs]
    in_specs = [big, big, small, small, small] + [tab(t) for t in tabs]
    out_specs = [big, big, small]
    out_shape = [win_sds, win_sds, jax.ShapeDtypeStruct((B, T, A), F32)]
    return functools.partial(_window_pieces, lows, T), args, in_specs, out_specs, out_shape


def _ffn_pieces(d_ff, ff_chunk, x_ref, ro_ref, ao_ref, wo_ref, wgu_ref, wd_ref, g_pm, g_pf, g_of, y_ref, act_ref):
    R = RET_WIDTH
    mix = _dot(ro_ref[...].astype(BF16), wo_ref[:R, :]) + _dot(ao_ref[...].astype(BF16), wo_ref[R:, :])
    x1 = x_ref[...] + _rms(mix, g_pm[...])
    h = _rms(x1, g_pf[...]).astype(BF16)
    yield
    for c in range(0, d_ff, ff_chunk):
        gate = _dot(h, wgu_ref[:, c:c + ff_chunk])
        up = _dot(h, wgu_ref[:, d_ff + c:d_ff + c + ff_chunk])
        act_ref[:, c:c + ff_chunk] = (gate * jax.nn.sigmoid(gate) * up).astype(BF16)
        yield
    f = _dot(act_ref[...], wd_ref[...])
    y_ref[...] = x1 + _rms(f, g_of[...])
    yield


def _out_kernel(d_ff, ff_chunk, *refs):
    for _ in _ffn_pieces(d_ff, ff_chunk, *refs):
        pass


def _out_window_kernel(d_ff, ff_chunk, window, n_ffn_in, n_win_in, *refs):
    ffn_in = refs[:n_ffn_in]
    win_in = refs[n_ffn_in:n_ffn_in + n_win_in]
    y_ref, koT_ref, voT_ref, sao_ref, act_ref = refs[n_ffn_in + n_win_in:]
    win = window(*win_in, koT_ref, voT_ref, sao_ref)
    for _ in _ffn_pieces(d_ff, ff_chunk, *ffn_in, y_ref, act_ref):
        next(win, None)
    for _ in win:
        pass


def _out_block(x, ro, ao, wo_bf, wgu_bf, wd_bf, g_post_mix, g_pre_ffn, g_post_ffn, tm, window=None):
    N, D = x.shape
    d_ff = wd_bf.shape[0]
    row = lambda width: pl.BlockSpec((tm, width), lambda i: (i, 0))
    const = lambda shape: pl.BlockSpec(shape, lambda i: (0,) * len(shape), pipeline_mode=pl.Buffered(1))
    args = [x, ro, ao, wo_bf, wgu_bf, wd_bf, g_post_mix.reshape(1, D), g_pre_ffn.reshape(1, D),
            g_post_ffn.reshape(1, D)]
    in_specs = [row(D), row(RET_WIDTH), row(ATT_WIDTH), const(wo_bf.shape), const(wgu_bf.shape),
                const(wd_bf.shape), const((1, D)), const((1, D)), const((1, D))]
    body = functools.partial(_out_kernel, d_ff, 256)
    out_specs, out_shape = [row(D)], [jax.ShapeDtypeStruct((N, D), F32)]
    if window is not None:
        win_body, win_args, win_in_specs, win_out_specs, win_out_shape = window
        assert N // tm == win_out_shape[0].shape[0] * (win_out_shape[0].shape[1] // win_out_specs[0].block_shape[1])
        body = functools.partial(_out_window_kernel, d_ff, 256, win_body, len(args), len(win_args))
        args, in_specs = args + win_args, in_specs + win_in_specs
        out_specs, out_shape = out_specs + win_out_specs, out_shape + win_out_shape
    outs = pl.pallas_call(
        body,
        grid=(N // tm,),
        in_specs=in_specs,
        out_specs=out_specs,
        out_shape=out_shape,
        scratch_shapes=[pltpu.VMEM((tm, d_ff), BF16)],
        compiler_params=_cparams(("arbitrary",), 56),
        name="out_ffn",
    )(*args)
    return outs[0] if window is None else outs


def kernel(x_prompt, x_sample, state_ret, cache_k_win, cache_v_win, rel_bias, w_in, g_ret, w_out,
           g_pre_mix, g_post_mix, g_pre_ffn, g_post_ffn, w_gu, w_down):
    depth = w_in.shape[0]
    assert depth == 1
    B, S, D = x_prompt.shape
    Bs, Ts, _ = x_sample.shape
    H, Dh = N_ATT_HEADS, HEAD_DIM
    l = 0
    w_in_bf = w_in[l].astype(BF16)
    wo_bf = w_out[l].astype(BF16)
    wgu_bf = w_gu[l].astype(BF16)
    wd_bf = w_down[l].astype(BF16)

    pos_s = PAST_LEN + jnp.arange(Ts, dtype=jnp.int32)
    N_s = Bs * Ts
    outs = _proj(x_sample.reshape(N_s, D), g_pre_mix[l], w_in_bf, jnp.tile(pos_s, Bs))
    srq, srk, srv, srg, saq, sak, sav = [t.reshape(Bs, Ts, RET_WIDTH) for t in outs]
    sro, s_s = _retention_step(srq, srk, srv, srg, g_ret[l], state_ret[l])
    to_t = lambda t: jnp.transpose(t, (0, 2, 3, 1))
    from_t = lambda t: jnp.transpose(t, (0, 3, 1, 2))
    window = _sample_window_operands(saq, sak, sav, to_t(cache_k_win[l]), to_t(cache_v_win[l]), rel_bias, 2)

    ro, s_p, aq, ak, av = _proj_retention(x_prompt, g_pre_mix[l], w_in_bf, g_ret[l], 512)
    ao = _prompt_attention(aq, ak, av, rel_bias)
    y_p, k_sT, v_sT, sao = _out_block(x_prompt.reshape(B * S, D), ro.reshape(B * S, RET_WIDTH),
                                      ao.reshape(B * S, ATT_WIDTH), wo_bf, wgu_bf, wd_bf, g_post_mix[l],
                                      g_pre_ffn[l], g_post_ffn[l], 256, window)
    k_s, v_s = from_t(k_sT), from_t(v_sT)

    y_s = _out_block(x_sample.reshape(N_s, D), sro.reshape(N_s, RET_WIDTH), sao.reshape(N_s, ATT_WIDTH),
                     wo_bf, wgu_bf, wd_bf, g_post_mix[l], g_pre_ffn[l], g_post_ffn[l], N_s)

    return (y_p.reshape(B, S, D), y_s.reshape(Bs, Ts, D),
            s_p[None], ak.reshape(1, B, S, H, Dh), av.reshape(1, B, S, H, Dh),
            s_s[None], k_s[None], v_s[None])
```

```python
import functools
import math

import jax
import jax.numpy as jnp
from jax import lax
from jax.experimental import pallas as pl
from jax.experimental.pallas import tpu as pltpu

F32 = jnp.float32
BF16 = jnp.bfloat16

HEAD_DIM = 64
N_RET_HEADS = 8
N_ATT_HEADS = 8
RET_WIDTH = N_RET_HEADS * HEAD_DIM
ATT_WIDTH = N_ATT_HEADS * HEAD_DIM
RET_CHUNK = 128
DIL_PATTERNS = ((128, 1), (512, 4), (2048, 16))
ATT_BLOCK = 128
N_BUCKETS = 32
MAX_DISTANCE = 2048
ROPE_BASE = 10000.0
NORM_EPS = 1e-6
GN_EPS = 1e-5
PAST_LEN = 16384
LANES = 128
ATT_CHUNKS = ATT_WIDTH // LANES
NEG_INF = float("-inf")


def _cparams(sem, vmem_mb):
    return pltpu.CompilerParams(dimension_semantics=sem, vmem_limit_bytes=vmem_mb * 1024 * 1024)


def _rms(x, g):
    return x * lax.rsqrt(jnp.mean(x * x, axis=-1, keepdims=True) + NORM_EPS) * g


def _dot(a, b):
    return jnp.dot(a, b, preferred_element_type=F32)


def _dot_nt(a, b):
    return lax.dot_general(a, b, (((1,), (1,)), ((), ())), preferred_element_type=F32)


def _dot_tn(a, b):
    return lax.dot_general(a, b, (((0,), (0,)), ((), ())), preferred_element_type=F32)


def _rope_tables(pos):
    inv = ROPE_BASE ** (-jnp.arange(0, HEAD_DIM, 2, dtype=F32) / HEAD_DIM)
    ang = pos.astype(F32)[:, None] * inv[None, :]
    cos, sin = jnp.cos(ang), jnp.sin(ang)
    zero = jnp.zeros_like(sin)
    rep = LANES // HEAD_DIM
    cosf = jnp.tile(jnp.concatenate([cos, cos], axis=-1), (1, rep))
    sa = jnp.tile(jnp.concatenate([-sin, zero], axis=-1), (1, rep))
    sb = jnp.tile(jnp.concatenate([zero, sin], axis=-1), (1, rep))
    return cosf, sa, sb


def _retention_consts(C):
    log_g = jnp.log1p(-jnp.exp2(-5.0 - jnp.arange(N_RET_HEADS, dtype=F32)))
    i = jnp.arange(C, dtype=F32)
    rel = i[:, None] - i[None, :]
    decay = jnp.where(rel[None] >= 0, jnp.exp(jnp.maximum(rel, 0.0)[None] * log_g[:, None, None]), 0.0)
    cross = jnp.exp((i + 1.0)[:, None] * log_g[None, :])
    kdec = jnp.exp((C - 1.0 - i)[:, None] * log_g[None, :])
    sdec = jnp.exp(C * log_g)[None, :]
    expand = lambda t: jnp.repeat(t, HEAD_DIM, axis=-1)
    return decay, expand(cross), expand(kdec), expand(sdec)


def _t5_bucket(dist):
    max_exact = N_BUCKETS // 2
    d_f = jnp.maximum(dist, 1).astype(F32)
    large = max_exact + (jnp.log(d_f / max_exact) / math.log(MAX_DISTANCE / max_exact)
                         * (N_BUCKETS - max_exact)).astype(jnp.int32)
    large = jnp.minimum(large, N_BUCKETS - 1)
    return jnp.where(dist < max_exact, dist, large)


def _pattern_bias_rev(rel_bias, dil, nk):
    dist = jnp.arange(nk, -1, -1, dtype=jnp.int32) * dil
    return rel_bias[_t5_bucket(dist)].astype(F32).T


def _prompt_bias_tables(rel_bias):
    BLK, H = ATT_BLOCK, N_ATT_HEADS
    out = []
    for win, dil in DIL_PATTERNS:
        nk = win // dil
        assert nk == BLK
        period = 3 * BLK
        v = jnp.concatenate([_pattern_bias_rev(rel_bias, dil, nk),
                             jnp.full((H, period - nk - 1), NEG_INF, F32)], axis=1)
        flat = jnp.tile(v, (1, BLK))[:, :BLK * (period - 1)]
        skew = flat.reshape(H, BLK, period - 1)[:, :, :2 * BLK]
        out.append(skew.reshape(H * BLK, 2 * BLK))
    return out


def _sample_bias_tables(rel_bias, L, T):
    H = N_ATT_HEADS
    cache_part, new_part = [], []
    for win, dil in DIL_PATTERNS:
        nk = win // dil
        rev = _pattern_bias_rev(rel_bias, dil, nk)
        if dil > 1:
            gaps = jnp.full((H, nk + 1, dil - 1), NEG_INF, F32)
            rev = jnp.concatenate([rev[:, :, None], gaps], axis=2).reshape(H, (nk + 1) * dil)[:, :nk * dil + 1]
        pv = jnp.concatenate([jnp.full((H, T), NEG_INF, F32), rev, jnp.full((H, 2 * T), NEG_INF, F32)], axis=1)
        width = win + T
        rows = jnp.stack([pv[:, T - t:T - t + width] for t in range(T)], axis=1)
        cache_part.append(rows[:, :, :win].reshape(H * T, win))
        new = jnp.concatenate([jnp.full((H, T, LANES - T), NEG_INF, F32), rows[:, :, win:]], axis=2)
        new_part.append(new.reshape(H * T, LANES))
    return cache_part + new_part


def _proj_steps(x, g_ref, w_ref, cos, sa, sb):
    h = _rms(x, g_ref[...]).astype(BF16)
    W = 2 * LANES
    rep = W // LANES
    cosf = jnp.concatenate([cos] * rep, axis=-1)
    saf = jnp.concatenate([sa] * rep, axis=-1)
    sbf = jnp.concatenate([sb] * rep, axis=-1)
    half = HEAD_DIM // 2

    def col(c):
        return _dot(h, w_ref[:, c:c + W])

    def rope(z):
        return z * cosf + pltpu.roll(z, W - half, 1) * saf + pltpu.roll(z, half, 1) * sbf

    R, A = RET_WIDTH, ATT_WIDTH
    scale = HEAD_DIM ** -0.5
    steps = []
    for c in range(0, R, W):
        steps.append(lambda c=c: rope(col(c)))
    for c in range(R, 2 * R, W):
        steps.append(lambda c=c: rope(col(c)) * scale)
    for c in range(2 * R, 4 * R, W):
        steps.append(lambda c=c: col(c))
    for c in range(4 * R, 4 * R + A, W):
        steps.append(lambda c=c: col(c) * scale)
    for c in range(4 * R + A, 4 * R + 3 * A, W):
        steps.append(lambda c=c: col(c))
    return steps


def _join_groups(pieces):
    return [jnp.concatenate(pieces[i:i + 2], axis=-1) for i in range(0, len(pieces), 2)]


def _proj_kernel(x_ref, g_ref, w_ref, cos_ref, sa_ref, sb_ref, *outs):
    steps = _proj_steps(x_ref[...], g_ref, w_ref, cos_ref[...], sa_ref[...], sb_ref[...])
    for ref, z in zip(outs, _join_groups([step() for step in steps])):
        ref[...] = z


def _proj(x, g, w_bf, pos):
    N, D = x.shape
    tabs = _rope_tables(pos)
    full = lambda shape: pl.BlockSpec(shape, lambda i: (0,) * len(shape))
    return pl.pallas_call(
        _proj_kernel,
        grid=(1,),
        in_specs=[full((N, D)), full((1, D)), full(w_bf.shape)] + [full(t.shape) for t in tabs],
        out_specs=[full((N, RET_WIDTH))] * 7,
        out_shape=[jax.ShapeDtypeStruct((N, RET_WIDTH), F32)] * 7,
        compiler_params=_cparams(("arbitrary",), 40),
        name="proj",
    )(x, g.reshape(1, D), w_bf, *tabs)


def _proj_ret_kernel(tiles_per_seq, x_ref, g_ref, w_ref, cos_ref, sa_ref, sb_ref, dec_ref, cd_ref, kd_ref,
                     sd_ref, gr_ref, ro_ref, s_out_ref, aq_ref, ak_ref, av_ref, sbd_ref, qkv_ref, gate_ref):
    s = pl.program_id(0)
    n_tiles = pl.num_programs(0) - 1
    tm = x_ref.shape[1]
    C = RET_CHUNK
    cur = s % 2
    prv = 1 - cur

    @pl.when(s == 0)
    def _():
        sbd_ref[...] = jnp.zeros_like(sbd_ref)
        qkv_ref[1] = jnp.zeros(qkv_ref.shape[1:], qkv_ref.dtype)
        gate_ref[1] = jnp.zeros(gate_ref.shape[1:], gate_ref.dtype)

    lane = lax.broadcasted_iota(jnp.int32, (C, LANES), 1)
    half = lane < HEAD_DIM
    same_head = (lax.broadcasted_iota(jnp.int32, (LANES, LANES), 0) // HEAD_DIM
                 == lax.broadcasted_iota(jnp.int32, (LANES, LANES), 1) // HEAD_DIM)
    inv_d = 1.0 / HEAD_DIM
    starts_seq = (s - 1) % tiles_per_seq == 0

    def half_mean(t):
        lo = jnp.sum(jnp.where(half, t, 0.0), axis=-1, keepdims=True)
        hi = jnp.sum(jnp.where(half, 0.0, t), axis=-1, keepdims=True)
        return jnp.where(half, lo, hi) * inv_d

    n_pairs = RET_WIDTH // LANES
    n_chunks = tm // C
    state = [jnp.where(starts_seq, 0.0, sbd_ref[c]) for c in range(n_pairs)]

    def retention_unit(c, cc):
        ls = slice(c * LANES, (c + 1) * LANES)
        rs = slice(cc * C, (cc + 1) * C)
        S = state[c]
        qb = qkv_ref[prv, 0, rs, ls]
        kb = qkv_ref[prv, 1, rs, ls]
        vb = qkv_ref[prv, 2, rs, ls]
        q = qb.astype(F32)
        q_pair = jnp.concatenate([jnp.where(half, q, 0.0), jnp.where(half, 0.0, q)], axis=0).astype(BF16)
        inner = _dot_nt(q_pair, kb) * dec_ref[c]
        cross = _dot(qb, S.astype(BF16)) * cd_ref[:, ls]
        kd = (kb.astype(F32) * kd_ref[:, ls]).astype(BF16)
        state[c] = S * sd_ref[:, ls] + jnp.where(same_head, _dot_tn(kd, vb), 0.0)
        yield
        o2 = _dot(inner.astype(BF16), vb)
        o = jnp.where(half, o2[:C], o2[C:]) + cross
        yield
        mu = half_mean(o)
        d = o - mu
        var = half_mean(d * d)
        on = d * lax.rsqrt(var + GN_EPS) * gr_ref[:, ls]
        gate = gate_ref[prv, rs, ls]
        ro_ref[0, rs, ls] = (on * (gate * jax.nn.sigmoid(gate))).astype(ro_ref.dtype)
        yield

    def retention_pieces():
        for cc in range(n_chunks):
            units = [retention_unit(c, cc) for c in range(n_pairs)]
            for _ in range(3):
                for u in units:
                    next(u)
                    yield

    j = jnp.minimum(s, n_tiles - 1) % tiles_per_seq
    rows = pl.ds(pl.multiple_of(j * tm, tm), tm)
    steps = _proj_steps(x_ref[0], g_ref, w_ref, cos_ref[rows, :], sa_ref[rows, :], sb_ref[rows, :])
    pieces = retention_pieces()
    n_pieces = 3 * n_chunks * n_pairs
    cols = []
    for i, step in enumerate(steps):
        cols.append(step())
        for _ in range((i + 1) * n_pieces // len(steps) - i * n_pieces // len(steps)):
            next(pieces)
    for c in range(n_pairs):
        S = state[c]
        sbd_ref[c] = S
        s_out_ref[0, 2 * c] = S[:HEAD_DIM, :HEAD_DIM]
        s_out_ref[0, 2 * c + 1] = S[HEAD_DIM:, HEAD_DIM:]
    rq, rk, rv, rg, aq, ak, av = _join_groups(cols)
    aq_ref[0] = aq.astype(aq_ref.dtype)
    ak_ref[0] = ak
    av_ref[0] = av
    qkv_ref[cur, 0] = rq.astype(BF16)
    qkv_ref[cur, 1] = rk.astype(BF16)
    qkv_ref[cur, 2] = rv.astype(BF16)
    gate_ref[cur] = rg


def _proj_retention(x, g, w_bf, g_ret, tm):
    B, T, D = x.shape
    R, A, H, Dh = RET_WIDTH, ATT_WIDTH, N_RET_HEADS, HEAD_DIM
    tps = T // tm
    n_tiles = B * tps
    tabs = _rope_tables(jnp.arange(T, dtype=jnp.int32))
    dec, cd, kd, sd = _retention_consts(RET_CHUNK)
    dec_pair = dec.reshape(H // 2, 2 * RET_CHUNK, RET_CHUNK)

    def this_tile(width):
        def index(s):
            t = jnp.minimum(s, n_tiles - 1)
            return (t // tps, t % tps, 0)
        return pl.BlockSpec((1, tm, width), index)

    def prev_tile(s):
        t = jnp.maximum(s - 1, 0)
        return (t // tps, t % tps, 0)

    full = lambda shape: pl.BlockSpec(shape, lambda s: (0,) * len(shape))
    consts = [g.reshape(1, D), w_bf, *tabs, dec_pair, cd, kd, sd, g_ret.reshape(1, R)]
    return pl.pallas_call(
        functools.partial(_proj_ret_kernel, tps),
        grid=(n_tiles + 1,),
        in_specs=[this_tile(D)] + [full(t.shape) for t in consts],
        out_specs=[pl.BlockSpec((1, tm, R), prev_tile),
                   pl.BlockSpec((1, H, Dh, Dh), lambda s: (jnp.maximum(s - 1, 0) // tps, 0, 0, 0)),
                   this_tile(A), this_tile(A), this_tile(A)],
        out_shape=[jax.ShapeDtypeStruct((B, T, R), BF16), jax.ShapeDtypeStruct((B, H, Dh, Dh), F32),
                   jax.ShapeDtypeStruct((B, T, A), BF16), jax.ShapeDtypeStruct((B, T, A), F32),
                   jax.ShapeDtypeStruct((B, T, A), F32)],
        scratch_shapes=[pltpu.VMEM((R // LANES, LANES, LANES), F32), pltpu.VMEM((2, 3, tm, R), BF16),
                        pltpu.VMEM((2, tm, R), F32)],
        compiler_params=_cparams(("arbitrary",), 56),
        name="proj_retention",
    )(x, *consts)


def _ret_kernel(q_ref, k_ref, v_ref, gate_ref, dec_ref, cd_ref, kd_ref, sd_ref, gr_ref, s0_ref, o_ref, s_ref):
    for b in range(q_ref.shape[0]):
        q = q_ref[b].astype(BF16)
        k = k_ref[b].astype(BF16)
        v = v_ref[b].astype(BF16)
        for h in range(N_RET_HEADS):
            sl = slice(h * HEAD_DIM, (h + 1) * HEAD_DIM)
            qh, kh, vh = q[:, sl], k[:, sl], v[:, sl]
            S = s0_ref[b, h]
            inner = _dot_nt(qh, kh) * dec_ref[h]
            o = _dot(inner.astype(BF16), vh) + _dot(qh, S.astype(BF16)) * cd_ref[:, sl]
            kd = (kh.astype(F32) * kd_ref[:, sl]).astype(BF16)
            s_ref[b, h] = S * sd_ref[:, sl] + _dot_tn(kd, vh)
            mu = jnp.mean(o, axis=-1, keepdims=True)
            var = jnp.mean(jnp.square(o - mu), axis=-1, keepdims=True)
            on = (o - mu) * lax.rsqrt(var + GN_EPS) * gr_ref[:, sl]
            gate = gate_ref[b, :, sl]
            o_ref[b, :, sl] = on * (gate * jax.nn.sigmoid(gate))


def _retention_step(rq, rk, rv, rg, g_ret, state):
    B, T, R = rq.shape
    H, Dh = N_RET_HEADS, HEAD_DIM
    consts = [*_retention_consts(T), g_ret.reshape(1, R)]
    bb = math.gcd(B, 8)
    row = pl.BlockSpec((bb, T, R), lambda b: (b, 0, 0))
    full = lambda shape: pl.BlockSpec(shape, lambda b: (0,) * len(shape))
    st = pl.BlockSpec((bb, H, Dh, Dh), lambda b: (b, 0, 0, 0))
    return pl.pallas_call(
        _ret_kernel,
        grid=(B // bb,),
        in_specs=[row, row, row, row] + [full(t.shape) for t in consts] + [st],
        out_specs=[row, st],
        out_shape=[jax.ShapeDtypeStruct((B, T, R), F32), jax.ShapeDtypeStruct((B, H, Dh, Dh), F32)],
        compiler_params=_cparams(("arbitrary",), 32),
        name="retention",
    )(rq, rk, rv, rg, *consts, state)


def _split3_dot(c, e):
    c1 = c.astype(BF16)
    r1 = c - c1.astype(F32)
    c2 = r1.astype(BF16)
    c3 = (r1 - c2.astype(F32)).astype(BF16)
    return _dot(c1, e) + _dot(c2, e) + _dot(c3, e)


def _chunk_rows(ref, stride):
    def load(start, c):
        idx = pl.ds(start, ATT_BLOCK) if stride == 1 else pl.ds(start, ATT_BLOCK, stride=stride)
        return ref[c, idx, :]
    return load


def _token_rows(ref):
    def load(start, c):
        return ref[0, pl.ds(start, ATT_BLOCK), c * LANES:(c + 1) * LANES].astype(F32)
    return load


def _att_block(load_q, load_k, load_v, bias_ref, cur, prev, first):
    BLK, H = ATT_BLOCK, N_ATT_HEADS
    half = lax.broadcasted_iota(jnp.int32, (BLK, LANES), 1) < HEAD_DIM
    s_parts = []
    for c in range(ATT_CHUNKS):
        q = load_q(cur, c)
        q_pair = jnp.concatenate([jnp.where(half, q, 0.0), jnp.where(half, 0.0, q)], axis=0).astype(BF16)
        k = load_k(cur, c)
        if prev is not None:
            k = jnp.concatenate([load_k(prev, c), k], axis=0)
        s_parts.append(_dot_nt(q_pair, k.astype(BF16)))
    s = jnp.concatenate(s_parts, axis=0)
    if prev is not None:
        s = s + bias_ref[...]
        col = lax.broadcasted_iota(jnp.int32, s.shape, 1)
        s = jnp.where(jnp.logical_and(first, col < BLK), NEG_INF, s)
    else:
        s = s + bias_ref[:, BLK:]
    m = jnp.max(s, axis=-1, keepdims=True)
    p = jnp.exp(s - m)
    l = jnp.sum(p, axis=-1, keepdims=True)
    pb = p.astype(BF16)
    accs = []
    for c in range(ATT_CHUNKS):
        v = load_v(cur, c)
        if prev is not None:
            v = jnp.concatenate([load_v(prev, c), v], axis=0)
        o = _dot(pb[2 * c * BLK:(2 * c + 2) * BLK], v.astype(BF16))
        accs.append(jnp.where(half, o[:BLK], o[BLK:]))
    lane = lax.broadcasted_iota(jnp.int32, (BLK, LANES), 1)
    ml = jnp.zeros((BLK, LANES), F32)
    for h in range(H):
        ml = jnp.where(lane == h, m[h * BLK:(h + 1) * BLK], ml)
        ml = jnp.where(lane == H + h, l[h * BLK:(h + 1) * BLK], ml)
    return accs, ml


def _att_kernel(q_ref, k_ref, v_ref, b1_ref, b4_ref, b16_ref, e_ref, ao_ref, q4, k4, v4, stage, acc4, ml4,
                acc16, ml16):
    BLK, H = ATT_BLOCK, N_ATT_HEADS
    S = q_ref.shape[1]
    nblk = S // BLK
    d4, d16 = DIL_PATTERNS[1][1], DIL_PATTERNS[2][1]
    nb4 = nblk // d4
    per_res = S // d4
    for src, dst in ((q_ref, q4), (k_ref, k4), (v_ref, v4)):
        for c in range(ATT_CHUNKS):
            stage[...] = src[0, :, c * LANES:(c + 1) * LANES].astype(F32)
            for r in range(d4):
                dst[c, r * per_res:(r + 1) * per_res, :] = stage[pl.ds(r, per_res, stride=d4), :]
    contiguous = [_chunk_rows(t, 1) for t in (q4, k4, v4)]
    every_4th = [_chunk_rows(t, d16 // d4) for t in (q4, k4, v4)]
    tokens = [_token_rows(t) for t in (q_ref, k_ref, v_ref)]

    def dilated(n, carry):
        r, j = n // nb4, n % nb4
        cur = r * per_res + j * BLK
        prev = r * per_res + jnp.maximum(j - 1, 0) * BLK
        accs, ml = _att_block(*contiguous, b4_ref, cur, prev, j == 0)
        tok = r + j * (BLK * d4)
        for c in range(ATT_CHUNKS):
            acc4[c, pl.ds(tok, BLK, stride=d4), :] = accs[c]
        ml4[pl.ds(tok, BLK, stride=d4), :] = ml
        accs, ml = _att_block(*every_4th, b16_ref, (n % d4) * per_res + n // d4, None, None)
        for c in range(ATT_CHUNKS):
            acc16[c, pl.ds(n, BLK, stride=d16), :] = accs[c]
        ml16[pl.ds(n, BLK, stride=d16), :] = ml
        return carry

    def dense_and_merge(j, carry):
        cur = pl.multiple_of(j * BLK, BLK)
        prev = pl.multiple_of(jnp.maximum(j - 1, 0) * BLK, BLK)
        accs1, ml1 = _att_block(*tokens, b1_ref, cur, prev, j == 0)
        rows = pl.ds(cur, BLK)
        mls = [ml1, ml4[rows, :], ml16[rows, :]]
        accs = [accs1, [acc4[c, rows, :] for c in range(ATT_CHUNKS)], [acc16[c, rows, :] for c in range(ATT_CHUNKS)]]
        lane = lax.broadcasted_iota(jnp.int32, (BLK, LANES), 1)
        m_all = jnp.maximum(jnp.maximum(mls[0], mls[1]), mls[2])
        ws = [jnp.exp(ml - m_all) for ml in mls]
        den = sum(w * pltpu.roll(ml, LANES - H, 1) for w, ml in zip(ws, mls))
        ao = [jnp.zeros((BLK, LANES), F32) for _ in range(ATT_CHUNKS)]
        for w, acc in zip(ws, accs):
            coef = _split3_dot(jnp.where(lane < H, w / den, 0.0), e_ref[...])
            for c in range(ATT_CHUNKS):
                ao[c] = ao[c] + coef[:, c * LANES:(c + 1) * LANES] * acc[c]
        ao_ref[0, rows, :] = jnp.concatenate(ao, axis=-1).astype(ao_ref.dtype)
        return carry

    lax.fori_loop(0, nblk, dilated, 0)
    lax.fori_loop(0, nblk, dense_and_merge, 0)


def _prompt_attention(aq, ak, av, rel_bias):
    B, S, A = aq.shape
    CH = ATT_CHUNKS
    assert [d for _, d in DIL_PATTERNS] == [1, 4, 16] and S // ATT_BLOCK == 16
    b1, b4, b16 = _prompt_bias_tables(rel_bias)
    head_of_lane = jnp.arange(A)[None, :] // HEAD_DIM
    expand = (jnp.arange(LANES)[:, None] == head_of_lane).astype(BF16)
    seq = pl.BlockSpec((1, S, A), lambda b: (b, 0, 0))
    const = lambda shape: pl.BlockSpec(shape, lambda b: (0,) * len(shape), pipeline_mode=pl.Buffered(1))
    chunked = pltpu.VMEM((CH, S, LANES), F32)
    stats = pltpu.VMEM((S, LANES), F32)
    return pl.pallas_call(
        _att_kernel,
        grid=(B,),
        in_specs=[seq, seq, seq, const(b1.shape), const(b4.shape), const(b16.shape), const(expand.shape)],
        out_specs=seq,
        out_shape=jax.ShapeDtypeStruct((B, S, A), BF16),
        scratch_shapes=[chunked, chunked, chunked, stats, chunked, stats, chunked, stats],
        compiler_params=_cparams(("arbitrary",), 58),
        name="prompt_att",
    )(aq, ak, av, b1, b4, b16, expand)


def _window_pieces(lows, T, kT_ref, vT_ref, qn_ref, kn_ref, vn_ref, c1_ref, c4_ref, c16_ref,
                   n1_ref, n4_ref, n16_ref, koT_ref, voT_ref, ao_ref):
    H, Dh, L = kT_ref.shape[1:]
    A = H * Dh
    zpad = jnp.zeros((LANES - T, A), F32)
    tail_lane = lax.broadcasted_iota(jnp.int32, (Dh, LANES), 1) >= LANES - T
    shifted = {}

    def shift_in(name, xT_ref, new_ref, out_ref):
        new_p = jnp.concatenate([zpad, new_ref[0]], axis=0)
        new_t = new_p.T
        heads = []
        for h in range(H):
            x = xT_ref[0, h]
            rolled = pltpu.roll(x, L - T, 1)
            tail = jnp.where(tail_lane, new_t[h * Dh:(h + 1) * Dh], rolled[:, L - LANES:])
            out_ref[0, h] = jnp.concatenate([rolled[:, :L - LANES], tail], axis=1)
            heads.append(x.astype(BF16))
            yield
        shifted[name] = (jnp.concatenate(heads, axis=0), new_p.astype(BF16))

    yield from shift_in("k", kT_ref, kn_ref, koT_ref)
    yield from shift_in("v", vT_ref, vn_ref, voT_ref)
    (kT, kn), (vT, vn) = shifted["k"], shifted["v"]
    HT = H * T
    row_head = lax.broadcasted_iota(jnp.int32, (HT, A), 0) // T
    lane_head = lax.broadcasted_iota(jnp.int32, (HT, A), 1) // HEAD_DIM
    diag = row_head == lane_head
    q_rows = jnp.where(diag, jnp.concatenate([qn_ref[0]] * H, axis=0), 0.0).astype(BF16)
    s_cache = _dot(q_rows, kT)
    s_new = _dot_nt(q_rows, kn)
    yield
    stats = []
    for lo, c_ref, n_ref in zip(lows, (c1_ref, c4_ref, c16_ref), (n1_ref, n4_ref, n16_ref)):
        sc = s_cache[:, lo:] + c_ref[...]
        sn = s_new + n_ref[...]
        m = jnp.maximum(jnp.max(sc, axis=-1, keepdims=True), jnp.max(sn, axis=-1, keepdims=True))
        pc = jnp.exp(sc - m)
        pn = jnp.exp(sn - m)
        l = jnp.sum(pc, axis=-1, keepdims=True) + jnp.sum(pn, axis=-1, keepdims=True)
        acc = _dot_nt(pc.astype(BF16), vT[:, lo:]) + _dot(pn.astype(BF16), vn)
        stats.append((m, l, acc))
        yield
    m_all = jnp.maximum(jnp.maximum(stats[0][0], stats[1][0]), stats[2][0])
    num = jnp.zeros((HT, A), F32)
    den = jnp.zeros((HT, 1), F32)
    for m, l, acc in stats:
        w = jnp.exp(m - m_all)
        num = num + w * acc
        den = den + w * l
    comb = jnp.where(diag, num / den, 0.0)
    out = comb[0:T]
    for h in range(1, H):
        out = out + comb[h * T:(h + 1) * T]
    ao_ref[0] = out
    yield


def _sample_window_operands(aq, ak, av, cache_kT, cache_vT, rel_bias, groups):
    B, T, A = aq.shape
    _, H, Dh, L = cache_kT.shape
    Hg, Ag = H // groups, A // groups
    lows = tuple(L - win for win, _ in DIL_PATTERNS)
    assert all(lo >= 0 and lo % LANES == 0 for lo in lows) and T <= LANES and Ag % LANES == 0
    tabs = _sample_bias_tables(rel_bias, L, T)
    big = pl.BlockSpec((1, Hg, Dh, L), lambda i: (i // groups, i % groups, 0, 0))
    small = pl.BlockSpec((1, T, Ag), lambda i: (i // groups, 0, i % groups))
    tab = lambda t: pl.BlockSpec((Hg * T, t.shape[1]), lambda i: (i % groups, 0))
    win_sds = jax.ShapeDtypeStruct((B, H, Dh, L), F32)
    args = [cache_kT, cache_vT, aq, ak, av, *tabs]
    in_specs = [big, big, small, small, small] + [tab(t) for t in tabs]
    out_specs = [big, big, small]
    out_shape = [win_sds, win_sds, jax.ShapeDtypeStruct((B, T, A), F32)]
    return functools.partial(_window_pieces, lows, T), args, in_specs, out_specs, out_shape


def _ffn_pieces(d_ff, ff_chunk, x_ref, ro_ref, ao_ref, wo_ref, wgu_ref, wd_ref, g_pm, g_pf, g_of, y_ref, act_ref):
    R = RET_WIDTH
    mix = _dot(ro_ref[...].astype(BF16), wo_ref[:R, :]) + _dot(ao_ref[...].astype(BF16), wo_ref[R:, :])
    x1 = x_ref[...] + _rms(mix, g_pm[...])
    h = _rms(x1, g_pf[...]).astype(BF16)
    yield
    for c in range(0, d_ff, ff_chunk):
        gate = _dot(h, wgu_ref[:, c:c + ff_chunk])
        up = _dot(h, wgu_ref[:, d_ff + c:d_ff + c + ff_chunk])
        act_ref[:, c:c + ff_chunk] = (gate * jax.nn.sigmoid(gate) * up).astype(BF16)
        yield
    f = _dot(act_ref[...], wd_ref[...])
    y_ref[...] = x1 + _rms(f, g_of[...])
    yield


def _out_kernel(d_ff, ff_chunk, *refs):
    for _ in _ffn_pieces(d_ff, ff_chunk, *refs):
        pass


def _out_window_kernel(d_ff, ff_chunk, window, n_ffn_in, n_win_in, *refs):
    ffn_in = refs[:n_ffn_in]
    win_in = refs[n_ffn_in:n_ffn_in + n_win_in]
    y_ref, koT_ref, voT_ref, sao_ref, act_ref = refs[n_ffn_in + n_win_in:]
    win = window(*win_in, koT_ref, voT_ref, sao_ref)
    for _ in _ffn_pieces(d_ff, ff_chunk, *ffn_in, y_ref, act_ref):
        next(win, None)
    for _ in win:
        pass


def _out_block(x, ro, ao, wo_bf, wgu_bf, wd_bf, g_post_mix, g_pre_ffn, g_post_ffn, tm, window=None):
    N, D = x.shape
    d_ff = wd_bf.shape[0]
    row = lambda width: pl.BlockSpec((tm, width), lambda i: (i, 0))
    const = lambda shape: pl.BlockSpec(shape, lambda i: (0,) * len(shape), pipeline_mode=pl.Buffered(1))
    args = [x, ro, ao, wo_bf, wgu_bf, wd_bf, g_post_mix.reshape(1, D), g_pre_ffn.reshape(1, D),
            g_post_ffn.reshape(1, D)]
    in_specs = [row(D), row(RET_WIDTH), row(ATT_WIDTH), const(wo_bf.shape), const(wgu_bf.shape),
                const(wd_bf.shape), const((1, D)), const((1, D)), const((1, D))]
    body = functools.partial(_out_kernel, d_ff, 256)
    out_specs, out_shape = [row(D)], [jax.ShapeDtypeStruct((N, D), F32)]
    if window is not None:
        win_body, win_args, win_in_specs, win_out_specs, win_out_shape = window
        assert N // tm == win_out_shape[0].shape[0] * (win_out_shape[0].shape[1] // win_out_specs[0].block_shape[1])
        body = functools.partial(_out_window_kernel, d_ff, 256, win_body, len(args), len(win_args))
        args, in_specs = args + win_args, in_specs + win_in_specs
        out_specs, out_shape = out_specs + win_out_specs, out_shape + win_out_shape
    outs = pl.pallas_call(
        body,
        grid=(N // tm,),
        in_specs=in_specs,
        out_specs=out_specs,
        out_shape=out_shape,
        scratch_shapes=[pltpu.VMEM((tm, d_ff), BF16)],
        compiler_params=_cparams(("arbitrary",), 56),
        name="out_ffn",
    )(*args)
    return outs[0] if window is None else outs


def kernel(x_prompt, x_sample, state_ret, cache_k_win, cache_v_win, rel_bias, w_in, g_ret, w_out,
           g_pre_mix, g_post_mix, g_pre_ffn, g_post_ffn, w_gu, w_down):
    depth = w_in.shape[0]
    assert depth == 1
    B, S, D = x_prompt.shape
    Bs, Ts, _ = x_sample.shape
    H, Dh = N_ATT_HEADS, HEAD_DIM
    l = 0
    w_in_bf = w_in[l].astype(BF16)
    wo_bf = w_out[l].astype(BF16)
    wgu_bf = w_gu[l].astype(BF16)
    wd_bf = w_down[l].astype(BF16)

    pos_s = PAST_LEN + jnp.arange(Ts, dtype=jnp.int32)
    N_s = Bs * Ts
    outs = _proj(x_sample.reshape(N_s, D), g_pre_mix[l], w_in_bf, jnp.tile(pos_s, Bs))
    srq, srk, srv, srg, saq, sak, sav = [t.reshape(Bs, Ts, RET_WIDTH) for t in outs]
    sro, s_s = _retention_step(srq, srk, srv, srg, g_ret[l], state_ret[l])
    to_t = lambda t: jnp.transpose(t, (0, 2, 3, 1))
    from_t = lambda t: jnp.transpose(t, (0, 3, 1, 2))
    window = _sample_window_operands(saq, sak, sav, to_t(cache_k_win[l]), to_t(cache_v_win[l]), rel_bias, 2)

    ro, s_p, aq, ak, av = _proj_retention(x_prompt, g_pre_mix[l], w_in_bf, g_ret[l], 512)
    ao = _prompt_attention(aq, ak, av, rel_bias)
    y_p, k_sT, v_sT, sao = _out_block(x_prompt.reshape(B * S, D), ro.reshape(B * S, RET_WIDTH),
                                      ao.reshape(B * S, ATT_WIDTH), wo_bf, wgu_bf, wd_bf, g_post_mix[l],
                                      g_pre_ffn[l], g_post_ffn[l], 256, window)
    k_s, v_s = from_t(k_sT), from_t(v_sT)

    y_s = _out_block(x_sample.reshape(N_s, D), sro.reshape(N_s, RET_WIDTH), sao.reshape(N_s, ATT_WIDTH),
                     wo_bf, wgu_bf, wd_bf, g_post_mix[l], g_pre_ffn[l], g_post_ffn[l], N_s)

    return (y_p.reshape(B, S, D), y_s.reshape(Bs, Ts, D),
            s_p[None], ak.reshape(1, B, S, H, Dh), av.reshape(1, B, S, H, Dh),
            s_s[None], k_s[None], v_s[None])
```

```python
import functools
import math

import jax
import jax.numpy as jnp
from jax import lax
from jax.experimental import pallas as pl
from jax.experimental.pallas import tpu as pltpu

F32 = jnp.float32
BF16 = jnp.bfloat16

HEAD_DIM = 64
N_RET_HEADS = 8
N_ATT_HEADS = 8
RET_WIDTH = N_RET_HEADS * HEAD_DIM
ATT_WIDTH = N_ATT_HEADS * HEAD_DIM
RET_CHUNK = 128
DIL_PATTERNS = ((128, 1), (512, 4), (2048, 16))
ATT_BLOCK = 128
N_BUCKETS = 32
MAX_DISTANCE = 2048
ROPE_BASE = 10000.0
NORM_EPS = 1e-6
GN_EPS = 1e-5
PAST_LEN = 16384
LANES = 128
ATT_CHUNKS = ATT_WIDTH // LANES
NEG_INF = float("-inf")


def _cparams(sem, vmem_mb):
    return pltpu.CompilerParams(dimension_semantics=sem, vmem_limit_bytes=vmem_mb * 1024 * 1024)


def _rms(x, g):
    return x * lax.rsqrt(jnp.mean(x * x, axis=-1, keepdims=True) + NORM_EPS) * g


def _dot(a, b):
    return jnp.dot(a, b, preferred_element_type=F32)


def _dot_nt(a, b):
    return lax.dot_general(a, b, (((1,), (1,)), ((), ())), preferred_element_type=F32)


def _dot_tn(a, b):
    return lax.dot_general(a, b, (((0,), (0,)), ((), ())), preferred_element_type=F32)


def _rope_tables(pos):
    inv = ROPE_BASE ** (-jnp.arange(0, HEAD_DIM, 2, dtype=F32) / HEAD_DIM)
    ang = pos.astype(F32)[:, None] * inv[None, :]
    cos, sin = jnp.cos(ang), jnp.sin(ang)
    zero = jnp.zeros_like(sin)
    rep = LANES // HEAD_DIM
    cosf = jnp.tile(jnp.concatenate([cos, cos], axis=-1), (1, rep))
    sa = jnp.tile(jnp.concatenate([-sin, zero], axis=-1), (1, rep))
    sb = jnp.tile(jnp.concatenate([zero, sin], axis=-1), (1, rep))
    return cosf, sa, sb


def _retention_consts(C):
    log_g = jnp.log1p(-jnp.exp2(-5.0 - jnp.arange(N_RET_HEADS, dtype=F32)))
    i = jnp.arange(C, dtype=F32)
    rel = i[:, None] - i[None, :]
    decay = jnp.where(rel[None] >= 0, jnp.exp(jnp.maximum(rel, 0.0)[None] * log_g[:, None, None]), 0.0)
    cross = jnp.exp((i + 1.0)[:, None] * log_g[None, :])
    kdec = jnp.exp((C - 1.0 - i)[:, None] * log_g[None, :])
    sdec = jnp.exp(C * log_g)[None, :]
    expand = lambda t: jnp.repeat(t, HEAD_DIM, axis=-1)
    return decay, expand(cross), expand(kdec), expand(sdec)


def _t5_bucket(dist):
    max_exact = N_BUCKETS // 2
    d_f = jnp.maximum(dist, 1).astype(F32)
    large = max_exact + (jnp.log(d_f / max_exact) / math.log(MAX_DISTANCE / max_exact)
                         * (N_BUCKETS - max_exact)).astype(jnp.int32)
    large = jnp.minimum(large, N_BUCKETS - 1)
    return jnp.where(dist < max_exact, dist, large)


def _pattern_bias_rev(rel_bias, dil, nk):
    dist = jnp.arange(nk, -1, -1, dtype=jnp.int32) * dil
    return rel_bias[_t5_bucket(dist)].astype(F32).T


def _prompt_bias_tables(rel_bias):
    BLK, H = ATT_BLOCK, N_ATT_HEADS
    out = []
    for win, dil in DIL_PATTERNS:
        nk = win // dil
        assert nk == BLK
        period = 3 * BLK
        v = jnp.concatenate([_pattern_bias_rev(rel_bias, dil, nk),
                             jnp.full((H, period - nk - 1), NEG_INF, F32)], axis=1)
        flat = jnp.tile(v, (1, BLK))[:, :BLK * (period - 1)]
        skew = flat.reshape(H, BLK, period - 1)[:, :, :2 * BLK]
        out.append(skew.reshape(H * BLK, 2 * BLK))
    return out


def _sample_bias_tables(rel_bias, L, T):
    H = N_ATT_HEADS
    cache_part, new_part = [], []
    for win, dil in DIL_PATTERNS:
        nk = win // dil
        rev = _pattern_bias_rev(rel_bias, dil, nk)
        if dil > 1:
            gaps = jnp.full((H, nk + 1, dil - 1), NEG_INF, F32)
            rev = jnp.concatenate([rev[:, :, None], gaps], axis=2).reshape(H, (nk + 1) * dil)[:, :nk * dil + 1]
        pv = jnp.concatenate([jnp.full((H, T), NEG_INF, F32), rev, jnp.full((H, 2 * T), NEG_INF, F32)], axis=1)
        width = win + T
        rows = jnp.stack([pv[:, T - t:T - t + width] for t in range(T)], axis=1)
        cache_part.append(rows[:, :, :win].reshape(H * T, win))
        new = jnp.concatenate([jnp.full((H, T, LANES - T), NEG_INF, F32), rows[:, :, win:]], axis=2)
        new_part.append(new.reshape(H * T, LANES))
    return cache_part + new_part


def _proj_steps(x, g_ref, w_ref, cos, sa, sb):
    h = _rms(x, g_ref[...]).astype(BF16)
    W = 2 * LANES
    rep = W // LANES
    cosf = jnp.concatenate([cos] * rep, axis=-1)
    saf = jnp.concatenate([sa] * rep, axis=-1)
    sbf = jnp.concatenate([sb] * rep, axis=-1)
    half = HEAD_DIM // 2

    def col(c):
        return _dot(h, w_ref[:, c:c + W])

    def rope(z):
        return z * cosf + pltpu.roll(z, W - half, 1) * saf + pltpu.roll(z, half, 1) * sbf

    R, A = RET_WIDTH, ATT_WIDTH
    scale = HEAD_DIM ** -0.5
    steps = []
    for c in range(0, R, W):
        steps.append(lambda c=c: rope(col(c)))
    for c in range(R, 2 * R, W):
        steps.append(lambda c=c: rope(col(c)) * scale)
    for c in range(2 * R, 4 * R, W):
        steps.append(lambda c=c: col(c))
    for c in range(4 * R, 4 * R + A, W):
        steps.append(lambda c=c: col(c) * scale)
    for c in range(4 * R + A, 4 * R + 3 * A, W):
        steps.append(lambda c=c: col(c))
    return steps


def _join_groups(pieces):
    return [jnp.concatenate(pieces[i:i + 2], axis=-1) for i in range(0, len(pieces), 2)]


def _proj_kernel(x_ref, g_ref, w_ref, cos_ref, sa_ref, sb_ref, *outs):
    steps = _proj_steps(x_ref[...], g_ref, w_ref, cos_ref[...], sa_ref[...], sb_ref[...])
    for ref, z in zip(outs, _join_groups([step() for step in steps])):
        ref[...] = z


def _proj(x, g, w_bf, pos):
    N, D = x.shape
    tabs = _rope_tables(pos)
    full = lambda shape: pl.BlockSpec(shape, lambda i: (0,) * len(shape))
    return pl.pallas_call(
        _proj_kernel,
        grid=(1,),
        in_specs=[full((N, D)), full((1, D)), full(w_bf.shape)] + [full(t.shape) for t in tabs],
        out_specs=[full((N, RET_WIDTH))] * 7,
        out_shape=[jax.ShapeDtypeStruct((N, RET_WIDTH), F32)] * 7,
        compiler_params=_cparams(("arbitrary",), 40),
        name="proj",
    )(x, g.reshape(1, D), w_bf, *tabs)


def _proj_ret_kernel(tiles_per_seq, x_ref, g_ref, w_ref, cos_ref, sa_ref, sb_ref, dec_ref, cd_ref, kd_ref,
                     sd_ref, gr_ref, wa_ref, wb_ref, wc_ref, ro_ref, s_out_ref, aq_ref, ak_ref, av_ref,
                     wa_bf_ref, wb_bf_ref, wc_bf_ref, sbd_ref, qkv_ref, gate_ref):
    s = pl.program_id(0)
    n_tiles = pl.num_programs(0) - 1
    tm = x_ref.shape[1]
    C = RET_CHUNK
    cur = s % 2
    prv = 1 - cur

    @pl.when(s == 0)
    def _():
        sbd_ref[...] = jnp.zeros_like(sbd_ref)
        qkv_ref[1] = jnp.zeros(qkv_ref.shape[1:], qkv_ref.dtype)
        gate_ref[1] = jnp.zeros(gate_ref.shape[1:], gate_ref.dtype)

    for w32, w16 in ((wa_ref, wa_bf_ref), (wb_ref, wb_bf_ref), (wc_ref, wc_bf_ref)):
        w16[...] = w32[...].astype(w16.dtype)

    lane = lax.broadcasted_iota(jnp.int32, (C, LANES), 1)
    half = lane < HEAD_DIM
    same_head = (lax.broadcasted_iota(jnp.int32, (LANES, LANES), 0) // HEAD_DIM
                 == lax.broadcasted_iota(jnp.int32, (LANES, LANES), 1) // HEAD_DIM)
    inv_d = 1.0 / HEAD_DIM
    starts_seq = (s - 1) % tiles_per_seq == 0

    def half_mean(t):
        lo = jnp.sum(jnp.where(half, t, 0.0), axis=-1, keepdims=True)
        hi = jnp.sum(jnp.where(half, 0.0, t), axis=-1, keepdims=True)
        return jnp.where(half, lo, hi) * inv_d

    n_pairs = RET_WIDTH // LANES
    n_chunks = tm // C
    state = [jnp.where(starts_seq, 0.0, sbd_ref[c]) for c in range(n_pairs)]

    def retention_unit(c, cc):
        ls = slice(c * LANES, (c + 1) * LANES)
        rs = slice(cc * C, (cc + 1) * C)
        S = state[c]
        qb = qkv_ref[prv, 0, rs, ls]
        kb = qkv_ref[prv, 1, rs, ls]
        vb = qkv_ref[prv, 2, rs, ls]
        q = qb.astype(F32)
        q_pair = jnp.concatenate([jnp.where(half, q, 0.0), jnp.where(half, 0.0, q)], axis=0).astype(BF16)
        inner = _dot_nt(q_pair, kb) * dec_ref[c]
        cross = _dot(qb, S.astype(BF16)) * cd_ref[:, ls]
        kd = (kb.astype(F32) * kd_ref[:, ls]).astype(BF16)
        state[c] = S * sd_ref[:, ls] + jnp.where(same_head, _dot_tn(kd, vb), 0.0)
        yield
        o2 = _dot(inner.astype(BF16), vb)
        o = jnp.where(half, o2[:C], o2[C:]) + cross
        yield
        mu = half_mean(o)
        d = o - mu
        var = half_mean(d * d)
        on = d * lax.rsqrt(var + GN_EPS) * gr_ref[:, ls]
        gate = gate_ref[prv, rs, ls]
        ro_ref[0, rs, ls] = (on * (gate * jax.nn.sigmoid(gate))).astype(ro_ref.dtype)
        yield

    def retention_pieces():
        for cc in range(n_chunks):
            units = [retention_unit(c, cc) for c in range(n_pairs)]
            for _ in range(3):
                for u in units:
                    next(u)
                    yield

    j = jnp.minimum(s, n_tiles - 1) % tiles_per_seq
    rows = pl.ds(pl.multiple_of(j * tm, tm), tm)
    steps = _proj_steps(x_ref[0], g_ref, w_ref, cos_ref[rows, :], sa_ref[rows, :], sb_ref[rows, :])
    pieces = retention_pieces()
    n_pieces = 3 * n_chunks * n_pairs
    cols = []
    for i, step in enumerate(steps):
        cols.append(step())
        for _ in range((i + 1) * n_pieces // len(steps) - i * n_pieces // len(steps)):
            next(pieces)
    for c in range(n_pairs):
        S = state[c]
        sbd_ref[c] = S
        s_out_ref[0, 2 * c] = S[:HEAD_DIM, :HEAD_DIM]
        s_out_ref[0, 2 * c + 1] = S[HEAD_DIM:, HEAD_DIM:]
    rq, rk, rv, rg, aq, ak, av = _join_groups(cols)
    aq_ref[0] = aq.astype(aq_ref.dtype)
    ak_ref[0] = ak
    av_ref[0] = av
    qkv_ref[cur, 0] = rq.astype(BF16)
    qkv_ref[cur, 1] = rk.astype(BF16)
    qkv_ref[cur, 2] = rv.astype(BF16)
    gate_ref[cur] = rg


def _proj_retention(x, g, w_bf, g_ret, tm, later_weights):
    B, T, D = x.shape
    R, A, H, Dh = RET_WIDTH, ATT_WIDTH, N_RET_HEADS, HEAD_DIM
    tps = T // tm
    n_tiles = B * tps
    tabs = _rope_tables(jnp.arange(T, dtype=jnp.int32))
    dec, cd, kd, sd = _retention_consts(RET_CHUNK)
    dec_pair = dec.reshape(H // 2, 2 * RET_CHUNK, RET_CHUNK)

    def this_tile(width):
        def index(s):
            t = jnp.minimum(s, n_tiles - 1)
            return (t // tps, t % tps, 0)
        return pl.BlockSpec((1, tm, width), index)

    def prev_tile(s):
        t = jnp.maximum(s - 1, 0)
        return (t // tps, t % tps, 0)

    def slab(w):
        rows = max(16, -(-w.shape[0] // n_tiles) // 16 * 16)
        while w.shape[0] % rows:
            rows += 16
        last = w.shape[0] // rows - 1
        return pl.BlockSpec((rows, w.shape[1]), lambda s: (jnp.minimum(s, last), 0))

    full = lambda shape: pl.BlockSpec(shape, lambda s: (0,) * len(shape))
    consts = [g.reshape(1, D), w_bf, *tabs, dec_pair, cd, kd, sd, g_ret.reshape(1, R)]
    slabs = [slab(w) for w in later_weights]
    return pl.pallas_call(
        functools.partial(_proj_ret_kernel, tps),
        grid=(n_tiles + 1,),
        in_specs=[this_tile(D)] + [full(t.shape) for t in consts] + slabs,
        out_specs=[pl.BlockSpec((1, tm, R), prev_tile),
                   pl.BlockSpec((1, H, Dh, Dh), lambda s: (jnp.maximum(s - 1, 0) // tps, 0, 0, 0)),
                   this_tile(A), this_tile(A), this_tile(A)] + slabs,
        out_shape=[jax.ShapeDtypeStruct((B, T, R), BF16), jax.ShapeDtypeStruct((B, H, Dh, Dh), F32),
                   jax.ShapeDtypeStruct((B, T, A), BF16), jax.ShapeDtypeStruct((B, T, A), F32),
                   jax.ShapeDtypeStruct((B, T, A), F32)]
                  + [jax.ShapeDtypeStruct(w.shape, BF16) for w in later_weights],
        scratch_shapes=[pltpu.VMEM((R // LANES, LANES, LANES), F32), pltpu.VMEM((2, 3, tm, R), BF16),
                        pltpu.VMEM((2, tm, R), F32)],
        compiler_params=_cparams(("arbitrary",), 56),
        name="proj_retention",
    )(x, *consts, *later_weights)


def _ret_kernel(q_ref, k_ref, v_ref, gate_ref, dec_ref, cd_ref, kd_ref, sd_ref, gr_ref, s0_ref, o_ref, s_ref):
    def unit(b, h):
        sl = slice(h * HEAD_DIM, (h + 1) * HEAD_DIM)
        qh = q_ref[b, :, sl].astype(BF16)
        kh = k_ref[b, :, sl].astype(BF16)
        vh = v_ref[b, :, sl].astype(BF16)
        S = s0_ref[b, h]
        inner = _dot_nt(qh, kh) * dec_ref[h]
        cross = _dot(qh, S.astype(BF16)) * cd_ref[:, sl]
        kd = (kh.astype(F32) * kd_ref[:, sl]).astype(BF16)
        s_ref[b, h] = S * sd_ref[:, sl] + _dot_tn(kd, vh)
        yield
        o = _dot(inner.astype(BF16), vh) + cross
        yield
        mu = jnp.mean(o, axis=-1, keepdims=True)
        var = jnp.mean(jnp.square(o - mu), axis=-1, keepdims=True)
        on = (o - mu) * lax.rsqrt(var + GN_EPS) * gr_ref[:, sl]
        gate = gate_ref[b, :, sl]
        o_ref[b, :, sl] = on * (gate * jax.nn.sigmoid(gate))
        yield

    for b in range(q_ref.shape[0]):
        units = [unit(b, h) for h in range(N_RET_HEADS)]
        for _ in range(3):
            for u in units:
                next(u)


def _retention_step(rq, rk, rv, rg, g_ret, state):
    B, T, R = rq.shape
    H, Dh = N_RET_HEADS, HEAD_DIM
    consts = [*_retention_consts(T), g_ret.reshape(1, R)]
    bb = math.gcd(B, 8)
    row = pl.BlockSpec((bb, T, R), lambda b: (b, 0, 0))
    full = lambda shape: pl.BlockSpec(shape, lambda b: (0,) * len(shape))
    st = pl.BlockSpec((bb, H, Dh, Dh), lambda b: (b, 0, 0, 0))
    return pl.pallas_call(
        _ret_kernel,
        grid=(B // bb,),
        in_specs=[row, row, row, row] + [full(t.shape) for t in consts] + [st],
        out_specs=[row, st],
        out_shape=[jax.ShapeDtypeStruct((B, T, R), F32), jax.ShapeDtypeStruct((B, H, Dh, Dh), F32)],
        compiler_params=_cparams(("arbitrary",), 32),
        name="retention",
    )(rq, rk, rv, rg, *consts, state)


def _split3_dot(c, e):
    c1 = c.astype(BF16)
    r1 = c - c1.astype(F32)
    c2 = r1.astype(BF16)
    c3 = (r1 - c2.astype(F32)).astype(BF16)
    return _dot(c1, e) + _dot(c2, e) + _dot(c3, e)


def _chunk_rows(ref, stride):
    def load(start, c):
        idx = pl.ds(start, ATT_BLOCK) if stride == 1 else pl.ds(start, ATT_BLOCK, stride=stride)
        return ref[c, idx, :]
    return load


def _token_rows(ref):
    def load(start, c):
        return ref[0, pl.ds(start, ATT_BLOCK), c * LANES:(c + 1) * LANES].astype(F32)
    return load


def _att_block(load_q, load_k, load_v, bias_ref, cur, prev, first):
    BLK, H = ATT_BLOCK, N_ATT_HEADS
    half = lax.broadcasted_iota(jnp.int32, (BLK, LANES), 1) < HEAD_DIM
    s_parts = []
    for c in range(ATT_CHUNKS):
        q = load_q(cur, c)
        q_pair = jnp.concatenate([jnp.where(half, q, 0.0), jnp.where(half, 0.0, q)], axis=0).astype(BF16)
        k = load_k(cur, c)
        if prev is not None:
            k = jnp.concatenate([load_k(prev, c), k], axis=0)
        s_parts.append(_dot_nt(q_pair, k.astype(BF16)))
    s = jnp.concatenate(s_parts, axis=0)
    if prev is not None:
        s = s + bias_ref[...]
        col = lax.broadcasted_iota(jnp.int32, s.shape, 1)
        s = jnp.where(jnp.logical_and(first, col < BLK), NEG_INF, s)
    else:
        s = s + bias_ref[:, BLK:]
    m = jnp.max(s, axis=-1, keepdims=True)
    p = jnp.exp(s - m)
    l = jnp.sum(p, axis=-1, keepdims=True)
    pb = p.astype(BF16)
    accs = []
    for c in range(ATT_CHUNKS):
        v = load_v(cur, c)
        if prev is not None:
            v = jnp.concatenate([load_v(prev, c), v], axis=0)
        o = _dot(pb[2 * c * BLK:(2 * c + 2) * BLK], v.astype(BF16))
        accs.append(jnp.where(half, o[:BLK], o[BLK:]))
    lane = lax.broadcasted_iota(jnp.int32, (BLK, LANES), 1)
    ml = jnp.zeros((BLK, LANES), F32)
    for h in range(H):
        ml = jnp.where(lane == h, m[h * BLK:(h + 1) * BLK], ml)
        ml = jnp.where(lane == H + h, l[h * BLK:(h + 1) * BLK], ml)
    return accs, ml


def _att_kernel(q_ref, k_ref, v_ref, b1_ref, b4_ref, b16_ref, e_ref, ao_ref, q4, k4, v4, stage, acc4, ml4,
                acc16, ml16):
    BLK, H = ATT_BLOCK, N_ATT_HEADS
    S = q_ref.shape[1]
    nblk = S // BLK
    d4, d16 = DIL_PATTERNS[1][1], DIL_PATTERNS[2][1]
    nb4 = nblk // d4
    per_res = S // d4
    for src, dst in ((q_ref, q4), (k_ref, k4), (v_ref, v4)):
        for c in range(ATT_CHUNKS):
            stage[...] = src[0, :, c * LANES:(c + 1) * LANES].astype(F32)
            for r in range(d4):
                dst[c, r * per_res:(r + 1) * per_res, :] = stage[pl.ds(r, per_res, stride=d4), :]
    contiguous = [_chunk_rows(t, 1) for t in (q4, k4, v4)]
    every_4th = [_chunk_rows(t, d16 // d4) for t in (q4, k4, v4)]
    tokens = [_token_rows(t) for t in (q_ref, k_ref, v_ref)]

    def dilated(n, carry):
        r, j = n // nb4, n % nb4
        cur = r * per_res + j * BLK
        prev = r * per_res + jnp.maximum(j - 1, 0) * BLK
        accs, ml = _att_block(*contiguous, b4_ref, cur, prev, j == 0)
        tok = r + j * (BLK * d4)
        for c in range(ATT_CHUNKS):
            acc4[c, pl.ds(tok, BLK, stride=d4), :] = accs[c]
        ml4[pl.ds(tok, BLK, stride=d4), :] = ml
        accs, ml = _att_block(*every_4th, b16_ref, (n % d4) * per_res + n // d4, None, None)
        for c in range(ATT_CHUNKS):
            acc16[c, pl.ds(n, BLK, stride=d16), :] = accs[c]
        ml16[pl.ds(n, BLK, stride=d16), :] = ml
        return carry

    def dense_and_merge(j, carry):
        cur = pl.multiple_of(j * BLK, BLK)
        prev = pl.multiple_of(jnp.maximum(j - 1, 0) * BLK, BLK)
        accs1, ml1 = _att_block(*tokens, b1_ref, cur, prev, j == 0)
        rows = pl.ds(cur, BLK)
        mls = [ml1, ml4[rows, :], ml16[rows, :]]
        accs = [accs1, [acc4[c, rows, :] for c in range(ATT_CHUNKS)], [acc16[c, rows, :] for c in range(ATT_CHUNKS)]]
        lane = lax.broadcasted_iota(jnp.int32, (BLK, LANES), 1)
        m_all = jnp.maximum(jnp.maximum(mls[0], mls[1]), mls[2])
        ws = [jnp.exp(ml - m_all) for ml in mls]
        den = sum(w * pltpu.roll(ml, LANES - H, 1) for w, ml in zip(ws, mls))
        ao = [jnp.zeros((BLK, LANES), F32) for _ in range(ATT_CHUNKS)]
        for w, acc in zip(ws, accs):
            coef = _split3_dot(jnp.where(lane < H, w / den, 0.0), e_ref[...])
            for c in range(ATT_CHUNKS):
                ao[c] = ao[c] + coef[:, c * LANES:(c + 1) * LANES] * acc[c]
        ao_ref[0, rows, :] = jnp.concatenate(ao, axis=-1).astype(ao_ref.dtype)
        return carry

    lax.fori_loop(0, nblk, dilated, 0)
    lax.fori_loop(0, nblk, dense_and_merge, 0)


def _prompt_attention(aq, ak, av, rel_bias):
    B, S, A = aq.shape
    CH = ATT_CHUNKS
    assert [d for _, d in DIL_PATTERNS] == [1, 4, 16] and S // ATT_BLOCK == 16
    b1, b4, b16 = _prompt_bias_tables(rel_bias)
    head_of_lane = jnp.arange(A)[None, :] // HEAD_DIM
    expand = (jnp.arange(LANES)[:, None] == head_of_lane).astype(BF16)
    seq = pl.BlockSpec((1, S, A), lambda b: (b, 0, 0))
    const = lambda shape: pl.BlockSpec(shape, lambda b: (0,) * len(shape), pipeline_mode=pl.Buffered(1))
    chunked = pltpu.VMEM((CH, S, LANES), F32)
    stats = pltpu.VMEM((S, LANES), F32)
    return pl.pallas_call(
        _att_kernel,
        grid=(B,),
        in_specs=[seq, seq, seq, const(b1.shape), const(b4.shape), const(b16.shape), const(expand.shape)],
        out_specs=seq,
        out_shape=jax.ShapeDtypeStruct((B, S, A), BF16),
        scratch_shapes=[chunked, chunked, chunked, stats, chunked, stats, chunked, stats],
        compiler_params=_cparams(("arbitrary",), 58),
        name="prompt_att",
    )(aq, ak, av, b1, b4, b16, expand)


def _window_pieces(lows, T, kT_ref, vT_ref, qn_ref, kn_ref, vn_ref, c1_ref, c4_ref, c16_ref,
                   n1_ref, n4_ref, n16_ref, koT_ref, voT_ref, ao_ref):
    H, Dh, L = kT_ref.shape[1:]
    A = H * Dh
    zpad = jnp.zeros((LANES - T, A), F32)
    tail_lane = lax.broadcasted_iota(jnp.int32, (Dh, LANES), 1) >= LANES - T
    shifted = {}

    def shift_in(name, xT_ref, new_ref, out_ref):
        new_p = jnp.concatenate([zpad, new_ref[0]], axis=0)
        new_t = new_p.T
        heads = []
        for h in range(H):
            x = xT_ref[0, h]
            rolled = pltpu.roll(x, L - T, 1)
            tail = jnp.where(tail_lane, new_t[h * Dh:(h + 1) * Dh], rolled[:, L - LANES:])
            out_ref[0, h] = jnp.concatenate([rolled[:, :L - LANES], tail], axis=1)
            heads.append(x.astype(BF16))
            yield
        shifted[name] = (jnp.concatenate(heads, axis=0), new_p.astype(BF16))

    yield from shift_in("k", kT_ref, kn_ref, koT_ref)
    yield from shift_in("v", vT_ref, vn_ref, voT_ref)
    (kT, kn), (vT, vn) = shifted["k"], shifted["v"]
    HT = H * T
    row_head = lax.broadcasted_iota(jnp.int32, (HT, A), 0) // T
    lane_head = lax.broadcasted_iota(jnp.int32, (HT, A), 1) // HEAD_DIM
    diag = row_head == lane_head
    q_rows = jnp.where(diag, jnp.concatenate([qn_ref[0]] * H, axis=0), 0.0).astype(BF16)
    s_cache = _dot(q_rows, kT)
    s_new = _dot_nt(q_rows, kn)
    yield
    soft = []
    for lo, c_ref, n_ref in zip(lows, (c1_ref, c4_ref, c16_ref), (n1_ref, n4_ref, n16_ref)):
        sc = s_cache[:, lo:] + c_ref[...]
        sn = s_new + n_ref[...]
        m = jnp.maximum(jnp.max(sc, axis=-1, keepdims=True), jnp.max(sn, axis=-1, keepdims=True))
        pc = jnp.exp(sc - m)
        pn = jnp.exp(sn - m)
        l = jnp.sum(pc, axis=-1, keepdims=True) + jnp.sum(pn, axis=-1, keepdims=True)
        soft.append((m, l, pc.astype(BF16), pn.astype(BF16)))
        yield
    stats = []
    for lo, (m, l, pc, pn) in zip(lows, soft):
        stats.append((m, l, _dot_nt(pc, vT[:, lo:]) + _dot(pn, vn)))
        yield
    m_all = jnp.maximum(jnp.maximum(stats[0][0], stats[1][0]), stats[2][0])
    num = jnp.zeros((HT, A), F32)
    den = jnp.zeros((HT, 1), F32)
    for m, l, acc in stats:
        w = jnp.exp(m - m_all)
        num = num + w * acc
        den = den + w * l
    comb = jnp.where(diag, num / den, 0.0)
    out = comb[0:T]
    for h in range(1, H):
        out = out + comb[h * T:(h + 1) * T]
    ao_ref[0] = out
    yield


def _sample_window_operands(aq, ak, av, cache_kT, cache_vT, rel_bias, groups):
    B, T, A = aq.shape
    _, H, Dh, L = cache_kT.shape
    Hg, Ag = H // groups, A // groups
    lows = tuple(L - win for win, _ in DIL_PATTERNS)
    assert all(lo >= 0 and lo % LANES == 0 for lo in lows) and T <= LANES and Ag % LANES == 0
    tabs = _sample_bias_tables(rel_bias, L, T)
    big = pl.BlockSpec((1, Hg, Dh, L), lambda i: (i // groups, i % groups, 0, 0))
    small = pl.BlockSpec((1, T, Ag), lambda i: (i // groups, 0, i % groups))
    tab = lambda t: pl.BlockSpec((Hg * T, t.shape[1]), lambda i: (i % groups, 0))
    win_sds = jax.ShapeDtypeStruct((B, H, Dh, L), F32)
    args = [cache_kT, cache_vT, aq, ak, av, *tabs]
    in_specs = [big, big, small, small, small] + [tab(t) for t in tabs]
    out_specs = [big, big, small]
    out_shape = [win_sds, win_sds, jax.ShapeDtypeStruct((B, T, A), F32)]
    return functools.partial(_window_pieces, lows, T), args, in_specs, out_specs, out_shape


def _ffn_pieces(d_ff, ff_chunk, x_ref, ro_ref, ao_ref, wo_ref, wgu_ref, wd_ref, g_pm, g_pf, g_of, y_ref, act_ref):
    R = RET_WIDTH
    mix = _dot(ro_ref[...].astype(BF16), wo_ref[:R, :]) + _dot(ao_ref[...].astype(BF16), wo_ref[R:, :])
    x1 = x_ref[...] + _rms(mix, g_pm[...])
    h = _rms(x1, g_pf[...]).astype(BF16)
    yield
    for c in range(0, d_ff, ff_chunk):
        gate = _dot(h, wgu_ref[:, c:c + ff_chunk])
        up = _dot(h, wgu_ref[:, d_ff + c:d_ff + c + ff_chunk])
        act_ref[:, c:c + ff_chunk] = (gate * jax.nn.sigmoid(gate) * up).astype(BF16)
        yield
    f = _dot(act_ref[...], wd_ref[...])
    y_ref[...] = x1 + _rms(f, g_of[...])
    yield


def _out_kernel(d_ff, ff_chunk, *refs):
    for _ in _ffn_pieces(d_ff, ff_chunk, *refs):
        pass


def _out_window_kernel(d_ff, ff_chunk, window, n_ffn_in, n_win_in, *refs):
    ffn_in = refs[:n_ffn_in]
    win_in = refs[n_ffn_in:n_ffn_in + n_win_in]
    y_ref, koT_ref, voT_ref, sao_ref, act_ref = refs[n_ffn_in + n_win_in:]
    win = window(*win_in, koT_ref, voT_ref, sao_ref)
    for _ in _ffn_pieces(d_ff, ff_chunk, *ffn_in, y_ref, act_ref):
        next(win, None)
    for _ in win:
        pass


def _out_block(x, ro, ao, wo_bf, wgu_bf, wd_bf, g_post_mix, g_pre_ffn, g_post_ffn, tm, window=None):
    N, D = x.shape
    d_ff = wd_bf.shape[0]
    row = lambda width: pl.BlockSpec((tm, width), lambda i: (i, 0))
    const = lambda shape: pl.BlockSpec(shape, lambda i: (0,) * len(shape), pipeline_mode=pl.Buffered(1))
    args = [x, ro, ao, wo_bf, wgu_bf, wd_bf, g_post_mix.reshape(1, D), g_pre_ffn.reshape(1, D),
            g_post_ffn.reshape(1, D)]
    in_specs = [row(D), row(RET_WIDTH), row(ATT_WIDTH), const(wo_bf.shape), const(wgu_bf.shape),
                const(wd_bf.shape), const((1, D)), const((1, D)), const((1, D))]
    body = functools.partial(_out_kernel, d_ff, 256)
    out_specs, out_shape = [row(D)], [jax.ShapeDtypeStruct((N, D), F32)]
    if window is not None:
        win_body, win_args, win_in_specs, win_out_specs, win_out_shape = window
        assert N // tm == win_out_shape[0].shape[0] * (win_out_shape[0].shape[1] // win_out_specs[0].block_shape[1])
        body = functools.partial(_out_window_kernel, d_ff, 256, win_body, len(args), len(win_args))
        args, in_specs = args + win_args, in_specs + win_in_specs
        out_specs, out_shape = out_specs + win_out_specs, out_shape + win_out_shape
    outs = pl.pallas_call(
        body,
        grid=(N // tm,),
        in_specs=in_specs,
        out_specs=out_specs,
        out_shape=out_shape,
        scratch_shapes=[pltpu.VMEM((tm, d_ff), BF16)],
        compiler_params=_cparams(("arbitrary",), 56),
        name="out_ffn",
    )(*args)
    return outs[0] if window is None else outs


def kernel(x_prompt, x_sample, state_ret, cache_k_win, cache_v_win, rel_bias, w_in, g_ret, w_out,
           g_pre_mix, g_post_mix, g_pre_ffn, g_post_ffn, w_gu, w_down):
    depth = w_in.shape[0]
    assert depth == 1
    B, S, D = x_prompt.shape
    Bs, Ts, _ = x_sample.shape
    H, Dh = N_ATT_HEADS, HEAD_DIM
    l = 0
    w_in_bf = w_in[l].astype(BF16)

    pos_s = PAST_LEN + jnp.arange(Ts, dtype=jnp.int32)
    N_s = Bs * Ts
    outs = _proj(x_sample.reshape(N_s, D), g_pre_mix[l], w_in_bf, jnp.tile(pos_s, Bs))
    srq, srk, srv, srg, saq, sak, sav = [t.reshape(Bs, Ts, RET_WIDTH) for t in outs]
    sro, s_s = _retention_step(srq, srk, srv, srg, g_ret[l], state_ret[l])
    to_t = lambda t: jnp.transpose(t, (0, 2, 3, 1))
    from_t = lambda t: jnp.transpose(t, (0, 3, 1, 2))
    window = _sample_window_operands(saq, sak, sav, to_t(cache_k_win[l]), to_t(cache_v_win[l]), rel_bias, 2)

    ro, s_p, aq, ak, av, wo_bf, wgu_bf, wd_bf = _proj_retention(
        x_prompt, g_pre_mix[l], w_in_bf, g_ret[l], 512, (w_out[l], w_gu[l], w_down[l]))
    ao = _prompt_attention(aq, ak, av, rel_bias)
    y_p, k_sT, v_sT, sao = _out_block(x_prompt.reshape(B * S, D), ro.reshape(B * S, RET_WIDTH),
                                      ao.reshape(B * S, ATT_WIDTH), wo_bf, wgu_bf, wd_bf, g_post_mix[l],
                                      g_pre_ffn[l], g_post_ffn[l], 256, window)
    k_s, v_s = from_t(k_sT), from_t(v_sT)

    y_s = _out_block(x_sample.reshape(N_s, D), sro.reshape(N_s, RET_WIDTH), sao.reshape(N_s, ATT_WIDTH),
                     wo_bf, wgu_bf, wd_bf, g_post_mix[l], g_pre_ffn[l], g_post_ffn[l], N_s)

    return (y_p.reshape(B, S, D), y_s.reshape(Bs, Ts, D),
            s_p[None], ak.reshape(1, B, S, H, Dh), av.reshape(1, B, S, H, Dh),
            s_s[None], k_s[None], v_s[None])
```

```python
import functools
import math

import jax
import jax.numpy as jnp
from jax import lax
from jax.experimental import pallas as pl
from jax.experimental.pallas import tpu as pltpu

F32 = jnp.float32
BF16 = jnp.bfloat16

HEAD_DIM = 64
N_RET_HEADS = 8
N_ATT_HEADS = 8
RET_WIDTH = N_RET_HEADS * HEAD_DIM
ATT_WIDTH = N_ATT_HEADS * HEAD_DIM
RET_CHUNK = 128
DIL_PATTERNS = ((128, 1), (512, 4), (2048, 16))
ATT_BLOCK = 128
N_BUCKETS = 32
MAX_DISTANCE = 2048
ROPE_BASE = 10000.0
NORM_EPS = 1e-6
GN_EPS = 1e-5
PAST_LEN = 16384
LANES = 128
ATT_CHUNKS = ATT_WIDTH // LANES
NEG_INF = float("-inf")


def _cparams(sem, vmem_mb):
    return pltpu.CompilerParams(dimension_semantics=sem, vmem_limit_bytes=vmem_mb * 1024 * 1024)


def _rms(x, g):
    return x * lax.rsqrt(jnp.mean(x * x, axis=-1, keepdims=True) + NORM_EPS) * g


def _dot(a, b):
    return jnp.dot(a, b, preferred_element_type=F32)


def _dot_nt(a, b):
    return lax.dot_general(a, b, (((1,), (1,)), ((), ())), preferred_element_type=F32)


def _dot_tn(a, b):
    return lax.dot_general(a, b, (((0,), (0,)), ((), ())), preferred_element_type=F32)


def _rope_tables(pos):
    inv = ROPE_BASE ** (-jnp.arange(0, HEAD_DIM, 2, dtype=F32) / HEAD_DIM)
    ang = pos.astype(F32)[:, None] * inv[None, :]
    cos, sin = jnp.cos(ang), jnp.sin(ang)
    zero = jnp.zeros_like(sin)
    rep = LANES // HEAD_DIM
    cosf = jnp.tile(jnp.concatenate([cos, cos], axis=-1), (1, rep))
    sa = jnp.tile(jnp.concatenate([-sin, zero], axis=-1), (1, rep))
    sb = jnp.tile(jnp.concatenate([zero, sin], axis=-1), (1, rep))
    return cosf, sa, sb


def _retention_consts(C):
    log_g = jnp.log1p(-jnp.exp2(-5.0 - jnp.arange(N_RET_HEADS, dtype=F32)))
    i = jnp.arange(C, dtype=F32)
    rel = i[:, None] - i[None, :]
    decay = jnp.where(rel[None] >= 0, jnp.exp(jnp.maximum(rel, 0.0)[None] * log_g[:, None, None]), 0.0)
    cross = jnp.exp((i + 1.0)[:, None] * log_g[None, :])
    kdec = jnp.exp((C - 1.0 - i)[:, None] * log_g[None, :])
    sdec = jnp.exp(C * log_g)[None, :]
    expand = lambda t: jnp.repeat(t, HEAD_DIM, axis=-1)
    return decay, expand(cross), expand(kdec), expand(sdec)


def _t5_bucket(dist):
    max_exact = N_BUCKETS // 2
    d_f = jnp.maximum(dist, 1).astype(F32)
    large = max_exact + (jnp.log(d_f / max_exact) / math.log(MAX_DISTANCE / max_exact)
                         * (N_BUCKETS - max_exact)).astype(jnp.int32)
    large = jnp.minimum(large, N_BUCKETS - 1)
    return jnp.where(dist < max_exact, dist, large)


def _pattern_bias_rev(rel_bias, dil, nk):
    dist = jnp.arange(nk, -1, -1, dtype=jnp.int32) * dil
    return rel_bias[_t5_bucket(dist)].astype(F32).T


def _prompt_bias_vectors(rel_bias):
    BLK, H = ATT_BLOCK, N_ATT_HEADS
    out = []
    for win, dil in DIL_PATTERNS:
        nk = win // dil
        assert nk == BLK
        out.append(jnp.concatenate([_pattern_bias_rev(rel_bias, dil, nk),
                                    jnp.full((H, 3 * BLK - nk - 1), NEG_INF, F32)], axis=1))
    return jnp.stack(out)


def _sample_bias_tables(rel_bias, L, T):
    H = N_ATT_HEADS
    cache_part, new_part = [], []
    for win, dil in DIL_PATTERNS:
        nk = win // dil
        rev = _pattern_bias_rev(rel_bias, dil, nk)
        if dil > 1:
            gaps = jnp.full((H, nk + 1, dil - 1), NEG_INF, F32)
            rev = jnp.concatenate([rev[:, :, None], gaps], axis=2).reshape(H, (nk + 1) * dil)[:, :nk * dil + 1]
        pv = jnp.concatenate([jnp.full((H, T), NEG_INF, F32), rev, jnp.full((H, 2 * T), NEG_INF, F32)], axis=1)
        width = win + T
        rows = jnp.stack([pv[:, T - t:T - t + width] for t in range(T)], axis=1)
        cache_part.append(rows[:, :, :win].reshape(H * T, win))
        new = jnp.concatenate([jnp.full((H, T, LANES - T), NEG_INF, F32), rows[:, :, win:]], axis=2)
        new_part.append(new.reshape(H * T, LANES))
    return cache_part + new_part


def _proj_steps(x, g_ref, w_ref, cos, sa, sb):
    h = _rms(x, g_ref[...]).astype(BF16)
    W = 2 * LANES
    rep = W // LANES
    cosf = jnp.concatenate([cos] * rep, axis=-1)
    saf = jnp.concatenate([sa] * rep, axis=-1)
    sbf = jnp.concatenate([sb] * rep, axis=-1)
    half = HEAD_DIM // 2

    def col(c):
        return _dot(h, w_ref[:, c:c + W])

    def rope(z):
        return z * cosf + pltpu.roll(z, W - half, 1) * saf + pltpu.roll(z, half, 1) * sbf

    R, A = RET_WIDTH, ATT_WIDTH
    scale = HEAD_DIM ** -0.5
    steps = []
    for c in range(0, R, W):
        steps.append(lambda c=c: rope(col(c)))
    for c in range(R, 2 * R, W):
        steps.append(lambda c=c: rope(col(c)) * scale)
    for c in range(2 * R, 4 * R, W):
        steps.append(lambda c=c: col(c))
    for c in range(4 * R, 4 * R + A, W):
        steps.append(lambda c=c: col(c) * scale)
    for c in range(4 * R + A, 4 * R + 3 * A, W):
        steps.append(lambda c=c: col(c))
    return steps


def _join_groups(pieces):
    return [jnp.concatenate(pieces[i:i + 2], axis=-1) for i in range(0, len(pieces), 2)]


def _proj_kernel(x_ref, g_ref, w_ref, cos_ref, sa_ref, sb_ref, *outs):
    steps = _proj_steps(x_ref[...], g_ref, w_ref, cos_ref[...], sa_ref[...], sb_ref[...])
    for ref, z in zip(outs, _join_groups([step() for step in steps])):
        ref[...] = z


def _proj(x, g, w_bf, pos):
    N, D = x.shape
    tabs = _rope_tables(pos)
    full = lambda shape: pl.BlockSpec(shape, lambda i: (0,) * len(shape))
    return pl.pallas_call(
        _proj_kernel,
        grid=(1,),
        in_specs=[full((N, D)), full((1, D)), full(w_bf.shape)] + [full(t.shape) for t in tabs],
        out_specs=[full((N, RET_WIDTH))] * 7,
        out_shape=[jax.ShapeDtypeStruct((N, RET_WIDTH), F32)] * 7,
        compiler_params=_cparams(("arbitrary",), 40),
        name="proj",
    )(x, g.reshape(1, D), w_bf, *tabs)


def _proj_ret_kernel(tiles_per_seq, x_ref, g_ref, w_ref, cos_ref, sa_ref, sb_ref, dec_ref, cd_ref, kd_ref,
                     sd_ref, gr_ref, wa_ref, wb_ref, wc_ref, ro_ref, s_out_ref, aq_ref, ak_ref, av_ref,
                     wa_bf_ref, wb_bf_ref, wc_bf_ref, sbd_ref, qkv_ref, gate_ref):
    s = pl.program_id(0)
    n_tiles = pl.num_programs(0) - 1
    tm = x_ref.shape[1]
    C = RET_CHUNK
    cur = s % 2
    prv = 1 - cur

    @pl.when(s == 0)
    def _():
        sbd_ref[...] = jnp.zeros_like(sbd_ref)
        qkv_ref[1] = jnp.zeros(qkv_ref.shape[1:], qkv_ref.dtype)
        gate_ref[1] = jnp.zeros(gate_ref.shape[1:], gate_ref.dtype)

    for w32, w16 in ((wa_ref, wa_bf_ref), (wb_ref, wb_bf_ref), (wc_ref, wc_bf_ref)):
        w16[...] = w32[...].astype(w16.dtype)

    lane = lax.broadcasted_iota(jnp.int32, (C, LANES), 1)
    half = lane < HEAD_DIM
    same_head = (lax.broadcasted_iota(jnp.int32, (LANES, LANES), 0) // HEAD_DIM
                 == lax.broadcasted_iota(jnp.int32, (LANES, LANES), 1) // HEAD_DIM)
    inv_d = 1.0 / HEAD_DIM
    starts_seq = (s - 1) % tiles_per_seq == 0

    def half_mean(t):
        lo = jnp.sum(jnp.where(half, t, 0.0), axis=-1, keepdims=True)
        hi = jnp.sum(jnp.where(half, 0.0, t), axis=-1, keepdims=True)
        return jnp.where(half, lo, hi) * inv_d

    n_pairs = RET_WIDTH // LANES
    n_chunks = tm // C
    state = [jnp.where(starts_seq, 0.0, sbd_ref[c]) for c in range(n_pairs)]

    def retention_unit(c, cc):
        ls = slice(c * LANES, (c + 1) * LANES)
        rs = slice(cc * C, (cc + 1) * C)
        S = state[c]
        qb = qkv_ref[prv, 0, rs, ls]
        kb = qkv_ref[prv, 1, rs, ls]
        vb = qkv_ref[prv, 2, rs, ls]
        q = qb.astype(F32)
        q_pair = jnp.concatenate([jnp.where(half, q, 0.0), jnp.where(half, 0.0, q)], axis=0).astype(BF16)
        inner = _dot_nt(q_pair, kb) * dec_ref[c]
        cross = _dot(qb, S.astype(BF16)) * cd_ref[:, ls]
        kd = (kb.astype(F32) * kd_ref[:, ls]).astype(BF16)
        state[c] = S * sd_ref[:, ls] + jnp.where(same_head, _dot_tn(kd, vb), 0.0)
        yield
        o2 = _dot(inner.astype(BF16), vb)
        o = jnp.where(half, o2[:C], o2[C:]) + cross
        yield
        mu = half_mean(o)
        d = o - mu
        var = half_mean(d * d)
        on = d * lax.rsqrt(var + GN_EPS) * gr_ref[:, ls]
        gate = gate_ref[prv, rs, ls]
        ro_ref[0, rs, ls] = (on * (gate * jax.nn.sigmoid(gate))).astype(ro_ref.dtype)
        yield

    def retention_pieces():
        for cc in range(n_chunks):
            units = [retention_unit(c, cc) for c in range(n_pairs)]
            for _ in range(3):
                for u in units:
                    next(u)
                    yield

    j = jnp.minimum(s, n_tiles - 1) % tiles_per_seq
    rows = pl.ds(pl.multiple_of(j * tm, tm), tm)
    steps = _proj_steps(x_ref[0], g_ref, w_ref, cos_ref[rows, :], sa_ref[rows, :], sb_ref[rows, :])
    pieces = retention_pieces()
    n_pieces = 3 * n_chunks * n_pairs
    cols = []
    for i, step in enumerate(steps):
        cols.append(step())
        for _ in range((i + 1) * n_pieces // len(steps) - i * n_pieces // len(steps)):
            next(pieces)
    for c in range(n_pairs):
        S = state[c]
        sbd_ref[c] = S
        s_out_ref[0, 2 * c] = S[:HEAD_DIM, :HEAD_DIM]
        s_out_ref[0, 2 * c + 1] = S[HEAD_DIM:, HEAD_DIM:]
    rq, rk, rv, rg, aq, ak, av = _join_groups(cols)
    aq_ref[0] = aq.astype(aq_ref.dtype)
    ak_ref[0] = ak
    av_ref[0] = av
    qkv_ref[cur, 0] = rq.astype(BF16)
    qkv_ref[cur, 1] = rk.astype(BF16)
    qkv_ref[cur, 2] = rv.astype(BF16)
    gate_ref[cur] = rg


def _proj_retention(x, g, w_bf, g_ret, tm, later_weights):
    B, T, D = x.shape
    R, A, H, Dh = RET_WIDTH, ATT_WIDTH, N_RET_HEADS, HEAD_DIM
    tps = T // tm
    n_tiles = B * tps
    tabs = _rope_tables(jnp.arange(T, dtype=jnp.int32))
    dec, cd, kd, sd = _retention_consts(RET_CHUNK)
    dec_pair = dec.reshape(H // 2, 2 * RET_CHUNK, RET_CHUNK)

    def this_tile(width):
        def index(s):
            t = jnp.minimum(s, n_tiles - 1)
            return (t // tps, t % tps, 0)
        return pl.BlockSpec((1, tm, width), index)

    def prev_tile(s):
        t = jnp.maximum(s - 1, 0)
        return (t // tps, t % tps, 0)

    def slab(w):
        rows = max(16, -(-w.shape[0] // n_tiles) // 16 * 16)
        while w.shape[0] % rows:
            rows += 16
        last = w.shape[0] // rows - 1
        return pl.BlockSpec((rows, w.shape[1]), lambda s: (jnp.minimum(s, last), 0))

    full = lambda shape: pl.BlockSpec(shape, lambda s: (0,) * len(shape))
    consts = [g.reshape(1, D), w_bf, *tabs, dec_pair, cd, kd, sd, g_ret.reshape(1, R)]
    slabs = [slab(w) for w in later_weights]
    return pl.pallas_call(
        functools.partial(_proj_ret_kernel, tps),
        grid=(n_tiles + 1,),
        in_specs=[this_tile(D)] + [full(t.shape) for t in consts] + slabs,
        out_specs=[pl.BlockSpec((1, tm, R), prev_tile),
                   pl.BlockSpec((1, H, Dh, Dh), lambda s: (jnp.maximum(s - 1, 0) // tps, 0, 0, 0)),
                   this_tile(A), this_tile(A), this_tile(A)] + slabs,
        out_shape=[jax.ShapeDtypeStruct((B, T, R), BF16), jax.ShapeDtypeStruct((B, H, Dh, Dh), F32),
                   jax.ShapeDtypeStruct((B, T, A), BF16), jax.ShapeDtypeStruct((B, T, A), F32),
                   jax.ShapeDtypeStruct((B, T, A), F32)]
                  + [jax.ShapeDtypeStruct(w.shape, BF16) for w in later_weights],
        scratch_shapes=[pltpu.VMEM((R // LANES, LANES, LANES), F32), pltpu.VMEM((2, 3, tm, R), BF16),
                        pltpu.VMEM((2, tm, R), F32)],
        compiler_params=_cparams(("arbitrary",), 56),
        name="proj_retention",
    )(x, *consts, *later_weights)


def _ret_kernel(q_ref, k_ref, v_ref, gate_ref, dec_ref, cd_ref, kd_ref, sd_ref, gr_ref, s0_ref, o_ref, s_ref):
    def unit(b, h):
        sl = slice(h * HEAD_DIM, (h + 1) * HEAD_DIM)
        qh = q_ref[b, :, sl].astype(BF16)
        kh = k_ref[b, :, sl].astype(BF16)
        vh = v_ref[b, :, sl].astype(BF16)
        S = s0_ref[b, h]
        inner = _dot_nt(qh, kh) * dec_ref[h]
        cross = _dot(qh, S.astype(BF16)) * cd_ref[:, sl]
        kd = (kh.astype(F32) * kd_ref[:, sl]).astype(BF16)
        s_ref[b, h] = S * sd_ref[:, sl] + _dot_tn(kd, vh)
        yield
        o = _dot(inner.astype(BF16), vh) + cross
        yield
        mu = jnp.mean(o, axis=-1, keepdims=True)
        var = jnp.mean(jnp.square(o - mu), axis=-1, keepdims=True)
        on = (o - mu) * lax.rsqrt(var + GN_EPS) * gr_ref[:, sl]
        gate = gate_ref[b, :, sl]
        o_ref[b, :, sl] = on * (gate * jax.nn.sigmoid(gate))
        yield

    for b in range(q_ref.shape[0]):
        units = [unit(b, h) for h in range(N_RET_HEADS)]
        for _ in range(3):
            for u in units:
                next(u)


def _retention_step(rq, rk, rv, rg, g_ret, state):
    B, T, R = rq.shape
    H, Dh = N_RET_HEADS, HEAD_DIM
    consts = [*_retention_consts(T), g_ret.reshape(1, R)]
    bb = math.gcd(B, 8)
    row = pl.BlockSpec((bb, T, R), lambda b: (b, 0, 0))
    full = lambda shape: pl.BlockSpec(shape, lambda b: (0,) * len(shape))
    st = pl.BlockSpec((bb, H, Dh, Dh), lambda b: (b, 0, 0, 0))
    return pl.pallas_call(
        _ret_kernel,
        grid=(B // bb,),
        in_specs=[row, row, row, row] + [full(t.shape) for t in consts] + [st],
        out_specs=[row, st],
        out_shape=[jax.ShapeDtypeStruct((B, T, R), F32), jax.ShapeDtypeStruct((B, H, Dh, Dh), F32)],
        compiler_params=_cparams(("arbitrary",), 32),
        name="retention",
    )(rq, rk, rv, rg, *consts, state)


def _split3_dot(c, e):
    c1 = c.astype(BF16)
    r1 = c - c1.astype(F32)
    c2 = r1.astype(BF16)
    c3 = (r1 - c2.astype(F32)).astype(BF16)
    return _dot(c1, e) + _dot(c2, e) + _dot(c3, e)


def _chunk_rows(ref, stride):
    def load(start, c):
        idx = pl.ds(start, ATT_BLOCK) if stride == 1 else pl.ds(start, ATT_BLOCK, stride=stride)
        return ref[c, idx, :]
    return load


def _token_rows(ref):
    def load(start, c):
        return ref[0, pl.ds(start, ATT_BLOCK), c * LANES:(c + 1) * LANES].astype(F32)
    return load


def _att_block(load_q, load_k, load_v, bias_ref, cur, prev, first):
    BLK, H = ATT_BLOCK, N_ATT_HEADS
    half = lax.broadcasted_iota(jnp.int32, (BLK, LANES), 1) < HEAD_DIM
    s_parts = []
    for c in range(ATT_CHUNKS):
        q = load_q(cur, c)
        q_pair = jnp.concatenate([jnp.where(half, q, 0.0), jnp.where(half, 0.0, q)], axis=0).astype(BF16)
        k = load_k(cur, c)
        if prev is not None:
            k = jnp.concatenate([load_k(prev, c), k], axis=0)
        s_parts.append(_dot_nt(q_pair, k.astype(BF16)))
    s = jnp.concatenate(s_parts, axis=0)
    s = s + bias_ref[...]
    if prev is not None:
        col = lax.broadcasted_iota(jnp.int32, s.shape, 1)
        s = jnp.where(jnp.logical_and(first, col < BLK), NEG_INF, s)
    m = jnp.max(s, axis=-1, keepdims=True)
    p = jnp.exp(s - m)
    l = jnp.sum(p, axis=-1, keepdims=True)
    pb = p.astype(BF16)
    accs = []
    for c in range(ATT_CHUNKS):
        v = load_v(cur, c)
        if prev is not None:
            v = jnp.concatenate([load_v(prev, c), v], axis=0)
        o = _dot(pb[2 * c * BLK:(2 * c + 2) * BLK], v.astype(BF16))
        accs.append(jnp.where(half, o[:BLK], o[BLK:]))
    lane = lax.broadcasted_iota(jnp.int32, (BLK, LANES), 1)
    ml = jnp.zeros((BLK, LANES), F32)
    for h in range(H):
        ml = jnp.where(lane == h, m[h * BLK:(h + 1) * BLK], ml)
        ml = jnp.where(lane == H + h, l[h * BLK:(h + 1) * BLK], ml)
    return accs, ml


def _att_kernel(q_ref, k_ref, v_ref, bv_ref, e_ref, ao_ref, b1_ref, b4_ref, b16_ref, q4, k4, v4, stage,
                acc4, ml4, acc16, ml16):
    BLK, H = ATT_BLOCK, N_ATT_HEADS
    S = q_ref.shape[1]
    nblk = S // BLK
    d4, d16 = DIL_PATTERNS[1][1], DIL_PATTERNS[2][1]
    nb4 = nblk // d4
    per_res = S // d4

    @pl.when(pl.program_id(0) == 0)
    def _():
        for g, tab in enumerate((b1_ref, b4_ref, b16_ref)):
            for h in range(H):
                rows = jnp.broadcast_to(bv_ref[g, h:h + 1, :], (BLK, bv_ref.shape[2]))
                skew = pltpu.roll(rows, 0, 1, stride=1, stride_axis=0)
                tab[h * BLK:(h + 1) * BLK, :] = skew[:, 2 * BLK - tab.shape[1]:2 * BLK]

    for src, dst in ((q_ref, q4), (k_ref, k4), (v_ref, v4)):
        for c in range(ATT_CHUNKS):
            stage[...] = src[0, :, c * LANES:(c + 1) * LANES].astype(F32)
            for r in range(d4):
                dst[c, r * per_res:(r + 1) * per_res, :] = stage[pl.ds(r, per_res, stride=d4), :]
    contiguous = [_chunk_rows(t, 1) for t in (q4, k4, v4)]
    every_4th = [_chunk_rows(t, d16 // d4) for t in (q4, k4, v4)]
    tokens = [_token_rows(t) for t in (q_ref, k_ref, v_ref)]

    def dilated(n, carry):
        r, j = n // nb4, n % nb4
        cur = r * per_res + j * BLK
        prev = r * per_res + jnp.maximum(j - 1, 0) * BLK
        accs, ml = _att_block(*contiguous, b4_ref, cur, prev, j == 0)
        tok = r + j * (BLK * d4)
        for c in range(ATT_CHUNKS):
            acc4[c, pl.ds(tok, BLK, stride=d4), :] = accs[c]
        ml4[pl.ds(tok, BLK, stride=d4), :] = ml
        accs, ml = _att_block(*every_4th, b16_ref, (n % d4) * per_res + n // d4, None, None)
        for c in range(ATT_CHUNKS):
            acc16[c, pl.ds(n, BLK, stride=d16), :] = accs[c]
        ml16[pl.ds(n, BLK, stride=d16), :] = ml
        return carry

    def dense_and_merge(j, carry):
        cur = pl.multiple_of(j * BLK, BLK)
        prev = pl.multiple_of(jnp.maximum(j - 1, 0) * BLK, BLK)
        accs1, ml1 = _att_block(*tokens, b1_ref, cur, prev, j == 0)
        rows = pl.ds(cur, BLK)
        mls = [ml1, ml4[rows, :], ml16[rows, :]]
        accs = [accs1, [acc4[c, rows, :] for c in range(ATT_CHUNKS)], [acc16[c, rows, :] for c in range(ATT_CHUNKS)]]
        lane = lax.broadcasted_iota(jnp.int32, (BLK, LANES), 1)
        m_all = jnp.maximum(jnp.maximum(mls[0], mls[1]), mls[2])
        ws = [jnp.exp(ml - m_all) for ml in mls]
        den = sum(w * pltpu.roll(ml, LANES - H, 1) for w, ml in zip(ws, mls))
        ao = [jnp.zeros((BLK, LANES), F32) for _ in range(ATT_CHUNKS)]
        for w, acc in zip(ws, accs):
            coef = _split3_dot(jnp.where(lane < H, w / den, 0.0), e_ref[...])
            for c in range(ATT_CHUNKS):
                ao[c] = ao[c] + coef[:, c * LANES:(c + 1) * LANES] * acc[c]
        ao_ref[0, rows, :] = jnp.concatenate(ao, axis=-1).astype(ao_ref.dtype)
        return carry

    lax.fori_loop(0, nblk, dilated, 0)
    lax.fori_loop(0, nblk, dense_and_merge, 0)


def _prompt_attention(aq, ak, av, rel_bias):
    B, S, A = aq.shape
    CH = ATT_CHUNKS
    assert [d for _, d in DIL_PATTERNS] == [1, 4, 16] and S // ATT_BLOCK == 16
    bias_vectors = _prompt_bias_vectors(rel_bias)
    head_of_lane = jnp.arange(A)[None, :] // HEAD_DIM
    expand = (jnp.arange(LANES)[:, None] == head_of_lane).astype(BF16)
    seq = pl.BlockSpec((1, S, A), lambda b: (b, 0, 0))
    const = lambda shape: pl.BlockSpec(shape, lambda b: (0,) * len(shape), pipeline_mode=pl.Buffered(1))
    chunked = pltpu.VMEM((CH, S, LANES), F32)
    stats = pltpu.VMEM((S, LANES), F32)
    table = lambda width: pltpu.VMEM((N_ATT_HEADS * ATT_BLOCK, width), F32)
    return pl.pallas_call(
        _att_kernel,
        grid=(B,),
        in_specs=[seq, seq, seq, const(bias_vectors.shape), const(expand.shape)],
        out_specs=seq,
        out_shape=jax.ShapeDtypeStruct((B, S, A), BF16),
        scratch_shapes=[table(2 * ATT_BLOCK), table(2 * ATT_BLOCK), table(ATT_BLOCK),
                        chunked, chunked, chunked, stats, chunked, stats, chunked, stats],
        compiler_params=_cparams(("arbitrary",), 58),
        name="prompt_att",
    )(aq, ak, av, bias_vectors, expand)


def _window_pieces(lows, T, kT_ref, vT_ref, qn_ref, kn_ref, vn_ref, c1_ref, c4_ref, c16_ref,
                   n1_ref, n4_ref, n16_ref, koT_ref, voT_ref, ao_ref):
    H, Dh, L = kT_ref.shape[1:]
    A = H * Dh
    zpad = jnp.zeros((LANES - T, A), F32)
    tail_lane = lax.broadcasted_iota(jnp.int32, (Dh, LANES), 1) >= LANES - T
    shifted = {}

    def shift_in(name, xT_ref, new_ref, out_ref):
        new_p = jnp.concatenate([zpad, new_ref[0]], axis=0)
        new_t = new_p.T
        heads = []
        for h in range(H):
            x = xT_ref[0, h]
            rolled = pltpu.roll(x, L - T, 1)
            tail = jnp.where(tail_lane, new_t[h * Dh:(h + 1) * Dh], rolled[:, L - LANES:])
            out_ref[0, h] = jnp.concatenate([rolled[:, :L - LANES], tail], axis=1)
            heads.append(x.astype(BF16))
            yield
        shifted[name] = (jnp.concatenate(heads, axis=0), new_p.astype(BF16))

    yield from shift_in("k", kT_ref, kn_ref, koT_ref)
    yield from shift_in("v", vT_ref, vn_ref, voT_ref)
    (kT, kn), (vT, vn) = shifted["k"], shifted["v"]
    HT = H * T
    row_head = lax.broadcasted_iota(jnp.int32, (HT, A), 0) // T
    lane_head = lax.broadcasted_iota(jnp.int32, (HT, A), 1) // HEAD_DIM
    diag = row_head == lane_head
    q_rows = jnp.where(diag, jnp.concatenate([qn_ref[0]] * H, axis=0), 0.0).astype(BF16)
    s_cache = _dot(q_rows, kT)
    s_new = _dot_nt(q_rows, kn)
    yield
    soft = []
    for lo, c_ref, n_ref in zip(lows, (c1_ref, c4_ref, c16_ref), (n1_ref, n4_ref, n16_ref)):
        sc = s_cache[:, lo:] + c_ref[...]
        sn = s_new + n_ref[...]
        m = jnp.maximum(jnp.max(sc, axis=-1, keepdims=True), jnp.max(sn, axis=-1, keepdims=True))
        pc = jnp.exp(sc - m)
        pn = jnp.exp(sn - m)
        l = jnp.sum(pc, axis=-1, keepdims=True) + jnp.sum(pn, axis=-1, keepdims=True)
        soft.append((m, l, pc.astype(BF16), pn.astype(BF16)))
        yield
    stats = []
    for lo, (m, l, pc, pn) in zip(lows, soft):
        stats.append((m, l, _dot_nt(pc, vT[:, lo:]) + _dot(pn, vn)))
        yield
    m_all = jnp.maximum(jnp.maximum(stats[0][0], stats[1][0]), stats[2][0])
    num = jnp.zeros((HT, A), F32)
    den = jnp.zeros((HT, 1), F32)
    for m, l, acc in stats:
        w = jnp.exp(m - m_all)
        num = num + w * acc
        den = den + w * l
    comb = jnp.where(diag, num / den, 0.0)
    out = comb[0:T]
    for h in range(1, H):
        out = out + comb[h * T:(h + 1) * T]
    ao_ref[0] = out
    yield


def _sample_window_operands(aq, ak, av, cache_kT, cache_vT, rel_bias, groups):
    B, T, A = aq.shape
    _, H, Dh, L = cache_kT.shape
    Hg, Ag = H // groups, A // groups
    lows = tuple(L - win for win, _ in DIL_PATTERNS)
    assert all(lo >= 0 and lo % LANES == 0 for lo in lows) and T <= LANES and Ag % LANES == 0
    tabs = _sample_bias_tables(rel_bias, L, T)
    big = pl.BlockSpec((1, Hg, Dh, L), lambda i: (i // groups, i % groups, 0, 0))
    small = pl.BlockSpec((1, T, Ag), lambda i: (i // groups, 0, i % groups))
    tab = lambda t: pl.BlockSpec((Hg * T, t.shape[1]), lambda i: (i % groups, 0))
    win_sds = jax.ShapeDtypeStruct((B, H, Dh, L), F32)
    args = [cache_kT, cache_vT, aq, ak, av, *tabs]
    in_specs = [big, big, small, small, small] + [tab(t) for t in tabs]
    out_specs = [big, big, small]
    out_shape = [win_sds, win_sds, jax.ShapeDtypeStruct((B, T, A), F32)]
    return functools.partial(_window_pieces, lows, T), args, in_specs, out_specs, out_shape


def _ffn_pieces(d_ff, ff_chunk, x_ref, ro_ref, ao_ref, wo_ref, wgu_ref, wd_ref, g_pm, g_pf, g_of, y_ref, act_ref):
    R = RET_WIDTH
    mix = _dot(ro_ref[...].astype(BF16), wo_ref[:R, :]) + _dot(ao_ref[...].astype(BF16), wo_ref[R:, :])
    x1 = x_ref[...] + _rms(mix, g_pm[...])
    h = _rms(x1, g_pf[...]).astype(BF16)
    yield
    for c in range(0, d_ff, ff_chunk):
        gate = _dot(h, wgu_ref[:, c:c + ff_chunk])
        up = _dot(h, wgu_ref[:, d_ff + c:d_ff + c + ff_chunk])
        act_ref[:, c:c + ff_chunk] = (gate * jax.nn.sigmoid(gate) * up).astype(BF16)
        yield
    f = _dot(act_ref[...], wd_ref[...])
    y_ref[...] = x1 + _rms(f, g_of[...])
    yield


def _out_kernel(d_ff, ff_chunk, *refs):
    for _ in _ffn_pieces(d_ff, ff_chunk, *refs):
        pass


def _out_window_kernel(d_ff, ff_chunk, window, n_ffn_in, n_win_in, *refs):
    ffn_in = refs[:n_ffn_in]
    win_in = refs[n_ffn_in:n_ffn_in + n_win_in]
    y_ref, koT_ref, voT_ref, sao_ref, act_ref = refs[n_ffn_in + n_win_in:]
    win = window(*win_in, koT_ref, voT_ref, sao_ref)
    for _ in _ffn_pieces(d_ff, ff_chunk, *ffn_in, y_ref, act_ref):
        next(win, None)
        next(win, None)
    for _ in win:
        pass


def _out_block(x, ro, ao, wo_bf, wgu_bf, wd_bf, g_post_mix, g_pre_ffn, g_post_ffn, tm, window=None):
    N, D = x.shape
    d_ff = wd_bf.shape[0]
    row = lambda width: pl.BlockSpec((tm, width), lambda i: (i, 0))
    const = lambda shape: pl.BlockSpec(shape, lambda i: (0,) * len(shape), pipeline_mode=pl.Buffered(1))
    args = [x, ro, ao, wo_bf, wgu_bf, wd_bf, g_post_mix.reshape(1, D), g_pre_ffn.reshape(1, D),
            g_post_ffn.reshape(1, D)]
    in_specs = [row(D), row(RET_WIDTH), row(ATT_WIDTH), const(wo_bf.shape), const(wgu_bf.shape),
                const(wd_bf.shape), const((1, D)), const((1, D)), const((1, D))]
    body = functools.partial(_out_kernel, d_ff, 256)
    out_specs, out_shape = [row(D)], [jax.ShapeDtypeStruct((N, D), F32)]
    if window is not None:
        win_body, win_args, win_in_specs, win_out_specs, win_out_shape = window
        assert N // tm == win_out_shape[0].shape[0] * (win_out_shape[0].shape[1] // win_out_specs[0].block_shape[1])
        body = functools.partial(_out_window_kernel, d_ff, 256, win_body, len(args), len(win_args))
        args, in_specs = args + win_args, in_specs + win_in_specs
        out_specs, out_shape = out_specs + win_out_specs, out_shape + win_out_shape
    outs = pl.pallas_call(
        body,
        grid=(N // tm,),
        in_specs=in_specs,
        out_specs=out_specs,
        out_shape=out_shape,
        scratch_shapes=[pltpu.VMEM((tm, d_ff), BF16)],
        compiler_params=_cparams(("arbitrary",), 56),
        name="out_ffn",
    )(*args)
    return outs[0] if window is None else outs


def kernel(x_prompt, x_sample, state_ret, cache_k_win, cache_v_win, rel_bias, w_in, g_ret, w_out,
           g_pre_mix, g_post_mix, g_pre_ffn, g_post_ffn, w_gu, w_down):
    depth = w_in.shape[0]
    assert depth == 1
    B, S, D = x_prompt.shape
    Bs, Ts, _ = x_sample.shape
    H, Dh = N_ATT_HEADS, HEAD_DIM
    l = 0
    w_in_bf = w_in[l].astype(BF16)

    pos_s = PAST_LEN + jnp.arange(Ts, dtype=jnp.int32)
    N_s = Bs * Ts
    outs = _proj(x_sample.reshape(N_s, D), g_pre_mix[l], w_in_bf, jnp.tile(pos_s, Bs))
    srq, srk, srv, srg, saq, sak, sav = [t.reshape(Bs, Ts, RET_WIDTH) for t in outs]
    sro, s_s = _retention_step(srq, srk, srv, srg, g_ret[l], state_ret[l])
    to_t = lambda t: jnp.transpose(t, (0, 2, 3, 1))
    from_t = lambda t: jnp.transpose(t, (0, 3, 1, 2))
    window = _sample_window_operands(saq, sak, sav, to_t(cache_k_win[l]), to_t(cache_v_win[l]), rel_bias, 2)

    ro, s_p, aq, ak, av, wo_bf, wgu_bf, wd_bf = _proj_retention(
        x_prompt, g_pre_mix[l], w_in_bf, g_ret[l], 512, (w_out[l], w_gu[l], w_down[l]))
    ao = _prompt_attention(aq, ak, av, rel_bias)
    y_p, k_sT, v_sT, sao = _out_block(x_prompt.reshape(B * S, D), ro.reshape(B * S, RET_WIDTH),
                                      ao.reshape(B * S, ATT_WIDTH), wo_bf, wgu_bf, wd_bf, g_post_mix[l],
                                      g_pre_ffn[l], g_post_ffn[l], 256, window)
    k_s, v_s = from_t(k_sT), from_t(v_sT)

    y_s = _out_block(x_sample.reshape(N_s, D), sro.reshape(N_s, RET_WIDTH), sao.reshape(N_s, ATT_WIDTH),
                     wo_bf, wgu_bf, wd_bf, g_post_mix[l], g_pre_ffn[l], g_post_ffn[l], N_s)

    return (y_p.reshape(B, S, D), y_s.reshape(Bs, Ts, D),
            s_p[None], ak.reshape(1, B, S, H, Dh), av.reshape(1, B, S, H, Dh),
            s_s[None], k_s[None], v_s[None])
```

```python
import functools
import math

import jax
import jax.numpy as jnp
from jax import lax
from jax.experimental import pallas as pl
from jax.experimental.pallas import tpu as pltpu

F32 = jnp.float32
BF16 = jnp.bfloat16

HEAD_DIM = 64
N_RET_HEADS = 8
N_ATT_HEADS = 8
RET_WIDTH = N_RET_HEADS * HEAD_DIM
ATT_WIDTH = N_ATT_HEADS * HEAD_DIM
RET_CHUNK = 128
DIL_PATTERNS = ((128, 1), (512, 4), (2048, 16))
ATT_BLOCK = 128
N_BUCKETS = 32
MAX_DISTANCE = 2048
ROPE_BASE = 10000.0
NORM_EPS = 1e-6
GN_EPS = 1e-5
PAST_LEN = 16384
LANES = 128
ATT_CHUNKS = ATT_WIDTH // LANES
NEG_INF = float("-inf")

PROJ_TILE = 512
FFN_TILE = 256
FF_CHUNK = 256
WINDOW_HEAD_GROUPS = 2
VMEM_MB = dict(proj=40, retention=32, proj_retention=56, prompt_att=58, out_ffn=56)


def _cparams(name, *sem):
    return pltpu.CompilerParams(dimension_semantics=sem, vmem_limit_bytes=VMEM_MB[name] * 1024 * 1024)


def _rms(x, g):
    return x * lax.rsqrt(jnp.mean(x * x, axis=-1, keepdims=True) + NORM_EPS) * g


def _dot(a, b):
    return jnp.dot(a, b, preferred_element_type=F32)


def _dot_nt(a, b):
    return lax.dot_general(a, b, (((1,), (1,)), ((), ())), preferred_element_type=F32)


def _dot_tn(a, b):
    return lax.dot_general(a, b, (((0,), (0,)), ((), ())), preferred_element_type=F32)


def _rope_tables(pos):
    inv = ROPE_BASE ** (-jnp.arange(0, HEAD_DIM, 2, dtype=F32) / HEAD_DIM)
    ang = pos.astype(F32)[:, None] * inv[None, :]
    cos, sin = jnp.cos(ang), jnp.sin(ang)
    zero = jnp.zeros_like(sin)
    rep = LANES // HEAD_DIM
    cosf = jnp.tile(jnp.concatenate([cos, cos], axis=-1), (1, rep))
    sa = jnp.tile(jnp.concatenate([-sin, zero], axis=-1), (1, rep))
    sb = jnp.tile(jnp.concatenate([zero, sin], axis=-1), (1, rep))
    return cosf, sa, sb


def _retention_consts(C):
    log_g = jnp.log1p(-jnp.exp2(-5.0 - jnp.arange(N_RET_HEADS, dtype=F32)))
    i = jnp.arange(C, dtype=F32)
    rel = i[:, None] - i[None, :]
    decay = jnp.where(rel[None] >= 0, jnp.exp(jnp.maximum(rel, 0.0)[None] * log_g[:, None, None]), 0.0)
    cross = jnp.exp((i + 1.0)[:, None] * log_g[None, :])
    kdec = jnp.exp((C - 1.0 - i)[:, None] * log_g[None, :])
    sdec = jnp.exp(C * log_g)[None, :]
    expand = lambda t: jnp.repeat(t, HEAD_DIM, axis=-1)
    return decay, expand(cross), expand(kdec), expand(sdec)


def _t5_bucket(dist):
    max_exact = N_BUCKETS // 2
    d_f = jnp.maximum(dist, 1).astype(F32)
    large = max_exact + (jnp.log(d_f / max_exact) / math.log(MAX_DISTANCE / max_exact)
                         * (N_BUCKETS - max_exact)).astype(jnp.int32)
    large = jnp.minimum(large, N_BUCKETS - 1)
    return jnp.where(dist < max_exact, dist, large)


def _pattern_bias_rev(rel_bias, dil, nk):
    dist = jnp.arange(nk, -1, -1, dtype=jnp.int32) * dil
    return rel_bias[_t5_bucket(dist)].astype(F32).T


def _prompt_bias_vectors(rel_bias):
    BLK, H = ATT_BLOCK, N_ATT_HEADS
    out = []
    for win, dil in DIL_PATTERNS:
        nk = win // dil
        assert nk == BLK
        out.append(jnp.concatenate([_pattern_bias_rev(rel_bias, dil, nk),
                                    jnp.full((H, 3 * BLK - nk - 1), NEG_INF, F32)], axis=1))
    return jnp.stack(out)


def _sample_bias_tables(rel_bias, L, T):
    H = N_ATT_HEADS
    cache_part, new_part = [], []
    for win, dil in DIL_PATTERNS:
        nk = win // dil
        rev = _pattern_bias_rev(rel_bias, dil, nk)
        if dil > 1:
            gaps = jnp.full((H, nk + 1, dil - 1), NEG_INF, F32)
            rev = jnp.concatenate([rev[:, :, None], gaps], axis=2).reshape(H, (nk + 1) * dil)[:, :nk * dil + 1]
        pv = jnp.concatenate([jnp.full((H, T), NEG_INF, F32), rev, jnp.full((H, 2 * T), NEG_INF, F32)], axis=1)
        width = win + T
        rows = jnp.stack([pv[:, T - t:T - t + width] for t in range(T)], axis=1)
        cache_part.append(rows[:, :, :win].reshape(H * T, win))
        new = jnp.concatenate([jnp.full((H, T, LANES - T), NEG_INF, F32), rows[:, :, win:]], axis=2)
        new_part.append(new.reshape(H * T, LANES))
    return cache_part + new_part


def _proj_steps(x, g_ref, w_ref, cos, sa, sb):
    h = _rms(x, g_ref[...]).astype(BF16)
    W = 2 * LANES
    rep = W // LANES
    cosf = jnp.concatenate([cos] * rep, axis=-1)
    saf = jnp.concatenate([sa] * rep, axis=-1)
    sbf = jnp.concatenate([sb] * rep, axis=-1)
    half = HEAD_DIM // 2

    def col(c):
        return _dot(h, w_ref[:, c:c + W])

    def rope(z):
        return z * cosf + pltpu.roll(z, W - half, 1) * saf + pltpu.roll(z, half, 1) * sbf

    R, A = RET_WIDTH, ATT_WIDTH
    scale = HEAD_DIM ** -0.5
    steps = []
    for c in range(0, R, W):
        steps.append(lambda c=c: rope(col(c)))
    for c in range(R, 2 * R, W):
        steps.append(lambda c=c: rope(col(c)) * scale)
    for c in range(2 * R, 4 * R, W):
        steps.append(lambda c=c: col(c))
    for c in range(4 * R, 4 * R + A, W):
        steps.append(lambda c=c: col(c) * scale)
    for c in range(4 * R + A, 4 * R + 3 * A, W):
        steps.append(lambda c=c: col(c))
    return steps


def _join_groups(pieces):
    return [jnp.concatenate(pieces[i:i + 2], axis=-1) for i in range(0, len(pieces), 2)]


def _proj_kernel(x_ref, g_ref, w_ref, cos_ref, sa_ref, sb_ref, *outs):
    steps = _proj_steps(x_ref[...], g_ref, w_ref, cos_ref[...], sa_ref[...], sb_ref[...])
    for ref, z in zip(outs, _join_groups([step() for step in steps])):
        ref[...] = z


def _proj(x, g, w_bf, pos):
    N, D = x.shape
    tabs = _rope_tables(pos)
    full = lambda shape: pl.BlockSpec(shape, lambda i: (0,) * len(shape))
    return pl.pallas_call(
        _proj_kernel,
        grid=(1,),
        in_specs=[full((N, D)), full((1, D)), full(w_bf.shape)] + [full(t.shape) for t in tabs],
        out_specs=[full((N, RET_WIDTH))] * 7,
        out_shape=[jax.ShapeDtypeStruct((N, RET_WIDTH), F32)] * 7,
        compiler_params=_cparams("proj", "arbitrary"),
        name="proj",
    )(x, g.reshape(1, D), w_bf, *tabs)


def _proj_ret_kernel(tiles_per_seq, x_ref, g_ref, w_ref, cos_ref, sa_ref, sb_ref, dec_ref, cd_ref, kd_ref,
                     sd_ref, gr_ref, wa_ref, wb_ref, wc_ref, ro_ref, s_out_ref, aq_ref, ak_ref, av_ref,
                     wa_bf_ref, wb_bf_ref, wc_bf_ref, sbd_ref, qkv_ref, gate_ref):
    s = pl.program_id(0)
    n_tiles = pl.num_programs(0) - 1
    tm = x_ref.shape[1]
    C = RET_CHUNK
    cur = s % 2
    prv = 1 - cur

    @pl.when(s == 0)
    def _():
        sbd_ref[...] = jnp.zeros_like(sbd_ref)
        qkv_ref[1] = jnp.zeros(qkv_ref.shape[1:], qkv_ref.dtype)
        gate_ref[1] = jnp.zeros(gate_ref.shape[1:], gate_ref.dtype)

    for w32, w16 in ((wa_ref, wa_bf_ref), (wb_ref, wb_bf_ref), (wc_ref, wc_bf_ref)):
        w16[...] = w32[...].astype(w16.dtype)

    lane = lax.broadcasted_iota(jnp.int32, (C, LANES), 1)
    half = lane < HEAD_DIM
    same_head = (lax.broadcasted_iota(jnp.int32, (LANES, LANES), 0) // HEAD_DIM
                 == lax.broadcasted_iota(jnp.int32, (LANES, LANES), 1) // HEAD_DIM)
    inv_d = 1.0 / HEAD_DIM
    starts_seq = (s - 1) % tiles_per_seq == 0

    def half_mean(t):
        lo = jnp.sum(jnp.where(half, t, 0.0), axis=-1, keepdims=True)
        hi = jnp.sum(jnp.where(half, 0.0, t), axis=-1, keepdims=True)
        return jnp.where(half, lo, hi) * inv_d

    n_pairs = RET_WIDTH // LANES
    n_chunks = tm // C
    state = [jnp.where(starts_seq, 0.0, sbd_ref[c]) for c in range(n_pairs)]

    def retention_unit(c, cc):
        ls = slice(c * LANES, (c + 1) * LANES)
        rs = slice(cc * C, (cc + 1) * C)
        S = state[c]
        qb = qkv_ref[prv, 0, rs, ls]
        kb = qkv_ref[prv, 1, rs, ls]
        vb = qkv_ref[prv, 2, rs, ls]
        q = qb.astype(F32)
        q_pair = jnp.concatenate([jnp.where(half, q, 0.0), jnp.where(half, 0.0, q)], axis=0).astype(BF16)
        inner = _dot_nt(q_pair, kb) * dec_ref[c]
        cross = _dot(qb, S.astype(BF16)) * cd_ref[:, ls]
        kd = (kb.astype(F32) * kd_ref[:, ls]).astype(BF16)
        state[c] = S * sd_ref[:, ls] + jnp.where(same_head, _dot_tn(kd, vb), 0.0)
        yield
        o2 = _dot(inner.astype(BF16), vb)
        o = jnp.where(half, o2[:C], o2[C:]) + cross
        yield
        mu = half_mean(o)
        d = o - mu
        var = half_mean(d * d)
        on = d * lax.rsqrt(var + GN_EPS) * gr_ref[:, ls]
        gate = gate_ref[prv, rs, ls]
        ro_ref[0, rs, ls] = (on * (gate * jax.nn.sigmoid(gate))).astype(ro_ref.dtype)
        yield

    def retention_pieces():
        for cc in range(n_chunks):
            units = [retention_unit(c, cc) for c in range(n_pairs)]
            for _ in range(3):
                for u in units:
                    next(u)
                    yield

    j = jnp.minimum(s, n_tiles - 1) % tiles_per_seq
    rows = pl.ds(pl.multiple_of(j * tm, tm), tm)
    steps = _proj_steps(x_ref[0], g_ref, w_ref, cos_ref[rows, :], sa_ref[rows, :], sb_ref[rows, :])
    pieces = retention_pieces()
    n_pieces = 3 * n_chunks * n_pairs
    cols = []
    for i, step in enumerate(steps):
        cols.append(step())
        for _ in range((i + 1) * n_pieces // len(steps) - i * n_pieces // len(steps)):
            next(pieces)
    for c in range(n_pairs):
        S = state[c]
        sbd_ref[c] = S
        s_out_ref[0, 2 * c] = S[:HEAD_DIM, :HEAD_DIM]
        s_out_ref[0, 2 * c + 1] = S[HEAD_DIM:, HEAD_DIM:]
    rq, rk, rv, rg, aq, ak, av = _join_groups(cols)
    aq_ref[0] = aq.astype(aq_ref.dtype)
    ak_ref[0] = ak
    av_ref[0] = av
    qkv_ref[cur, 0] = rq.astype(BF16)
    qkv_ref[cur, 1] = rk.astype(BF16)
    qkv_ref[cur, 2] = rv.astype(BF16)
    gate_ref[cur] = rg


def _proj_retention(x, g, w_bf, g_ret, tm, later_weights):
    B, T, D = x.shape
    R, A, H, Dh = RET_WIDTH, ATT_WIDTH, N_RET_HEADS, HEAD_DIM
    tps = T // tm
    n_tiles = B * tps
    tabs = _rope_tables(jnp.arange(T, dtype=jnp.int32))
    dec, cd, kd, sd = _retention_consts(RET_CHUNK)
    dec_pair = dec.reshape(H // 2, 2 * RET_CHUNK, RET_CHUNK)

    def this_tile(width):
        def index(s):
            t = jnp.minimum(s, n_tiles - 1)
            return (t // tps, t % tps, 0)
        return pl.BlockSpec((1, tm, width), index)

    def prev_tile(s):
        t = jnp.maximum(s - 1, 0)
        return (t // tps, t % tps, 0)

    def slab(w):
        rows = max(16, -(-w.shape[0] // n_tiles) // 16 * 16)
        while w.shape[0] % rows:
            rows += 16
        last = w.shape[0] // rows - 1
        return pl.BlockSpec((rows, w.shape[1]), lambda s: (jnp.minimum(s, last), 0))

    full = lambda shape: pl.BlockSpec(shape, lambda s: (0,) * len(shape))
    consts = [g.reshape(1, D), w_bf, *tabs, dec_pair, cd, kd, sd, g_ret.reshape(1, R)]
    slabs = [slab(w) for w in later_weights]
    return pl.pallas_call(
        functools.partial(_proj_ret_kernel, tps),
        grid=(n_tiles + 1,),
        in_specs=[this_tile(D)] + [full(t.shape) for t in consts] + slabs,
        out_specs=[pl.BlockSpec((1, tm, R), prev_tile),
                   pl.BlockSpec((1, H, Dh, Dh), lambda s: (jnp.maximum(s - 1, 0) // tps, 0, 0, 0)),
                   this_tile(A), this_tile(A), this_tile(A)] + slabs,
        out_shape=[jax.ShapeDtypeStruct((B, T, R), BF16), jax.ShapeDtypeStruct((B, H, Dh, Dh), F32),
                   jax.ShapeDtypeStruct((B, T, A), BF16), jax.ShapeDtypeStruct((B, T, A), F32),
                   jax.ShapeDtypeStruct((B, T, A), F32)]
                  + [jax.ShapeDtypeStruct(w.shape, BF16) for w in later_weights],
        scratch_shapes=[pltpu.VMEM((R // LANES, LANES, LANES), F32), pltpu.VMEM((2, 3, tm, R), BF16),
                        pltpu.VMEM((2, tm, R), F32)],
        compiler_params=_cparams("proj_retention", "arbitrary"),
        name="proj_retention",
    )(x, *consts, *later_weights)


def _ret_kernel(q_ref, k_ref, v_ref, gate_ref, dec_ref, cd_ref, kd_ref, sd_ref, gr_ref, s0_ref, o_ref, s_ref):
    def unit(b, h):
        sl = slice(h * HEAD_DIM, (h + 1) * HEAD_DIM)
        qh = q_ref[b, :, sl].astype(BF16)
        kh = k_ref[b, :, sl].astype(BF16)
        vh = v_ref[b, :, sl].astype(BF16)
        S = s0_ref[b, h]
        inner = _dot_nt(qh, kh) * dec_ref[h]
        cross = _dot(qh, S.astype(BF16)) * cd_ref[:, sl]
        kd = (kh.astype(F32) * kd_ref[:, sl]).astype(BF16)
        s_ref[b, h] = S * sd_ref[:, sl] + _dot_tn(kd, vh)
        yield
        o = _dot(inner.astype(BF16), vh) + cross
        yield
        mu = jnp.mean(o, axis=-1, keepdims=True)
        var = jnp.mean(jnp.square(o - mu), axis=-1, keepdims=True)
        on = (o - mu) * lax.rsqrt(var + GN_EPS) * gr_ref[:, sl]
        gate = gate_ref[b, :, sl]
        o_ref[b, :, sl] = on * (gate * jax.nn.sigmoid(gate))
        yield

    for b in range(q_ref.shape[0]):
        units = [unit(b, h) for h in range(N_RET_HEADS)]
        for _ in range(3):
            for u in units:
                next(u)


def _retention_step(rq, rk, rv, rg, g_ret, state):
    B, T, R = rq.shape
    H, Dh = N_RET_HEADS, HEAD_DIM
    consts = [*_retention_consts(T), g_ret.reshape(1, R)]
    bb = math.gcd(B, 8)
    row = pl.BlockSpec((bb, T, R), lambda b: (b, 0, 0))
    full = lambda shape: pl.BlockSpec(shape, lambda b: (0,) * len(shape))
    st = pl.BlockSpec((bb, H, Dh, Dh), lambda b: (b, 0, 0, 0))
    return pl.pallas_call(
        _ret_kernel,
        grid=(B // bb,),
        in_specs=[row, row, row, row] + [full(t.shape) for t in consts] + [st],
        out_specs=[row, st],
        out_shape=[jax.ShapeDtypeStruct((B, T, R), F32), jax.ShapeDtypeStruct((B, H, Dh, Dh), F32)],
        compiler_params=_cparams("retention", "arbitrary"),
        name="retention",
    )(rq, rk, rv, rg, *consts, state)


def _split3_dot(c, e):
    c1 = c.astype(BF16)
    r1 = c - c1.astype(F32)
    c2 = r1.astype(BF16)
    c3 = (r1 - c2.astype(F32)).astype(BF16)
    return _dot(c1, e) + _dot(c2, e) + _dot(c3, e)


def _chunk_rows(ref, stride):
    def load(start, c):
        idx = pl.ds(start, ATT_BLOCK) if stride == 1 else pl.ds(start, ATT_BLOCK, stride=stride)
        return ref[c, idx, :]
    return load


def _token_rows(ref):
    def load(start, c):
        return ref[0, pl.ds(start, ATT_BLOCK), c * LANES:(c + 1) * LANES].astype(F32)
    return load


def _att_block(load_q, load_k, load_v, bias_ref, cur, prev, first):
    BLK, H = ATT_BLOCK, N_ATT_HEADS
    half = lax.broadcasted_iota(jnp.int32, (BLK, LANES), 1) < HEAD_DIM
    s_parts = []
    for c in range(ATT_CHUNKS):
        q = load_q(cur, c)
        q_pair = jnp.concatenate([jnp.where(half, q, 0.0), jnp.where(half, 0.0, q)], axis=0).astype(BF16)
        k = load_k(cur, c)
        if prev is not None:
            k = jnp.concatenate([load_k(prev, c), k], axis=0)
        s_parts.append(_dot_nt(q_pair, k.astype(BF16)))
    s = jnp.concatenate(s_parts, axis=0)
    s = s + bias_ref[...]
    if prev is not None:
        col = lax.broadcasted_iota(jnp.int32, s.shape, 1)
        s = jnp.where(jnp.logical_and(first, col < BLK), NEG_INF, s)
    m = jnp.max(s, axis=-1, keepdims=True)
    p = jnp.exp(s - m)
    l = jnp.sum(p, axis=-1, keepdims=True)
    pb = p.astype(BF16)
    accs = []
    for c in range(ATT_CHUNKS):
        v = load_v(cur, c)
        if prev is not None:
            v = jnp.concatenate([load_v(prev, c), v], axis=0)
        o = _dot(pb[2 * c * BLK:(2 * c + 2) * BLK], v.astype(BF16))
        accs.append(jnp.where(half, o[:BLK], o[BLK:]))
    lane = lax.broadcasted_iota(jnp.int32, (BLK, LANES), 1)
    ml = jnp.zeros((BLK, LANES), F32)
    for h in range(H):
        ml = jnp.where(lane == h, m[h * BLK:(h + 1) * BLK], ml)
        ml = jnp.where(lane == H + h, l[h * BLK:(h + 1) * BLK], ml)
    return accs, ml


def _att_kernel(q_ref, k_ref, v_ref, bv_ref, e_ref, ao_ref, b1_ref, b4_ref, b16_ref, q4, k4, v4, stage,
                acc4, ml4, acc16, ml16):
    BLK, H = ATT_BLOCK, N_ATT_HEADS
    S = q_ref.shape[1]
    nblk = S // BLK
    d4, d16 = DIL_PATTERNS[1][1], DIL_PATTERNS[2][1]
    nb4 = nblk // d4
    per_res = S // d4

    @pl.when(pl.program_id(0) == 0)
    def _():
        for g, tab in enumerate((b1_ref, b4_ref, b16_ref)):
            for h in range(H):
                rows = jnp.broadcast_to(bv_ref[g, h:h + 1, :], (BLK, bv_ref.shape[2]))
                skew = pltpu.roll(rows, 0, 1, stride=1, stride_axis=0)
                tab[h * BLK:(h + 1) * BLK, :] = skew[:, 2 * BLK - tab.shape[1]:2 * BLK]

    for src, dst in ((q_ref, q4), (k_ref, k4), (v_ref, v4)):
        for c in range(ATT_CHUNKS):
            stage[...] = src[0, :, c * LANES:(c + 1) * LANES].astype(F32)
            for r in range(d4):
                dst[c, r * per_res:(r + 1) * per_res, :] = stage[pl.ds(r, per_res, stride=d4), :]
    contiguous = [_chunk_rows(t, 1) for t in (q4, k4, v4)]
    every_4th = [_chunk_rows(t, d16 // d4) for t in (q4, k4, v4)]
    tokens = [_token_rows(t) for t in (q_ref, k_ref, v_ref)]

    def dilated(n, carry):
        r, j = n // nb4, n % nb4
        cur = r * per_res + j * BLK
        prev = r * per_res + jnp.maximum(j - 1, 0) * BLK
        accs, ml = _att_block(*contiguous, b4_ref, cur, prev, j == 0)
        tok = r + j * (BLK * d4)
        for c in range(ATT_CHUNKS):
            acc4[c, pl.ds(tok, BLK, stride=d4), :] = accs[c]
        ml4[pl.ds(tok, BLK, stride=d4), :] = ml
        accs, ml = _att_block(*every_4th, b16_ref, (n % d4) * per_res + n // d4, None, None)
        for c in range(ATT_CHUNKS):
            acc16[c, pl.ds(n, BLK, stride=d16), :] = accs[c]
        ml16[pl.ds(n, BLK, stride=d16), :] = ml
        return carry

    def dense_and_merge(j, carry):
        cur = pl.multiple_of(j * BLK, BLK)
        prev = pl.multiple_of(jnp.maximum(j - 1, 0) * BLK, BLK)
        accs1, ml1 = _att_block(*tokens, b1_ref, cur, prev, j == 0)
        rows = pl.ds(cur, BLK)
        mls = [ml1, ml4[rows, :], ml16[rows, :]]
        accs = [accs1, [acc4[c, rows, :] for c in range(ATT_CHUNKS)], [acc16[c, rows, :] for c in range(ATT_CHUNKS)]]
        lane = lax.broadcasted_iota(jnp.int32, (BLK, LANES), 1)
        m_all = jnp.maximum(jnp.maximum(mls[0], mls[1]), mls[2])
        ws = [jnp.exp(ml - m_all) for ml in mls]
        den = sum(w * pltpu.roll(ml, LANES - H, 1) for w, ml in zip(ws, mls))
        ao = [jnp.zeros((BLK, LANES), F32) for _ in range(ATT_CHUNKS)]
        for w, acc in zip(ws, accs):
            coef = _split3_dot(jnp.where(lane < H, w / den, 0.0), e_ref[...])
            for c in range(ATT_CHUNKS):
                ao[c] = ao[c] + coef[:, c * LANES:(c + 1) * LANES] * acc[c]
        ao_ref[0, rows, :] = jnp.concatenate(ao, axis=-1).astype(ao_ref.dtype)
        return carry

    lax.fori_loop(0, nblk, dilated, 0)
    lax.fori_loop(0, nblk, dense_and_merge, 0)


def _prompt_attention(aq, ak, av, rel_bias):
    B, S, A = aq.shape
    CH = ATT_CHUNKS
    assert [d for _, d in DIL_PATTERNS] == [1, 4, 16] and S // ATT_BLOCK == 16
    bias_vectors = _prompt_bias_vectors(rel_bias)
    head_of_lane = jnp.arange(A)[None, :] // HEAD_DIM
    expand = (jnp.arange(LANES)[:, None] == head_of_lane).astype(BF16)
    seq = pl.BlockSpec((1, S, A), lambda b: (b, 0, 0))
    const = lambda shape: pl.BlockSpec(shape, lambda b: (0,) * len(shape), pipeline_mode=pl.Buffered(1))
    chunked = pltpu.VMEM((CH, S, LANES), F32)
    stats = pltpu.VMEM((S, LANES), F32)
    table = lambda width: pltpu.VMEM((N_ATT_HEADS * ATT_BLOCK, width), F32)
    return pl.pallas_call(
        _att_kernel,
        grid=(B,),
        in_specs=[seq, seq, seq, const(bias_vectors.shape), const(expand.shape)],
        out_specs=seq,
        out_shape=jax.ShapeDtypeStruct((B, S, A), BF16),
        scratch_shapes=[table(2 * ATT_BLOCK), table(2 * ATT_BLOCK), table(ATT_BLOCK),
                        chunked, chunked, chunked, stats, chunked, stats, chunked, stats],
        compiler_params=_cparams("prompt_att", "arbitrary"),
        name="prompt_att",
    )(aq, ak, av, bias_vectors, expand)


def _window_pieces(lows, T, kT_ref, vT_ref, qn_ref, kn_ref, vn_ref, c1_ref, c4_ref, c16_ref,
                   n1_ref, n4_ref, n16_ref, koT_ref, voT_ref, ao_ref):
    H, Dh, L = kT_ref.shape[1:]
    A = H * Dh
    zpad = jnp.zeros((LANES - T, A), F32)
    tail_lane = lax.broadcasted_iota(jnp.int32, (Dh, LANES), 1) >= LANES - T
    shifted = {}

    def shift_in(name, xT_ref, new_ref, out_ref):
        new_p = jnp.concatenate([zpad, new_ref[0]], axis=0)
        new_t = new_p.T
        heads = []
        for h in range(H):
            x = xT_ref[0, h]
            rolled = pltpu.roll(x, L - T, 1)
            tail = jnp.where(tail_lane, new_t[h * Dh:(h + 1) * Dh], rolled[:, L - LANES:])
            out_ref[0, h] = jnp.concatenate([rolled[:, :L - LANES], tail], axis=1)
            heads.append(x.astype(BF16))
            yield
        shifted[name] = (jnp.concatenate(heads, axis=0), new_p.astype(BF16))

    yield from shift_in("k", kT_ref, kn_ref, koT_ref)
    yield from shift_in("v", vT_ref, vn_ref, voT_ref)
    (kT, kn), (vT, vn) = shifted["k"], shifted["v"]
    HT = H * T
    row_head = lax.broadcasted_iota(jnp.int32, (HT, A), 0) // T
    lane_head = lax.broadcasted_iota(jnp.int32, (HT, A), 1) // HEAD_DIM
    diag = row_head == lane_head
    q_rows = jnp.where(diag, jnp.concatenate([qn_ref[0]] * H, axis=0), 0.0).astype(BF16)
    s_cache = _dot(q_rows, kT)
    s_new = _dot_nt(q_rows, kn)
    yield
    soft = []
    for lo, c_ref, n_ref in zip(lows, (c1_ref, c4_ref, c16_ref), (n1_ref, n4_ref, n16_ref)):
        sc = s_cache[:, lo:] + c_ref[...]
        sn = s_new + n_ref[...]
        m = jnp.maximum(jnp.max(sc, axis=-1, keepdims=True), jnp.max(sn, axis=-1, keepdims=True))
        pc = jnp.exp(sc - m)
        pn = jnp.exp(sn - m)
        l = jnp.sum(pc, axis=-1, keepdims=True) + jnp.sum(pn, axis=-1, keepdims=True)
        soft.append((m, l, pc.astype(BF16), pn.astype(BF16)))
        yield
    stats = []
    for lo, (m, l, pc, pn) in zip(lows, soft):
        stats.append((m, l, _dot_nt(pc, vT[:, lo:]) + _dot(pn, vn)))
        yield
    m_all = jnp.maximum(jnp.maximum(stats[0][0], stats[1][0]), stats[2][0])
    num = jnp.zeros((HT, A), F32)
    den = jnp.zeros((HT, 1), F32)
    for m, l, acc in stats:
        w = jnp.exp(m - m_all)
        num = num + w * acc
        den = den + w * l
    comb = jnp.where(diag, num / den, 0.0)
    out = comb[0:T]
    for h in range(1, H):
        out = out + comb[h * T:(h + 1) * T]
    ao_ref[0] = out
    yield


def _sample_window_operands(aq, ak, av, cache_kT, cache_vT, rel_bias, groups):
    B, T, A = aq.shape
    _, H, Dh, L = cache_kT.shape
    Hg, Ag = H // groups, A // groups
    lows = tuple(L - win for win, _ in DIL_PATTERNS)
    assert all(lo >= 0 and lo % LANES == 0 for lo in lows) and T <= LANES and Ag % LANES == 0
    tabs = _sample_bias_tables(rel_bias, L, T)
    big = pl.BlockSpec((1, Hg, Dh, L), lambda i: (i // groups, i % groups, 0, 0))
    small = pl.BlockSpec((1, T, Ag), lambda i: (i // groups, 0, i % groups))
    tab = lambda t: pl.BlockSpec((Hg * T, t.shape[1]), lambda i: (i % groups, 0))
    win_sds = jax.ShapeDtypeStruct((B, H, Dh, L), F32)
    args = [cache_kT, cache_vT, aq, ak, av, *tabs]
    in_specs = [big, big, small, small, small] + [tab(t) for t in tabs]
    out_specs = [big, big, small]
    out_shape = [win_sds, win_sds, jax.ShapeDtypeStruct((B, T, A), F32)]
    return functools.partial(_window_pieces, lows, T), args, in_specs, out_specs, out_shape


def _ffn_pieces(d_ff, ff_chunk, x_ref, ro_ref, ao_ref, wo_ref, wgu_ref, wd_ref, g_pm, g_pf, g_of, y_ref, act_ref):
    R = RET_WIDTH
    mix = _dot(ro_ref[...].astype(BF16), wo_ref[:R, :]) + _dot(ao_ref[...].astype(BF16), wo_ref[R:, :])
    x1 = x_ref[...] + _rms(mix, g_pm[...])
    h = _rms(x1, g_pf[...]).astype(BF16)
    yield
    for c in range(0, d_ff, ff_chunk):
        gate = _dot(h, wgu_ref[:, c:c + ff_chunk])
        up = _dot(h, wgu_ref[:, d_ff + c:d_ff + c + ff_chunk])
        act_ref[:, c:c + ff_chunk] = (gate * jax.nn.sigmoid(gate) * up).astype(BF16)
        yield
    f = _dot(act_ref[...], wd_ref[...])
    y_ref[...] = x1 + _rms(f, g_of[...])
    yield


def _out_kernel(d_ff, ff_chunk, *refs):
    for _ in _ffn_pieces(d_ff, ff_chunk, *refs):
        pass


def _out_window_kernel(d_ff, ff_chunk, window, n_ffn_in, n_win_in, *refs):
    ffn_in = refs[:n_ffn_in]
    win_in = refs[n_ffn_in:n_ffn_in + n_win_in]
    y_ref, koT_ref, voT_ref, sao_ref, act_ref = refs[n_ffn_in + n_win_in:]
    win = window(*win_in, koT_ref, voT_ref, sao_ref)
    for _ in _ffn_pieces(d_ff, ff_chunk, *ffn_in, y_ref, act_ref):
        next(win, None)
        next(win, None)
    for _ in win:
        pass


def _out_block(x, ro, ao, wo_bf, wgu_bf, wd_bf, g_post_mix, g_pre_ffn, g_post_ffn, tm, window=None):
    N, D = x.shape
    d_ff = wd_bf.shape[0]
    row = lambda width: pl.BlockSpec((tm, width), lambda i: (i, 0))
    const = lambda shape: pl.BlockSpec(shape, lambda i: (0,) * len(shape), pipeline_mode=pl.Buffered(1))
    args = [x, ro, ao, wo_bf, wgu_bf, wd_bf, g_post_mix.reshape(1, D), g_pre_ffn.reshape(1, D),
            g_post_ffn.reshape(1, D)]
    in_specs = [row(D), row(RET_WIDTH), row(ATT_WIDTH), const(wo_bf.shape), const(wgu_bf.shape),
                const(wd_bf.shape), const((1, D)), const((1, D)), const((1, D))]
    body = functools.partial(_out_kernel, d_ff, FF_CHUNK)
    out_specs, out_shape = [row(D)], [jax.ShapeDtypeStruct((N, D), F32)]
    if window is not None:
        win_body, win_args, win_in_specs, win_out_specs, win_out_shape = window
        assert N // tm == win_out_shape[0].shape[0] * (win_out_shape[0].shape[1] // win_out_specs[0].block_shape[1])
        body = functools.partial(_out_window_kernel, d_ff, FF_CHUNK, win_body, len(args), len(win_args))
        args, in_specs = args + win_args, in_specs + win_in_specs
        out_specs, out_shape = out_specs + win_out_specs, out_shape + win_out_shape
    outs = pl.pallas_call(
        body,
        grid=(N // tm,),
        in_specs=in_specs,
        out_specs=out_specs,
        out_shape=out_shape,
        scratch_shapes=[pltpu.VMEM((tm, d_ff), BF16)],
        compiler_params=_cparams("out_ffn", "arbitrary"),
        name="out_ffn",
    )(*args)
    return outs[0] if window is None else outs


def kernel(x_prompt, x_sample, state_ret, cache_k_win, cache_v_win, rel_bias, w_in, g_ret, w_out,
           g_pre_mix, g_post_mix, g_pre_ffn, g_post_ffn, w_gu, w_down):
    depth = w_in.shape[0]
    assert depth == 1
    B, S, D = x_prompt.shape
    Bs, Ts, _ = x_sample.shape
    H, Dh = N_ATT_HEADS, HEAD_DIM
    l = 0
    w_in_bf = w_in[l].astype(BF16)

    pos_s = PAST_LEN + jnp.arange(Ts, dtype=jnp.int32)
    N_s = Bs * Ts
    outs = _proj(x_sample.reshape(N_s, D), g_pre_mix[l], w_in_bf, jnp.tile(pos_s, Bs))
    srq, srk, srv, srg, saq, sak, sav = [t.reshape(Bs, Ts, RET_WIDTH) for t in outs]
    sro, s_s = _retention_step(srq, srk, srv, srg, g_ret[l], state_ret[l])
    to_t = lambda t: jnp.transpose(t, (0, 2, 3, 1))
    from_t = lambda t: jnp.transpose(t, (0, 3, 1, 2))
    window = _sample_window_operands(saq, sak, sav, to_t(cache_k_win[l]), to_t(cache_v_win[l]), rel_bias,
                                     WINDOW_HEAD_GROUPS)

    ro, s_p, aq, ak, av, wo_bf, wgu_bf, wd_bf = _proj_retention(
        x_prompt, g_pre_mix[l], w_in_bf, g_ret[l], PROJ_TILE, (w_out[l], w_gu[l], w_down[l]))
    ao = _prompt_attention(aq, ak, av, rel_bias)
    y_p, k_sT, v_sT, sao = _out_block(x_prompt.reshape(B * S, D), ro.reshape(B * S, RET_WIDTH),
                                      ao.reshape(B * S, ATT_WIDTH), wo_bf, wgu_bf, wd_bf, g_post_mix[l],
                                      g_pre_ffn[l], g_post_ffn[l], FFN_TILE, window)
    k_s, v_s = from_t(k_sT), from_t(v_sT)

    y_s = _out_block(x_sample.reshape(N_s, D), sro.reshape(N_s, RET_WIDTH), sao.reshape(N_s, ATT_WIDTH),
                     wo_bf, wgu_bf, wd_bf, g_post_mix[l], g_pre_ffn[l], g_post_ffn[l], N_s)

    return (y_p.reshape(B, S, D), y_s.reshape(Bs, Ts, D),
            s_p[None], ak.reshape(1, B, S, H, Dh), av.reshape(1, B, S, H, Dh),
            s_s[None], k_s[None], v_s[None])
```

```python
import functools
import math

import jax
import jax.numpy as jnp
from jax import lax
from jax.experimental import pallas as pl
from jax.experimental.pallas import tpu as pltpu

F32 = jnp.float32
BF16 = jnp.bfloat16

HEAD_DIM = 64
N_RET_HEADS = 8
N_ATT_HEADS = 8
RET_WIDTH = N_RET_HEADS * HEAD_DIM
ATT_WIDTH = N_ATT_HEADS * HEAD_DIM
RET_CHUNK = 128
DIL_PATTERNS = ((128, 1), (512, 4), (2048, 16))
ATT_BLOCK = 128
N_BUCKETS = 32
MAX_DISTANCE = 2048
ROPE_BASE = 10000.0
NORM_EPS = 1e-6
GN_EPS = 1e-5
PAST_LEN = 16384
LANES = 128
ATT_CHUNKS = ATT_WIDTH // LANES
NEG_INF = float("-inf")

PROJ_TILE = 512
FFN_TILE = 256
FF_CHUNK = 256
WINDOW_HEAD_GROUPS = 2
VMEM_MB = dict(proj=40, retention=32, proj_retention=56, prompt_att=58, out_ffn=56)


def _cparams(name, *sem):
    return pltpu.CompilerParams(dimension_semantics=sem, vmem_limit_bytes=VMEM_MB[name] * 1024 * 1024)


def _rms(x, g):
    return x * lax.rsqrt(jnp.mean(x * x, axis=-1, keepdims=True) + NORM_EPS) * g


def _dot(a, b):
    return jnp.dot(a, b, preferred_element_type=F32)


def _dot_nt(a, b):
    return lax.dot_general(a, b, (((1,), (1,)), ((), ())), preferred_element_type=F32)


def _dot_tn(a, b):
    return lax.dot_general(a, b, (((0,), (0,)), ((), ())), preferred_element_type=F32)


def _rope_tables(pos):
    inv = ROPE_BASE ** (-jnp.arange(0, HEAD_DIM, 2, dtype=F32) / HEAD_DIM)
    ang = pos.astype(F32)[:, None] * inv[None, :]
    cos, sin = jnp.cos(ang), jnp.sin(ang)
    zero = jnp.zeros_like(sin)
    rep = LANES // HEAD_DIM
    cosf = jnp.tile(jnp.concatenate([cos, cos], axis=-1), (1, rep))
    sa = jnp.tile(jnp.concatenate([-sin, zero], axis=-1), (1, rep))
    sb = jnp.tile(jnp.concatenate([zero, sin], axis=-1), (1, rep))
    return cosf, sa, sb


def _retention_consts(C):
    log_g = jnp.log1p(-jnp.exp2(-5.0 - jnp.arange(N_RET_HEADS, dtype=F32)))
    i = jnp.arange(C, dtype=F32)
    rel = i[:, None] - i[None, :]
    decay = jnp.where(rel[None] >= 0, jnp.exp(jnp.maximum(rel, 0.0)[None] * log_g[:, None, None]), 0.0)
    cross = jnp.exp((i + 1.0)[:, None] * log_g[None, :])
    kdec = jnp.exp((C - 1.0 - i)[:, None] * log_g[None, :])
    sdec = jnp.exp(C * log_g)[None, :]
    expand = lambda t: jnp.repeat(t, HEAD_DIM, axis=-1)
    return decay, expand(cross), expand(kdec), expand(sdec)


def _t5_bucket(dist):
    max_exact = N_BUCKETS // 2
    d_f = jnp.maximum(dist, 1).astype(F32)
    large = max_exact + (jnp.log(d_f / max_exact) / math.log(MAX_DISTANCE / max_exact)
                         * (N_BUCKETS - max_exact)).astype(jnp.int32)
    large = jnp.minimum(large, N_BUCKETS - 1)
    return jnp.where(dist < max_exact, dist, large)


def _pattern_bias_rev(rel_bias, dil, nk):
    dist = jnp.arange(nk, -1, -1, dtype=jnp.int32) * dil
    return rel_bias[_t5_bucket(dist)].astype(F32).T


def _prompt_bias_vectors(rel_bias):
    BLK, H = ATT_BLOCK, N_ATT_HEADS
    out = []
    for win, dil in DIL_PATTERNS:
        nk = win // dil
        assert nk == BLK
        out.append(jnp.concatenate([_pattern_bias_rev(rel_bias, dil, nk),
                                    jnp.full((H, 3 * BLK - nk - 1), NEG_INF, F32)], axis=1))
    return jnp.stack(out)


def _sample_bias_tables(rel_bias, L, T):
    H = N_ATT_HEADS
    cache_part, new_part = [], []
    for win, dil in DIL_PATTERNS:
        nk = win // dil
        rev = _pattern_bias_rev(rel_bias, dil, nk)
        if dil > 1:
            gaps = jnp.full((H, nk + 1, dil - 1), NEG_INF, F32)
            rev = jnp.concatenate([rev[:, :, None], gaps], axis=2).reshape(H, (nk + 1) * dil)[:, :nk * dil + 1]
        pv = jnp.concatenate([jnp.full((H, T), NEG_INF, F32), rev, jnp.full((H, 2 * T), NEG_INF, F32)], axis=1)
        width = win + T
        rows = jnp.stack([pv[:, T - t:T - t + width] for t in range(T)], axis=1)
        cache_part.append(rows[:, :, :win].reshape(H * T, win))
        new = jnp.concatenate([jnp.full((H, T, LANES - T), NEG_INF, F32), rows[:, :, win:]], axis=2)
        new_part.append(new.reshape(H * T, LANES))
    return cache_part + new_part


def _proj_steps(x, g_ref, w_ref, cos, sa, sb):
    h = _rms(x, g_ref[...]).astype(BF16)
    W = 2 * LANES
    rep = W // LANES
    cosf = jnp.concatenate([cos] * rep, axis=-1)
    saf = jnp.concatenate([sa] * rep, axis=-1)
    sbf = jnp.concatenate([sb] * rep, axis=-1)
    half = HEAD_DIM // 2

    def col(c):
        return _dot(h, w_ref[:, c:c + W])

    def rope(z):
        return z * cosf + pltpu.roll(z, W - half, 1) * saf + pltpu.roll(z, half, 1) * sbf

    R, A = RET_WIDTH, ATT_WIDTH
    scale = HEAD_DIM ** -0.5
    steps = []
    for c in range(0, R, W):
        steps.append(lambda c=c: rope(col(c)))
    for c in range(R, 2 * R, W):
        steps.append(lambda c=c: rope(col(c)) * scale)
    for c in range(2 * R, 4 * R, W):
        steps.append(lambda c=c: col(c))
    for c in range(4 * R, 4 * R + A, W):
        steps.append(lambda c=c: col(c) * scale)
    for c in range(4 * R + A, 4 * R + 3 * A, W):
        steps.append(lambda c=c: col(c))
    return steps


def _join_groups(pieces):
    return [jnp.concatenate(pieces[i:i + 2], axis=-1) for i in range(0, len(pieces), 2)]


def _proj_kernel(x_ref, g_ref, w_ref, cos_ref, sa_ref, sb_ref, *outs):
    steps = _proj_steps(x_ref[...], g_ref, w_ref, cos_ref[...], sa_ref[...], sb_ref[...])
    for ref, z in zip(outs, _join_groups([step() for step in steps])):
        ref[...] = z


def _proj(x, g, w_bf, pos):
    N, D = x.shape
    tabs = _rope_tables(pos)
    full = lambda shape: pl.BlockSpec(shape, lambda i: (0,) * len(shape))
    return pl.pallas_call(
        _proj_kernel,
        grid=(1,),
        in_specs=[full((N, D)), full((1, D)), full(w_bf.shape)] + [full(t.shape) for t in tabs],
        out_specs=[full((N, RET_WIDTH))] * 7,
        out_shape=[jax.ShapeDtypeStruct((N, RET_WIDTH), F32)] * 7,
        compiler_params=_cparams("proj", "arbitrary"),
        name="proj",
    )(x, g.reshape(1, D), w_bf, *tabs)


def _proj_ret_kernel(tiles_per_seq, x_ref, g_ref, w_ref, cos_ref, sa_ref, sb_ref, dec_ref, cd_ref, kd_ref,
                     sd_ref, gr_ref, wa_ref, wb_ref, wc_ref, ro_ref, s_out_ref, aq_ref, ak_ref, av_ref,
                     wa_bf_ref, wb_bf_ref, wc_bf_ref, sbd_ref, qkv_ref, gate_ref):
    s = pl.program_id(0)
    n_tiles = pl.num_programs(0) - 1
    tm = x_ref.shape[1]
    C = RET_CHUNK
    cur = s % 2
    prv = 1 - cur

    @pl.when(s == 0)
    def _():
        sbd_ref[...] = jnp.zeros_like(sbd_ref)
        qkv_ref[1] = jnp.zeros(qkv_ref.shape[1:], qkv_ref.dtype)
        gate_ref[1] = jnp.zeros(gate_ref.shape[1:], gate_ref.dtype)

    for w32, w16 in ((wa_ref, wa_bf_ref), (wb_ref, wb_bf_ref), (wc_ref, wc_bf_ref)):
        w16[...] = w32[...].astype(w16.dtype)

    lane = lax.broadcasted_iota(jnp.int32, (C, LANES), 1)
    half = lane < HEAD_DIM
    same_head = (lax.broadcasted_iota(jnp.int32, (LANES, LANES), 0) // HEAD_DIM
                 == lax.broadcasted_iota(jnp.int32, (LANES, LANES), 1) // HEAD_DIM)
    inv_d = 1.0 / HEAD_DIM
    starts_seq = (s - 1) % tiles_per_seq == 0

    def half_mean(t):
        lo = jnp.sum(jnp.where(half, t, 0.0), axis=-1, keepdims=True)
        hi = jnp.sum(jnp.where(half, 0.0, t), axis=-1, keepdims=True)
        return jnp.where(half, lo, hi) * inv_d

    n_pairs = RET_WIDTH // LANES
    n_chunks = tm // C
    state = [jnp.where(starts_seq, 0.0, sbd_ref[c]) for c in range(n_pairs)]

    def retention_unit(c, cc):
        ls = slice(c * LANES, (c + 1) * LANES)
        rs = slice(cc * C, (cc + 1) * C)
        S = state[c]
        qb = qkv_ref[prv, 0, rs, ls]
        kb = qkv_ref[prv, 1, rs, ls]
        vb = qkv_ref[prv, 2, rs, ls]
        q = qb.astype(F32)
        q_pair = jnp.concatenate([jnp.where(half, q, 0.0), jnp.where(half, 0.0, q)], axis=0).astype(BF16)
        inner = _dot_nt(q_pair, kb) * dec_ref[c]
        cross = _dot(qb, S.astype(BF16)) * cd_ref[:, ls]
        kd = (kb.astype(F32) * kd_ref[:, ls]).astype(BF16)
        state[c] = S * sd_ref[:, ls] + jnp.where(same_head, _dot_tn(kd, vb), 0.0)
        yield
        o2 = _dot(inner.astype(BF16), vb)
        o = jnp.where(half, o2[:C], o2[C:]) + cross
        yield
        mu = half_mean(o)
        d = o - mu
        var = half_mean(d * d)
        on = d * lax.rsqrt(var + GN_EPS) * gr_ref[:, ls]
        gate = gate_ref[prv, rs, ls]
        ro_ref[0, rs, ls] = (on * (gate * jax.nn.sigmoid(gate))).astype(ro_ref.dtype)
        yield

    def retention_pieces():
        for cc in range(n_chunks):
            units = [retention_unit(c, cc) for c in range(n_pairs)]
            for _ in range(3):
                for u in units:
                    next(u)
                    yield

    j = jnp.minimum(s, n_tiles - 1) % tiles_per_seq
    rows = pl.ds(pl.multiple_of(j * tm, tm), tm)
    steps = _proj_steps(x_ref[0], g_ref, w_ref, cos_ref[rows, :], sa_ref[rows, :], sb_ref[rows, :])
    pieces = retention_pieces()
    n_pieces = 3 * n_chunks * n_pairs
    cols = []
    for i, step in enumerate(steps):
        cols.append(step())
        for _ in range((i + 1) * n_pieces // len(steps) - i * n_pieces // len(steps)):
            next(pieces)
    for c in range(n_pairs):
        S = state[c]
        sbd_ref[c] = S
        s_out_ref[0, 2 * c] = S[:HEAD_DIM, :HEAD_DIM]
        s_out_ref[0, 2 * c + 1] = S[HEAD_DIM:, HEAD_DIM:]
    rq, rk, rv, rg, aq, ak, av = _join_groups(cols)
    aq_ref[0] = aq.astype(aq_ref.dtype)
    ak_ref[0] = ak
    av_ref[0] = av
    qkv_ref[cur, 0] = rq.astype(BF16)
    qkv_ref[cur, 1] = rk.astype(BF16)
    qkv_ref[cur, 2] = rv.astype(BF16)
    gate_ref[cur] = rg


def _proj_retention(x, g, w_bf, g_ret, tm, later_weights):
    B, T, D = x.shape
    R, A, H, Dh = RET_WIDTH, ATT_WIDTH, N_RET_HEADS, HEAD_DIM
    tps = T // tm
    n_tiles = B * tps
    tabs = _rope_tables(jnp.arange(T, dtype=jnp.int32))
    dec, cd, kd, sd = _retention_consts(RET_CHUNK)
    dec_pair = dec.reshape(H // 2, 2 * RET_CHUNK, RET_CHUNK)

    def this_tile(width):
        def index(s):
            t = jnp.minimum(s, n_tiles - 1)
            return (t // tps, t % tps, 0)
        return pl.BlockSpec((1, tm, width), index)

    def prev_tile(s):
        t = jnp.maximum(s - 1, 0)
        return (t // tps, t % tps, 0)

    def slab(w):
        rows = max(16, -(-w.shape[0] // n_tiles) // 16 * 16)
        while w.shape[0] % rows:
            rows += 16
        last = w.shape[0] // rows - 1
        return pl.BlockSpec((rows, w.shape[1]), lambda s: (jnp.minimum(s, last), 0))

    full = lambda shape: pl.BlockSpec(shape, lambda s: (0,) * len(shape))
    consts = [g.reshape(1, D), w_bf, *tabs, dec_pair, cd, kd, sd, g_ret.reshape(1, R)]
    slabs = [slab(w) for w in later_weights]
    return pl.pallas_call(
        functools.partial(_proj_ret_kernel, tps),
        grid=(n_tiles + 1,),
        in_specs=[this_tile(D)] + [full(t.shape) for t in consts] + slabs,
        out_specs=[pl.BlockSpec((1, tm, R), prev_tile),
                   pl.BlockSpec((1, H, Dh, Dh), lambda s: (jnp.maximum(s - 1, 0) // tps, 0, 0, 0)),
                   this_tile(A), this_tile(A), this_tile(A)] + slabs,
        out_shape=[jax.ShapeDtypeStruct((B, T, R), BF16), jax.ShapeDtypeStruct((B, H, Dh, Dh), F32),
                   jax.ShapeDtypeStruct((B, T, A), BF16), jax.ShapeDtypeStruct((B, T, A), F32),
                   jax.ShapeDtypeStruct((B, T, A), F32)]
                  + [jax.ShapeDtypeStruct(w.shape, BF16) for w in later_weights],
        scratch_shapes=[pltpu.VMEM((R // LANES, LANES, LANES), F32), pltpu.VMEM((2, 3, tm, R), BF16),
                        pltpu.VMEM((2, tm, R), F32)],
        compiler_params=_cparams("proj_retention", "arbitrary"),
        name="proj_retention",
    )(x, *consts, *later_weights)


def _ret_kernel(q_ref, k_ref, v_ref, gate_ref, dec_ref, cd_ref, kd_ref, sd_ref, gr_ref, s0_ref, o_ref, s_ref):
    def unit(b, h):
        sl = slice(h * HEAD_DIM, (h + 1) * HEAD_DIM)
        qh = q_ref[b, :, sl].astype(BF16)
        kh = k_ref[b, :, sl].astype(BF16)
        vh = v_ref[b, :, sl].astype(BF16)
        S = s0_ref[b, h]
        inner = _dot_nt(qh, kh) * dec_ref[h]
        cross = _dot(qh, S.astype(BF16)) * cd_ref[:, sl]
        kd = (kh.astype(F32) * kd_ref[:, sl]).astype(BF16)
        s_ref[b, h] = S * sd_ref[:, sl] + _dot_tn(kd, vh)
        yield
        o = _dot(inner.astype(BF16), vh) + cross
        yield
        mu = jnp.mean(o, axis=-1, keepdims=True)
        var = jnp.mean(jnp.square(o - mu), axis=-1, keepdims=True)
        on = (o - mu) * lax.rsqrt(var + GN_EPS) * gr_ref[:, sl]
        gate = gate_ref[b, :, sl]
        o_ref[b, :, sl] = on * (gate * jax.nn.sigmoid(gate))
        yield

    for b in range(q_ref.shape[0]):
        units = [unit(b, h) for h in range(N_RET_HEADS)]
        for _ in range(3):
            for u in units:
                next(u)


def _retention_step(rq, rk, rv, rg, g_ret, state):
    B, T, R = rq.shape
    H, Dh = N_RET_HEADS, HEAD_DIM
    consts = [*_retention_consts(T), g_ret.reshape(1, R)]
    bb = math.gcd(B, 8)
    row = pl.BlockSpec((bb, T, R), lambda b: (b, 0, 0))
    full = lambda shape: pl.BlockSpec(shape, lambda b: (0,) * len(shape))
    st = pl.BlockSpec((bb, H, Dh, Dh), lambda b: (b, 0, 0, 0))
    return pl.pallas_call(
        _ret_kernel,
        grid=(B // bb,),
        in_specs=[row, row, row, row] + [full(t.shape) for t in consts] + [st],
        out_specs=[row, st],
        out_shape=[jax.ShapeDtypeStruct((B, T, R), F32), jax.ShapeDtypeStruct((B, H, Dh, Dh), F32)],
        compiler_params=_cparams("retention", "arbitrary"),
        name="retention",
    )(rq, rk, rv, rg, *consts, state)


def _split3_dot(c, e):
    c1 = c.astype(BF16)
    r1 = c - c1.astype(F32)
    c2 = r1.astype(BF16)
    c3 = (r1 - c2.astype(F32)).astype(BF16)
    return _dot(c1, e) + _dot(c2, e) + _dot(c3, e)


def _chunk_rows(ref, stride):
    def load(start, c):
        idx = pl.ds(start, ATT_BLOCK) if stride == 1 else pl.ds(start, ATT_BLOCK, stride=stride)
        return ref[c, idx, :]
    return load


def _token_rows(ref):
    def load(start, c):
        return ref[0, pl.ds(start, ATT_BLOCK), c * LANES:(c + 1) * LANES].astype(F32)
    return load


def _att_pieces(load_q, load_k, load_v, bias_ref, cur, prev, first, out):
    BLK, H = ATT_BLOCK, N_ATT_HEADS
    half = lax.broadcasted_iota(jnp.int32, (BLK, LANES), 1) < HEAD_DIM
    scores = []
    for c in range(ATT_CHUNKS):
        q = load_q(cur, c)
        q_pair = jnp.concatenate([jnp.where(half, q, 0.0), jnp.where(half, 0.0, q)], axis=0).astype(BF16)
        k = load_k(cur, c)
        if prev is not None:
            k = jnp.concatenate([load_k(prev, c), k], axis=0)
        scores.append(_dot_nt(q_pair, k.astype(BF16)))
        yield
    stats = []
    for c in range(ATT_CHUNKS):
        s = scores[c] + bias_ref[2 * c * BLK:(2 * c + 2) * BLK, :]
        if prev is not None:
            col = lax.broadcasted_iota(jnp.int32, s.shape, 1)
            s = jnp.where(jnp.logical_and(first, col < BLK), NEG_INF, s)
        m = jnp.max(s, axis=-1, keepdims=True)
        p = jnp.exp(s - m)
        stats.append((m, jnp.sum(p, axis=-1, keepdims=True), p.astype(BF16)))
        yield
    accs = []
    for c in range(ATT_CHUNKS):
        v = load_v(cur, c)
        if prev is not None:
            v = jnp.concatenate([load_v(prev, c), v], axis=0)
        o = _dot(stats[c][2], v.astype(BF16))
        accs.append(jnp.where(half, o[:BLK], o[BLK:]))
        yield
    lane = lax.broadcasted_iota(jnp.int32, (BLK, LANES), 1)
    ml = jnp.zeros((BLK, LANES), F32)
    for h in range(H):
        m, l, _ = stats[h // 2]
        rows = slice((h % 2) * BLK, (h % 2 + 1) * BLK)
        ml = jnp.where(lane == h, m[rows], ml)
        ml = jnp.where(lane == H + h, l[rows], ml)
    out.append((accs, ml))
    yield


def _att_pipeline(blocks, after=None):
    outs = [[] for _ in blocks]
    gens = [_att_pieces(*b, o) for b, o in zip(blocks, outs)]
    n = ATT_CHUNKS
    total = 3 * n + 1
    tails = []
    for t in range((len(blocks) - 1) * n + total):
        for tail in tails:
            next(tail, None)
        for k, g in enumerate(gens):
            if 0 <= t - k * n < total:
                next(g)
                if t - k * n == total - 1 and after is not None:
                    tails.append(after(k, *outs[k][0]))
    for tail in tails:
        for _ in tail:
            pass
    return [o[0] for o in outs]


def _att_kernel(q_ref, k_ref, v_ref, bv_ref, e_ref, ao_ref, b1_ref, b4_ref, b16_ref, q4, k4, v4, stage,
                acc4, ml4, acc16, ml16):
    BLK, H = ATT_BLOCK, N_ATT_HEADS
    S = q_ref.shape[1]
    nblk = S // BLK
    d4, d16 = DIL_PATTERNS[1][1], DIL_PATTERNS[2][1]
    nb4 = nblk // d4
    per_res = S // d4

    @pl.when(pl.program_id(0) == 0)
    def _():
        for g, tab in enumerate((b1_ref, b4_ref, b16_ref)):
            for h in range(H):
                rows = jnp.broadcast_to(bv_ref[g, h:h + 1, :], (BLK, bv_ref.shape[2]))
                skew = pltpu.roll(rows, 0, 1, stride=1, stride_axis=0)
                tab[h * BLK:(h + 1) * BLK, :] = skew[:, 2 * BLK - tab.shape[1]:2 * BLK]

    for src, dst in ((q_ref, q4), (k_ref, k4), (v_ref, v4)):
        for c in range(ATT_CHUNKS):
            stage[...] = src[0, :, c * LANES:(c + 1) * LANES].astype(F32)
            for r in range(d4):
                dst[c, r * per_res:(r + 1) * per_res, :] = stage[pl.ds(r, per_res, stride=d4), :]
    contiguous = [_chunk_rows(t, 1) for t in (q4, k4, v4)]
    every_4th = [_chunk_rows(t, d16 // d4) for t in (q4, k4, v4)]
    tokens = [_token_rows(t) for t in (q_ref, k_ref, v_ref)]

    def dilated(i, carry):
        blocks, toks = [], []
        for n in (2 * i, 2 * i + 1):
            r, j = n // nb4, n % nb4
            cur = r * per_res + j * BLK
            prev = r * per_res + jnp.maximum(j - 1, 0) * BLK
            blocks.append((*contiguous, b4_ref, cur, prev, j == 0))
            toks.append((r + j * (BLK * d4), d4, acc4, ml4))
            blocks.append((*every_4th, b16_ref, (n % d4) * per_res + n // d4, None, None))
            toks.append((n, d16, acc16, ml16))
        for (accs, ml), (tok, dil, acc_ref, ml_ref) in zip(_att_pipeline(blocks), toks):
            for c in range(ATT_CHUNKS):
                acc_ref[c, pl.ds(tok, BLK, stride=dil), :] = accs[c]
            ml_ref[pl.ds(tok, BLK, stride=dil), :] = ml
        return carry

    def merge(cur, accs1, ml1):
        rows = pl.ds(cur, BLK)
        mls = [ml1, ml4[rows, :], ml16[rows, :]]
        accs = [accs1, [acc4[c, rows, :] for c in range(ATT_CHUNKS)], [acc16[c, rows, :] for c in range(ATT_CHUNKS)]]
        lane = lax.broadcasted_iota(jnp.int32, (BLK, LANES), 1)
        m_all = jnp.maximum(jnp.maximum(mls[0], mls[1]), mls[2])
        ws = [jnp.exp(ml - m_all) for ml in mls]
        den = sum(w * pltpu.roll(ml, LANES - H, 1) for w, ml in zip(ws, mls))
        ao = [jnp.zeros((BLK, LANES), F32) for _ in range(ATT_CHUNKS)]
        yield
        for w, acc in zip(ws, accs):
            coef = _split3_dot(jnp.where(lane < H, w / den, 0.0), e_ref[...])
            for c in range(ATT_CHUNKS):
                ao[c] = ao[c] + coef[:, c * LANES:(c + 1) * LANES] * acc[c]
            yield
        ao_ref[0, rows, :] = jnp.concatenate(ao, axis=-1).astype(ao_ref.dtype)
        yield

    def dense_and_merge(i, carry):
        per_trip = 4
        starts = [pl.multiple_of((per_trip * i + k) * BLK, BLK) for k in range(per_trip)]
        before = pl.multiple_of(jnp.maximum(per_trip * i - 1, 0) * BLK, BLK)
        blocks = [(*tokens, b1_ref, cur, prev, first)
                  for cur, prev, first in zip(starts, [before] + starts[:-1], [i == 0] + [False] * (per_trip - 1))]
        _att_pipeline(blocks, lambda k, accs, ml: merge(starts[k], accs, ml))
        return carry

    lax.fori_loop(0, nblk // 2, dilated, 0)
    lax.fori_loop(0, nblk // 4, dense_and_merge, 0)


def _prompt_attention(aq, ak, av, rel_bias):
    B, S, A = aq.shape
    CH = ATT_CHUNKS
    assert [d for _, d in DIL_PATTERNS] == [1, 4, 16] and S // ATT_BLOCK == 16
    bias_vectors = _prompt_bias_vectors(rel_bias)
    head_of_lane = jnp.arange(A)[None, :] // HEAD_DIM
    expand = (jnp.arange(LANES)[:, None] == head_of_lane).astype(BF16)
    seq = pl.BlockSpec((1, S, A), lambda b: (b, 0, 0))
    const = lambda shape: pl.BlockSpec(shape, lambda b: (0,) * len(shape), pipeline_mode=pl.Buffered(1))
    chunked = pltpu.VMEM((CH, S, LANES), F32)
    stats = pltpu.VMEM((S, LANES), F32)
    table = lambda width: pltpu.VMEM((N_ATT_HEADS * ATT_BLOCK, width), F32)
    return pl.pallas_call(
        _att_kernel,
        grid=(B,),
        in_specs=[seq, seq, seq, const(bias_vectors.shape), const(expand.shape)],
        out_specs=seq,
        out_shape=jax.ShapeDtypeStruct((B, S, A), BF16),
        scratch_shapes=[table(2 * ATT_BLOCK), table(2 * ATT_BLOCK), table(ATT_BLOCK),
                        chunked, chunked, chunked, stats, chunked, stats, chunked, stats],
        compiler_params=_cparams("prompt_att", "arbitrary"),
        name="prompt_att",
    )(aq, ak, av, bias_vectors, expand)


def _window_pieces(lows, T, kT_ref, vT_ref, qn_ref, kn_ref, vn_ref, c1_ref, c4_ref, c16_ref,
                   n1_ref, n4_ref, n16_ref, koT_ref, voT_ref, ao_ref):
    H, Dh, L = kT_ref.shape[1:]
    A = H * Dh
    zpad = jnp.zeros((LANES - T, A), F32)
    tail_lane = lax.broadcasted_iota(jnp.int32, (Dh, LANES), 1) >= LANES - T
    shifted = {}

    def shift_in(name, xT_ref, new_ref, out_ref):
        new_p = jnp.concatenate([zpad, new_ref[0]], axis=0)
        new_t = new_p.T
        heads = []
        for h in range(H):
            x = xT_ref[0, h]
            rolled = pltpu.roll(x, L - T, 1)
            tail = jnp.where(tail_lane, new_t[h * Dh:(h + 1) * Dh], rolled[:, L - LANES:])
            out_ref[0, h] = jnp.concatenate([rolled[:, :L - LANES], tail], axis=1)
            heads.append(x.astype(BF16))
            yield
        shifted[name] = (jnp.concatenate(heads, axis=0), new_p.astype(BF16))

    yield from shift_in("k", kT_ref, kn_ref, koT_ref)
    yield from shift_in("v", vT_ref, vn_ref, voT_ref)
    (kT, kn), (vT, vn) = shifted["k"], shifted["v"]
    HT = H * T
    row_head = lax.broadcasted_iota(jnp.int32, (HT, A), 0) // T
    lane_head = lax.broadcasted_iota(jnp.int32, (HT, A), 1) // HEAD_DIM
    diag = row_head == lane_head
    q_rows = jnp.where(diag, jnp.concatenate([qn_ref[0]] * H, axis=0), 0.0).astype(BF16)
    s_cache = _dot(q_rows, kT)
    s_new = _dot_nt(q_rows, kn)
    yield
    soft = []
    for lo, c_ref, n_ref in zip(lows, (c1_ref, c4_ref, c16_ref), (n1_ref, n4_ref, n16_ref)):
        sc = s_cache[:, lo:] + c_ref[...]
        sn = s_new + n_ref[...]
        m = jnp.maximum(jnp.max(sc, axis=-1, keepdims=True), jnp.max(sn, axis=-1, keepdims=True))
        pc = jnp.exp(sc - m)
        pn = jnp.exp(sn - m)
        l = jnp.sum(pc, axis=-1, keepdims=True) + jnp.sum(pn, axis=-1, keepdims=True)
        soft.append((m, l, pc.astype(BF16), pn.astype(BF16)))
        yield
    stats = []
    for lo, (m, l, pc, pn) in zip(lows, soft):
        stats.append((m, l, _dot_nt(pc, vT[:, lo:]) + _dot(pn, vn)))
        yield
    m_all = jnp.maximum(jnp.maximum(stats[0][0], stats[1][0]), stats[2][0])
    num = jnp.zeros((HT, A), F32)
    den = jnp.zeros((HT, 1), F32)
    for m, l, acc in stats:
        w = jnp.exp(m - m_all)
        num = num + w * acc
        den = den + w * l
    comb = jnp.where(diag, num / den, 0.0)
    out = comb[0:T]
    for h in range(1, H):
        out = out + comb[h * T:(h + 1) * T]
    ao_ref[0] = out
    yield


def _sample_window_operands(aq, ak, av, cache_kT, cache_vT, rel_bias, groups):
    B, T, A = aq.shape
    _, H, Dh, L = cache_kT.shape
    Hg, Ag = H // groups, A // groups
    lows = tuple(L - win for win, _ in DIL_PATTERNS)
    assert all(lo >= 0 and lo % LANES == 0 for lo in lows) and T <= LANES and Ag % LANES == 0
    tabs = _sample_bias_tables(rel_bias, L, T)
    big = pl.BlockSpec((1, Hg, Dh, L), lambda i: (i // groups, i % groups, 0, 0))
    small = pl.BlockSpec((1, T, Ag), lambda i: (i // groups, 0, i % groups))
    tab = lambda t: pl.BlockSpec((Hg * T, t.shape[1]), lambda i: (i % groups, 0))
    win_sds = jax.ShapeDtypeStruct((B, H, Dh, L), F32)
    args = [cache_kT, cache_vT, aq, ak, av, *tabs]
    in_specs = [big, big, small, small, small] + [tab(t) for t in tabs]
    out_specs = [big, big, small]
    out_shape = [win_sds, win_sds, jax.ShapeDtypeStruct((B, T, A), F32)]
    return functools.partial(_window_pieces, lows, T), args, in_specs, out_specs, out_shape


def _ffn_pieces(d_ff, ff_chunk, x_ref, ro_ref, ao_ref, wo_ref, wgu_ref, wd_ref, g_pm, g_pf, g_of, y_ref, act_ref):
    R = RET_WIDTH
    mix = _dot(ro_ref[...].astype(BF16), wo_ref[:R, :]) + _dot(ao_ref[...].astype(BF16), wo_ref[R:, :])
    x1 = x_ref[...] + _rms(mix, g_pm[...])
    h = _rms(x1, g_pf[...]).astype(BF16)
    yield
    for c in range(0, d_ff, ff_chunk):
        gate = _dot(h, wgu_ref[:, c:c + ff_chunk])
        up = _dot(h, wgu_ref[:, d_ff + c:d_ff + c + ff_chunk])
        act_ref[:, c:c + ff_chunk] = (gate * jax.nn.sigmoid(gate) * up).astype(BF16)
        yield
    f = _dot(act_ref[...], wd_ref[...])
    y_ref[...] = x1 + _rms(f, g_of[...])
    yield


def _out_kernel(d_ff, ff_chunk, *refs):
    for _ in _ffn_pieces(d_ff, ff_chunk, *refs):
        pass


def _out_window_kernel(d_ff, ff_chunk, window, n_ffn_in, n_win_in, *refs):
    ffn_in = refs[:n_ffn_in]
    win_in = refs[n_ffn_in:n_ffn_in + n_win_in]
    y_ref, koT_ref, voT_ref, sao_ref, act_ref = refs[n_ffn_in + n_win_in:]
    win = window(*win_in, koT_ref, voT_ref, sao_ref)
    for _ in _ffn_pieces(d_ff, ff_chunk, *ffn_in, y_ref, act_ref):
        next(win, None)
        next(win, None)
    for _ in win:
        pass


def _out_block(x, ro, ao, wo_bf, wgu_bf, wd_bf, g_post_mix, g_pre_ffn, g_post_ffn, tm, window=None):
    N, D = x.shape
    d_ff = wd_bf.shape[0]
    row = lambda width: pl.BlockSpec((tm, width), lambda i: (i, 0))
    const = lambda shape: pl.BlockSpec(shape, lambda i: (0,) * len(shape), pipeline_mode=pl.Buffered(1))
    args = [x, ro, ao, wo_bf, wgu_bf, wd_bf, g_post_mix.reshape(1, D), g_pre_ffn.reshape(1, D),
            g_post_ffn.reshape(1, D)]
    in_specs = [row(D), row(RET_WIDTH), row(ATT_WIDTH), const(wo_bf.shape), const(wgu_bf.shape),
                const(wd_bf.shape), const((1, D)), const((1, D)), const((1, D))]
    body = functools.partial(_out_kernel, d_ff, FF_CHUNK)
    out_specs, out_shape = [row(D)], [jax.ShapeDtypeStruct((N, D), F32)]
    if window is not None:
        win_body, win_args, win_in_specs, win_out_specs, win_out_shape = window
        assert N // tm == win_out_shape[0].shape[0] * (win_out_shape[0].shape[1] // win_out_specs[0].block_shape[1])
        body = functools.partial(_out_window_kernel, d_ff, FF_CHUNK, win_body, len(args), len(win_args))
        args, in_specs = args + win_args, in_specs + win_in_specs
        out_specs, out_shape = out_specs + win_out_specs, out_shape + win_out_shape
    outs = pl.pallas_call(
        body,
        grid=(N // tm,),
        in_specs=in_specs,
        out_specs=out_specs,
        out_shape=out_shape,
        scratch_shapes=[pltpu.VMEM((tm, d_ff), BF16)],
        compiler_params=_cparams("out_ffn", "arbitrary"),
        name="out_ffn",
    )(*args)
    return outs[0] if window is None else outs


def kernel(x_prompt, x_sample, state_ret, cache_k_win, cache_v_win, rel_bias, w_in, g_ret, w_out,
           g_pre_mix, g_post_mix, g_pre_ffn, g_post_ffn, w_gu, w_down):
    depth = w_in.shape[0]
    assert depth == 1
    B, S, D = x_prompt.shape
    Bs, Ts, _ = x_sample.shape
    H, Dh = N_ATT_HEADS, HEAD_DIM
    l = 0
    w_in_bf = w_in[l].astype(BF16)

    pos_s = PAST_LEN + jnp.arange(Ts, dtype=jnp.int32)
    N_s = Bs * Ts
    outs = _proj(x_sample.reshape(N_s, D), g_pre_mix[l], w_in_bf, jnp.tile(pos_s, Bs))
    srq, srk, srv, srg, saq, sak, sav = [t.reshape(Bs, Ts, RET_WIDTH) for t in outs]
    sro, s_s = _retention_step(srq, srk, srv, srg, g_ret[l], state_ret[l])
    to_t = lambda t: jnp.transpose(t, (0, 2, 3, 1))
    from_t = lambda t: jnp.transpose(t, (0, 3, 1, 2))
    window = _sample_window_operands(saq, sak, sav, to_t(cache_k_win[l]), to_t(cache_v_win[l]), rel_bias,
                                     WINDOW_HEAD_GROUPS)

    ro, s_p, aq, ak, av, wo_bf, wgu_bf, wd_bf = _proj_retention(
        x_prompt, g_pre_mix[l], w_in_bf, g_ret[l], PROJ_TILE, (w_out[l], w_gu[l], w_down[l]))
    ao = _prompt_attention(aq, ak, av, rel_bias)
    y_p, k_sT, v_sT, sao = _out_block(x_prompt.reshape(B * S, D), ro.reshape(B * S, RET_WIDTH),
                                      ao.reshape(B * S, ATT_WIDTH), wo_bf, wgu_bf, wd_bf, g_post_mix[l],
                                      g_pre_ffn[l], g_post_ffn[l], FFN_TILE, window)
    k_s, v_s = from_t(k_sT), from_t(v_sT)

    y_s = _out_block(x_sample.reshape(N_s, D), sro.reshape(N_s, RET_WIDTH), sao.reshape(N_s, ATT_WIDTH),
                     wo_bf, wgu_bf, wd_bf, g_post_mix[l], g_pre_ffn[l], g_post_ffn[l], N_s)

    return (y_p.reshape(B, S, D), y_s.reshape(Bs, Ts, D),
            s_p[None], ak.reshape(1, B, S, H, Dh), av.reshape(1, B, S, H, Dh),
            s_s[None], k_s[None], v_s[None])
```

```python
import functools
import math

import jax
import jax.numpy as jnp
from jax import lax
from jax.experimental import pallas as pl
from jax.experimental.pallas import tpu as pltpu

F32 = jnp.float32
BF16 = jnp.bfloat16

HEAD_DIM = 64
N_RET_HEADS = 8
N_ATT_HEADS = 8
RET_WIDTH = N_RET_HEADS * HEAD_DIM
ATT_WIDTH = N_ATT_HEADS * HEAD_DIM
RET_CHUNK = 128
DIL_PATTERNS = ((128, 1), (512, 4), (2048, 16))
ATT_BLOCK = 128
N_BUCKETS = 32
MAX_DISTANCE = 2048
ROPE_BASE = 10000.0
NORM_EPS = 1e-6
GN_EPS = 1e-5
PAST_LEN = 16384
LANES = 128
ATT_CHUNKS = ATT_WIDTH // LANES
NEG_INF = float("-inf")

PROJ_TILE = 512
FFN_TILE = 256
FF_CHUNK = 256
WINDOW_HEAD_GROUPS = 2
VMEM_MB = dict(proj=40, retention=32, proj_retention=56, prompt_att=58, out_ffn=56)


def _cparams(name, *sem):
    return pltpu.CompilerParams(dimension_semantics=sem, vmem_limit_bytes=VMEM_MB[name] * 1024 * 1024)


def _rms(x, g):
    return x * lax.rsqrt(jnp.mean(x * x, axis=-1, keepdims=True) + NORM_EPS) * g


def _dot(a, b):
    return jnp.dot(a, b, preferred_element_type=F32)


def _dot_nt(a, b):
    return lax.dot_general(a, b, (((1,), (1,)), ((), ())), preferred_element_type=F32)


def _dot_tn(a, b):
    return lax.dot_general(a, b, (((0,), (0,)), ((), ())), preferred_element_type=F32)


def _rope_tables(pos):
    inv = ROPE_BASE ** (-jnp.arange(0, HEAD_DIM, 2, dtype=F32) / HEAD_DIM)
    ang = pos.astype(F32)[:, None] * inv[None, :]
    cos, sin = jnp.cos(ang), jnp.sin(ang)
    zero = jnp.zeros_like(sin)
    rep = LANES // HEAD_DIM
    cosf = jnp.tile(jnp.concatenate([cos, cos], axis=-1), (1, rep))
    sa = jnp.tile(jnp.concatenate([-sin, zero], axis=-1), (1, rep))
    sb = jnp.tile(jnp.concatenate([zero, sin], axis=-1), (1, rep))
    return cosf, sa, sb


def _retention_consts(C):
    log_g = jnp.log1p(-jnp.exp2(-5.0 - jnp.arange(N_RET_HEADS, dtype=F32)))
    i = jnp.arange(C, dtype=F32)
    rel = i[:, None] - i[None, :]
    decay = jnp.where(rel[None] >= 0, jnp.exp(jnp.maximum(rel, 0.0)[None] * log_g[:, None, None]), 0.0)
    cross = jnp.exp((i + 1.0)[:, None] * log_g[None, :])
    kdec = jnp.exp((C - 1.0 - i)[:, None] * log_g[None, :])
    sdec = jnp.exp(C * log_g)[None, :]
    expand = lambda t: jnp.repeat(t, HEAD_DIM, axis=-1)
    return decay, expand(cross), expand(kdec), expand(sdec)


def _t5_bucket(dist):
    max_exact = N_BUCKETS // 2
    d_f = jnp.maximum(dist, 1).astype(F32)
    large = max_exact + (jnp.log(d_f / max_exact) / math.log(MAX_DISTANCE / max_exact)
                         * (N_BUCKETS - max_exact)).astype(jnp.int32)
    large = jnp.minimum(large, N_BUCKETS - 1)
    return jnp.where(dist < max_exact, dist, large)


def _pattern_bias_rev(rel_bias, dil, nk):
    dist = jnp.arange(nk, -1, -1, dtype=jnp.int32) * dil
    return rel_bias[_t5_bucket(dist)].astype(F32).T


def _prompt_bias_vectors(rel_bias):
    BLK, H = ATT_BLOCK, N_ATT_HEADS
    out = []
    for win, dil in DIL_PATTERNS:
        nk = win // dil
        assert nk == BLK
        out.append(jnp.concatenate([_pattern_bias_rev(rel_bias, dil, nk),
                                    jnp.full((H, 3 * BLK - nk - 1), NEG_INF, F32)], axis=1))
    return jnp.stack(out)


def _sample_bias_tables(rel_bias, L, T):
    H = N_ATT_HEADS
    cache_part, new_part = [], []
    for win, dil in DIL_PATTERNS:
        nk = win // dil
        rev = _pattern_bias_rev(rel_bias, dil, nk)
        if dil > 1:
            gaps = jnp.full((H, nk + 1, dil - 1), NEG_INF, F32)
            rev = jnp.concatenate([rev[:, :, None], gaps], axis=2).reshape(H, (nk + 1) * dil)[:, :nk * dil + 1]
        pv = jnp.concatenate([jnp.full((H, T), NEG_INF, F32), rev, jnp.full((H, 2 * T), NEG_INF, F32)], axis=1)
        width = win + T
        rows = jnp.stack([pv[:, T - t:T - t + width] for t in range(T)], axis=1)
        cache_part.append(rows[:, :, :win].reshape(H * T, win))
        new = jnp.concatenate([jnp.full((H, T, LANES - T), NEG_INF, F32), rows[:, :, win:]], axis=2)
        new_part.append(new.reshape(H * T, LANES))
    return cache_part + new_part


def _proj_steps(x, g_ref, w_ref, cos, sa, sb):
    h = _rms(x, g_ref[...]).astype(BF16)
    W = 2 * LANES
    rep = W // LANES
    cosf = jnp.concatenate([cos] * rep, axis=-1)
    saf = jnp.concatenate([sa] * rep, axis=-1)
    sbf = jnp.concatenate([sb] * rep, axis=-1)
    half = HEAD_DIM // 2

    def col(c):
        return _dot(h, w_ref[:, c:c + W])

    def rope(z):
        return z * cosf + pltpu.roll(z, W - half, 1) * saf + pltpu.roll(z, half, 1) * sbf

    R, A = RET_WIDTH, ATT_WIDTH
    scale = HEAD_DIM ** -0.5
    steps = []
    for c in range(0, R, W):
        steps.append(lambda c=c: rope(col(c)))
    for c in range(R, 2 * R, W):
        steps.append(lambda c=c: rope(col(c)) * scale)
    for c in range(2 * R, 4 * R, W):
        steps.append(lambda c=c: col(c))
    for c in range(4 * R, 4 * R + A, W):
        steps.append(lambda c=c: col(c) * scale)
    for c in range(4 * R + A, 4 * R + 3 * A, W):
        steps.append(lambda c=c: col(c))
    return steps


def _join_groups(pieces):
    return [jnp.concatenate(pieces[i:i + 2], axis=-1) for i in range(0, len(pieces), 2)]


def _proj_kernel(x_ref, g_ref, w_ref, cos_ref, sa_ref, sb_ref, *outs):
    steps = _proj_steps(x_ref[...], g_ref, w_ref, cos_ref[...], sa_ref[...], sb_ref[...])
    for ref, z in zip(outs, _join_groups([step() for step in steps])):
        ref[...] = z


def _proj(x, g, w_bf, pos):
    N, D = x.shape
    tabs = _rope_tables(pos)
    full = lambda shape: pl.BlockSpec(shape, lambda i: (0,) * len(shape))
    return pl.pallas_call(
        _proj_kernel,
        grid=(1,),
        in_specs=[full((N, D)), full((1, D)), full(w_bf.shape)] + [full(t.shape) for t in tabs],
        out_specs=[full((N, RET_WIDTH))] * 7,
        out_shape=[jax.ShapeDtypeStruct((N, RET_WIDTH), F32)] * 7,
        compiler_params=_cparams("proj", "arbitrary"),
        name="proj",
    )(x, g.reshape(1, D), w_bf, *tabs)


def _proj_ret_kernel(tiles_per_seq, x_ref, g_ref, w_ref, cos_ref, sa_ref, sb_ref, dec_ref, cd_ref, kd_ref,
                     sd_ref, gr_ref, wa_ref, wb_ref, wc_ref, ro_ref, s_out_ref, aq_ref, ak_ref, av_ref,
                     wa_bf_ref, wb_bf_ref, wc_bf_ref, sbd_ref, qkv_ref, gate_ref):
    s = pl.program_id(0)
    n_tiles = pl.num_programs(0) - 1
    tm = x_ref.shape[1]
    C = RET_CHUNK
    cur = s % 2
    prv = 1 - cur

    @pl.when(s == 0)
    def _():
        sbd_ref[...] = jnp.zeros_like(sbd_ref)
        qkv_ref[1] = jnp.zeros(qkv_ref.shape[1:], qkv_ref.dtype)
        gate_ref[1] = jnp.zeros(gate_ref.shape[1:], gate_ref.dtype)

    for w32, w16 in ((wa_ref, wa_bf_ref), (wb_ref, wb_bf_ref), (wc_ref, wc_bf_ref)):
        w16[...] = w32[...].astype(w16.dtype)

    lane = lax.broadcasted_iota(jnp.int32, (C, LANES), 1)
    half = lane < HEAD_DIM
    same_head = (lax.broadcasted_iota(jnp.int32, (LANES, LANES), 0) // HEAD_DIM
                 == lax.broadcasted_iota(jnp.int32, (LANES, LANES), 1) // HEAD_DIM)
    inv_d = 1.0 / HEAD_DIM
    starts_seq = (s - 1) % tiles_per_seq == 0

    def half_mean(t):
        lo = jnp.sum(jnp.where(half, t, 0.0), axis=-1, keepdims=True)
        hi = jnp.sum(jnp.where(half, 0.0, t), axis=-1, keepdims=True)
        return jnp.where(half, lo, hi) * inv_d

    n_pairs = RET_WIDTH // LANES
    n_chunks = tm // C
    state = [jnp.where(starts_seq, 0.0, sbd_ref[c]) for c in range(n_pairs)]

    def retention_unit(c, cc):
        ls = slice(c * LANES, (c + 1) * LANES)
        rs = slice(cc * C, (cc + 1) * C)
        S = state[c]
        qb = qkv_ref[prv, 0, rs, ls]
        kb = qkv_ref[prv, 1, rs, ls]
        vb = qkv_ref[prv, 2, rs, ls]
        q = qb.astype(F32)
        q_pair = jnp.concatenate([jnp.where(half, q, 0.0), jnp.where(half, 0.0, q)], axis=0).astype(BF16)
        inner = _dot_nt(q_pair, kb) * dec_ref[c]
        cross = _dot(qb, S.astype(BF16)) * cd_ref[:, ls]
        kd = (kb.astype(F32) * kd_ref[:, ls]).astype(BF16)
        state[c] = S * sd_ref[:, ls] + jnp.where(same_head, _dot_tn(kd, vb), 0.0)
        yield
        o2 = _dot(inner.astype(BF16), vb)
        o = jnp.where(half, o2[:C], o2[C:]) + cross
        yield
        mu = half_mean(o)
        d = o - mu
        var = half_mean(d * d)
        on = d * lax.rsqrt(var + GN_EPS) * gr_ref[:, ls]
        gate = gate_ref[prv, rs, ls]
        ro_ref[0, rs, ls] = (on * (gate * jax.nn.sigmoid(gate))).astype(ro_ref.dtype)
        yield

    def retention_pieces():
        for cc in range(n_chunks):
            units = [retention_unit(c, cc) for c in range(n_pairs)]
            for _ in range(3):
                for u in units:
                    next(u)
                    yield

    j = jnp.minimum(s, n_tiles - 1) % tiles_per_seq
    rows = pl.ds(pl.multiple_of(j * tm, tm), tm)
    steps = _proj_steps(x_ref[0], g_ref, w_ref, cos_ref[rows, :], sa_ref[rows, :], sb_ref[rows, :])
    pieces = retention_pieces()
    n_pieces = 3 * n_chunks * n_pairs
    cols = []
    for i, step in enumerate(steps):
        cols.append(step())
        for _ in range((i + 1) * n_pieces // len(steps) - i * n_pieces // len(steps)):
            next(pieces)
    for c in range(n_pairs):
        S = state[c]
        sbd_ref[c] = S
        s_out_ref[0, 2 * c] = S[:HEAD_DIM, :HEAD_DIM]
        s_out_ref[0, 2 * c + 1] = S[HEAD_DIM:, HEAD_DIM:]
    rq, rk, rv, rg, aq, ak, av = _join_groups(cols)
    aq_ref[0] = aq.astype(aq_ref.dtype)
    ak_ref[0] = ak
    av_ref[0] = av
    qkv_ref[cur, 0] = rq.astype(BF16)
    qkv_ref[cur, 1] = rk.astype(BF16)
    qkv_ref[cur, 2] = rv.astype(BF16)
    gate_ref[cur] = rg


def _proj_retention(x, g, w_bf, g_ret, tm, later_weights):
    B, T, D = x.shape
    R, A, H, Dh = RET_WIDTH, ATT_WIDTH, N_RET_HEADS, HEAD_DIM
    tps = T // tm
    n_tiles = B * tps
    tabs = _rope_tables(jnp.arange(T, dtype=jnp.int32))
    dec, cd, kd, sd = _retention_consts(RET_CHUNK)
    dec_pair = dec.reshape(H // 2, 2 * RET_CHUNK, RET_CHUNK)

    def this_tile(width):
        def index(s):
            t = jnp.minimum(s, n_tiles - 1)
            return (t // tps, t % tps, 0)
        return pl.BlockSpec((1, tm, width), index)

    def prev_tile(s):
        t = jnp.maximum(s - 1, 0)
        return (t // tps, t % tps, 0)

    def slab(w):
        rows = max(16, -(-w.shape[0] // n_tiles) // 16 * 16)
        while w.shape[0] % rows:
            rows += 16
        last = w.shape[0] // rows - 1
        return pl.BlockSpec((rows, w.shape[1]), lambda s: (jnp.minimum(s, last), 0))

    full = lambda shape: pl.BlockSpec(shape, lambda s: (0,) * len(shape))
    consts = [g.reshape(1, D), w_bf, *tabs, dec_pair, cd, kd, sd, g_ret.reshape(1, R)]
    slabs = [slab(w) for w in later_weights]
    return pl.pallas_call(
        functools.partial(_proj_ret_kernel, tps),
        grid=(n_tiles + 1,),
        in_specs=[this_tile(D)] + [full(t.shape) for t in consts] + slabs,
        out_specs=[pl.BlockSpec((1, tm, R), prev_tile),
                   pl.BlockSpec((1, H, Dh, Dh), lambda s: (jnp.maximum(s - 1, 0) // tps, 0, 0, 0)),
                   this_tile(A), this_tile(A), this_tile(A)] + slabs,
        out_shape=[jax.ShapeDtypeStruct((B, T, R), BF16), jax.ShapeDtypeStruct((B, H, Dh, Dh), F32),
                   jax.ShapeDtypeStruct((B, T, A), BF16), jax.ShapeDtypeStruct((B, T, A), F32),
                   jax.ShapeDtypeStruct((B, T, A), F32)]
                  + [jax.ShapeDtypeStruct(w.shape, BF16) for w in later_weights],
        scratch_shapes=[pltpu.VMEM((R // LANES, LANES, LANES), F32), pltpu.VMEM((2, 3, tm, R), BF16),
                        pltpu.VMEM((2, tm, R), F32)],
        compiler_params=_cparams("proj_retention", "arbitrary"),
        name="proj_retention",
    )(x, *consts, *later_weights)


def _ret_kernel(q_ref, k_ref, v_ref, gate_ref, dec_ref, cd_ref, kd_ref, sd_ref, gr_ref, s0_ref, o_ref, s_ref):
    def unit(b, h):
        sl = slice(h * HEAD_DIM, (h + 1) * HEAD_DIM)
        qh = q_ref[b, :, sl].astype(BF16)
        kh = k_ref[b, :, sl].astype(BF16)
        vh = v_ref[b, :, sl].astype(BF16)
        S = s0_ref[b, h]
        inner = _dot_nt(qh, kh) * dec_ref[h]
        cross = _dot(qh, S.astype(BF16)) * cd_ref[:, sl]
        kd = (kh.astype(F32) * kd_ref[:, sl]).astype(BF16)
        s_ref[b, h] = S * sd_ref[:, sl] + _dot_tn(kd, vh)
        yield
        o = _dot(inner.astype(BF16), vh) + cross
        yield
        mu = jnp.mean(o, axis=-1, keepdims=True)
        var = jnp.mean(jnp.square(o - mu), axis=-1, keepdims=True)
        on = (o - mu) * lax.rsqrt(var + GN_EPS) * gr_ref[:, sl]
        gate = gate_ref[b, :, sl]
        o_ref[b, :, sl] = on * (gate * jax.nn.sigmoid(gate))
        yield

    for b in range(q_ref.shape[0]):
        units = [unit(b, h) for h in range(N_RET_HEADS)]
        for _ in range(3):
            for u in units:
                next(u)


def _retention_step(rq, rk, rv, rg, g_ret, state):
    B, T, R = rq.shape
    H, Dh = N_RET_HEADS, HEAD_DIM
    consts = [*_retention_consts(T), g_ret.reshape(1, R)]
    bb = math.gcd(B, 8)
    row = pl.BlockSpec((bb, T, R), lambda b: (b, 0, 0))
    full = lambda shape: pl.BlockSpec(shape, lambda b: (0,) * len(shape))
    st = pl.BlockSpec((bb, H, Dh, Dh), lambda b: (b, 0, 0, 0))
    return pl.pallas_call(
        _ret_kernel,
        grid=(B // bb,),
        in_specs=[row, row, row, row] + [full(t.shape) for t in consts] + [st],
        out_specs=[row, st],
        out_shape=[jax.ShapeDtypeStruct((B, T, R), F32), jax.ShapeDtypeStruct((B, H, Dh, Dh), F32)],
        compiler_params=_cparams("retention", "arbitrary"),
        name="retention",
    )(rq, rk, rv, rg, *consts, state)


def _split3_dot(c, e):
    c1 = c.astype(BF16)
    r1 = c - c1.astype(F32)
    c2 = r1.astype(BF16)
    c3 = (r1 - c2.astype(F32)).astype(BF16)
    return _dot(c1, e) + _dot(c2, e) + _dot(c3, e)


def _chunk_rows(ref, stride):
    def load(start, c):
        idx = pl.ds(start, ATT_BLOCK) if stride == 1 else pl.ds(start, ATT_BLOCK, stride=stride)
        return ref[c, idx, :]
    return load


def _token_rows(ref):
    def load(start, c):
        return ref[0, pl.ds(start, ATT_BLOCK), c * LANES:(c + 1) * LANES].astype(F32)
    return load


def _att_pieces(load_q, load_k, load_v, bias_ref, cur, prev, first, out):
    BLK, H = ATT_BLOCK, N_ATT_HEADS
    half = lax.broadcasted_iota(jnp.int32, (BLK, LANES), 1) < HEAD_DIM
    scores = []
    for c in range(ATT_CHUNKS):
        q = load_q(cur, c)
        q_pair = jnp.concatenate([jnp.where(half, q, 0.0), jnp.where(half, 0.0, q)], axis=0).astype(BF16)
        k = load_k(cur, c)
        if prev is not None:
            k = jnp.concatenate([load_k(prev, c), k], axis=0)
        scores.append(_dot_nt(q_pair, k.astype(BF16)))
        yield
    stats = []
    for c in range(ATT_CHUNKS):
        s = scores[c] + bias_ref[2 * c * BLK:(2 * c + 2) * BLK, :]
        if prev is not None:
            col = lax.broadcasted_iota(jnp.int32, s.shape, 1)
            s = jnp.where(jnp.logical_and(first, col < BLK), NEG_INF, s)
        m = jnp.max(s, axis=-1, keepdims=True)
        p = jnp.exp(s - m)
        stats.append((m, jnp.sum(p, axis=-1, keepdims=True), p.astype(BF16)))
        yield
    accs = []
    for c in range(ATT_CHUNKS):
        v = load_v(cur, c)
        if prev is not None:
            v = jnp.concatenate([load_v(prev, c), v], axis=0)
        o = _dot(stats[c][2], v.astype(BF16))
        accs.append(jnp.where(half, o[:BLK], o[BLK:]))
        yield
    lane = lax.broadcasted_iota(jnp.int32, (BLK, LANES), 1)
    ml = jnp.zeros((BLK, LANES), F32)
    for h in range(H):
        m, l, _ = stats[h // 2]
        rows = slice((h % 2) * BLK, (h % 2 + 1) * BLK)
        ml = jnp.where(lane == h, m[rows], ml)
        ml = jnp.where(lane == H + h, l[rows], ml)
    out.append((accs, ml))
    yield


def _att_pipeline(blocks, after=None):
    outs = [[] for _ in blocks]
    gens = [_att_pieces(*b, o) for b, o in zip(blocks, outs)]
    n = ATT_CHUNKS
    total = 3 * n + 1
    tails = []
    for t in range((len(blocks) - 1) * n + total):
        for tail in tails:
            next(tail, None)
        for k, g in enumerate(gens):
            if 0 <= t - k * n < total:
                next(g)
                if t - k * n == total - 1 and after is not None:
                    tails.append(after(k, *outs[k][0]))
    for tail in tails:
        for _ in tail:
            pass
    return [o[0] for o in outs]


def _att_kernel(q_ref, k_ref, v_ref, bv_ref, e_ref, ao_ref, b1_ref, b4_ref, b16_ref, q4, k4, v4, stage,
                acc4, ml4, acc16, ml16):
    BLK, H = ATT_BLOCK, N_ATT_HEADS
    S = q_ref.shape[1]
    nblk = S // BLK
    d4, d16 = DIL_PATTERNS[1][1], DIL_PATTERNS[2][1]
    nb4 = nblk // d4
    per_res = S // d4

    @pl.when(pl.program_id(0) == 0)
    def _():
        for g, tab in enumerate((b1_ref, b4_ref, b16_ref)):
            for h in range(H):
                rows = jnp.broadcast_to(bv_ref[g, h:h + 1, :], (BLK, bv_ref.shape[2]))
                skew = pltpu.roll(rows, 0, 1, stride=1, stride_axis=0)
                tab[h * BLK:(h + 1) * BLK, :] = skew[:, 2 * BLK - tab.shape[1]:2 * BLK]

    for src, dst in ((q_ref, q4), (k_ref, k4), (v_ref, v4)):
        for c in range(ATT_CHUNKS):
            stage[...] = src[0, :, c * LANES:(c + 1) * LANES].astype(F32)
            for r in range(d4):
                dst[c, r * per_res:(r + 1) * per_res, :] = stage[pl.ds(r, per_res, stride=d4), :]
    contiguous = [_chunk_rows(t, 1) for t in (q4, k4, v4)]
    every_4th = [_chunk_rows(t, d16 // d4) for t in (q4, k4, v4)]
    tokens = [_token_rows(t) for t in (q_ref, k_ref, v_ref)]

    def dilated(i, carry):
        blocks, toks = [], []
        for n in (2 * i, 2 * i + 1):
            r, j = n // nb4, n % nb4
            cur = r * per_res + j * BLK
            prev = r * per_res + jnp.maximum(j - 1, 0) * BLK
            blocks.append((*contiguous, b4_ref, cur, prev, j == 0))
            toks.append((r + j * (BLK * d4), d4, acc4, ml4))
            blocks.append((*every_4th, b16_ref, (n % d4) * per_res + n // d4, None, None))
            toks.append((n, d16, acc16, ml16))
        for (accs, ml), (tok, dil, acc_ref, ml_ref) in zip(_att_pipeline(blocks), toks):
            for c in range(ATT_CHUNKS):
                acc_ref[c, pl.ds(tok, BLK, stride=dil), :] = accs[c]
            ml_ref[pl.ds(tok, BLK, stride=dil), :] = ml
        return carry

    def merge(cur, accs1, ml1):
        rows = pl.ds(cur, BLK)
        mls = [ml1, ml4[rows, :], ml16[rows, :]]
        accs = [accs1, [acc4[c, rows, :] for c in range(ATT_CHUNKS)], [acc16[c, rows, :] for c in range(ATT_CHUNKS)]]
        lane = lax.broadcasted_iota(jnp.int32, (BLK, LANES), 1)
        m_all = jnp.maximum(jnp.maximum(mls[0], mls[1]), mls[2])
        ws = [jnp.exp(ml - m_all) for ml in mls]
        den = sum(w * pltpu.roll(ml, LANES - H, 1) for w, ml in zip(ws, mls))
        ao = [jnp.zeros((BLK, LANES), F32) for _ in range(ATT_CHUNKS)]
        yield
        for w, acc in zip(ws, accs):
            coef = _split3_dot(jnp.where(lane < H, w / den, 0.0), e_ref[...])
            for c in range(ATT_CHUNKS):
                ao[c] = ao[c] + coef[:, c * LANES:(c + 1) * LANES] * acc[c]
            yield
        ao_ref[0, rows, :] = jnp.concatenate(ao, axis=-1).astype(ao_ref.dtype)
        yield

    def dense_and_merge(i, carry):
        per_trip = 4
        starts = [pl.multiple_of((per_trip * i + k) * BLK, BLK) for k in range(per_trip)]
        before = pl.multiple_of(jnp.maximum(per_trip * i - 1, 0) * BLK, BLK)
        blocks = [(*tokens, b1_ref, cur, prev, first)
                  for cur, prev, first in zip(starts, [before] + starts[:-1], [i == 0] + [False] * (per_trip - 1))]
        _att_pipeline(blocks, lambda k, accs, ml: merge(starts[k], accs, ml))
        return carry

    lax.fori_loop(0, nblk // 2, dilated, 0)
    lax.fori_loop(0, nblk // 4, dense_and_merge, 0)


def _prompt_attention(aq, ak, av, rel_bias):
    B, S, A = aq.shape
    CH = ATT_CHUNKS
    assert [d for _, d in DIL_PATTERNS] == [1, 4, 16] and S // ATT_BLOCK == 16
    bias_vectors = _prompt_bias_vectors(rel_bias)
    head_of_lane = jnp.arange(A)[None, :] // HEAD_DIM
    expand = (jnp.arange(LANES)[:, None] == head_of_lane).astype(BF16)
    seq = pl.BlockSpec((1, S, A), lambda b: (b, 0, 0))
    const = lambda shape: pl.BlockSpec(shape, lambda b: (0,) * len(shape), pipeline_mode=pl.Buffered(1))
    chunked = pltpu.VMEM((CH, S, LANES), F32)
    stats = pltpu.VMEM((S, LANES), F32)
    table = lambda width: pltpu.VMEM((N_ATT_HEADS * ATT_BLOCK, width), F32)
    return pl.pallas_call(
        _att_kernel,
        grid=(B,),
        in_specs=[seq, seq, seq, const(bias_vectors.shape), const(expand.shape)],
        out_specs=seq,
        out_shape=jax.ShapeDtypeStruct((B, S, A), BF16),
        scratch_shapes=[table(2 * ATT_BLOCK), table(2 * ATT_BLOCK), table(ATT_BLOCK),
                        chunked, chunked, chunked, stats, chunked, stats, chunked, stats],
        compiler_params=_cparams("prompt_att", "arbitrary"),
        name="prompt_att",
    )(aq, ak, av, bias_vectors, expand)


def _window_pieces(lows, T, groups, kT_ref, vT_ref, qn_ref, kn_ref, vn_ref, c1_ref, c4_ref, c16_ref,
                   n1_ref, n4_ref, n16_ref, koT_ref, voT_ref, ao_ref):
    H, Dh, L = kT_ref.shape[1:]
    A = H * Dh
    zpad = jnp.zeros((LANES - T, A), F32)
    tail_lane = lax.broadcasted_iota(jnp.int32, (Dh, LANES), 1) >= LANES - T
    shifted = {}

    def shift_in(name, xT_ref, new_ref, out_ref):
        new_p = jnp.concatenate([zpad, new_ref[0]], axis=0)
        new_t = new_p.T
        heads = []
        for h in range(H):
            x = xT_ref[0, h]
            rolled = pltpu.roll(x, L - T, 1)
            tail = jnp.where(tail_lane, new_t[h * Dh:(h + 1) * Dh], rolled[:, L - LANES:])
            out_ref[0, h] = jnp.concatenate([rolled[:, :L - LANES], tail], axis=1)
            heads.append(x.astype(BF16))
            yield
        shifted[name] = (jnp.concatenate(heads, axis=0), new_p.astype(BF16))

    yield from shift_in("k", kT_ref, kn_ref, koT_ref)
    yield from shift_in("v", vT_ref, vn_ref, voT_ref)
    (kT, kn), (vT, vn) = shifted["k"], shifted["v"]
    HT = H * T
    row_head = lax.broadcasted_iota(jnp.int32, (HT, A), 0) // T
    lane_head = lax.broadcasted_iota(jnp.int32, (HT, A), 1) // HEAD_DIM
    diag = row_head == lane_head
    q_rows = jnp.where(diag, jnp.concatenate([qn_ref[0]] * H, axis=0), 0.0).astype(BF16)
    s_cache = _dot(q_rows, kT)
    s_new = _dot_nt(q_rows, kn)
    yield
    soft = []
    mine = pl.ds(pl.multiple_of((pl.program_id(0) % groups) * HT, HT), HT)
    for lo, c_ref, n_ref in zip(lows, (c1_ref, c4_ref, c16_ref), (n1_ref, n4_ref, n16_ref)):
        sc = s_cache[:, lo:] + c_ref[mine, :]
        sn = s_new + n_ref[mine, :]
        m = jnp.maximum(jnp.max(sc, axis=-1, keepdims=True), jnp.max(sn, axis=-1, keepdims=True))
        pc = jnp.exp(sc - m)
        pn = jnp.exp(sn - m)
        l = jnp.sum(pc, axis=-1, keepdims=True) + jnp.sum(pn, axis=-1, keepdims=True)
        soft.append((m, l, pc.astype(BF16), pn.astype(BF16)))
        yield
    stats = []
    for lo, (m, l, pc, pn) in zip(lows, soft):
        stats.append((m, l, _dot_nt(pc, vT[:, lo:]) + _dot(pn, vn)))
        yield
    m_all = jnp.maximum(jnp.maximum(stats[0][0], stats[1][0]), stats[2][0])
    num = jnp.zeros((HT, A), F32)
    den = jnp.zeros((HT, 1), F32)
    for m, l, acc in stats:
        w = jnp.exp(m - m_all)
        num = num + w * acc
        den = den + w * l
    comb = jnp.where(diag, num / den, 0.0)
    out = comb[0:T]
    for h in range(1, H):
        out = out + comb[h * T:(h + 1) * T]
    ao_ref[0] = out
    yield


def _sample_window_operands(aq, ak, av, cache_kT, cache_vT, rel_bias, groups):
    B, T, A = aq.shape
    _, H, Dh, L = cache_kT.shape
    Hg, Ag = H // groups, A // groups
    lows = tuple(L - win for win, _ in DIL_PATTERNS)
    assert all(lo >= 0 and lo % LANES == 0 for lo in lows) and T <= LANES and Ag % LANES == 0
    tabs = _sample_bias_tables(rel_bias, L, T)
    big = pl.BlockSpec((1, Hg, Dh, L), lambda i: (i // groups, i % groups, 0, 0))
    small = pl.BlockSpec((1, T, Ag), lambda i: (i // groups, 0, i % groups))
    tab = lambda t: pl.BlockSpec(t.shape, lambda i: (0, 0), pipeline_mode=pl.Buffered(1))
    win_sds = jax.ShapeDtypeStruct((B, H, Dh, L), F32)
    args = [cache_kT, cache_vT, aq, ak, av, *tabs]
    in_specs = [big, big, small, small, small] + [tab(t) for t in tabs]
    out_specs = [big, big, small]
    out_shape = [win_sds, win_sds, jax.ShapeDtypeStruct((B, T, A), F32)]
    return functools.partial(_window_pieces, lows, T, groups), args, in_specs, out_specs, out_shape


def _ffn_pieces(d_ff, ff_chunk, x_ref, ro_ref, ao_ref, wo_ref, wgu_ref, wd_ref, g_pm, g_pf, g_of, y_ref, act_ref):
    R = RET_WIDTH
    mix = _dot(ro_ref[...].astype(BF16), wo_ref[:R, :]) + _dot(ao_ref[...].astype(BF16), wo_ref[R:, :])
    x1 = x_ref[...] + _rms(mix, g_pm[...])
    h = _rms(x1, g_pf[...]).astype(BF16)
    yield
    for c in range(0, d_ff, ff_chunk):
        gate = _dot(h, wgu_ref[:, c:c + ff_chunk])
        up = _dot(h, wgu_ref[:, d_ff + c:d_ff + c + ff_chunk])
        act_ref[:, c:c + ff_chunk] = (gate * jax.nn.sigmoid(gate) * up).astype(BF16)
        yield
    f = _dot(act_ref[...], wd_ref[...])
    y_ref[...] = x1 + _rms(f, g_of[...])
    yield


def _out_kernel(d_ff, ff_chunk, *refs):
    for _ in _ffn_pieces(d_ff, ff_chunk, *refs):
        pass


def _out_window_kernel(d_ff, ff_chunk, window, n_ffn_in, n_win_in, *refs):
    ffn_in = refs[:n_ffn_in]
    win_in = refs[n_ffn_in:n_ffn_in + n_win_in]
    y_ref, koT_ref, voT_ref, sao_ref, act_ref = refs[n_ffn_in + n_win_in:]
    win = window(*win_in, koT_ref, voT_ref, sao_ref)
    for _ in _ffn_pieces(d_ff, ff_chunk, *ffn_in, y_ref, act_ref):
        next(win, None)
        next(win, None)
    for _ in win:
        pass


def _out_block(x, ro, ao, wo_bf, wgu_bf, wd_bf, g_post_mix, g_pre_ffn, g_post_ffn, tm, window=None):
    N, D = x.shape
    d_ff = wd_bf.shape[0]
    row = lambda width: pl.BlockSpec((tm, width), lambda i: (i, 0))
    const = lambda shape: pl.BlockSpec(shape, lambda i: (0,) * len(shape), pipeline_mode=pl.Buffered(1))
    args = [x, ro, ao, wo_bf, wgu_bf, wd_bf, g_post_mix.reshape(1, D), g_pre_ffn.reshape(1, D),
            g_post_ffn.reshape(1, D)]
    in_specs = [row(D), row(RET_WIDTH), row(ATT_WIDTH), const(wo_bf.shape), const(wgu_bf.shape),
                const(wd_bf.shape), const((1, D)), const((1, D)), const((1, D))]
    body = functools.partial(_out_kernel, d_ff, FF_CHUNK)
    out_specs, out_shape = [row(D)], [jax.ShapeDtypeStruct((N, D), F32)]
    if window is not None:
        win_body, win_args, win_in_specs, win_out_specs, win_out_shape = window
        assert N // tm == win_out_shape[0].shape[0] * (win_out_shape[0].shape[1] // win_out_specs[0].block_shape[1])
        body = functools.partial(_out_window_kernel, d_ff, FF_CHUNK, win_body, len(args), len(win_args))
        args, in_specs = args + win_args, in_specs + win_in_specs
        out_specs, out_shape = out_specs + win_out_specs, out_shape + win_out_shape
    outs = pl.pallas_call(
        body,
        grid=(N // tm,),
        in_specs=in_specs,
        out_specs=out_specs,
        out_shape=out_shape,
        scratch_shapes=[pltpu.VMEM((tm, d_ff), BF16)],
        compiler_params=_cparams("out_ffn", "arbitrary"),
        name="out_ffn",
    )(*args)
    return outs[0] if window is None else outs


def kernel(x_prompt, x_sample, state_ret, cache_k_win, cache_v_win, rel_bias, w_in, g_ret, w_out,
           g_pre_mix, g_post_mix, g_pre_ffn, g_post_ffn, w_gu, w_down):
    depth = w_in.shape[0]
    assert depth == 1
    B, S, D = x_prompt.shape
    Bs, Ts, _ = x_sample.shape
    H, Dh = N_ATT_HEADS, HEAD_DIM
    l = 0
    w_in_bf = w_in[l].astype(BF16)

    pos_s = PAST_LEN + jnp.arange(Ts, dtype=jnp.int32)
    N_s = Bs * Ts
    outs = _proj(x_sample.reshape(N_s, D), g_pre_mix[l], w_in_bf, jnp.tile(pos_s, Bs))
    srq, srk, srv, srg, saq, sak, sav = [t.reshape(Bs, Ts, RET_WIDTH) for t in outs]
    sro, s_s = _retention_step(srq, srk, srv, srg, g_ret[l], state_ret[l])
    to_t = lambda t: jnp.transpose(t, (0, 2, 3, 1))
    from_t = lambda t: jnp.transpose(t, (0, 3, 1, 2))
    window = _sample_window_operands(saq, sak, sav, to_t(cache_k_win[l]), to_t(cache_v_win[l]), rel_bias,
                                     WINDOW_HEAD_GROUPS)

    ro, s_p, aq, ak, av, wo_bf, wgu_bf, wd_bf = _proj_retention(
        x_prompt, g_pre_mix[l], w_in_bf, g_ret[l], PROJ_TILE, (w_out[l], w_gu[l], w_down[l]))
    ao = _prompt_attention(aq, ak, av, rel_bias)
    y_p, k_sT, v_sT, sao = _out_block(x_prompt.reshape(B * S, D), ro.reshape(B * S, RET_WIDTH),
                                      ao.reshape(B * S, ATT_WIDTH), wo_bf, wgu_bf, wd_bf, g_post_mix[l],
                                      g_pre_ffn[l], g_post_ffn[l], FFN_TILE, window)
    k_s, v_s = from_t(k_sT), from_t(v_sT)

    y_s = _out_block(x_sample.reshape(N_s, D), sro.reshape(N_s, RET_WIDTH), sao.reshape(N_s, ATT_WIDTH),
                     wo_bf, wgu_bf, wd_bf, g_post_mix[l], g_pre_ffn[l], g_post_ffn[l], N_s)

    return (y_p.reshape(B, S, D), y_s.reshape(Bs, Ts, D),
            s_p[None], ak.reshape(1, B, S, H, Dh), av.reshape(1, B, S, H, Dh),
            s_s[None], k_s[None], v_s[None])
```

```python
import functools
import math

import jax
import jax.numpy as jnp
from jax import lax
from jax.experimental import pallas as pl
from jax.experimental.pallas import tpu as pltpu

F32 = jnp.float32
BF16 = jnp.bfloat16

HEAD_DIM = 64
N_RET_HEADS = 8
N_ATT_HEADS = 8
RET_WIDTH = N_RET_HEADS * HEAD_DIM
ATT_WIDTH = N_ATT_HEADS * HEAD_DIM
RET_CHUNK = 128
DIL_PATTERNS = ((128, 1), (512, 4), (2048, 16))
ATT_BLOCK = 128
N_BUCKETS = 32
MAX_DISTANCE = 2048
ROPE_BASE = 10000.0
NORM_EPS = 1e-6
GN_EPS = 1e-5
PAST_LEN = 16384
LANES = 128
ATT_CHUNKS = ATT_WIDTH // LANES
NEG_INF = float("-inf")

PROJ_TILE = 512
FFN_TILE = 512
FF_CHUNK = 256
WINDOW_HEAD_GROUPS = 2
VMEM_MB = dict(proj=40, retention=32, proj_retention=56, prompt_att=58, out_ffn=56)


def _cparams(name, *sem):
    return pltpu.CompilerParams(dimension_semantics=sem, vmem_limit_bytes=VMEM_MB[name] * 1024 * 1024)


def _rms(x, g):
    return x * lax.rsqrt(jnp.mean(x * x, axis=-1, keepdims=True) + NORM_EPS) * g


def _dot(a, b):
    return jnp.dot(a, b, preferred_element_type=F32)


def _dot_nt(a, b):
    return lax.dot_general(a, b, (((1,), (1,)), ((), ())), preferred_element_type=F32)


def _dot_tn(a, b):
    return lax.dot_general(a, b, (((0,), (0,)), ((), ())), preferred_element_type=F32)


def _rope_tables(pos):
    inv = ROPE_BASE ** (-jnp.arange(0, HEAD_DIM, 2, dtype=F32) / HEAD_DIM)
    ang = pos.astype(F32)[:, None] * inv[None, :]
    cos, sin = jnp.cos(ang), jnp.sin(ang)
    zero = jnp.zeros_like(sin)
    rep = LANES // HEAD_DIM
    cosf = jnp.tile(jnp.concatenate([cos, cos], axis=-1), (1, rep))
    sa = jnp.tile(jnp.concatenate([-sin, zero], axis=-1), (1, rep))
    sb = jnp.tile(jnp.concatenate([zero, sin], axis=-1), (1, rep))
    return cosf, sa, sb


def _retention_consts(C):
    log_g = jnp.log1p(-jnp.exp2(-5.0 - jnp.arange(N_RET_HEADS, dtype=F32)))
    i = jnp.arange(C, dtype=F32)
    rel = i[:, None] - i[None, :]
    decay = jnp.where(rel[None] >= 0, jnp.exp(jnp.maximum(rel, 0.0)[None] * log_g[:, None, None]), 0.0)
    cross = jnp.exp((i + 1.0)[:, None] * log_g[None, :])
    kdec = jnp.exp((C - 1.0 - i)[:, None] * log_g[None, :])
    sdec = jnp.exp(C * log_g)[None, :]
    expand = lambda t: jnp.repeat(t, HEAD_DIM, axis=-1)
    return decay, expand(cross), expand(kdec), expand(sdec)


def _t5_bucket(dist):
    max_exact = N_BUCKETS // 2
    d_f = jnp.maximum(dist, 1).astype(F32)
    large = max_exact + (jnp.log(d_f / max_exact) / math.log(MAX_DISTANCE / max_exact)
                         * (N_BUCKETS - max_exact)).astype(jnp.int32)
    large = jnp.minimum(large, N_BUCKETS - 1)
    return jnp.where(dist < max_exact, dist, large)


def _pattern_bias_rev(rel_bias, dil, nk):
    dist = jnp.arange(nk, -1, -1, dtype=jnp.int32) * dil
    return rel_bias[_t5_bucket(dist)].astype(F32).T


def _prompt_bias_vectors(rel_bias):
    BLK, H = ATT_BLOCK, N_ATT_HEADS
    out = []
    for win, dil in DIL_PATTERNS:
        nk = win // dil
        assert nk == BLK
        out.append(jnp.concatenate([_pattern_bias_rev(rel_bias, dil, nk),
                                    jnp.full((H, 3 * BLK - nk - 1), NEG_INF, F32)], axis=1))
    return jnp.stack(out)


def _sample_bias_tables(rel_bias, L, T):
    H = N_ATT_HEADS
    cache_part, new_part = [], []
    for win, dil in DIL_PATTERNS:
        nk = win // dil
        rev = _pattern_bias_rev(rel_bias, dil, nk)
        if dil > 1:
            gaps = jnp.full((H, nk + 1, dil - 1), NEG_INF, F32)
            rev = jnp.concatenate([rev[:, :, None], gaps], axis=2).reshape(H, (nk + 1) * dil)[:, :nk * dil + 1]
        pv = jnp.concatenate([jnp.full((H, T), NEG_INF, F32), rev, jnp.full((H, 2 * T), NEG_INF, F32)], axis=1)
        width = win + T
        rows = jnp.stack([pv[:, T - t:T - t + width] for t in range(T)], axis=1)
        cache_part.append(rows[:, :, :win].reshape(H * T, win))
        new = jnp.concatenate([jnp.full((H, T, LANES - T), NEG_INF, F32), rows[:, :, win:]], axis=2)
        new_part.append(new.reshape(H * T, LANES))
    return cache_part + new_part


def _proj_steps(x, g_ref, w_ref, cos, sa, sb):
    h = _rms(x, g_ref[...]).astype(BF16)
    W = 2 * LANES
    rep = W // LANES
    cosf = jnp.concatenate([cos] * rep, axis=-1)
    saf = jnp.concatenate([sa] * rep, axis=-1)
    sbf = jnp.concatenate([sb] * rep, axis=-1)
    half = HEAD_DIM // 2

    def col(c):
        return _dot(h, w_ref[:, c:c + W])

    def rope(z):
        return z * cosf + pltpu.roll(z, W - half, 1) * saf + pltpu.roll(z, half, 1) * sbf

    R, A = RET_WIDTH, ATT_WIDTH
    scale = HEAD_DIM ** -0.5
    steps = []
    for c in range(0, R, W):
        steps.append(lambda c=c: rope(col(c)))
    for c in range(R, 2 * R, W):
        steps.append(lambda c=c: rope(col(c)) * scale)
    for c in range(2 * R, 4 * R, W):
        steps.append(lambda c=c: col(c))
    for c in range(4 * R, 4 * R + A, W):
        steps.append(lambda c=c: col(c) * scale)
    for c in range(4 * R + A, 4 * R + 3 * A, W):
        steps.append(lambda c=c: col(c))
    return steps


def _join_groups(pieces):
    return [jnp.concatenate(pieces[i:i + 2], axis=-1) for i in range(0, len(pieces), 2)]


def _proj_kernel(x_ref, g_ref, w_ref, cos_ref, sa_ref, sb_ref, *outs):
    steps = _proj_steps(x_ref[...], g_ref, w_ref, cos_ref[...], sa_ref[...], sb_ref[...])
    for ref, z in zip(outs, _join_groups([step() for step in steps])):
        ref[...] = z


def _proj(x, g, w_bf, pos):
    N, D = x.shape
    tabs = _rope_tables(pos)
    full = lambda shape: pl.BlockSpec(shape, lambda i: (0,) * len(shape))
    return pl.pallas_call(
        _proj_kernel,
        grid=(1,),
        in_specs=[full((N, D)), full((1, D)), full(w_bf.shape)] + [full(t.shape) for t in tabs],
        out_specs=[full((N, RET_WIDTH))] * 7,
        out_shape=[jax.ShapeDtypeStruct((N, RET_WIDTH), F32)] * 7,
        compiler_params=_cparams("proj", "arbitrary"),
        name="proj",
    )(x, g.reshape(1, D), w_bf, *tabs)


def _proj_ret_kernel(tiles_per_seq, x_ref, g_ref, w_ref, cos_ref, sa_ref, sb_ref, dec_ref, cd_ref, kd_ref,
                     sd_ref, gr_ref, wa_ref, wb_ref, wc_ref, ro_ref, s_out_ref, aq_ref, ak_ref, av_ref,
                     wa_bf_ref, wb_bf_ref, wc_bf_ref, sbd_ref, qkv_ref, gate_ref):
    s = pl.program_id(0)
    n_tiles = pl.num_programs(0) - 1
    tm = x_ref.shape[1]
    C = RET_CHUNK
    cur = s % 2
    prv = 1 - cur

    @pl.when(s == 0)
    def _():
        sbd_ref[...] = jnp.zeros_like(sbd_ref)
        qkv_ref[1] = jnp.zeros(qkv_ref.shape[1:], qkv_ref.dtype)
        gate_ref[1] = jnp.zeros(gate_ref.shape[1:], gate_ref.dtype)

    for w32, w16 in ((wa_ref, wa_bf_ref), (wb_ref, wb_bf_ref), (wc_ref, wc_bf_ref)):
        w16[...] = w32[...].astype(w16.dtype)

    lane = lax.broadcasted_iota(jnp.int32, (C, LANES), 1)
    half = lane < HEAD_DIM
    same_head = (lax.broadcasted_iota(jnp.int32, (LANES, LANES), 0) // HEAD_DIM
                 == lax.broadcasted_iota(jnp.int32, (LANES, LANES), 1) // HEAD_DIM)
    inv_d = 1.0 / HEAD_DIM
    starts_seq = (s - 1) % tiles_per_seq == 0

    def half_mean(t):
        lo = jnp.sum(jnp.where(half, t, 0.0), axis=-1, keepdims=True)
        hi = jnp.sum(jnp.where(half, 0.0, t), axis=-1, keepdims=True)
        return jnp.where(half, lo, hi) * inv_d

    n_pairs = RET_WIDTH // LANES
    n_chunks = tm // C
    state = [jnp.where(starts_seq, 0.0, sbd_ref[c]) for c in range(n_pairs)]

    def retention_unit(c, cc):
        ls = slice(c * LANES, (c + 1) * LANES)
        rs = slice(cc * C, (cc + 1) * C)
        S = state[c]
        qb = qkv_ref[prv, 0, rs, ls]
        kb = qkv_ref[prv, 1, rs, ls]
        vb = qkv_ref[prv, 2, rs, ls]
        q = qb.astype(F32)
        q_pair = jnp.concatenate([jnp.where(half, q, 0.0), jnp.where(half, 0.0, q)], axis=0).astype(BF16)
        inner = _dot_nt(q_pair, kb) * dec_ref[c]
        cross = _dot(qb, S.astype(BF16)) * cd_ref[:, ls]
        kd = (kb.astype(F32) * kd_ref[:, ls]).astype(BF16)
        state[c] = S * sd_ref[:, ls] + jnp.where(same_head, _dot_tn(kd, vb), 0.0)
        yield
        o2 = _dot(inner.astype(BF16), vb)
        o = jnp.where(half, o2[:C], o2[C:]) + cross
        yield
        mu = half_mean(o)
        d = o - mu
        var = half_mean(d * d)
        on = d * lax.rsqrt(var + GN_EPS) * gr_ref[:, ls]
        gate = gate_ref[prv, rs, ls]
        ro_ref[0, rs, ls] = (on * (gate * jax.nn.sigmoid(gate))).astype(ro_ref.dtype)
        yield

    def retention_pieces():
        for cc in range(n_chunks):
            units = [retention_unit(c, cc) for c in range(n_pairs)]
            for _ in range(3):
                for u in units:
                    next(u)
                    yield

    j = jnp.minimum(s, n_tiles - 1) % tiles_per_seq
    rows = pl.ds(pl.multiple_of(j * tm, tm), tm)
    steps = _proj_steps(x_ref[0], g_ref, w_ref, cos_ref[rows, :], sa_ref[rows, :], sb_ref[rows, :])
    pieces = retention_pieces()
    n_pieces = 3 * n_chunks * n_pairs
    cols = []
    for i, step in enumerate(steps):
        cols.append(step())
        for _ in range((i + 1) * n_pieces // len(steps) - i * n_pieces // len(steps)):
            next(pieces)
    for c in range(n_pairs):
        S = state[c]
        sbd_ref[c] = S
        s_out_ref[0, 2 * c] = S[:HEAD_DIM, :HEAD_DIM]
        s_out_ref[0, 2 * c + 1] = S[HEAD_DIM:, HEAD_DIM:]
    rq, rk, rv, rg, aq, ak, av = _join_groups(cols)
    aq_ref[0] = aq.astype(aq_ref.dtype)
    ak_ref[0] = ak
    av_ref[0] = av
    qkv_ref[cur, 0] = rq.astype(BF16)
    qkv_ref[cur, 1] = rk.astype(BF16)
    qkv_ref[cur, 2] = rv.astype(BF16)
    gate_ref[cur] = rg


def _proj_retention(x, g, w_bf, g_ret, tm, later_weights):
    B, T, D = x.shape
    R, A, H, Dh = RET_WIDTH, ATT_WIDTH, N_RET_HEADS, HEAD_DIM
    tps = T // tm
    n_tiles = B * tps
    tabs = _rope_tables(jnp.arange(T, dtype=jnp.int32))
    dec, cd, kd, sd = _retention_consts(RET_CHUNK)
    dec_pair = dec.reshape(H // 2, 2 * RET_CHUNK, RET_CHUNK)

    def this_tile(width):
        def index(s):
            t = jnp.minimum(s, n_tiles - 1)
            return (t // tps, t % tps, 0)
        return pl.BlockSpec((1, tm, width), index)

    def prev_tile(s):
        t = jnp.maximum(s - 1, 0)
        return (t // tps, t % tps, 0)

    def slab(w):
        rows = max(16, -(-w.shape[0] // n_tiles) // 16 * 16)
        while w.shape[0] % rows:
            rows += 16
        last = w.shape[0] // rows - 1
        return pl.BlockSpec((rows, w.shape[1]), lambda s: (jnp.minimum(s, last), 0))

    full = lambda shape: pl.BlockSpec(shape, lambda s: (0,) * len(shape))
    consts = [g.reshape(1, D), w_bf, *tabs, dec_pair, cd, kd, sd, g_ret.reshape(1, R)]
    slabs = [slab(w) for w in later_weights]
    return pl.pallas_call(
        functools.partial(_proj_ret_kernel, tps),
        grid=(n_tiles + 1,),
        in_specs=[this_tile(D)] + [full(t.shape) for t in consts] + slabs,
        out_specs=[pl.BlockSpec((1, tm, R), prev_tile),
                   pl.BlockSpec((1, H, Dh, Dh), lambda s: (jnp.maximum(s - 1, 0) // tps, 0, 0, 0)),
                   this_tile(A), this_tile(A), this_tile(A)] + slabs,
        out_shape=[jax.ShapeDtypeStruct((B, T, R), BF16), jax.ShapeDtypeStruct((B, H, Dh, Dh), F32),
                   jax.ShapeDtypeStruct((B, T, A), BF16), jax.ShapeDtypeStruct((B, T, A), F32),
                   jax.ShapeDtypeStruct((B, T, A), F32)]
                  + [jax.ShapeDtypeStruct(w.shape, BF16) for w in later_weights],
        scratch_shapes=[pltpu.VMEM((R // LANES, LANES, LANES), F32), pltpu.VMEM((2, 3, tm, R), BF16),
                        pltpu.VMEM((2, tm, R), F32)],
        compiler_params=_cparams("proj_retention", "arbitrary"),
        name="proj_retention",
    )(x, *consts, *later_weights)


def _ret_kernel(q_ref, k_ref, v_ref, gate_ref, dec_ref, cd_ref, kd_ref, sd_ref, gr_ref, s0_ref, o_ref, s_ref):
    def unit(b, h):
        sl = slice(h * HEAD_DIM, (h + 1) * HEAD_DIM)
        qh = q_ref[b, :, sl].astype(BF16)
        kh = k_ref[b, :, sl].astype(BF16)
        vh = v_ref[b, :, sl].astype(BF16)
        S = s0_ref[b, h]
        inner = _dot_nt(qh, kh) * dec_ref[h]
        cross = _dot(qh, S.astype(BF16)) * cd_ref[:, sl]
        kd = (kh.astype(F32) * kd_ref[:, sl]).astype(BF16)
        s_ref[b, h] = S * sd_ref[:, sl] + _dot_tn(kd, vh)
        yield
        o = _dot(inner.astype(BF16), vh) + cross
        yield
        mu = jnp.mean(o, axis=-1, keepdims=True)
        var = jnp.mean(jnp.square(o - mu), axis=-1, keepdims=True)
        on = (o - mu) * lax.rsqrt(var + GN_EPS) * gr_ref[:, sl]
        gate = gate_ref[b, :, sl]
        o_ref[b, :, sl] = on * (gate * jax.nn.sigmoid(gate))
        yield

    for b in range(q_ref.shape[0]):
        units = [unit(b, h) for h in range(N_RET_HEADS)]
        for _ in range(3):
            for u in units:
                next(u)


def _retention_step(rq, rk, rv, rg, g_ret, state):
    B, T, R = rq.shape
    H, Dh = N_RET_HEADS, HEAD_DIM
    consts = [*_retention_consts(T), g_ret.reshape(1, R)]
    bb = math.gcd(B, 8)
    row = pl.BlockSpec((bb, T, R), lambda b: (b, 0, 0))
    full = lambda shape: pl.BlockSpec(shape, lambda b: (0,) * len(shape))
    st = pl.BlockSpec((bb, H, Dh, Dh), lambda b: (b, 0, 0, 0))
    return pl.pallas_call(
        _ret_kernel,
        grid=(B // bb,),
        in_specs=[row, row, row, row] + [full(t.shape) for t in consts] + [st],
        out_specs=[row, st],
        out_shape=[jax.ShapeDtypeStruct((B, T, R), F32), jax.ShapeDtypeStruct((B, H, Dh, Dh), F32)],
        compiler_params=_cparams("retention", "arbitrary"),
        name="retention",
    )(rq, rk, rv, rg, *consts, state)


def _split3_dot(c, e):
    c1 = c.astype(BF16)
    r1 = c - c1.astype(F32)
    c2 = r1.astype(BF16)
    c3 = (r1 - c2.astype(F32)).astype(BF16)
    return _dot(c1, e) + _dot(c2, e) + _dot(c3, e)


def _chunk_rows(ref, stride):
    def load(start, c):
        idx = pl.ds(start, ATT_BLOCK) if stride == 1 else pl.ds(start, ATT_BLOCK, stride=stride)
        return ref[c, idx, :]
    return load


def _token_rows(ref):
    def load(start, c):
        return ref[0, pl.ds(start, ATT_BLOCK), c * LANES:(c + 1) * LANES].astype(F32)
    return load


def _att_pieces(load_q, load_k, load_v, bias_ref, cur, prev, first, out):
    BLK, H = ATT_BLOCK, N_ATT_HEADS
    half = lax.broadcasted_iota(jnp.int32, (BLK, LANES), 1) < HEAD_DIM
    scores = []
    for c in range(ATT_CHUNKS):
        q = load_q(cur, c)
        q_pair = jnp.concatenate([jnp.where(half, q, 0.0), jnp.where(half, 0.0, q)], axis=0).astype(BF16)
        k = load_k(cur, c)
        if prev is not None:
            k = jnp.concatenate([load_k(prev, c), k], axis=0)
        scores.append(_dot_nt(q_pair, k.astype(BF16)))
        yield
    stats = []
    for c in range(ATT_CHUNKS):
        s = scores[c] + bias_ref[2 * c * BLK:(2 * c + 2) * BLK, :]
        if prev is not None:
            col = lax.broadcasted_iota(jnp.int32, s.shape, 1)
            s = jnp.where(jnp.logical_and(first, col < BLK), NEG_INF, s)
        m = jnp.max(s, axis=-1, keepdims=True)
        p = jnp.exp(s - m)
        stats.append((m, jnp.sum(p, axis=-1, keepdims=True), p.astype(BF16)))
        yield
    accs = []
    for c in range(ATT_CHUNKS):
        v = load_v(cur, c)
        if prev is not None:
            v = jnp.concatenate([load_v(prev, c), v], axis=0)
        o = _dot(stats[c][2], v.astype(BF16))
        accs.append(jnp.where(half, o[:BLK], o[BLK:]))
        yield
    lane = lax.broadcasted_iota(jnp.int32, (BLK, LANES), 1)
    ml = jnp.zeros((BLK, LANES), F32)
    for h in range(H):
        m, l, _ = stats[h // 2]
        rows = slice((h % 2) * BLK, (h % 2 + 1) * BLK)
        ml = jnp.where(lane == h, m[rows], ml)
        ml = jnp.where(lane == H + h, l[rows], ml)
    out.append((accs, ml))
    yield


def _att_pipeline(blocks, after=None):
    outs = [[] for _ in blocks]
    gens = [_att_pieces(*b, o) for b, o in zip(blocks, outs)]
    n = ATT_CHUNKS
    total = 3 * n + 1
    tails = []
    for t in range((len(blocks) - 1) * n + total):
        for tail in tails:
            next(tail, None)
        for k, g in enumerate(gens):
            if 0 <= t - k * n < total:
                next(g)
                if t - k * n == total - 1 and after is not None:
                    tails.append(after(k, *outs[k][0]))
    for tail in tails:
        for _ in tail:
            pass
    return [o[0] for o in outs]


def _att_kernel(q_ref, k_ref, v_ref, bv_ref, e_ref, ao_ref, b1_ref, b4_ref, b16_ref, q4, k4, v4, stage,
                acc4, ml4, acc16, ml16):
    BLK, H = ATT_BLOCK, N_ATT_HEADS
    S = q_ref.shape[1]
    nblk = S // BLK
    d4, d16 = DIL_PATTERNS[1][1], DIL_PATTERNS[2][1]
    nb4 = nblk // d4
    per_res = S // d4

    @pl.when(pl.program_id(0) == 0)
    def _():
        for g, tab in enumerate((b1_ref, b4_ref, b16_ref)):
            for h in range(H):
                rows = jnp.broadcast_to(bv_ref[g, h:h + 1, :], (BLK, bv_ref.shape[2]))
                skew = pltpu.roll(rows, 0, 1, stride=1, stride_axis=0)
                tab[h * BLK:(h + 1) * BLK, :] = skew[:, 2 * BLK - tab.shape[1]:2 * BLK]

    for src, dst in ((q_ref, q4), (k_ref, k4), (v_ref, v4)):
        for c in range(ATT_CHUNKS):
            stage[...] = src[0, :, c * LANES:(c + 1) * LANES].astype(F32)
            for r in range(d4):
                dst[c, r * per_res:(r + 1) * per_res, :] = stage[pl.ds(r, per_res, stride=d4), :]
    contiguous = [_chunk_rows(t, 1) for t in (q4, k4, v4)]
    every_4th = [_chunk_rows(t, d16 // d4) for t in (q4, k4, v4)]
    tokens = [_token_rows(t) for t in (q_ref, k_ref, v_ref)]

    def dilated(i, carry):
        blocks, toks = [], []
        for n in (2 * i, 2 * i + 1):
            r, j = n // nb4, n % nb4
            cur = r * per_res + j * BLK
            prev = r * per_res + jnp.maximum(j - 1, 0) * BLK
            blocks.append((*contiguous, b4_ref, cur, prev, j == 0))
            toks.append((r + j * (BLK * d4), d4, acc4, ml4))
            blocks.append((*every_4th, b16_ref, (n % d4) * per_res + n // d4, None, None))
            toks.append((n, d16, acc16, ml16))
        for (accs, ml), (tok, dil, acc_ref, ml_ref) in zip(_att_pipeline(blocks), toks):
            for c in range(ATT_CHUNKS):
                acc_ref[c, pl.ds(tok, BLK, stride=dil), :] = accs[c]
            ml_ref[pl.ds(tok, BLK, stride=dil), :] = ml
        return carry

    def merge(cur, accs1, ml1):
        rows = pl.ds(cur, BLK)
        mls = [ml1, ml4[rows, :], ml16[rows, :]]
        accs = [accs1, [acc4[c, rows, :] for c in range(ATT_CHUNKS)], [acc16[c, rows, :] for c in range(ATT_CHUNKS)]]
        lane = lax.broadcasted_iota(jnp.int32, (BLK, LANES), 1)
        m_all = jnp.maximum(jnp.maximum(mls[0], mls[1]), mls[2])
        ws = [jnp.exp(ml - m_all) for ml in mls]
        den = sum(w * pltpu.roll(ml, LANES - H, 1) for w, ml in zip(ws, mls))
        ao = [jnp.zeros((BLK, LANES), F32) for _ in range(ATT_CHUNKS)]
        yield
        for w, acc in zip(ws, accs):
            coef = _split3_dot(jnp.where(lane < H, w / den, 0.0), e_ref[...])
            for c in range(ATT_CHUNKS):
                ao[c] = ao[c] + coef[:, c * LANES:(c + 1) * LANES] * acc[c]
            yield
        ao_ref[0, rows, :] = jnp.concatenate(ao, axis=-1).astype(ao_ref.dtype)
        yield

    def dense_and_merge(i, carry):
        per_trip = 4
        starts = [pl.multiple_of((per_trip * i + k) * BLK, BLK) for k in range(per_trip)]
        before = pl.multiple_of(jnp.maximum(per_trip * i - 1, 0) * BLK, BLK)
        blocks = [(*tokens, b1_ref, cur, prev, first)
                  for cur, prev, first in zip(starts, [before] + starts[:-1], [i == 0] + [False] * (per_trip - 1))]
        _att_pipeline(blocks, lambda k, accs, ml: merge(starts[k], accs, ml))
        return carry

    lax.fori_loop(0, nblk // 2, dilated, 0)
    lax.fori_loop(0, nblk // 4, dense_and_merge, 0)


def _prompt_attention(aq, ak, av, rel_bias):
    B, S, A = aq.shape
    CH = ATT_CHUNKS
    assert [d for _, d in DIL_PATTERNS] == [1, 4, 16] and S // ATT_BLOCK == 16
    bias_vectors = _prompt_bias_vectors(rel_bias)
    head_of_lane = jnp.arange(A)[None, :] // HEAD_DIM
    expand = (jnp.arange(LANES)[:, None] == head_of_lane).astype(BF16)
    seq = pl.BlockSpec((1, S, A), lambda b: (b, 0, 0))
    const = lambda shape: pl.BlockSpec(shape, lambda b: (0,) * len(shape), pipeline_mode=pl.Buffered(1))
    chunked = pltpu.VMEM((CH, S, LANES), F32)
    stats = pltpu.VMEM((S, LANES), F32)
    table = lambda width: pltpu.VMEM((N_ATT_HEADS * ATT_BLOCK, width), F32)
    return pl.pallas_call(
        _att_kernel,
        grid=(B,),
        in_specs=[seq, seq, seq, const(bias_vectors.shape), const(expand.shape)],
        out_specs=seq,
        out_shape=jax.ShapeDtypeStruct((B, S, A), BF16),
        scratch_shapes=[table(2 * ATT_BLOCK), table(2 * ATT_BLOCK), table(ATT_BLOCK),
                        chunked, chunked, chunked, stats, chunked, stats, chunked, stats],
        compiler_params=_cparams("prompt_att", "arbitrary"),
        name="prompt_att",
    )(aq, ak, av, bias_vectors, expand)


def _window_pieces(lows, T, kT_ref, vT_ref, qn_ref, kn_ref, vn_ref, c1_ref, c4_ref, c16_ref,
                   n1_ref, n4_ref, n16_ref, koT_ref, voT_ref, ao_ref):
    H, Dh, L = kT_ref.shape[1:]
    A = H * Dh
    zpad = jnp.zeros((LANES - T, A), F32)
    tail_lane = lax.broadcasted_iota(jnp.int32, (Dh, LANES), 1) >= LANES - T
    shifted = {}

    def shift_in(name, xT_ref, new_ref, out_ref):
        new_p = jnp.concatenate([zpad, new_ref[0]], axis=0)
        new_t = new_p.T
        heads = []
        for h in range(H):
            x = xT_ref[0, h]
            rolled = pltpu.roll(x, L - T, 1)
            tail = jnp.where(tail_lane, new_t[h * Dh:(h + 1) * Dh], rolled[:, L - LANES:])
            out_ref[0, h] = jnp.concatenate([rolled[:, :L - LANES], tail], axis=1)
            heads.append(x.astype(BF16))
            yield
        shifted[name] = (jnp.concatenate(heads, axis=0), new_p.astype(BF16))

    yield from shift_in("k", kT_ref, kn_ref, koT_ref)
    yield from shift_in("v", vT_ref, vn_ref, voT_ref)
    (kT, kn), (vT, vn) = shifted["k"], shifted["v"]
    HT = H * T
    row_head = lax.broadcasted_iota(jnp.int32, (HT, A), 0) // T
    lane_head = lax.broadcasted_iota(jnp.int32, (HT, A), 1) // HEAD_DIM
    diag = row_head == lane_head
    q_rows = jnp.where(diag, jnp.concatenate([qn_ref[0]] * H, axis=0), 0.0).astype(BF16)
    s_cache = _dot(q_rows, kT)
    s_new = _dot_nt(q_rows, kn)
    yield
    soft = []
    for lo, c_ref, n_ref in zip(lows, (c1_ref, c4_ref, c16_ref), (n1_ref, n4_ref, n16_ref)):
        sc = s_cache[:, lo:] + c_ref[...]
        sn = s_new + n_ref[...]
        m = jnp.maximum(jnp.max(sc, axis=-1, keepdims=True), jnp.max(sn, axis=-1, keepdims=True))
        pc = jnp.exp(sc - m)
        pn = jnp.exp(sn - m)
        l = jnp.sum(pc, axis=-1, keepdims=True) + jnp.sum(pn, axis=-1, keepdims=True)
        soft.append((m, l, pc.astype(BF16), pn.astype(BF16)))
        yield
    stats = []
    for lo, (m, l, pc, pn) in zip(lows, soft):
        stats.append((m, l, _dot_nt(pc, vT[:, lo:]) + _dot(pn, vn)))
        yield
    m_all = jnp.maximum(jnp.maximum(stats[0][0], stats[1][0]), stats[2][0])
    num = jnp.zeros((HT, A), F32)
    den = jnp.zeros((HT, 1), F32)
    for m, l, acc in stats:
        w = jnp.exp(m - m_all)
        num = num + w * acc
        den = den + w * l
    comb = jnp.where(diag, num / den, 0.0)
    out = comb[0:T]
    for h in range(1, H):
        out = out + comb[h * T:(h + 1) * T]
    ao_ref[0] = out
    yield


def _sample_window_operands(aq, ak, av, cache_kT, cache_vT, rel_bias, groups):
    B, T, A = aq.shape
    _, H, Dh, L = cache_kT.shape
    Hg, Ag = H // groups, A // groups
    lows = tuple(L - win for win, _ in DIL_PATTERNS)
    assert all(lo >= 0 and lo % LANES == 0 for lo in lows) and T <= LANES and Ag % LANES == 0
    tabs = _sample_bias_tables(rel_bias, L, T)
    big = pl.BlockSpec((1, Hg, Dh, L), lambda i, j: (i, j, 0, 0))
    small = pl.BlockSpec((1, T, Ag), lambda i, j: (i, 0, j))
    tab = lambda t: pl.BlockSpec((Hg * T, t.shape[1]), lambda i, j: (j, 0))
    win_sds = jax.ShapeDtypeStruct((B, H, Dh, L), F32)
    args = [cache_kT, cache_vT, aq, ak, av, *tabs]
    in_specs = [big, big, small, small, small] + [tab(t) for t in tabs]
    out_specs = [big, big, small]
    out_shape = [win_sds, win_sds, jax.ShapeDtypeStruct((B, T, A), F32)]
    return functools.partial(_window_pieces, lows, T), args, in_specs, out_specs, out_shape


def _mix_pieces(x_ref, ro_ref, ao_ref, wo_ref, g_pm, g_pf, out):
    R = RET_WIDTH
    mix = _dot(ro_ref[...].astype(BF16), wo_ref[:R, :]) + _dot(ao_ref[...].astype(BF16), wo_ref[R:, :])
    yield
    x1 = x_ref[...] + _rms(mix, g_pm[...])
    out["x1"] = x1
    out["h"] = _rms(x1, g_pf[...]).astype(BF16)
    yield


def _ffn_chunk_pieces(d_ff, chunks, get_h, wgu_ref, act_ref):
    for c, w in chunks:
        h = get_h()
        gate = _dot(h, wgu_ref[:, c:c + w])
        up = _dot(h, wgu_ref[:, d_ff + c:d_ff + c + w])
        act_ref[:, c:c + w] = (gate * jax.nn.sigmoid(gate) * up).astype(BF16)
        yield


def _ffn_out_pieces(get_x1, act_ref, wd_ref, g_of, y_ref):
    f = _dot(act_ref[...], wd_ref[...])
    y_ref[...] = get_x1() + _rms(f, g_of[...])
    yield


def _ff_chunks(d_ff):
    return [(c, min(FF_CHUNK, d_ff - c)) for c in range(0, d_ff, FF_CHUNK)]


def _out_kernel(x_ref, ro_ref, ao_ref, wo_ref, wgu_ref, wd_ref, g_pm, g_pf, g_of, y_ref, act_ref):
    d_ff = wd_ref.shape[0]
    mixed = {}
    for _ in _mix_pieces(x_ref, ro_ref, ao_ref, wo_ref, g_pm, g_pf, mixed):
        pass
    for _ in _ffn_chunk_pieces(d_ff, _ff_chunks(d_ff), lambda: mixed["h"], wgu_ref, act_ref):
        pass
    for _ in _ffn_out_pieces(lambda: mixed["x1"], act_ref, wd_ref, g_of, y_ref):
        pass


def _alternate(main, side, side_per_main):
    for _ in main:
        for _ in range(side_per_main):
            next(side, None)
    for _ in side:
        pass


def _out_window_kernel(window, n_win_in, x_ref, ro_ref, ao_ref, wo_ref, wgu_ref, wd_ref, g_pm, g_pf, g_of, *refs):
    win_in = refs[:n_win_in]
    y_ref, koT_ref, voT_ref, sao_ref, act_ref, h_ref, x1_ref = refs[n_win_in:]
    d_ff = wd_ref.shape[0]
    chunks = _ff_chunks(d_ff)
    split = (len(chunks) + 1) // 2
    j = pl.program_id(1)

    @pl.when(j == 0)
    def _():
        def first_half():
            mixed = {}
            yield from _mix_pieces(x_ref, ro_ref, ao_ref, wo_ref, g_pm, g_pf, mixed)
            h_ref[...] = mixed["h"]
            x1_ref[...] = mixed["x1"]
            yield from _ffn_chunk_pieces(d_ff, chunks[:split], lambda: mixed["h"], wgu_ref, act_ref)
        _alternate(first_half(), window(*win_in, koT_ref, voT_ref, sao_ref), 2)

    @pl.when(j == 1)
    def _():
        def second_half():
            yield from _ffn_chunk_pieces(d_ff, chunks[split:], lambda: h_ref[...], wgu_ref, act_ref)
            yield from _ffn_out_pieces(lambda: x1_ref[...], act_ref, wd_ref, g_of, y_ref)
        _alternate(second_half(), window(*win_in, koT_ref, voT_ref, sao_ref), 3)


def _out_block(x, ro, ao, wo_bf, wgu_bf, wd_bf, g_post_mix, g_pre_ffn, g_post_ffn, tm, window=None):
    N, D = x.shape
    d_ff = wd_bf.shape[0]
    args = [x, ro, ao, wo_bf, wgu_bf, wd_bf, g_post_mix.reshape(1, D), g_pre_ffn.reshape(1, D),
            g_post_ffn.reshape(1, D)]
    weight_shapes = [wo_bf.shape, wgu_bf.shape, wd_bf.shape, (1, D), (1, D), (1, D)]
    act = pltpu.VMEM((tm, d_ff), BF16)
    if window is None:
        row = lambda width: pl.BlockSpec((tm, width), lambda i: (i, 0))
        const = lambda shape: pl.BlockSpec(shape, lambda i: (0,) * len(shape), pipeline_mode=pl.Buffered(1))
        return pl.pallas_call(
            _out_kernel,
            grid=(N // tm,),
            in_specs=[row(D), row(RET_WIDTH), row(ATT_WIDTH)] + [const(sh) for sh in weight_shapes],
            out_specs=row(D),
            out_shape=jax.ShapeDtypeStruct((N, D), F32),
            scratch_shapes=[act],
            compiler_params=_cparams("out_ffn", "arbitrary"),
            name="out_ffn",
        )(*args)
    win_body, win_args, win_in_specs, win_out_specs, win_out_shape = window
    n_seq, heads = win_out_shape[0].shape[:2]
    groups = heads // win_out_specs[0].block_shape[1]
    assert N // tm == n_seq and groups == 2
    row = lambda width: pl.BlockSpec((tm, width), lambda i, j: (i, 0))
    const = lambda shape: pl.BlockSpec(shape, lambda i, j: (0,) * len(shape), pipeline_mode=pl.Buffered(1))
    return pl.pallas_call(
        functools.partial(_out_window_kernel, win_body, len(win_args)),
        grid=(n_seq, groups),
        in_specs=[row(D), row(RET_WIDTH), row(ATT_WIDTH)] + [const(sh) for sh in weight_shapes] + win_in_specs,
        out_specs=[row(D)] + win_out_specs,
        out_shape=[jax.ShapeDtypeStruct((N, D), F32)] + win_out_shape,
        scratch_shapes=[act, pltpu.VMEM((tm, D), BF16), pltpu.VMEM((tm, D), F32)],
        compiler_params=_cparams("out_ffn", "arbitrary", "arbitrary"),
        name="out_ffn",
    )(*args, *win_args)


def kernel(x_prompt, x_sample, state_ret, cache_k_win, cache_v_win, rel_bias, w_in, g_ret, w_out,
           g_pre_mix, g_post_mix, g_pre_ffn, g_post_ffn, w_gu, w_down):
    depth = w_in.shape[0]
    assert depth == 1
    B, S, D = x_prompt.shape
    Bs, Ts, _ = x_sample.shape
    H, Dh = N_ATT_HEADS, HEAD_DIM
    l = 0
    w_in_bf = w_in[l].astype(BF16)

    pos_s = PAST_LEN + jnp.arange(Ts, dtype=jnp.int32)
    N_s = Bs * Ts
    outs = _proj(x_sample.reshape(N_s, D), g_pre_mix[l], w_in_bf, jnp.tile(pos_s, Bs))
    srq, srk, srv, srg, saq, sak, sav = [t.reshape(Bs, Ts, RET_WIDTH) for t in outs]
    sro, s_s = _retention_step(srq, srk, srv, srg, g_ret[l], state_ret[l])
    to_t = lambda t: jnp.transpose(t, (0, 2, 3, 1))
    from_t = lambda t: jnp.transpose(t, (0, 3, 1, 2))
    window = _sample_window_operands(saq, sak, sav, to_t(cache_k_win[l]), to_t(cache_v_win[l]), rel_bias,
                                     WINDOW_HEAD_GROUPS)

    ro, s_p, aq, ak, av, wo_bf, wgu_bf, wd_bf = _proj_retention(
        x_prompt, g_pre_mix[l], w_in_bf, g_ret[l], PROJ_TILE, (w_out[l], w_gu[l], w_down[l]))
    ao = _prompt_attention(aq, ak, av, rel_bias)
    y_p, k_sT, v_sT, sao = _out_block(x_prompt.reshape(B * S, D), ro.reshape(B * S, RET_WIDTH),
                                      ao.reshape(B * S, ATT_WIDTH), wo_bf, wgu_bf, wd_bf, g_post_mix[l],
                                      g_pre_ffn[l], g_post_ffn[l], FFN_TILE, window)
    k_s, v_s = from_t(k_sT), from_t(v_sT)

    y_s = _out_block(x_sample.reshape(N_s, D), sro.reshape(N_s, RET_WIDTH), sao.reshape(N_s, ATT_WIDTH),
                     wo_bf, wgu_bf, wd_bf, g_post_mix[l], g_pre_ffn[l], g_post_ffn[l], N_s)

    return (y_p.reshape(B, S, D), y_s.reshape(Bs, Ts, D),
            s_p[None], ak.reshape(1, B, S, H, Dh), av.reshape(1, B, S, H, Dh),
            s_s[None], k_s[None], v_s[None])
```

```python
import functools
import math

import jax
import jax.numpy as jnp
from jax import lax
from jax.experimental import pallas as pl
from jax.experimental.pallas import tpu as pltpu

F32 = jnp.float32
BF16 = jnp.bfloat16

HEAD_DIM = 64
N_RET_HEADS = 8
N_ATT_HEADS = 8
RET_WIDTH = N_RET_HEADS * HEAD_DIM
ATT_WIDTH = N_ATT_HEADS * HEAD_DIM
RET_CHUNK = 128
DIL_PATTERNS = ((128, 1), (512, 4), (2048, 16))
ATT_BLOCK = 128
N_BUCKETS = 32
MAX_DISTANCE = 2048
ROPE_BASE = 10000.0
NORM_EPS = 1e-6
GN_EPS = 1e-5
PAST_LEN = 16384
LANES = 128
ATT_CHUNKS = ATT_WIDTH // LANES
NEG_INF = float("-inf")

PROJ_TILE = 512
FFN_TILE = 512
FF_CHUNK = 256
WINDOW_HEAD_GROUPS = 2
VMEM_MB = dict(proj=40, retention=32, proj_retention=58, prompt_att=58, out_ffn=58)


def _cparams(name, *sem):
    return pltpu.CompilerParams(dimension_semantics=sem, vmem_limit_bytes=VMEM_MB[name] * 1024 * 1024)


def _rms(x, g):
    return x * lax.rsqrt(jnp.mean(x * x, axis=-1, keepdims=True) + NORM_EPS) * g


def _dot(a, b):
    return jnp.dot(a, b, preferred_element_type=F32)


def _dot_nt(a, b):
    return lax.dot_general(a, b, (((1,), (1,)), ((), ())), preferred_element_type=F32)


def _dot_tn(a, b):
    return lax.dot_general(a, b, (((0,), (0,)), ((), ())), preferred_element_type=F32)


def _rope_tables(pos):
    inv = ROPE_BASE ** (-jnp.arange(0, HEAD_DIM, 2, dtype=F32) / HEAD_DIM)
    ang = pos.astype(F32)[:, None] * inv[None, :]
    cos, sin = jnp.cos(ang), jnp.sin(ang)
    zero = jnp.zeros_like(sin)
    rep = LANES // HEAD_DIM
    cosf = jnp.tile(jnp.concatenate([cos, cos], axis=-1), (1, rep))
    sa = jnp.tile(jnp.concatenate([-sin, zero], axis=-1), (1, rep))
    sb = jnp.tile(jnp.concatenate([zero, sin], axis=-1), (1, rep))
    return cosf, sa, sb


def _retention_consts(C):
    log_g = jnp.log1p(-jnp.exp2(-5.0 - jnp.arange(N_RET_HEADS, dtype=F32)))
    i = jnp.arange(C, dtype=F32)
    rel = i[:, None] - i[None, :]
    decay = jnp.where(rel[None] >= 0, jnp.exp(jnp.maximum(rel, 0.0)[None] * log_g[:, None, None]), 0.0)
    cross = jnp.exp((i + 1.0)[:, None] * log_g[None, :])
    kdec = jnp.exp((C - 1.0 - i)[:, None] * log_g[None, :])
    sdec = jnp.exp(C * log_g)[None, :]
    expand = lambda t: jnp.repeat(t, HEAD_DIM, axis=-1)
    return decay, expand(cross), expand(kdec), expand(sdec)


def _t5_bucket(dist):
    max_exact = N_BUCKETS // 2
    d_f = jnp.maximum(dist, 1).astype(F32)
    large = max_exact + (jnp.log(d_f / max_exact) / math.log(MAX_DISTANCE / max_exact)
                         * (N_BUCKETS - max_exact)).astype(jnp.int32)
    large = jnp.minimum(large, N_BUCKETS - 1)
    return jnp.where(dist < max_exact, dist, large)


def _pattern_bias_rev(rel_bias, dil, nk):
    dist = jnp.arange(nk, -1, -1, dtype=jnp.int32) * dil
    return rel_bias[_t5_bucket(dist)].astype(F32).T


def _prompt_bias_vectors(rel_bias):
    BLK, H = ATT_BLOCK, N_ATT_HEADS
    out = []
    for win, dil in DIL_PATTERNS:
        nk = win // dil
        assert nk == BLK
        out.append(jnp.concatenate([_pattern_bias_rev(rel_bias, dil, nk),
                                    jnp.full((H, 3 * BLK - nk - 1), NEG_INF, F32)], axis=1))
    return jnp.stack(out)


def _sample_bias_tables(rel_bias, L, T):
    H = N_ATT_HEADS
    cache_part, new_part = [], []
    for win, dil in DIL_PATTERNS:
        nk = win // dil
        rev = _pattern_bias_rev(rel_bias, dil, nk)
        if dil > 1:
            gaps = jnp.full((H, nk + 1, dil - 1), NEG_INF, F32)
            rev = jnp.concatenate([rev[:, :, None], gaps], axis=2).reshape(H, (nk + 1) * dil)[:, :nk * dil + 1]
        pv = jnp.concatenate([jnp.full((H, T), NEG_INF, F32), rev, jnp.full((H, 2 * T), NEG_INF, F32)], axis=1)
        width = win + T
        rows = jnp.stack([pv[:, T - t:T - t + width] for t in range(T)], axis=1)
        cache_part.append(rows[:, :, :win].reshape(H * T, win))
        new = jnp.concatenate([jnp.full((H, T, LANES - T), NEG_INF, F32), rows[:, :, win:]], axis=2)
        new_part.append(new.reshape(H * T, LANES))
    return cache_part + new_part


def _proj_steps(x, g_ref, w_ref, cos, sa, sb):
    h = _rms(x, g_ref[...]).astype(BF16)
    W = 2 * LANES
    rep = W // LANES
    cosf = jnp.concatenate([cos] * rep, axis=-1)
    saf = jnp.concatenate([sa] * rep, axis=-1)
    sbf = jnp.concatenate([sb] * rep, axis=-1)
    half = HEAD_DIM // 2

    def col(c):
        return _dot(h, w_ref[:, c:c + W])

    def rope(z):
        return z * cosf + pltpu.roll(z, W - half, 1) * saf + pltpu.roll(z, half, 1) * sbf

    R, A = RET_WIDTH, ATT_WIDTH
    scale = HEAD_DIM ** -0.5
    steps = []
    for c in range(0, R, W):
        steps.append(lambda c=c: rope(col(c)))
    for c in range(R, 2 * R, W):
        steps.append(lambda c=c: rope(col(c)) * scale)
    for c in range(2 * R, 4 * R, W):
        steps.append(lambda c=c: col(c))
    for c in range(4 * R, 4 * R + A, W):
        steps.append(lambda c=c: col(c) * scale)
    for c in range(4 * R + A, 4 * R + 3 * A, W):
        steps.append(lambda c=c: col(c))
    return steps


def _join_groups(pieces):
    return [jnp.concatenate(pieces[i:i + 2], axis=-1) for i in range(0, len(pieces), 2)]


def _proj_kernel(x_ref, g_ref, w_ref, cos_ref, sa_ref, sb_ref, *outs):
    steps = _proj_steps(x_ref[...], g_ref, w_ref, cos_ref[...], sa_ref[...], sb_ref[...])
    for ref, z in zip(outs, _join_groups([step() for step in steps])):
        ref[...] = z


def _proj(x, g, w_bf, pos):
    N, D = x.shape
    tabs = _rope_tables(pos)
    full = lambda shape: pl.BlockSpec(shape, lambda i: (0,) * len(shape))
    return pl.pallas_call(
        _proj_kernel,
        grid=(1,),
        in_specs=[full((N, D)), full((1, D)), full(w_bf.shape)] + [full(t.shape) for t in tabs],
        out_specs=[full((N, RET_WIDTH))] * 7,
        out_shape=[jax.ShapeDtypeStruct((N, RET_WIDTH), F32)] * 7,
        compiler_params=_cparams("proj", "arbitrary"),
        name="proj",
    )(x, g.reshape(1, D), w_bf, *tabs)


def _proj_ret_kernel(tiles_per_seq, window, n_win_in, x_ref, g_ref, w_ref, cos_ref, sa_ref, sb_ref, dec_ref,
                     cd_ref, kd_ref, sd_ref, gr_ref, wa_ref, wb_ref, wc_ref, *refs):
    win_in = refs[:n_win_in]
    (ro_ref, s_out_ref, aq_ref, ak_ref, av_ref, wa_bf_ref, wb_bf_ref, wc_bf_ref, koT_ref, voT_ref, sao_ref,
     sbd_ref, qkv_ref, gate_ref) = refs[n_win_in:]
    s = pl.program_id(0)
    n_tiles = pl.num_programs(0) - 1
    tm = x_ref.shape[1]
    C = RET_CHUNK
    cur = s % 2
    prv = 1 - cur

    @pl.when(s == 0)
    def _():
        sbd_ref[...] = jnp.zeros_like(sbd_ref)
        qkv_ref[1] = jnp.zeros(qkv_ref.shape[1:], qkv_ref.dtype)
        gate_ref[1] = jnp.zeros(gate_ref.shape[1:], gate_ref.dtype)

    for w32, w16 in ((wa_ref, wa_bf_ref), (wb_ref, wb_bf_ref), (wc_ref, wc_bf_ref)):
        w16[...] = w32[...].astype(w16.dtype)

    lane = lax.broadcasted_iota(jnp.int32, (C, LANES), 1)
    half = lane < HEAD_DIM
    same_head = (lax.broadcasted_iota(jnp.int32, (LANES, LANES), 0) // HEAD_DIM
                 == lax.broadcasted_iota(jnp.int32, (LANES, LANES), 1) // HEAD_DIM)
    inv_d = 1.0 / HEAD_DIM
    starts_seq = (s - 1) % tiles_per_seq == 0

    def half_mean(t):
        lo = jnp.sum(jnp.where(half, t, 0.0), axis=-1, keepdims=True)
        hi = jnp.sum(jnp.where(half, 0.0, t), axis=-1, keepdims=True)
        return jnp.where(half, lo, hi) * inv_d

    n_pairs = RET_WIDTH // LANES
    n_chunks = tm // C
    state = [jnp.where(starts_seq, 0.0, sbd_ref[c]) for c in range(n_pairs)]

    def retention_unit(c, cc):
        ls = slice(c * LANES, (c + 1) * LANES)
        rs = slice(cc * C, (cc + 1) * C)
        S = state[c]
        qb = qkv_ref[prv, 0, rs, ls]
        kb = qkv_ref[prv, 1, rs, ls]
        vb = qkv_ref[prv, 2, rs, ls]
        q = qb.astype(F32)
        q_pair = jnp.concatenate([jnp.where(half, q, 0.0), jnp.where(half, 0.0, q)], axis=0).astype(BF16)
        inner = _dot_nt(q_pair, kb) * dec_ref[c]
        cross = _dot(qb, S.astype(BF16)) * cd_ref[:, ls]
        kd = (kb.astype(F32) * kd_ref[:, ls]).astype(BF16)
        state[c] = S * sd_ref[:, ls] + jnp.where(same_head, _dot_tn(kd, vb), 0.0)
        yield
        o2 = _dot(inner.astype(BF16), vb)
        o = jnp.where(half, o2[:C], o2[C:]) + cross
        yield
        mu = half_mean(o)
        d = o - mu
        var = half_mean(d * d)
        on = d * lax.rsqrt(var + GN_EPS) * gr_ref[:, ls]
        gate = gate_ref[prv, rs, ls]
        ro_ref[0, rs, ls] = (on * (gate * jax.nn.sigmoid(gate))).astype(ro_ref.dtype)
        yield

    def retention_pieces():
        for cc in range(n_chunks):
            units = [retention_unit(c, cc) for c in range(n_pairs)]
            for _ in range(3):
                for u in units:
                    next(u)
                    yield

    j = jnp.minimum(s, n_tiles - 1) % tiles_per_seq
    rows = pl.ds(pl.multiple_of(j * tm, tm), tm)
    steps = _proj_steps(x_ref[0], g_ref, w_ref, cos_ref[rows, :], sa_ref[rows, :], sb_ref[rows, :])
    pieces = retention_pieces()
    n_pieces = 3 * n_chunks * n_pairs
    win = window(*win_in, koT_ref, voT_ref, sao_ref)
    cols = []
    for i, step in enumerate(steps):
        cols.append(step())
        for _ in range((i + 1) * n_pieces // len(steps) - i * n_pieces // len(steps)):
            next(pieces)
        next(win, None)
    for _ in win:
        pass
    for c in range(n_pairs):
        S = state[c]
        sbd_ref[c] = S
        s_out_ref[0, 2 * c] = S[:HEAD_DIM, :HEAD_DIM]
        s_out_ref[0, 2 * c + 1] = S[HEAD_DIM:, HEAD_DIM:]
    rq, rk, rv, rg, aq, ak, av = _join_groups(cols)
    aq_ref[0] = aq.astype(aq_ref.dtype)
    ak_ref[0] = ak
    av_ref[0] = av
    qkv_ref[cur, 0] = rq.astype(BF16)
    qkv_ref[cur, 1] = rk.astype(BF16)
    qkv_ref[cur, 2] = rv.astype(BF16)
    gate_ref[cur] = rg


def _proj_retention(x, g, w_bf, g_ret, tm, later_weights, window):
    B, T, D = x.shape
    R, A, H, Dh = RET_WIDTH, ATT_WIDTH, N_RET_HEADS, HEAD_DIM
    tps = T // tm
    n_tiles = B * tps
    tabs = _rope_tables(jnp.arange(T, dtype=jnp.int32))
    dec, cd, kd, sd = _retention_consts(RET_CHUNK)
    dec_pair = dec.reshape(H // 2, 2 * RET_CHUNK, RET_CHUNK)

    def this_tile(width):
        def index(s):
            t = jnp.minimum(s, n_tiles - 1)
            return (t // tps, t % tps, 0)
        return pl.BlockSpec((1, tm, width), index)

    def prev_tile(s):
        t = jnp.maximum(s - 1, 0)
        return (t // tps, t % tps, 0)

    def slab(w):
        rows = max(16, -(-w.shape[0] // n_tiles) // 16 * 16)
        while w.shape[0] % rows:
            rows += 16
        last = w.shape[0] // rows - 1
        return pl.BlockSpec((rows, w.shape[1]), lambda s: (jnp.minimum(s, last), 0))

    full = lambda shape: pl.BlockSpec(shape, lambda s: (0,) * len(shape))
    consts = [g.reshape(1, D), w_bf, *tabs, dec_pair, cd, kd, sd, g_ret.reshape(1, R)]
    slabs = [slab(w) for w in later_weights]
    win_body, win_args, win_in_specs, win_out_specs, win_out_shape = window
    return pl.pallas_call(
        functools.partial(_proj_ret_kernel, tps, win_body, len(win_args)),
        grid=(n_tiles + 1,),
        in_specs=[this_tile(D)] + [full(t.shape) for t in consts] + slabs + win_in_specs,
        out_specs=[pl.BlockSpec((1, tm, R), prev_tile),
                   pl.BlockSpec((1, H, Dh, Dh), lambda s: (jnp.maximum(s - 1, 0) // tps, 0, 0, 0)),
                   this_tile(A), this_tile(A), this_tile(A)] + slabs + win_out_specs,
        out_shape=[jax.ShapeDtypeStruct((B, T, R), BF16), jax.ShapeDtypeStruct((B, H, Dh, Dh), F32),
                   jax.ShapeDtypeStruct((B, T, A), BF16), jax.ShapeDtypeStruct((B, T, A), F32),
                   jax.ShapeDtypeStruct((B, T, A), F32)]
                  + [jax.ShapeDtypeStruct(w.shape, BF16) for w in later_weights] + win_out_shape,
        scratch_shapes=[pltpu.VMEM((R // LANES, LANES, LANES), F32), pltpu.VMEM((2, 3, tm, R), BF16),
                        pltpu.VMEM((2, tm, R), F32)],
        compiler_params=_cparams("proj_retention", "arbitrary"),
        name="proj_retention",
    )(x, *consts, *later_weights, *win_args)


def _ret_kernel(q_ref, k_ref, v_ref, gate_ref, dec_ref, cd_ref, kd_ref, sd_ref, gr_ref, s0_ref, o_ref, s_ref):
    def unit(b, h):
        sl = slice(h * HEAD_DIM, (h + 1) * HEAD_DIM)
        qh = q_ref[b, :, sl].astype(BF16)
        kh = k_ref[b, :, sl].astype(BF16)
        vh = v_ref[b, :, sl].astype(BF16)
        S = s0_ref[b, h]
        inner = _dot_nt(qh, kh) * dec_ref[h]
        cross = _dot(qh, S.astype(BF16)) * cd_ref[:, sl]
        kd = (kh.astype(F32) * kd_ref[:, sl]).astype(BF16)
        s_ref[b, h] = S * sd_ref[:, sl] + _dot_tn(kd, vh)
        yield
        o = _dot(inner.astype(BF16), vh) + cross
        yield
        mu = jnp.mean(o, axis=-1, keepdims=True)
        var = jnp.mean(jnp.square(o - mu), axis=-1, keepdims=True)
        on = (o - mu) * lax.rsqrt(var + GN_EPS) * gr_ref[:, sl]
        gate = gate_ref[b, :, sl]
        o_ref[b, :, sl] = on * (gate * jax.nn.sigmoid(gate))
        yield

    for b in range(q_ref.shape[0]):
        units = [unit(b, h) for h in range(N_RET_HEADS)]
        for _ in range(3):
            for u in units:
                next(u)


def _retention_step(rq, rk, rv, rg, g_ret, state):
    B, T, R = rq.shape
    H, Dh = N_RET_HEADS, HEAD_DIM
    consts = [*_retention_consts(T), g_ret.reshape(1, R)]
    bb = math.gcd(B, 8)
    row = pl.BlockSpec((bb, T, R), lambda b: (b, 0, 0))
    full = lambda shape: pl.BlockSpec(shape, lambda b: (0,) * len(shape))
    st = pl.BlockSpec((bb, H, Dh, Dh), lambda b: (b, 0, 0, 0))
    return pl.pallas_call(
        _ret_kernel,
        grid=(B // bb,),
        in_specs=[row, row, row, row] + [full(t.shape) for t in consts] + [st],
        out_specs=[row, st],
        out_shape=[jax.ShapeDtypeStruct((B, T, R), F32), jax.ShapeDtypeStruct((B, H, Dh, Dh), F32)],
        compiler_params=_cparams("retention", "arbitrary"),
        name="retention",
    )(rq, rk, rv, rg, *consts, state)


def _split3_dot(c, e):
    c1 = c.astype(BF16)
    r1 = c - c1.astype(F32)
    c2 = r1.astype(BF16)
    c3 = (r1 - c2.astype(F32)).astype(BF16)
    return _dot(c1, e) + _dot(c2, e) + _dot(c3, e)


def _chunk_rows(ref, stride):
    def load(start, c):
        idx = pl.ds(start, ATT_BLOCK) if stride == 1 else pl.ds(start, ATT_BLOCK, stride=stride)
        return ref[c, idx, :]
    return load


def _token_rows(ref):
    def load(start, c):
        return ref[0, pl.ds(start, ATT_BLOCK), c * LANES:(c + 1) * LANES].astype(F32)
    return load


def _att_pieces(load_q, load_k, load_v, bias_ref, cur, prev, first, out):
    BLK, H = ATT_BLOCK, N_ATT_HEADS
    half = lax.broadcasted_iota(jnp.int32, (BLK, LANES), 1) < HEAD_DIM
    scores = []
    for c in range(ATT_CHUNKS):
        q = load_q(cur, c)
        q_pair = jnp.concatenate([jnp.where(half, q, 0.0), jnp.where(half, 0.0, q)], axis=0).astype(BF16)
        k = load_k(cur, c)
        if prev is not None:
            k = jnp.concatenate([load_k(prev, c), k], axis=0)
        scores.append(_dot_nt(q_pair, k.astype(BF16)))
        yield
    stats = []
    for c in range(ATT_CHUNKS):
        s = scores[c] + bias_ref[2 * c * BLK:(2 * c + 2) * BLK, :]
        if prev is not None:
            col = lax.broadcasted_iota(jnp.int32, s.shape, 1)
            s = jnp.where(jnp.logical_and(first, col < BLK), NEG_INF, s)
        m = jnp.max(s, axis=-1, keepdims=True)
        p = jnp.exp(s - m)
        stats.append((m, jnp.sum(p, axis=-1, keepdims=True), p.astype(BF16)))
        yield
    accs = []
    for c in range(ATT_CHUNKS):
        v = load_v(cur, c)
        if prev is not None:
            v = jnp.concatenate([load_v(prev, c), v], axis=0)
        o = _dot(stats[c][2], v.astype(BF16))
        accs.append(jnp.where(half, o[:BLK], o[BLK:]))
        yield
    lane = lax.broadcasted_iota(jnp.int32, (BLK, LANES), 1)
    ml = jnp.zeros((BLK, LANES), F32)
    for h in range(H):
        m, l, _ = stats[h // 2]
        rows = slice((h % 2) * BLK, (h % 2 + 1) * BLK)
        ml = jnp.where(lane == h, m[rows], ml)
        ml = jnp.where(lane == H + h, l[rows], ml)
    out.append((accs, ml))
    yield


def _att_pipeline(blocks, after=None):
    outs = [[] for _ in blocks]
    gens = [_att_pieces(*b, o) for b, o in zip(blocks, outs)]
    n = ATT_CHUNKS
    total = 3 * n + 1
    tails = []
    for t in range((len(blocks) - 1) * n + total):
        for tail in tails:
            next(tail, None)
        for k, g in enumerate(gens):
            if 0 <= t - k * n < total:
                next(g)
                if t - k * n == total - 1 and after is not None:
                    tails.append(after(k, *outs[k][0]))
    for tail in tails:
        for _ in tail:
            pass
    return [o[0] for o in outs]


def _att_kernel(q_ref, k_ref, v_ref, bv_ref, e_ref, ao_ref, b1_ref, b4_ref, b16_ref, q4, k4, v4, stage,
                acc4, ml4, acc16, ml16):
    BLK, H = ATT_BLOCK, N_ATT_HEADS
    S = q_ref.shape[1]
    nblk = S // BLK
    d4, d16 = DIL_PATTERNS[1][1], DIL_PATTERNS[2][1]
    nb4 = nblk // d4
    per_res = S // d4

    @pl.when(pl.program_id(0) == 0)
    def _():
        for g, tab in enumerate((b1_ref, b4_ref, b16_ref)):
            for h in range(H):
                rows = jnp.broadcast_to(bv_ref[g, h:h + 1, :], (BLK, bv_ref.shape[2]))
                skew = pltpu.roll(rows, 0, 1, stride=1, stride_axis=0)
                tab[h * BLK:(h + 1) * BLK, :] = skew[:, 2 * BLK - tab.shape[1]:2 * BLK]

    for src, dst in ((q_ref, q4), (k_ref, k4), (v_ref, v4)):
        for c in range(ATT_CHUNKS):
            stage[...] = src[0, :, c * LANES:(c + 1) * LANES].astype(F32)
            for r in range(d4):
                dst[c, r * per_res:(r + 1) * per_res, :] = stage[pl.ds(r, per_res, stride=d4), :]
    contiguous = [_chunk_rows(t, 1) for t in (q4, k4, v4)]
    every_4th = [_chunk_rows(t, d16 // d4) for t in (q4, k4, v4)]
    tokens = [_token_rows(t) for t in (q_ref, k_ref, v_ref)]

    def dilated(i, carry):
        blocks, toks = [], []
        for n in (2 * i, 2 * i + 1):
            r, j = n // nb4, n % nb4
            cur = r * per_res + j * BLK
            prev = r * per_res + jnp.maximum(j - 1, 0) * BLK
            blocks.append((*contiguous, b4_ref, cur, prev, j == 0))
            toks.append((r + j * (BLK * d4), d4, acc4, ml4))
            blocks.append((*every_4th, b16_ref, (n % d4) * per_res + n // d4, None, None))
            toks.append((n, d16, acc16, ml16))
        for (accs, ml), (tok, dil, acc_ref, ml_ref) in zip(_att_pipeline(blocks), toks):
            for c in range(ATT_CHUNKS):
                acc_ref[c, pl.ds(tok, BLK, stride=dil), :] = accs[c]
            ml_ref[pl.ds(tok, BLK, stride=dil), :] = ml
        return carry

    def merge(cur, accs1, ml1):
        rows = pl.ds(cur, BLK)
        mls = [ml1, ml4[rows, :], ml16[rows, :]]
        accs = [accs1, [acc4[c, rows, :] for c in range(ATT_CHUNKS)], [acc16[c, rows, :] for c in range(ATT_CHUNKS)]]
        lane = lax.broadcasted_iota(jnp.int32, (BLK, LANES), 1)
        m_all = jnp.maximum(jnp.maximum(mls[0], mls[1]), mls[2])
        ws = [jnp.exp(ml - m_all) for ml in mls]
        den = sum(w * pltpu.roll(ml, LANES - H, 1) for w, ml in zip(ws, mls))
        ao = [jnp.zeros((BLK, LANES), F32) for _ in range(ATT_CHUNKS)]
        yield
        for w, acc in zip(ws, accs):
            coef = _split3_dot(jnp.where(lane < H, w / den, 0.0), e_ref[...])
            for c in range(ATT_CHUNKS):
                ao[c] = ao[c] + coef[:, c * LANES:(c + 1) * LANES] * acc[c]
            yield
        ao_ref[0, rows, :] = jnp.concatenate(ao, axis=-1).astype(ao_ref.dtype)
        yield

    def dense_and_merge(i, carry):
        per_trip = 4
        starts = [pl.multiple_of((per_trip * i + k) * BLK, BLK) for k in range(per_trip)]
        before = pl.multiple_of(jnp.maximum(per_trip * i - 1, 0) * BLK, BLK)
        blocks = [(*tokens, b1_ref, cur, prev, first)
                  for cur, prev, first in zip(starts, [before] + starts[:-1], [i == 0] + [False] * (per_trip - 1))]
        _att_pipeline(blocks, lambda k, accs, ml: merge(starts[k], accs, ml))
        return carry

    lax.fori_loop(0, nblk // 2, dilated, 0)
    lax.fori_loop(0, nblk // 4, dense_and_merge, 0)


def _prompt_attention(aq, ak, av, rel_bias):
    B, S, A = aq.shape
    CH = ATT_CHUNKS
    assert [d for _, d in DIL_PATTERNS] == [1, 4, 16] and S // ATT_BLOCK == 16
    bias_vectors = _prompt_bias_vectors(rel_bias)
    head_of_lane = jnp.arange(A)[None, :] // HEAD_DIM
    expand = (jnp.arange(LANES)[:, None] == head_of_lane).astype(BF16)
    seq = pl.BlockSpec((1, S, A), lambda b: (b, 0, 0))
    const = lambda shape: pl.BlockSpec(shape, lambda b: (0,) * len(shape), pipeline_mode=pl.Buffered(1))
    chunked = pltpu.VMEM((CH, S, LANES), F32)
    stats = pltpu.VMEM((S, LANES), F32)
    table = lambda width: pltpu.VMEM((N_ATT_HEADS * ATT_BLOCK, width), F32)
    return pl.pallas_call(
        _att_kernel,
        grid=(B,),
        in_specs=[seq, seq, seq, const(bias_vectors.shape), const(expand.shape)],
        out_specs=seq,
        out_shape=jax.ShapeDtypeStruct((B, S, A), BF16),
        scratch_shapes=[table(2 * ATT_BLOCK), table(2 * ATT_BLOCK), table(ATT_BLOCK),
                        chunked, chunked, chunked, stats, chunked, stats, chunked, stats],
        compiler_params=_cparams("prompt_att", "arbitrary"),
        name="prompt_att",
    )(aq, ak, av, bias_vectors, expand)


def _window_pieces(lows, T, kT_ref, vT_ref, qn_ref, kn_ref, vn_ref, c1_ref, c4_ref, c16_ref,
                   n1_ref, n4_ref, n16_ref, koT_ref, voT_ref, ao_ref):
    H, Dh, L = kT_ref.shape[1:]
    A = H * Dh
    zpad = jnp.zeros((LANES - T, A), F32)
    tail_lane = lax.broadcasted_iota(jnp.int32, (Dh, LANES), 1) >= LANES - T
    shifted = {}

    def shift_in(name, xT_ref, new_ref, out_ref):
        new_p = jnp.concatenate([zpad, new_ref[0]], axis=0)
        new_t = new_p.T
        heads = []
        for h in range(H):
            x = xT_ref[0, h]
            rolled = pltpu.roll(x, L - T, 1)
            tail = jnp.where(tail_lane, new_t[h * Dh:(h + 1) * Dh], rolled[:, L - LANES:])
            out_ref[0, h] = jnp.concatenate([rolled[:, :L - LANES], tail], axis=1)
            heads.append(x.astype(BF16))
            yield
        shifted[name] = (jnp.concatenate(heads, axis=0), new_p.astype(BF16))

    yield from shift_in("k", kT_ref, kn_ref, koT_ref)
    yield from shift_in("v", vT_ref, vn_ref, voT_ref)
    (kT, kn), (vT, vn) = shifted["k"], shifted["v"]
    HT = H * T
    row_head = lax.broadcasted_iota(jnp.int32, (HT, A), 0) // T
    lane_head = lax.broadcasted_iota(jnp.int32, (HT, A), 1) // HEAD_DIM
    diag = row_head == lane_head
    q_rows = jnp.where(diag, jnp.concatenate([qn_ref[0]] * H, axis=0), 0.0).astype(BF16)
    s_cache = _dot(q_rows, kT)
    s_new = _dot_nt(q_rows, kn)
    yield
    soft = []
    for lo, c_ref, n_ref in zip(lows, (c1_ref, c4_ref, c16_ref), (n1_ref, n4_ref, n16_ref)):
        sc = s_cache[:, lo:] + c_ref[...]
        sn = s_new + n_ref[...]
        m = jnp.maximum(jnp.max(sc, axis=-1, keepdims=True), jnp.max(sn, axis=-1, keepdims=True))
        pc = jnp.exp(sc - m)
        pn = jnp.exp(sn - m)
        l = jnp.sum(pc, axis=-1, keepdims=True) + jnp.sum(pn, axis=-1, keepdims=True)
        soft.append((m, l, pc.astype(BF16), pn.astype(BF16)))
        yield
    stats = []
    for lo, (m, l, pc, pn) in zip(lows, soft):
        stats.append((m, l, _dot_nt(pc, vT[:, lo:]) + _dot(pn, vn)))
        yield
    m_all = jnp.maximum(jnp.maximum(stats[0][0], stats[1][0]), stats[2][0])
    num = jnp.zeros((HT, A), F32)
    den = jnp.zeros((HT, 1), F32)
    for m, l, acc in stats:
        w = jnp.exp(m - m_all)
        num = num + w * acc
        den = den + w * l
    comb = jnp.where(diag, num / den, 0.0)
    out = comb[0:T]
    for h in range(1, H):
        out = out + comb[h * T:(h + 1) * T]
    ao_ref[0] = out
    yield


def _sample_window_operands(aq, ak, av, cache_kT, cache_vT, rel_bias, groups, unit_of_step):
    B, T, A = aq.shape
    _, H, Dh, L = cache_kT.shape
    Hg, Ag = H // groups, A // groups
    lows = tuple(L - win for win, _ in DIL_PATTERNS)
    assert all(lo >= 0 and lo % LANES == 0 for lo in lows) and T <= LANES and Ag % LANES == 0
    tabs = _sample_bias_tables(rel_bias, L, T)
    u = unit_of_step
    big = pl.BlockSpec((1, Hg, Dh, L), lambda i: (u(i) // groups, u(i) % groups, 0, 0))
    small = pl.BlockSpec((1, T, Ag), lambda i: (u(i) // groups, 0, u(i) % groups))
    tab = lambda t: pl.BlockSpec((Hg * T, t.shape[1]), lambda i: (u(i) % groups, 0))
    win_sds = jax.ShapeDtypeStruct((B, H, Dh, L), F32)
    args = [cache_kT, cache_vT, aq, ak, av, *tabs]
    in_specs = [big, big, small, small, small] + [tab(t) for t in tabs]
    out_specs = [big, big, small]
    out_shape = [win_sds, win_sds, jax.ShapeDtypeStruct((B, T, A), F32)]
    return functools.partial(_window_pieces, lows, T), args, in_specs, out_specs, out_shape


def _ffn_pieces(d_ff, ff_chunk, x_ref, ro_ref, ao_ref, wo_ref, wgu_ref, wd_ref, g_pm, g_pf, g_of, y_ref, act_ref):
    R = RET_WIDTH
    mix = _dot(ro_ref[...].astype(BF16), wo_ref[:R, :]) + _dot(ao_ref[...].astype(BF16), wo_ref[R:, :])
    x1 = x_ref[...] + _rms(mix, g_pm[...])
    h = _rms(x1, g_pf[...]).astype(BF16)
    yield
    for c in range(0, d_ff, ff_chunk):
        gate = _dot(h, wgu_ref[:, c:c + ff_chunk])
        up = _dot(h, wgu_ref[:, d_ff + c:d_ff + c + ff_chunk])
        act_ref[:, c:c + ff_chunk] = (gate * jax.nn.sigmoid(gate) * up).astype(BF16)
        yield
    f = _dot(act_ref[...], wd_ref[...])
    y_ref[...] = x1 + _rms(f, g_of[...])
    yield


def _out_kernel(d_ff, ff_chunk, *refs):
    for _ in _ffn_pieces(d_ff, ff_chunk, *refs):
        pass


def _out_window_kernel(d_ff, ff_chunk, window, n_ffn_in, n_win_in, n_alias, *refs):
    ffn_in = refs[:n_ffn_in]
    win_in = refs[n_ffn_in:n_ffn_in + n_win_in]
    y_ref, koT_ref, voT_ref, sao_ref, act_ref = refs[n_ffn_in + n_win_in + n_alias:]
    win = window(*win_in, koT_ref, voT_ref, sao_ref)
    for _ in _ffn_pieces(d_ff, ff_chunk, *ffn_in, y_ref, act_ref):
        next(win, None)
        next(win, None)
    for _ in win:
        pass


def _out_block(x, ro, ao, wo_bf, wgu_bf, wd_bf, g_post_mix, g_pre_ffn, g_post_ffn, tm, window=None, partial=()):
    N, D = x.shape
    d_ff = wd_bf.shape[0]
    row = lambda width: pl.BlockSpec((tm, width), lambda i: (i, 0))
    const = lambda shape: pl.BlockSpec(shape, lambda i: (0,) * len(shape), pipeline_mode=pl.Buffered(1))
    args = [x, ro, ao, wo_bf, wgu_bf, wd_bf, g_post_mix.reshape(1, D), g_pre_ffn.reshape(1, D),
            g_post_ffn.reshape(1, D)]
    in_specs = [row(D), row(RET_WIDTH), row(ATT_WIDTH), const(wo_bf.shape), const(wgu_bf.shape),
                const(wd_bf.shape), const((1, D)), const((1, D)), const((1, D))]
    body = functools.partial(_out_kernel, d_ff, FF_CHUNK)
    out_specs, out_shape = [row(D)], [jax.ShapeDtypeStruct((N, D), F32)]
    if window is not None:
        win_body, win_args, win_in_specs, win_out_specs, win_out_shape = window
        body = functools.partial(_out_window_kernel, d_ff, FF_CHUNK, win_body, len(args), len(win_args),
                                 len(partial))
        args, in_specs = args + win_args, in_specs + win_in_specs
        aliases = {len(args) + k: 1 + k for k in range(len(partial))}
        args = args + list(partial)
        in_specs = in_specs + [pl.BlockSpec(memory_space=pl.ANY)] * len(partial)
        out_specs, out_shape = out_specs + win_out_specs, out_shape + win_out_shape
    else:
        aliases = {}
    outs = pl.pallas_call(
        body,
        grid=(N // tm,),
        in_specs=in_specs,
        out_specs=out_specs,
        out_shape=out_shape,
        input_output_aliases=aliases,
        scratch_shapes=[pltpu.VMEM((tm, d_ff), BF16)],
        compiler_params=_cparams("out_ffn", "arbitrary"),
        name="out_ffn",
    )(*args)
    return outs[0] if window is None else outs


def kernel(x_prompt, x_sample, state_ret, cache_k_win, cache_v_win, rel_bias, w_in, g_ret, w_out,
           g_pre_mix, g_post_mix, g_pre_ffn, g_post_ffn, w_gu, w_down):
    depth = w_in.shape[0]
    assert depth == 1
    B, S, D = x_prompt.shape
    Bs, Ts, _ = x_sample.shape
    H, Dh = N_ATT_HEADS, HEAD_DIM
    l = 0
    w_in_bf = w_in[l].astype(BF16)

    pos_s = PAST_LEN + jnp.arange(Ts, dtype=jnp.int32)
    N_s = Bs * Ts
    outs = _proj(x_sample.reshape(N_s, D), g_pre_mix[l], w_in_bf, jnp.tile(pos_s, Bs))
    srq, srk, srv, srg, saq, sak, sav = [t.reshape(Bs, Ts, RET_WIDTH) for t in outs]
    sro, s_s = _retention_step(srq, srk, srv, srg, g_ret[l], state_ret[l])
    to_t = lambda t: jnp.transpose(t, (0, 2, 3, 1))
    from_t = lambda t: jnp.transpose(t, (0, 3, 1, 2))
    n_units = Bs * WINDOW_HEAD_GROUPS
    first_units = B * S // PROJ_TILE
    assert first_units + B * S // FFN_TILE == n_units
    window_of = lambda unit_of_step: _sample_window_operands(
        saq, sak, sav, to_t(cache_k_win[l]), to_t(cache_v_win[l]), rel_bias, WINDOW_HEAD_GROUPS, unit_of_step)

    ro, s_p, aq, ak, av, wo_bf, wgu_bf, wd_bf, *partial = _proj_retention(
        x_prompt, g_pre_mix[l], w_in_bf, g_ret[l], PROJ_TILE, (w_out[l], w_gu[l], w_down[l]),
        window_of(lambda i: jnp.minimum(i, first_units - 1)))
    ao = _prompt_attention(aq, ak, av, rel_bias)
    y_p, k_sT, v_sT, sao = _out_block(x_prompt.reshape(B * S, D), ro.reshape(B * S, RET_WIDTH),
                                      ao.reshape(B * S, ATT_WIDTH), wo_bf, wgu_bf, wd_bf, g_post_mix[l],
                                      g_pre_ffn[l], g_post_ffn[l], FFN_TILE,
                                      window_of(lambda i: first_units + i), partial)
    k_s, v_s = from_t(k_sT), from_t(v_sT)

    y_s = _out_block(x_sample.reshape(N_s, D), sro.reshape(N_s, RET_WIDTH), sao.reshape(N_s, ATT_WIDTH),
                     wo_bf, wgu_bf, wd_bf, g_post_mix[l], g_pre_ffn[l], g_post_ffn[l], N_s)

    return (y_p.reshape(B, S, D), y_s.reshape(Bs, Ts, D),
            s_p[None], ak.reshape(1, B, S, H, Dh), av.reshape(1, B, S, H, Dh),
            s_s[None], k_s[None], v_s[None])
```

```python
import functools
import math

import jax
import jax.numpy as jnp
from jax import lax
from jax.experimental import pallas as pl
from jax.experimental.pallas import tpu as pltpu

F32 = jnp.float32
BF16 = jnp.bfloat16

HEAD_DIM = 64
N_RET_HEADS = 8
N_ATT_HEADS = 8
RET_WIDTH = N_RET_HEADS * HEAD_DIM
ATT_WIDTH = N_ATT_HEADS * HEAD_DIM
RET_CHUNK = 128
DIL_PATTERNS = ((128, 1), (512, 4), (2048, 16))
ATT_BLOCK = 128
N_BUCKETS = 32
MAX_DISTANCE = 2048
ROPE_BASE = 10000.0
NORM_EPS = 1e-6
GN_EPS = 1e-5
PAST_LEN = 16384
LANES = 128
ATT_CHUNKS = ATT_WIDTH // LANES
NEG_INF = float("-inf")

PROJ_TILE = 512
FFN_TILE = 256
FF_CHUNK = 256
WINDOW_HEAD_GROUPS = 2
VMEM_MB = dict(proj=40, retention=32, proj_retention=56, prompt_att=58, out_ffn=56)


def _cparams(name, *sem):
    return pltpu.CompilerParams(dimension_semantics=sem, vmem_limit_bytes=VMEM_MB[name] * 1024 * 1024)


def _rms(x, g):
    return x * lax.rsqrt(jnp.mean(x * x, axis=-1, keepdims=True) + NORM_EPS) * g


def _dot(a, b):
    return jnp.dot(a, b, preferred_element_type=F32)


def _dot_nt(a, b):
    return lax.dot_general(a, b, (((1,), (1,)), ((), ())), preferred_element_type=F32)


def _dot_tn(a, b):
    return lax.dot_general(a, b, (((0,), (0,)), ((), ())), preferred_element_type=F32)


def _rope_tables(pos):
    inv = ROPE_BASE ** (-jnp.arange(0, HEAD_DIM, 2, dtype=F32) / HEAD_DIM)
    ang = pos.astype(F32)[:, None] * inv[None, :]
    cos, sin = jnp.cos(ang), jnp.sin(ang)
    zero = jnp.zeros_like(sin)
    rep = LANES // HEAD_DIM
    cosf = jnp.tile(jnp.concatenate([cos, cos], axis=-1), (1, rep))
    sa = jnp.tile(jnp.concatenate([-sin, zero], axis=-1), (1, rep))
    sb = jnp.tile(jnp.concatenate([zero, sin], axis=-1), (1, rep))
    return cosf, sa, sb


def _retention_consts(C):
    log_g = jnp.log1p(-jnp.exp2(-5.0 - jnp.arange(N_RET_HEADS, dtype=F32)))
    i = jnp.arange(C, dtype=F32)
    rel = i[:, None] - i[None, :]
    decay = jnp.where(rel[None] >= 0, jnp.exp(jnp.maximum(rel, 0.0)[None] * log_g[:, None, None]), 0.0)
    cross = jnp.exp((i + 1.0)[:, None] * log_g[None, :])
    kdec = jnp.exp((C - 1.0 - i)[:, None] * log_g[None, :])
    sdec = jnp.exp(C * log_g)[None, :]
    expand = lambda t: jnp.repeat(t, HEAD_DIM, axis=-1)
    return decay, expand(cross), expand(kdec), expand(sdec)


def _t5_bucket(dist):
    max_exact = N_BUCKETS // 2
    d_f = jnp.maximum(dist, 1).astype(F32)
    large = max_exact + (jnp.log(d_f / max_exact) / math.log(MAX_DISTANCE / max_exact)
                         * (N_BUCKETS - max_exact)).astype(jnp.int32)
    large = jnp.minimum(large, N_BUCKETS - 1)
    return jnp.where(dist < max_exact, dist, large)


def _pattern_bias_rev(rel_bias, dil, nk):
    dist = jnp.arange(nk, -1, -1, dtype=jnp.int32) * dil
    return rel_bias[_t5_bucket(dist)].astype(F32).T


def _prompt_bias_vectors(rel_bias):
    BLK, H = ATT_BLOCK, N_ATT_HEADS
    out = []
    for win, dil in DIL_PATTERNS:
        nk = win // dil
        assert nk == BLK
        v = jnp.concatenate([_pattern_bias_rev(rel_bias, dil, nk),
                             jnp.full((H, 3 * BLK - nk - 1), NEG_INF, F32)], axis=1)
        out.append(jnp.roll(v[:, ::-1], 1, axis=1))
    return jnp.stack(out)


def _sample_bias_tables(rel_bias, L, T):
    H = N_ATT_HEADS
    cache_part, new_part = [], []
    for win, dil in DIL_PATTERNS:
        nk = win // dil
        rev = _pattern_bias_rev(rel_bias, dil, nk)
        if dil > 1:
            gaps = jnp.full((H, nk + 1, dil - 1), NEG_INF, F32)
            rev = jnp.concatenate([rev[:, :, None], gaps], axis=2).reshape(H, (nk + 1) * dil)[:, :nk * dil + 1]
        pv = jnp.concatenate([jnp.full((H, T), NEG_INF, F32), rev, jnp.full((H, 2 * T), NEG_INF, F32)], axis=1)
        width = win + T
        rows = jnp.stack([pv[:, T - t:T - t + width] for t in range(T)], axis=1)
        cache_part.append(rows[:, :, :win].reshape(H * T, win))
        new = jnp.concatenate([jnp.full((H, T, LANES - T), NEG_INF, F32), rows[:, :, win:]], axis=2)
        new_part.append(new.reshape(H * T, LANES))
    return cache_part + new_part


def _proj_steps(x, g_ref, w_ref, cos, sa, sb):
    h = _rms(x, g_ref[...]).astype(BF16)
    W = 2 * LANES
    rep = W // LANES
    cosf = jnp.concatenate([cos] * rep, axis=-1)
    saf = jnp.concatenate([sa] * rep, axis=-1)
    sbf = jnp.concatenate([sb] * rep, axis=-1)
    half = HEAD_DIM // 2

    def col(c):
        return _dot(h, w_ref[:, c:c + W])

    def rope(z):
        return z * cosf + pltpu.roll(z, W - half, 1) * saf + pltpu.roll(z, half, 1) * sbf

    R, A = RET_WIDTH, ATT_WIDTH
    scale = HEAD_DIM ** -0.5
    steps = []
    for c in range(0, R, W):
        steps.append(lambda c=c: rope(col(c)))
    for c in range(R, 2 * R, W):
        steps.append(lambda c=c: rope(col(c)) * scale)
    for c in range(2 * R, 4 * R, W):
        steps.append(lambda c=c: col(c))
    for c in range(4 * R, 4 * R + A, W):
        steps.append(lambda c=c: col(c) * scale)
    for c in range(4 * R + A, 4 * R + 3 * A, W):
        steps.append(lambda c=c: col(c))
    return steps


def _join_groups(pieces):
    return [jnp.concatenate(pieces[i:i + 2], axis=-1) for i in range(0, len(pieces), 2)]


def _proj_kernel(x_ref, g_ref, w_ref, cos_ref, sa_ref, sb_ref, *outs):
    steps = _proj_steps(x_ref[...], g_ref, w_ref, cos_ref[...], sa_ref[...], sb_ref[...])
    for ref, z in zip(outs, _join_groups([step() for step in steps])):
        ref[...] = z


def _proj(x, g, w_bf, pos):
    N, D = x.shape
    tabs = _rope_tables(pos)
    full = lambda shape: pl.BlockSpec(shape, lambda i: (0,) * len(shape))
    return pl.pallas_call(
        _proj_kernel,
        grid=(1,),
        in_specs=[full((N, D)), full((1, D)), full(w_bf.shape)] + [full(t.shape) for t in tabs],
        out_specs=[full((N, RET_WIDTH))] * 7,
        out_shape=[jax.ShapeDtypeStruct((N, RET_WIDTH), F32)] * 7,
        compiler_params=_cparams("proj", "arbitrary"),
        name="proj",
    )(x, g.reshape(1, D), w_bf, *tabs)


def _proj_ret_kernel(tiles_per_seq, x_ref, g_ref, w_ref, cos_ref, sa_ref, sb_ref, dec_ref, cd_ref, kd_ref,
                     sd_ref, gr_ref, wa_ref, wb_ref, wc_ref, ro_ref, s_out_ref, aq_ref, ak_ref, av_ref,
                     wa_bf_ref, wb_bf_ref, wc_bf_ref, sbd_ref, qkv_ref, gate_ref):
    s = pl.program_id(0)
    n_tiles = pl.num_programs(0) - 1
    tm = x_ref.shape[1]
    C = RET_CHUNK
    cur = s % 2
    prv = 1 - cur

    @pl.when(s == 0)
    def _():
        sbd_ref[...] = jnp.zeros_like(sbd_ref)
        qkv_ref[1] = jnp.zeros(qkv_ref.shape[1:], qkv_ref.dtype)
        gate_ref[1] = jnp.zeros(gate_ref.shape[1:], gate_ref.dtype)

    for w32, w16 in ((wa_ref, wa_bf_ref), (wb_ref, wb_bf_ref), (wc_ref, wc_bf_ref)):
        w16[...] = w32[...].astype(w16.dtype)

    lane = lax.broadcasted_iota(jnp.int32, (C, LANES), 1)
    half = lane < HEAD_DIM
    same_head = (lax.broadcasted_iota(jnp.int32, (LANES, LANES), 0) // HEAD_DIM
                 == lax.broadcasted_iota(jnp.int32, (LANES, LANES), 1) // HEAD_DIM)
    inv_d = 1.0 / HEAD_DIM
    starts_seq = (s - 1) % tiles_per_seq == 0

    def half_mean(t):
        lo = jnp.sum(jnp.where(half, t, 0.0), axis=-1, keepdims=True)
        hi = jnp.sum(jnp.where(half, 0.0, t), axis=-1, keepdims=True)
        return jnp.where(half, lo, hi) * inv_d

    n_pairs = RET_WIDTH // LANES
    n_chunks = tm // C
    state = [jnp.where(starts_seq, 0.0, sbd_ref[c]) for c in range(n_pairs)]

    def retention_unit(c, cc):
        ls = slice(c * LANES, (c + 1) * LANES)
        rs = slice(cc * C, (cc + 1) * C)
        S = state[c]
        qb = qkv_ref[prv, 0, rs, ls]
        kb = qkv_ref[prv, 1, rs, ls]
        vb = qkv_ref[prv, 2, rs, ls]
        q = qb.astype(F32)
        q_pair = jnp.concatenate([jnp.where(half, q, 0.0), jnp.where(half, 0.0, q)], axis=0).astype(BF16)
        inner = _dot_nt(q_pair, kb) * dec_ref[c]
        cross = _dot(qb, S.astype(BF16)) * cd_ref[:, ls]
        kd = (kb.astype(F32) * kd_ref[:, ls]).astype(BF16)
        state[c] = S * sd_ref[:, ls] + jnp.where(same_head, _dot_tn(kd, vb), 0.0)
        yield
        o2 = _dot(inner.astype(BF16), vb)
        o = jnp.where(half, o2[:C], o2[C:]) + cross
        yield
        mu = half_mean(o)
        d = o - mu
        var = half_mean(d * d)
        on = d * lax.rsqrt(var + GN_EPS) * gr_ref[:, ls]
        gate = gate_ref[prv, rs, ls]
        ro_ref[0, rs, ls] = (on * (gate * jax.nn.sigmoid(gate))).astype(ro_ref.dtype)
        yield

    def retention_pieces():
        for cc in range(n_chunks):
            units = [retention_unit(c, cc) for c in range(n_pairs)]
            for _ in range(3):
                for u in units:
                    next(u)
                    yield

    j = jnp.minimum(s, n_tiles - 1) % tiles_per_seq
    rows = pl.ds(pl.multiple_of(j * tm, tm), tm)
    steps = _proj_steps(x_ref[0], g_ref, w_ref, cos_ref[rows, :], sa_ref[rows, :], sb_ref[rows, :])
    pieces = retention_pieces()
    n_pieces = 3 * n_chunks * n_pairs
    cols = []
    for i, step in enumerate(steps):
        cols.append(step())
        for _ in range((i + 1) * n_pieces // len(steps) - i * n_pieces // len(steps)):
            next(pieces)
    for c in range(n_pairs):
        S = state[c]
        sbd_ref[c] = S
        s_out_ref[0, 2 * c] = S[:HEAD_DIM, :HEAD_DIM]
        s_out_ref[0, 2 * c + 1] = S[HEAD_DIM:, HEAD_DIM:]
    rq, rk, rv, rg, aq, ak, av = _join_groups(cols)
    aq_ref[0] = aq.astype(aq_ref.dtype)
    ak_ref[0] = ak
    av_ref[0] = av
    qkv_ref[cur, 0] = rq.astype(BF16)
    qkv_ref[cur, 1] = rk.astype(BF16)
    qkv_ref[cur, 2] = rv.astype(BF16)
    gate_ref[cur] = rg


def _proj_retention(x, g, w_bf, g_ret, tm, later_weights):
    B, T, D = x.shape
    R, A, H, Dh = RET_WIDTH, ATT_WIDTH, N_RET_HEADS, HEAD_DIM
    tps = T // tm
    n_tiles = B * tps
    tabs = _rope_tables(jnp.arange(T, dtype=jnp.int32))
    dec, cd, kd, sd = _retention_consts(RET_CHUNK)
    dec_pair = dec.reshape(H // 2, 2 * RET_CHUNK, RET_CHUNK)

    def this_tile(width):
        def index(s):
            t = jnp.minimum(s, n_tiles - 1)
            return (t // tps, t % tps, 0)
        return pl.BlockSpec((1, tm, width), index)

    def prev_tile(s):
        t = jnp.maximum(s - 1, 0)
        return (t // tps, t % tps, 0)

    def slab(w):
        rows = max(16, -(-w.shape[0] // n_tiles) // 16 * 16)
        while w.shape[0] % rows:
            rows += 16
        last = w.shape[0] // rows - 1
        return pl.BlockSpec((rows, w.shape[1]), lambda s: (jnp.minimum(s, last), 0))

    full = lambda shape: pl.BlockSpec(shape, lambda s: (0,) * len(shape))
    consts = [g.reshape(1, D), w_bf, *tabs, dec_pair, cd, kd, sd, g_ret.reshape(1, R)]
    slabs = [slab(w) for w in later_weights]
    return pl.pallas_call(
        functools.partial(_proj_ret_kernel, tps),
        grid=(n_tiles + 1,),
        in_specs=[this_tile(D)] + [full(t.shape) for t in consts] + slabs,
        out_specs=[pl.BlockSpec((1, tm, R), prev_tile),
                   pl.BlockSpec((1, H, Dh, Dh), lambda s: (jnp.maximum(s - 1, 0) // tps, 0, 0, 0)),
                   this_tile(A), this_tile(A), this_tile(A)] + slabs,
        out_shape=[jax.ShapeDtypeStruct((B, T, R), BF16), jax.ShapeDtypeStruct((B, H, Dh, Dh), F32),
                   jax.ShapeDtypeStruct((B, T, A), BF16), jax.ShapeDtypeStruct((B, T, A), F32),
                   jax.ShapeDtypeStruct((B, T, A), F32)]
                  + [jax.ShapeDtypeStruct(w.shape, BF16) for w in later_weights],
        scratch_shapes=[pltpu.VMEM((R // LANES, LANES, LANES), F32), pltpu.VMEM((2, 3, tm, R), BF16),
                        pltpu.VMEM((2, tm, R), F32)],
        compiler_params=_cparams("proj_retention", "arbitrary"),
        name="proj_retention",
    )(x, *consts, *later_weights)


def _ret_kernel(q_ref, k_ref, v_ref, gate_ref, dec_ref, cd_ref, kd_ref, sd_ref, gr_ref, s0_ref, o_ref, s_ref):
    def unit(b, h):
        sl = slice(h * HEAD_DIM, (h + 1) * HEAD_DIM)
        qh = q_ref[b, :, sl].astype(BF16)
        kh = k_ref[b, :, sl].astype(BF16)
        vh = v_ref[b, :, sl].astype(BF16)
        S = s0_ref[b, h]
        inner = _dot_nt(qh, kh) * dec_ref[h]
        cross = _dot(qh, S.astype(BF16)) * cd_ref[:, sl]
        kd = (kh.astype(F32) * kd_ref[:, sl]).astype(BF16)
        s_ref[b, h] = S * sd_ref[:, sl] + _dot_tn(kd, vh)
        yield
        o = _dot(inner.astype(BF16), vh) + cross
        yield
        mu = jnp.mean(o, axis=-1, keepdims=True)
        var = jnp.mean(jnp.square(o - mu), axis=-1, keepdims=True)
        on = (o - mu) * lax.rsqrt(var + GN_EPS) * gr_ref[:, sl]
        gate = gate_ref[b, :, sl]
        o_ref[b, :, sl] = on * (gate * jax.nn.sigmoid(gate))
        yield

    for b in range(q_ref.shape[0]):
        units = [unit(b, h) for h in range(N_RET_HEADS)]
        for _ in range(3):
            for u in units:
                next(u)


def _retention_step(rq, rk, rv, rg, g_ret, state):
    B, T, R = rq.shape
    H, Dh = N_RET_HEADS, HEAD_DIM
    consts = [*_retention_consts(T), g_ret.reshape(1, R)]
    bb = math.gcd(B, 8)
    row = pl.BlockSpec((bb, T, R), lambda b: (b, 0, 0))
    full = lambda shape: pl.BlockSpec(shape, lambda b: (0,) * len(shape))
    st = pl.BlockSpec((bb, H, Dh, Dh), lambda b: (b, 0, 0, 0))
    return pl.pallas_call(
        _ret_kernel,
        grid=(B // bb,),
        in_specs=[row, row, row, row] + [full(t.shape) for t in consts] + [st],
        out_specs=[row, st],
        out_shape=[jax.ShapeDtypeStruct((B, T, R), F32), jax.ShapeDtypeStruct((B, H, Dh, Dh), F32)],
        compiler_params=_cparams("retention", "arbitrary"),
        name="retention",
    )(rq, rk, rv, rg, *consts, state)


def _split3_dot(c, e):
    c1 = c.astype(BF16)
    r1 = c - c1.astype(F32)
    c2 = r1.astype(BF16)
    c3 = (r1 - c2.astype(F32)).astype(BF16)
    return _dot(c1, e) + _dot(c2, e) + _dot(c3, e)


def _chunk_rows(ref, stride):
    def load(start, c):
        idx = pl.ds(start, ATT_BLOCK) if stride == 1 else pl.ds(start, ATT_BLOCK, stride=stride)
        return ref[c, idx, :]
    return load


def _token_rows(ref):
    def load(start, c):
        return ref[0, pl.ds(start, ATT_BLOCK), c * LANES:(c + 1) * LANES].astype(F32)
    return load


def _att_pieces(load_q, load_k, load_v, bias_ref, cur, prev, first, out):
    BLK, H = ATT_BLOCK, N_ATT_HEADS
    half = lax.broadcasted_iota(jnp.int32, (BLK, LANES), 1) < HEAD_DIM
    scores = []
    for c in range(ATT_CHUNKS):
        q = load_q(cur, c)
        q_pair = jnp.concatenate([jnp.where(half, q, 0.0), jnp.where(half, 0.0, q)], axis=0).astype(BF16)
        k = load_k(cur, c)
        if prev is not None:
            k = jnp.concatenate([load_k(prev, c), k], axis=0)
        scores.append(_dot_nt(k.astype(BF16), q_pair))
        yield
    stats = []
    for c in range(ATT_CHUNKS):
        s = scores[c] + bias_ref[:, 2 * c * BLK:(2 * c + 2) * BLK]
        if prev is not None:
            key = lax.broadcasted_iota(jnp.int32, s.shape, 0)
            s = jnp.where(jnp.logical_and(first, key < BLK), NEG_INF, s)
        m = jnp.max(s, axis=0, keepdims=True)
        p = jnp.exp(s - m)
        stats.append((m, jnp.sum(p, axis=0, keepdims=True), p.astype(BF16)))
        yield
    accs = []
    upper = lax.broadcasted_iota(jnp.int32, (LANES, BLK), 0) < HEAD_DIM
    for c in range(ATT_CHUNKS):
        v = load_v(cur, c)
        if prev is not None:
            v = jnp.concatenate([load_v(prev, c), v], axis=0)
        o = _dot_tn(v.astype(BF16), stats[c][2])
        accs.append(jnp.where(upper, o[:, :BLK], o[:, BLK:]).T)
        yield
    rows = [stats[h // 2][0][:, (h % 2) * BLK:(h % 2 + 1) * BLK] for h in range(H)]
    rows += [stats[h // 2][1][:, (h % 2) * BLK:(h % 2 + 1) * BLK] for h in range(H)]
    rows.append(jnp.zeros((LANES - 2 * H, BLK), F32))
    out.append((accs, jnp.concatenate(rows, axis=0).T))
    yield


def _att_pipeline(blocks, after=None):
    outs = [[] for _ in blocks]
    gens = [_att_pieces(*b, o) for b, o in zip(blocks, outs)]
    n = ATT_CHUNKS
    total = 3 * n + 1
    tails = []
    for t in range((len(blocks) - 1) * n + total):
        for tail in tails:
            next(tail, None)
        for k, g in enumerate(gens):
            if 0 <= t - k * n < total:
                next(g)
                if t - k * n == total - 1 and after is not None:
                    tails.append(after(k, *outs[k][0]))
    for tail in tails:
        for _ in tail:
            pass
    return [o[0] for o in outs]


def _att_kernel(q_ref, k_ref, v_ref, bv_ref, e_ref, ao_ref, b1_ref, b4_ref, b16_ref, q4, k4, v4, stage,
                acc4, ml4, acc16, ml16):
    BLK, H = ATT_BLOCK, N_ATT_HEADS
    S = q_ref.shape[1]
    nblk = S // BLK
    d4, d16 = DIL_PATTERNS[1][1], DIL_PATTERNS[2][1]
    nb4 = nblk // d4
    per_res = S // d4

    @pl.when(pl.program_id(0) == 0)
    def _():
        for g, tab in enumerate((b1_ref, b4_ref, b16_ref)):
            for h in range(H):
                rows = jnp.broadcast_to(bv_ref[g, h:h + 1, :], (2 * BLK, bv_ref.shape[2]))
                skew = pltpu.roll(rows, 0, 1, stride=1, stride_axis=0)
                tab[:, h * BLK:(h + 1) * BLK] = skew[2 * BLK - tab.shape[0]:, :BLK]

    for src, dst in ((q_ref, q4), (k_ref, k4), (v_ref, v4)):
        for c in range(ATT_CHUNKS):
            stage[...] = src[0, :, c * LANES:(c + 1) * LANES].astype(F32)
            for r in range(d4):
                dst[c, r * per_res:(r + 1) * per_res, :] = stage[pl.ds(r, per_res, stride=d4), :]
    contiguous = [_chunk_rows(t, 1) for t in (q4, k4, v4)]
    every_4th = [_chunk_rows(t, d16 // d4) for t in (q4, k4, v4)]
    tokens = [_token_rows(t) for t in (q_ref, k_ref, v_ref)]

    def dilated(i, carry):
        blocks, toks = [], []
        for n in (2 * i, 2 * i + 1):
            r, j = n // nb4, n % nb4
            cur = r * per_res + j * BLK
            prev = r * per_res + jnp.maximum(j - 1, 0) * BLK
            blocks.append((*contiguous, b4_ref, cur, prev, j == 0))
            toks.append((r + j * (BLK * d4), d4, acc4, ml4))
            blocks.append((*every_4th, b16_ref, (n % d4) * per_res + n // d4, None, None))
            toks.append((n, d16, acc16, ml16))
        for (accs, ml), (tok, dil, acc_ref, ml_ref) in zip(_att_pipeline(blocks), toks):
            for c in range(ATT_CHUNKS):
                acc_ref[c, pl.ds(tok, BLK, stride=dil), :] = accs[c]
            ml_ref[pl.ds(tok, BLK, stride=dil), :] = ml
        return carry

    def merge(cur, accs1, ml1):
        rows = pl.ds(cur, BLK)
        mls = [ml1, ml4[rows, :], ml16[rows, :]]
        accs = [accs1, [acc4[c, rows, :] for c in range(ATT_CHUNKS)], [acc16[c, rows, :] for c in range(ATT_CHUNKS)]]
        lane = lax.broadcasted_iota(jnp.int32, (BLK, LANES), 1)
        m_all = jnp.maximum(jnp.maximum(mls[0], mls[1]), mls[2])
        ws = [jnp.exp(ml - m_all) for ml in mls]
        den = sum(w * pltpu.roll(ml, LANES - H, 1) for w, ml in zip(ws, mls))
        ao = [jnp.zeros((BLK, LANES), F32) for _ in range(ATT_CHUNKS)]
        yield
        for w, acc in zip(ws, accs):
            coef = _split3_dot(jnp.where(lane < H, w / den, 0.0), e_ref[...])
            for c in range(ATT_CHUNKS):
                ao[c] = ao[c] + coef[:, c * LANES:(c + 1) * LANES] * acc[c]
            yield
        ao_ref[0, rows, :] = jnp.concatenate(ao, axis=-1).astype(ao_ref.dtype)
        yield

    def dense_and_merge(i, carry):
        per_trip = 4
        starts = [pl.multiple_of((per_trip * i + k) * BLK, BLK) for k in range(per_trip)]
        before = pl.multiple_of(jnp.maximum(per_trip * i - 1, 0) * BLK, BLK)
        blocks = [(*tokens, b1_ref, cur, prev, first)
                  for cur, prev, first in zip(starts, [before] + starts[:-1], [i == 0] + [False] * (per_trip - 1))]
        _att_pipeline(blocks, lambda k, accs, ml: merge(starts[k], accs, ml))
        return carry

    lax.fori_loop(0, nblk // 2, dilated, 0)
    lax.fori_loop(0, nblk // 4, dense_and_merge, 0)


def _prompt_attention(aq, ak, av, rel_bias):
    B, S, A = aq.shape
    CH = ATT_CHUNKS
    assert [d for _, d in DIL_PATTERNS] == [1, 4, 16] and S // ATT_BLOCK == 16
    bias_vectors = _prompt_bias_vectors(rel_bias)
    head_of_lane = jnp.arange(A)[None, :] // HEAD_DIM
    expand = (jnp.arange(LANES)[:, None] == head_of_lane).astype(BF16)
    seq = pl.BlockSpec((1, S, A), lambda b: (b, 0, 0))
    const = lambda shape: pl.BlockSpec(shape, lambda b: (0,) * len(shape), pipeline_mode=pl.Buffered(1))
    chunked = pltpu.VMEM((CH, S, LANES), F32)
    stats = pltpu.VMEM((S, LANES), F32)
    table = lambda keys: pltpu.VMEM((keys, N_ATT_HEADS * ATT_BLOCK), F32)
    return pl.pallas_call(
        _att_kernel,
        grid=(B,),
        in_specs=[seq, seq, seq, const(bias_vectors.shape), const(expand.shape)],
        out_specs=seq,
        out_shape=jax.ShapeDtypeStruct((B, S, A), BF16),
        scratch_shapes=[table(2 * ATT_BLOCK), table(2 * ATT_BLOCK), table(ATT_BLOCK),
                        chunked, chunked, chunked, stats, chunked, stats, chunked, stats],
        compiler_params=_cparams("prompt_att", "arbitrary"),
        name="prompt_att",
    )(aq, ak, av, bias_vectors, expand)


def _window_pieces(lows, T, kT_ref, vT_ref, qn_ref, kn_ref, vn_ref, c1_ref, c4_ref, c16_ref,
                   n1_ref, n4_ref, n16_ref, koT_ref, voT_ref, ao_ref):
    H, Dh, L = kT_ref.shape[1:]
    A = H * Dh
    zpad = jnp.zeros((LANES - T, A), F32)
    tail_lane = lax.broadcasted_iota(jnp.int32, (Dh, LANES), 1) >= LANES - T
    shifted = {}

    def shift_in(name, xT_ref, new_ref, out_ref):
        new_p = jnp.concatenate([zpad, new_ref[0]], axis=0)
        new_t = new_p.T
        heads = []
        for h in range(H):
            x = xT_ref[0, h]
            rolled = pltpu.roll(x, L - T, 1)
            tail = jnp.where(tail_lane, new_t[h * Dh:(h + 1) * Dh], rolled[:, L - LANES:])
            out_ref[0, h] = jnp.concatenate([rolled[:, :L - LANES], tail], axis=1)
            heads.append(x.astype(BF16))
            yield
        shifted[name] = (jnp.concatenate(heads, axis=0), new_p.astype(BF16))

    yield from shift_in("k", kT_ref, kn_ref, koT_ref)
    yield from shift_in("v", vT_ref, vn_ref, voT_ref)
    (kT, kn), (vT, vn) = shifted["k"], shifted["v"]
    HT = H * T
    row_head = lax.broadcasted_iota(jnp.int32, (HT, A), 0) // T
    lane_head = lax.broadcasted_iota(jnp.int32, (HT, A), 1) // HEAD_DIM
    diag = row_head == lane_head
    q_rows = jnp.where(diag, jnp.concatenate([qn_ref[0]] * H, axis=0), 0.0).astype(BF16)
    s_cache = _dot(q_rows, kT)
    s_new = _dot_nt(q_rows, kn)
    yield
    soft = []
    for lo, c_ref, n_ref in zip(lows, (c1_ref, c4_ref, c16_ref), (n1_ref, n4_ref, n16_ref)):
        sc = s_cache[:, lo:] + c_ref[...]
        sn = s_new + n_ref[...]
        m = jnp.maximum(jnp.max(sc, axis=-1, keepdims=True), jnp.max(sn, axis=-1, keepdims=True))
        pc = jnp.exp(sc - m)
        pn = jnp.exp(sn - m)
        l = jnp.sum(pc, axis=-1, keepdims=True) + jnp.sum(pn, axis=-1, keepdims=True)
        soft.append((m, l, pc.astype(BF16), pn.astype(BF16)))
        yield
    stats = []
    for lo, (m, l, pc, pn) in zip(lows, soft):
        stats.append((m, l, _dot_nt(pc, vT[:, lo:]) + _dot(pn, vn)))
        yield
    m_all = jnp.maximum(jnp.maximum(stats[0][0], stats[1][0]), stats[2][0])
    num = jnp.zeros((HT, A), F32)
    den = jnp.zeros((HT, 1), F32)
    for m, l, acc in stats:
        w = jnp.exp(m - m_all)
        num = num + w * acc
        den = den + w * l
    comb = jnp.where(diag, num / den, 0.0)
    out = comb[0:T]
    for h in range(1, H):
        out = out + comb[h * T:(h + 1) * T]
    ao_ref[0] = out
    yield


def _sample_window_operands(aq, ak, av, cache_kT, cache_vT, rel_bias, groups):
    B, T, A = aq.shape
    _, H, Dh, L = cache_kT.shape
    Hg, Ag = H // groups, A // groups
    lows = tuple(L - win for win, _ in DIL_PATTERNS)
    assert all(lo >= 0 and lo % LANES == 0 for lo in lows) and T <= LANES and Ag % LANES == 0
    tabs = _sample_bias_tables(rel_bias, L, T)
    big = pl.BlockSpec((1, Hg, Dh, L), lambda i: (i // groups, i % groups, 0, 0))
    small = pl.BlockSpec((1, T, Ag), lambda i: (i // groups, 0, i % groups))
    tab = lambda t: pl.BlockSpec((Hg * T, t.shape[1]), lambda i: (i % groups, 0))
    win_sds = jax.ShapeDtypeStruct((B, H, Dh, L), F32)
    args = [cache_kT, cache_vT, aq, ak, av, *tabs]
    in_specs = [big, big, small, small, small] + [tab(t) for t in tabs]
    out_specs = [big, big, small]
    out_shape = [win_sds, win_sds, jax.ShapeDtypeStruct((B, T, A), F32)]
    return functools.partial(_window_pieces, lows, T), args, in_specs, out_specs, out_shape


def _ffn_pieces(d_ff, ff_chunk, x_ref, ro_ref, ao_ref, wo_ref, wgu_ref, wd_ref, g_pm, g_pf, g_of, y_ref, act_ref):
    R = RET_WIDTH
    mix = _dot(ro_ref[...].astype(BF16), wo_ref[:R, :]) + _dot(ao_ref[...].astype(BF16), wo_ref[R:, :])
    x1 = x_ref[...] + _rms(mix, g_pm[...])
    h = _rms(x1, g_pf[...]).astype(BF16)
    yield
    for c in range(0, d_ff, ff_chunk):
        gate = _dot(h, wgu_ref[:, c:c + ff_chunk])
        up = _dot(h, wgu_ref[:, d_ff + c:d_ff + c + ff_chunk])
        act_ref[:, c:c + ff_chunk] = (gate * jax.nn.sigmoid(gate) * up).astype(BF16)
        yield
    f = _dot(act_ref[...], wd_ref[...])
    y_ref[...] = x1 + _rms(f, g_of[...])
    yield


def _out_kernel(d_ff, ff_chunk, *refs):
    for _ in _ffn_pieces(d_ff, ff_chunk, *refs):
        pass


def _out_window_kernel(d_ff, ff_chunk, window, n_ffn_in, n_win_in, *refs):
    ffn_in = refs[:n_ffn_in]
    win_in = refs[n_ffn_in:n_ffn_in + n_win_in]
    y_ref, koT_ref, voT_ref, sao_ref, act_ref = refs[n_ffn_in + n_win_in:]
    win = window(*win_in, koT_ref, voT_ref, sao_ref)
    for _ in _ffn_pieces(d_ff, ff_chunk, *ffn_in, y_ref, act_ref):
        next(win, None)
        next(win, None)
    for _ in win:
        pass


def _out_block(x, ro, ao, wo_bf, wgu_bf, wd_bf, g_post_mix, g_pre_ffn, g_post_ffn, tm, window=None):
    N, D = x.shape
    d_ff = wd_bf.shape[0]
    row = lambda width: pl.BlockSpec((tm, width), lambda i: (i, 0))
    const = lambda shape: pl.BlockSpec(shape, lambda i: (0,) * len(shape), pipeline_mode=pl.Buffered(1))
    args = [x, ro, ao, wo_bf, wgu_bf, wd_bf, g_post_mix.reshape(1, D), g_pre_ffn.reshape(1, D),
            g_post_ffn.reshape(1, D)]
    in_specs = [row(D), row(RET_WIDTH), row(ATT_WIDTH), const(wo_bf.shape), const(wgu_bf.shape),
                const(wd_bf.shape), const((1, D)), const((1, D)), const((1, D))]
    body = functools.partial(_out_kernel, d_ff, FF_CHUNK)
    out_specs, out_shape = [row(D)], [jax.ShapeDtypeStruct((N, D), F32)]
    if window is not None:
        win_body, win_args, win_in_specs, win_out_specs, win_out_shape = window
        assert N // tm == win_out_shape[0].shape[0] * (win_out_shape[0].shape[1] // win_out_specs[0].block_shape[1])
        body = functools.partial(_out_window_kernel, d_ff, FF_CHUNK, win_body, len(args), len(win_args))
        args, in_specs = args + win_args, in_specs + win_in_specs
        out_specs, out_shape = out_specs + win_out_specs, out_shape + win_out_shape
    outs = pl.pallas_call(
        body,
        grid=(N // tm,),
        in_specs=in_specs,
        out_specs=out_specs,
        out_shape=out_shape,
        scratch_shapes=[pltpu.VMEM((tm, d_ff), BF16)],
        compiler_params=_cparams("out_ffn", "arbitrary"),
        name="out_ffn",
    )(*args)
    return outs[0] if window is None else outs


def kernel(x_prompt, x_sample, state_ret, cache_k_win, cache_v_win, rel_bias, w_in, g_ret, w_out,
           g_pre_mix, g_post_mix, g_pre_ffn, g_post_ffn, w_gu, w_down):
    depth = w_in.shape[0]
    assert depth == 1
    B, S, D = x_prompt.shape
    Bs, Ts, _ = x_sample.shape
    H, Dh = N_ATT_HEADS, HEAD_DIM
    l = 0
    w_in_bf = w_in[l].astype(BF16)

    pos_s = PAST_LEN + jnp.arange(Ts, dtype=jnp.int32)
    N_s = Bs * Ts
    outs = _proj(x_sample.reshape(N_s, D), g_pre_mix[l], w_in_bf, jnp.tile(pos_s, Bs))
    srq, srk, srv, srg, saq, sak, sav = [t.reshape(Bs, Ts, RET_WIDTH) for t in outs]
    sro, s_s = _retention_step(srq, srk, srv, srg, g_ret[l], state_ret[l])
    to_t = lambda t: jnp.transpose(t, (0, 2, 3, 1))
    from_t = lambda t: jnp.transpose(t, (0, 3, 1, 2))
    window = _sample_window_operands(saq, sak, sav, to_t(cache_k_win[l]), to_t(cache_v_win[l]), rel_bias,
                                     WINDOW_HEAD_GROUPS)

    ro, s_p, aq, ak, av, wo_bf, wgu_bf, wd_bf = _proj_retention(
        x_prompt, g_pre_mix[l], w_in_bf, g_ret[l], PROJ_TILE, (w_out[l], w_gu[l], w_down[l]))
    ao = _prompt_attention(aq, ak, av, rel_bias)
    y_p, k_sT, v_sT, sao = _out_block(x_prompt.reshape(B * S, D), ro.reshape(B * S, RET_WIDTH),
                                      ao.reshape(B * S, ATT_WIDTH), wo_bf, wgu_bf, wd_bf, g_post_mix[l],
                                      g_pre_ffn[l], g_post_ffn[l], FFN_TILE, window)
    k_s, v_s = from_t(k_sT), from_t(v_sT)

    y_s = _out_block(x_sample.reshape(N_s, D), sro.reshape(N_s, RET_WIDTH), sao.reshape(N_s, ATT_WIDTH),
                     wo_bf, wgu_bf, wd_bf, g_post_mix[l], g_pre_ffn[l], g_post_ffn[l], N_s)

    return (y_p.reshape(B, S, D), y_s.reshape(Bs, Ts, D),
            s_p[None], ak.reshape(1, B, S, H, Dh), av.reshape(1, B, S, H, Dh),
            s_s[None], k_s[None], v_s[None])
```

```python
import functools
import math

import jax
import jax.numpy as jnp
from jax import lax
from jax.experimental import pallas as pl
from jax.experimental.pallas import tpu as pltpu

F32 = jnp.float32
BF16 = jnp.bfloat16

HEAD_DIM = 64
N_RET_HEADS = 8
N_ATT_HEADS = 8
RET_WIDTH = N_RET_HEADS * HEAD_DIM
ATT_WIDTH = N_ATT_HEADS * HEAD_DIM
RET_CHUNK = 128
DIL_PATTERNS = ((128, 1), (512, 4), (2048, 16))
ATT_BLOCK = 128
N_BUCKETS = 32
MAX_DISTANCE = 2048
ROPE_BASE = 10000.0
NORM_EPS = 1e-6
GN_EPS = 1e-5
PAST_LEN = 16384
LANES = 128
ATT_CHUNKS = ATT_WIDTH // LANES
NEG_INF = float("-inf")

PROJ_TILE = 512
FFN_TILE = 256
FF_CHUNK = 256
WINDOW_HEAD_GROUPS = 2
VMEM_MB = dict(proj=40, retention=32, proj_retention=56, prompt_att=58, out_ffn=56)


def _cparams(name, *sem):
    return pltpu.CompilerParams(dimension_semantics=sem, vmem_limit_bytes=VMEM_MB[name] * 1024 * 1024)


def _rms(x, g):
    return x * lax.rsqrt(jnp.mean(x * x, axis=-1, keepdims=True) + NORM_EPS) * g


def _dot(a, b):
    return jnp.dot(a, b, preferred_element_type=F32)


def _dot_nt(a, b):
    return lax.dot_general(a, b, (((1,), (1,)), ((), ())), preferred_element_type=F32)


def _dot_tn(a, b):
    return lax.dot_general(a, b, (((0,), (0,)), ((), ())), preferred_element_type=F32)


def _rope_tables(pos):
    inv = ROPE_BASE ** (-jnp.arange(0, HEAD_DIM, 2, dtype=F32) / HEAD_DIM)
    ang = pos.astype(F32)[:, None] * inv[None, :]
    cos, sin = jnp.cos(ang), jnp.sin(ang)
    zero = jnp.zeros_like(sin)
    rep = LANES // HEAD_DIM
    cosf = jnp.tile(jnp.concatenate([cos, cos], axis=-1), (1, rep))
    sa = jnp.tile(jnp.concatenate([-sin, zero], axis=-1), (1, rep))
    sb = jnp.tile(jnp.concatenate([zero, sin], axis=-1), (1, rep))
    return cosf, sa, sb


def _retention_consts(C):
    log_g = jnp.log1p(-jnp.exp2(-5.0 - jnp.arange(N_RET_HEADS, dtype=F32)))
    i = jnp.arange(C, dtype=F32)
    rel = i[:, None] - i[None, :]
    decay = jnp.where(rel[None] >= 0, jnp.exp(jnp.maximum(rel, 0.0)[None] * log_g[:, None, None]), 0.0)
    cross = jnp.exp((i + 1.0)[:, None] * log_g[None, :])
    kdec = jnp.exp((C - 1.0 - i)[:, None] * log_g[None, :])
    sdec = jnp.exp(C * log_g)[None, :]
    expand = lambda t: jnp.repeat(t, HEAD_DIM, axis=-1)
    return decay, expand(cross), expand(kdec), expand(sdec)


def _t5_bucket(dist):
    max_exact = N_BUCKETS // 2
    d_f = jnp.maximum(dist, 1).astype(F32)
    large = max_exact + (jnp.log(d_f / max_exact) / math.log(MAX_DISTANCE / max_exact)
                         * (N_BUCKETS - max_exact)).astype(jnp.int32)
    large = jnp.minimum(large, N_BUCKETS - 1)
    return jnp.where(dist < max_exact, dist, large)


def _pattern_bias_rev(rel_bias, dil, nk):
    dist = jnp.arange(nk, -1, -1, dtype=jnp.int32) * dil
    return rel_bias[_t5_bucket(dist)].astype(F32).T


def _prompt_bias_vectors(rel_bias):
    BLK, H = ATT_BLOCK, N_ATT_HEADS
    out = []
    for win, dil in DIL_PATTERNS:
        nk = win // dil
        assert nk == BLK
        v = jnp.concatenate([_pattern_bias_rev(rel_bias, dil, nk),
                             jnp.full((H, 3 * BLK - nk - 1), NEG_INF, F32)], axis=1)
        out.append(jnp.roll(v[:, ::-1], 1, axis=1))
    return jnp.stack(out)


def _sample_bias_tables(rel_bias, L, T):
    H = N_ATT_HEADS
    cache_part, new_part = [], []
    for win, dil in DIL_PATTERNS:
        nk = win // dil
        rev = _pattern_bias_rev(rel_bias, dil, nk)
        if dil > 1:
            gaps = jnp.full((H, nk + 1, dil - 1), NEG_INF, F32)
            rev = jnp.concatenate([rev[:, :, None], gaps], axis=2).reshape(H, (nk + 1) * dil)[:, :nk * dil + 1]
        pv = jnp.concatenate([jnp.full((H, T), NEG_INF, F32), rev, jnp.full((H, 2 * T), NEG_INF, F32)], axis=1)
        width = win + T
        rows = jnp.stack([pv[:, T - t:T - t + width] for t in range(T)], axis=1)
        cache_part.append(rows[:, :, :win].reshape(H * T, win))
        new = jnp.concatenate([jnp.full((H, T, LANES - T), NEG_INF, F32), rows[:, :, win:]], axis=2)
        new_part.append(new.reshape(H * T, LANES))
    return cache_part + new_part


def _proj_steps(x, g_ref, w_ref, cos, sa, sb):
    h = _rms(x, g_ref[...]).astype(BF16)
    W = 2 * LANES
    rep = W // LANES
    cosf = jnp.concatenate([cos] * rep, axis=-1)
    saf = jnp.concatenate([sa] * rep, axis=-1)
    sbf = jnp.concatenate([sb] * rep, axis=-1)
    half = HEAD_DIM // 2

    def col(c):
        return _dot(h, w_ref[:, c:c + W])

    def rope(z):
        return z * cosf + pltpu.roll(z, W - half, 1) * saf + pltpu.roll(z, half, 1) * sbf

    R, A = RET_WIDTH, ATT_WIDTH
    scale = HEAD_DIM ** -0.5
    steps = []
    for c in range(0, R, W):
        steps.append(lambda c=c: rope(col(c)))
    for c in range(R, 2 * R, W):
        steps.append(lambda c=c: rope(col(c)) * scale)
    for c in range(2 * R, 4 * R, W):
        steps.append(lambda c=c: col(c))
    for c in range(4 * R, 4 * R + A, W):
        steps.append(lambda c=c: col(c) * scale)
    for c in range(4 * R + A, 4 * R + 3 * A, W):
        steps.append(lambda c=c: col(c))
    return steps


def _join_groups(pieces):
    return [jnp.concatenate(pieces[i:i + 2], axis=-1) for i in range(0, len(pieces), 2)]


def _proj_kernel(x_ref, g_ref, w_ref, cos_ref, sa_ref, sb_ref, *outs):
    steps = _proj_steps(x_ref[...], g_ref, w_ref, cos_ref[...], sa_ref[...], sb_ref[...])
    for ref, z in zip(outs, _join_groups([step() for step in steps])):
        ref[...] = z


def _proj(x, g, w_bf, pos):
    N, D = x.shape
    tabs = _rope_tables(pos)
    full = lambda shape: pl.BlockSpec(shape, lambda i: (0,) * len(shape))
    return pl.pallas_call(
        _proj_kernel,
        grid=(1,),
        in_specs=[full((N, D)), full((1, D)), full(w_bf.shape)] + [full(t.shape) for t in tabs],
        out_specs=[full((N, RET_WIDTH))] * 7,
        out_shape=[jax.ShapeDtypeStruct((N, RET_WIDTH), F32)] * 7,
        compiler_params=_cparams("proj", "arbitrary"),
        name="proj",
    )(x, g.reshape(1, D), w_bf, *tabs)


def _proj_ret_kernel(tiles_per_seq, x_ref, g_ref, w_ref, cos_ref, sa_ref, sb_ref, dec_ref, cd_ref, kd_ref,
                     sd_ref, gr_ref, wa_ref, wb_ref, wc_ref, ro_ref, s_out_ref, aq_ref, ak_ref, av_ref,
                     wa_bf_ref, wb_bf_ref, wc_bf_ref, sbd_ref, qkv_ref, gate_ref):
    s = pl.program_id(0)
    n_tiles = pl.num_programs(0) - 1
    tm = x_ref.shape[1]
    C = RET_CHUNK
    cur = s % 2
    prv = 1 - cur

    @pl.when(s == 0)
    def _():
        sbd_ref[...] = jnp.zeros_like(sbd_ref)
        qkv_ref[1] = jnp.zeros(qkv_ref.shape[1:], qkv_ref.dtype)
        gate_ref[1] = jnp.zeros(gate_ref.shape[1:], gate_ref.dtype)

    for w32, w16 in ((wa_ref, wa_bf_ref), (wb_ref, wb_bf_ref), (wc_ref, wc_bf_ref)):
        w16[...] = w32[...].astype(w16.dtype)

    lane = lax.broadcasted_iota(jnp.int32, (C, LANES), 1)
    half = lane < HEAD_DIM
    same_head = (lax.broadcasted_iota(jnp.int32, (LANES, LANES), 0) // HEAD_DIM
                 == lax.broadcasted_iota(jnp.int32, (LANES, LANES), 1) // HEAD_DIM)
    inv_d = 1.0 / HEAD_DIM
    starts_seq = (s - 1) % tiles_per_seq == 0

    def half_mean(t):
        lo = jnp.sum(jnp.where(half, t, 0.0), axis=-1, keepdims=True)
        hi = jnp.sum(jnp.where(half, 0.0, t), axis=-1, keepdims=True)
        return jnp.where(half, lo, hi) * inv_d

    n_pairs = RET_WIDTH // LANES
    n_chunks = tm // C
    state = [jnp.where(starts_seq, 0.0, sbd_ref[c]) for c in range(n_pairs)]

    def retention_unit(c, cc):
        ls = slice(c * LANES, (c + 1) * LANES)
        rs = slice(cc * C, (cc + 1) * C)
        S = state[c]
        qb = qkv_ref[prv, 0, rs, ls]
        kb = qkv_ref[prv, 1, rs, ls]
        vb = qkv_ref[prv, 2, rs, ls]
        q = qb.astype(F32)
        q_pair = jnp.concatenate([jnp.where(half, q, 0.0), jnp.where(half, 0.0, q)], axis=0).astype(BF16)
        inner = _dot_nt(q_pair, kb) * dec_ref[c]
        cross = _dot(qb, S.astype(BF16)) * cd_ref[:, ls]
        kd = (kb.astype(F32) * kd_ref[:, ls]).astype(BF16)
        state[c] = S * sd_ref[:, ls] + jnp.where(same_head, _dot_tn(kd, vb), 0.0)
        yield
        o2 = _dot(inner.astype(BF16), vb)
        o = jnp.where(half, o2[:C], o2[C:]) + cross
        yield
        mu = half_mean(o)
        d = o - mu
        var = half_mean(d * d)
        on = d * lax.rsqrt(var + GN_EPS) * gr_ref[:, ls]
        gate = gate_ref[prv, rs, ls]
        ro_ref[0, rs, ls] = (on * (gate * jax.nn.sigmoid(gate))).astype(ro_ref.dtype)
        yield

    def retention_pieces():
        for cc in range(n_chunks):
            units = [retention_unit(c, cc) for c in range(n_pairs)]
            for _ in range(3):
                for u in units:
                    next(u)
                    yield

    j = jnp.minimum(s, n_tiles - 1) % tiles_per_seq
    rows = pl.ds(pl.multiple_of(j * tm, tm), tm)
    steps = _proj_steps(x_ref[0], g_ref, w_ref, cos_ref[rows, :], sa_ref[rows, :], sb_ref[rows, :])
    pieces = retention_pieces()
    n_pieces = 3 * n_chunks * n_pairs
    cols = []
    for i, step in enumerate(steps):
        cols.append(step())
        for _ in range((i + 1) * n_pieces // len(steps) - i * n_pieces // len(steps)):
            next(pieces)
    for c in range(n_pairs):
        S = state[c]
        sbd_ref[c] = S
        s_out_ref[0, 2 * c] = S[:HEAD_DIM, :HEAD_DIM]
        s_out_ref[0, 2 * c + 1] = S[HEAD_DIM:, HEAD_DIM:]
    rq, rk, rv, rg, aq, ak, av = _join_groups(cols)
    aq_ref[0] = aq.astype(aq_ref.dtype)
    ak_ref[0] = ak
    av_ref[0] = av
    qkv_ref[cur, 0] = rq.astype(BF16)
    qkv_ref[cur, 1] = rk.astype(BF16)
    qkv_ref[cur, 2] = rv.astype(BF16)
    gate_ref[cur] = rg


def _proj_retention(x, g, w_bf, g_ret, tm, later_weights):
    B, T, D = x.shape
    R, A, H, Dh = RET_WIDTH, ATT_WIDTH, N_RET_HEADS, HEAD_DIM
    tps = T // tm
    n_tiles = B * tps
    tabs = _rope_tables(jnp.arange(T, dtype=jnp.int32))
    dec, cd, kd, sd = _retention_consts(RET_CHUNK)
    dec_pair = dec.reshape(H // 2, 2 * RET_CHUNK, RET_CHUNK)

    def this_tile(width):
        def index(s):
            t = jnp.minimum(s, n_tiles - 1)
            return (t // tps, t % tps, 0)
        return pl.BlockSpec((1, tm, width), index)

    def prev_tile(s):
        t = jnp.maximum(s - 1, 0)
        return (t // tps, t % tps, 0)

    def slab(w):
        rows = max(16, -(-w.shape[0] // n_tiles) // 16 * 16)
        while w.shape[0] % rows:
            rows += 16
        last = w.shape[0] // rows - 1
        return pl.BlockSpec((rows, w.shape[1]), lambda s: (jnp.minimum(s, last), 0))

    full = lambda shape: pl.BlockSpec(shape, lambda s: (0,) * len(shape))
    consts = [g.reshape(1, D), w_bf, *tabs, dec_pair, cd, kd, sd, g_ret.reshape(1, R)]
    slabs = [slab(w) for w in later_weights]
    return pl.pallas_call(
        functools.partial(_proj_ret_kernel, tps),
        grid=(n_tiles + 1,),
        in_specs=[this_tile(D)] + [full(t.shape) for t in consts] + slabs,
        out_specs=[pl.BlockSpec((1, tm, R), prev_tile),
                   pl.BlockSpec((1, H, Dh, Dh), lambda s: (jnp.maximum(s - 1, 0) // tps, 0, 0, 0)),
                   this_tile(A), this_tile(A), this_tile(A)] + slabs,
        out_shape=[jax.ShapeDtypeStruct((B, T, R), BF16), jax.ShapeDtypeStruct((B, H, Dh, Dh), F32),
                   jax.ShapeDtypeStruct((B, T, A), BF16), jax.ShapeDtypeStruct((B, T, A), F32),
                   jax.ShapeDtypeStruct((B, T, A), F32)]
                  + [jax.ShapeDtypeStruct(w.shape, BF16) for w in later_weights],
        scratch_shapes=[pltpu.VMEM((R // LANES, LANES, LANES), F32), pltpu.VMEM((2, 3, tm, R), BF16),
                        pltpu.VMEM((2, tm, R), F32)],
        compiler_params=_cparams("proj_retention", "arbitrary"),
        name="proj_retention",
    )(x, *consts, *later_weights)


def _ret_kernel(q_ref, k_ref, v_ref, gate_ref, dec_ref, cd_ref, kd_ref, sd_ref, gr_ref, s0_ref, o_ref, s_ref):
    def unit(b, h):
        sl = slice(h * HEAD_DIM, (h + 1) * HEAD_DIM)
        qh = q_ref[b, :, sl].astype(BF16)
        kh = k_ref[b, :, sl].astype(BF16)
        vh = v_ref[b, :, sl].astype(BF16)
        S = s0_ref[b, h]
        inner = _dot_nt(qh, kh) * dec_ref[h]
        cross = _dot(qh, S.astype(BF16)) * cd_ref[:, sl]
        kd = (kh.astype(F32) * kd_ref[:, sl]).astype(BF16)
        s_ref[b, h] = S * sd_ref[:, sl] + _dot_tn(kd, vh)
        yield
        o = _dot(inner.astype(BF16), vh) + cross
        yield
        mu = jnp.mean(o, axis=-1, keepdims=True)
        var = jnp.mean(jnp.square(o - mu), axis=-1, keepdims=True)
        on = (o - mu) * lax.rsqrt(var + GN_EPS) * gr_ref[:, sl]
        gate = gate_ref[b, :, sl]
        o_ref[b, :, sl] = on * (gate * jax.nn.sigmoid(gate))
        yield

    for b in range(q_ref.shape[0]):
        units = [unit(b, h) for h in range(N_RET_HEADS)]
        for _ in range(3):
            for u in units:
                next(u)


def _retention_step(rq, rk, rv, rg, g_ret, state):
    B, T, R = rq.shape
    H, Dh = N_RET_HEADS, HEAD_DIM
    consts = [*_retention_consts(T), g_ret.reshape(1, R)]
    bb = math.gcd(B, 8)
    row = pl.BlockSpec((bb, T, R), lambda b: (b, 0, 0))
    full = lambda shape: pl.BlockSpec(shape, lambda b: (0,) * len(shape))
    st = pl.BlockSpec((bb, H, Dh, Dh), lambda b: (b, 0, 0, 0))
    return pl.pallas_call(
        _ret_kernel,
        grid=(B // bb,),
        in_specs=[row, row, row, row] + [full(t.shape) for t in consts] + [st],
        out_specs=[row, st],
        out_shape=[jax.ShapeDtypeStruct((B, T, R), F32), jax.ShapeDtypeStruct((B, H, Dh, Dh), F32)],
        compiler_params=_cparams("retention", "arbitrary"),
        name="retention",
    )(rq, rk, rv, rg, *consts, state)


def _split3_dot(c, e):
    c1 = c.astype(BF16)
    r1 = c - c1.astype(F32)
    c2 = r1.astype(BF16)
    c3 = (r1 - c2.astype(F32)).astype(BF16)
    return _dot(c1, e) + _dot(c2, e) + _dot(c3, e)


def _chunk_rows(ref, stride):
    def load(start, c):
        idx = pl.ds(start, ATT_BLOCK) if stride == 1 else pl.ds(start, ATT_BLOCK, stride=stride)
        return ref[c, idx, :]
    return load


def _token_rows(ref):
    def load(start, c):
        return ref[0, pl.ds(start, ATT_BLOCK), c * LANES:(c + 1) * LANES].astype(F32)
    return load


def _att_pieces(load_q, load_k, load_v, bias_ref, cur, prev, first, out):
    BLK, H = ATT_BLOCK, N_ATT_HEADS
    half = lax.broadcasted_iota(jnp.int32, (BLK, LANES), 1) < HEAD_DIM
    scores = []
    for c in range(ATT_CHUNKS):
        q = load_q(cur, c)
        q_pair = jnp.concatenate([jnp.where(half, q, 0.0), jnp.where(half, 0.0, q)], axis=0).astype(BF16)
        k = load_k(cur, c)
        if prev is not None:
            k = jnp.concatenate([load_k(prev, c), k], axis=0)
        scores.append(_dot_nt(k.astype(BF16), q_pair))
        yield
    stats = []
    for c in range(ATT_CHUNKS):
        s = scores[c] + bias_ref[:, 2 * c * BLK:(2 * c + 2) * BLK]
        if prev is not None:
            key = lax.broadcasted_iota(jnp.int32, s.shape, 0)
            s = jnp.where(jnp.logical_and(first, key < BLK), NEG_INF, s)
        m = jnp.max(s, axis=0, keepdims=True)
        p = jnp.exp(s - m)
        stats.append((m, jnp.sum(p, axis=0, keepdims=True), p.astype(BF16)))
        yield
    accs = []
    upper = lax.broadcasted_iota(jnp.int32, (LANES, BLK), 0) < HEAD_DIM
    for c in range(ATT_CHUNKS):
        v = load_v(cur, c)
        if prev is not None:
            v = jnp.concatenate([load_v(prev, c), v], axis=0)
        o = _dot_tn(v.astype(BF16), stats[c][2])
        accs.append(jnp.where(upper, o[:, :BLK], o[:, BLK:]).T)
        yield
    rows = [stats[h // 2][0][:, (h % 2) * BLK:(h % 2 + 1) * BLK] for h in range(H)]
    rows += [stats[h // 2][1][:, (h % 2) * BLK:(h % 2 + 1) * BLK] for h in range(H)]
    rows.append(jnp.zeros((LANES - 2 * H, BLK), F32))
    out.append((accs, jnp.concatenate(rows, axis=0).T))
    yield


def _att_pipeline(blocks, after=None):
    outs = [[] for _ in blocks]
    gens = [_att_pieces(*b, o) for b, o in zip(blocks, outs)]
    n = ATT_CHUNKS
    total = 3 * n + 1
    tails = []
    for t in range((len(blocks) - 1) * n + total):
        for tail in tails:
            next(tail, None)
        for k, g in enumerate(gens):
            if 0 <= t - k * n < total:
                next(g)
                if t - k * n == total - 1 and after is not None:
                    tails.append(after(k, *outs[k][0]))
    for tail in tails:
        for _ in tail:
            pass
    return [o[0] for o in outs]


def _att_kernel(q_ref, k_ref, v_ref, bv_ref, e_ref, ao_ref, b1_ref, b4_ref, b16_ref, q4, k4, v4, stage,
                acc4, ml4, acc16, ml16):
    BLK, H = ATT_BLOCK, N_ATT_HEADS
    S = q_ref.shape[1]
    nblk = S // BLK
    d4, d16 = DIL_PATTERNS[1][1], DIL_PATTERNS[2][1]
    nb4 = nblk // d4
    per_res = S // d4

    @pl.when(pl.program_id(0) == 0)
    def _():
        for g, tab in enumerate((b1_ref, b4_ref, b16_ref)):
            for h in range(H):
                rows = jnp.broadcast_to(bv_ref[g, h:h + 1, :], (2 * BLK, bv_ref.shape[2]))
                skew = pltpu.roll(rows, 0, 1, stride=1, stride_axis=0)
                tab[:, h * BLK:(h + 1) * BLK] = skew[2 * BLK - tab.shape[0]:, :BLK]

    for src, dst in ((q_ref, q4), (k_ref, k4), (v_ref, v4)):
        for c in range(ATT_CHUNKS):
            stage[...] = src[0, :, c * LANES:(c + 1) * LANES].astype(F32)
            for r in range(d4):
                dst[c, r * per_res:(r + 1) * per_res, :] = stage[pl.ds(r, per_res, stride=d4), :]
    contiguous = [_chunk_rows(t, 1) for t in (q4, k4, v4)]
    every_4th = [_chunk_rows(t, d16 // d4) for t in (q4, k4, v4)]
    tokens = [_token_rows(t) for t in (q_ref, k_ref, v_ref)]

    def dilated(i, carry):
        blocks, toks = [], []
        for n in (2 * i, 2 * i + 1):
            r, j = n // nb4, n % nb4
            cur = r * per_res + j * BLK
            prev = r * per_res + jnp.maximum(j - 1, 0) * BLK
            blocks.append((*contiguous, b4_ref, cur, prev, j == 0))
            toks.append((r + j * (BLK * d4), d4, acc4, ml4))
            blocks.append((*every_4th, b16_ref, (n % d4) * per_res + n // d4, None, None))
            toks.append((n, d16, acc16, ml16))
        for (accs, ml), (tok, dil, acc_ref, ml_ref) in zip(_att_pipeline(blocks), toks):
            for c in range(ATT_CHUNKS):
                acc_ref[c, pl.ds(tok, BLK, stride=dil), :] = accs[c]
            ml_ref[pl.ds(tok, BLK, stride=dil), :] = ml
        return carry

    def merge(cur, accs1, ml1):
        rows = pl.ds(cur, BLK)
        mls = [ml1, ml4[rows, :], ml16[rows, :]]
        accs = [accs1, [acc4[c, rows, :] for c in range(ATT_CHUNKS)], [acc16[c, rows, :] for c in range(ATT_CHUNKS)]]
        lane = lax.broadcasted_iota(jnp.int32, (BLK, LANES), 1)
        m_all = jnp.maximum(jnp.maximum(mls[0], mls[1]), mls[2])
        ws = [jnp.exp(ml - m_all) for ml in mls]
        den = sum(w * pltpu.roll(ml, LANES - H, 1) for w, ml in zip(ws, mls))
        ao = [jnp.zeros((BLK, LANES), F32) for _ in range(ATT_CHUNKS)]
        yield
        for w, acc in zip(ws, accs):
            coef = _split3_dot(jnp.where(lane < H, w / den, 0.0), e_ref[...])
            for c in range(ATT_CHUNKS):
                ao[c] = ao[c] + coef[:, c * LANES:(c + 1) * LANES] * acc[c]
            yield
        ao_ref[0, rows, :] = jnp.concatenate(ao, axis=-1).astype(ao_ref.dtype)
        yield

    def dense_and_merge(i, carry):
        per_trip = 8
        starts = [pl.multiple_of((per_trip * i + k) * BLK, BLK) for k in range(per_trip)]
        before = pl.multiple_of(jnp.maximum(per_trip * i - 1, 0) * BLK, BLK)
        blocks = [(*tokens, b1_ref, cur, prev, first)
                  for cur, prev, first in zip(starts, [before] + starts[:-1], [i == 0] + [False] * (per_trip - 1))]
        _att_pipeline(blocks, lambda k, accs, ml: merge(starts[k], accs, ml))
        return carry

    lax.fori_loop(0, nblk // 2, dilated, 0)
    lax.fori_loop(0, nblk // 8, dense_and_merge, 0)


def _prompt_attention(aq, ak, av, rel_bias):
    B, S, A = aq.shape
    CH = ATT_CHUNKS
    assert [d for _, d in DIL_PATTERNS] == [1, 4, 16] and S // ATT_BLOCK == 16
    bias_vectors = _prompt_bias_vectors(rel_bias)
    head_of_lane = jnp.arange(A)[None, :] // HEAD_DIM
    expand = (jnp.arange(LANES)[:, None] == head_of_lane).astype(BF16)
    seq = pl.BlockSpec((1, S, A), lambda b: (b, 0, 0))
    const = lambda shape: pl.BlockSpec(shape, lambda b: (0,) * len(shape), pipeline_mode=pl.Buffered(1))
    chunked = pltpu.VMEM((CH, S, LANES), F32)
    stats = pltpu.VMEM((S, LANES), F32)
    table = lambda keys: pltpu.VMEM((keys, N_ATT_HEADS * ATT_BLOCK), F32)
    return pl.pallas_call(
        _att_kernel,
        grid=(B,),
        in_specs=[seq, seq, seq, const(bias_vectors.shape), const(expand.shape)],
        out_specs=seq,
        out_shape=jax.ShapeDtypeStruct((B, S, A), BF16),
        scratch_shapes=[table(2 * ATT_BLOCK), table(2 * ATT_BLOCK), table(ATT_BLOCK),
                        chunked, chunked, chunked, stats, chunked, stats, chunked, stats],
        compiler_params=_cparams("prompt_att", "arbitrary"),
        name="prompt_att",
    )(aq, ak, av, bias_vectors, expand)


def _window_pieces(lows, T, kT_ref, vT_ref, qn_ref, kn_ref, vn_ref, c1_ref, c4_ref, c16_ref,
                   n1_ref, n4_ref, n16_ref, koT_ref, voT_ref, ao_ref):
    H, Dh, L = kT_ref.shape[1:]
    A = H * Dh
    zpad = jnp.zeros((LANES - T, A), F32)
    tail_lane = lax.broadcasted_iota(jnp.int32, (Dh, LANES), 1) >= LANES - T
    shifted = {}

    def shift_in(name, xT_ref, new_ref, out_ref):
        new_p = jnp.concatenate([zpad, new_ref[0]], axis=0)
        new_t = new_p.T
        heads = []
        for h in range(H):
            x = xT_ref[0, h]
            rolled = pltpu.roll(x, L - T, 1)
            tail = jnp.where(tail_lane, new_t[h * Dh:(h + 1) * Dh], rolled[:, L - LANES:])
            out_ref[0, h] = jnp.concatenate([rolled[:, :L - LANES], tail], axis=1)
            heads.append(x.astype(BF16))
            yield
        shifted[name] = (jnp.concatenate(heads, axis=0), new_p.astype(BF16))

    yield from shift_in("k", kT_ref, kn_ref, koT_ref)
    yield from shift_in("v", vT_ref, vn_ref, voT_ref)
    (kT, kn), (vT, vn) = shifted["k"], shifted["v"]
    HT = H * T
    row_head = lax.broadcasted_iota(jnp.int32, (HT, A), 0) // T
    lane_head = lax.broadcasted_iota(jnp.int32, (HT, A), 1) // HEAD_DIM
    diag = row_head == lane_head
    q_rows = jnp.where(diag, jnp.concatenate([qn_ref[0]] * H, axis=0), 0.0).astype(BF16)
    s_cache = _dot(q_rows, kT)
    s_new = _dot_nt(q_rows, kn)
    yield
    soft = []
    for lo, c_ref, n_ref in zip(lows, (c1_ref, c4_ref, c16_ref), (n1_ref, n4_ref, n16_ref)):
        sc = s_cache[:, lo:] + c_ref[...]
        sn = s_new + n_ref[...]
        m = jnp.maximum(jnp.max(sc, axis=-1, keepdims=True), jnp.max(sn, axis=-1, keepdims=True))
        pc = jnp.exp(sc - m)
        pn = jnp.exp(sn - m)
        l = jnp.sum(pc, axis=-1, keepdims=True) + jnp.sum(pn, axis=-1, keepdims=True)
        soft.append((m, l, pc.astype(BF16), pn.astype(BF16)))
        yield
    stats = []
    for lo, (m, l, pc, pn) in zip(lows, soft):
        stats.append((m, l, _dot_nt(pc, vT[:, lo:]) + _dot(pn, vn)))
        yield
    m_all = jnp.maximum(jnp.maximum(stats[0][0], stats[1][0]), stats[2][0])
    num = jnp.zeros((HT, A), F32)
    den = jnp.zeros((HT, 1), F32)
    for m, l, acc in stats:
        w = jnp.exp(m - m_all)
        num = num + w * acc
        den = den + w * l
    comb = jnp.where(diag, num / den, 0.0)
    out = comb[0:T]
    for h in range(1, H):
        out = out + comb[h * T:(h + 1) * T]
    ao_ref[0] = out
    yield


def _sample_window_operands(aq, ak, av, cache_kT, cache_vT, rel_bias, groups):
    B, T, A = aq.shape
    _, H, Dh, L = cache_kT.shape
    Hg, Ag = H // groups, A // groups
    lows = tuple(L - win for win, _ in DIL_PATTERNS)
    assert all(lo >= 0 and lo % LANES == 0 for lo in lows) and T <= LANES and Ag % LANES == 0
    tabs = _sample_bias_tables(rel_bias, L, T)
    big = pl.BlockSpec((1, Hg, Dh, L), lambda i: (i // groups, i % groups, 0, 0))
    small = pl.BlockSpec((1, T, Ag), lambda i: (i // groups, 0, i % groups))
    tab = lambda t: pl.BlockSpec((Hg * T, t.shape[1]), lambda i: (i % groups, 0))
    win_sds = jax.ShapeDtypeStruct((B, H, Dh, L), F32)
    args = [cache_kT, cache_vT, aq, ak, av, *tabs]
    in_specs = [big, big, small, small, small] + [tab(t) for t in tabs]
    out_specs = [big, big, small]
    out_shape = [win_sds, win_sds, jax.ShapeDtypeStruct((B, T, A), F32)]
    return functools.partial(_window_pieces, lows, T), args, in_specs, out_specs, out_shape


def _ffn_pieces(d_ff, ff_chunk, x_ref, ro_ref, ao_ref, wo_ref, wgu_ref, wd_ref, g_pm, g_pf, g_of, y_ref, act_ref):
    R = RET_WIDTH
    mix = _dot(ro_ref[...].astype(BF16), wo_ref[:R, :]) + _dot(ao_ref[...].astype(BF16), wo_ref[R:, :])
    x1 = x_ref[...] + _rms(mix, g_pm[...])
    h = _rms(x1, g_pf[...]).astype(BF16)
    yield
    for c in range(0, d_ff, ff_chunk):
        gate = _dot(h, wgu_ref[:, c:c + ff_chunk])
        up = _dot(h, wgu_ref[:, d_ff + c:d_ff + c + ff_chunk])
        act_ref[:, c:c + ff_chunk] = (gate * jax.nn.sigmoid(gate) * up).astype(BF16)
        yield
    f = _dot(act_ref[...], wd_ref[...])
    y_ref[...] = x1 + _rms(f, g_of[...])
    yield


def _out_kernel(d_ff, ff_chunk, *refs):
    for _ in _ffn_pieces(d_ff, ff_chunk, *refs):
        pass


def _out_window_kernel(d_ff, ff_chunk, window, n_ffn_in, n_win_in, *refs):
    ffn_in = refs[:n_ffn_in]
    win_in = refs[n_ffn_in:n_ffn_in + n_win_in]
    y_ref, koT_ref, voT_ref, sao_ref, act_ref = refs[n_ffn_in + n_win_in:]
    win = window(*win_in, koT_ref, voT_ref, sao_ref)
    for _ in _ffn_pieces(d_ff, ff_chunk, *ffn_in, y_ref, act_ref):
        next(win, None)
        next(win, None)
    for _ in win:
        pass


def _out_block(x, ro, ao, wo_bf, wgu_bf, wd_bf, g_post_mix, g_pre_ffn, g_post_ffn, tm, window=None):
    N, D = x.shape
    d_ff = wd_bf.shape[0]
    row = lambda width: pl.BlockSpec((tm, width), lambda i: (i, 0))
    const = lambda shape: pl.BlockSpec(shape, lambda i: (0,) * len(shape), pipeline_mode=pl.Buffered(1))
    args = [x, ro, ao, wo_bf, wgu_bf, wd_bf, g_post_mix.reshape(1, D), g_pre_ffn.reshape(1, D),
            g_post_ffn.reshape(1, D)]
    in_specs = [row(D), row(RET_WIDTH), row(ATT_WIDTH), const(wo_bf.shape), const(wgu_bf.shape),
                const(wd_bf.shape), const((1, D)), const((1, D)), const((1, D))]
    body = functools.partial(_out_kernel, d_ff, FF_CHUNK)
    out_specs, out_shape = [row(D)], [jax.ShapeDtypeStruct((N, D), F32)]
    if window is not None:
        win_body, win_args, win_in_specs, win_out_specs, win_out_shape = window
        assert N // tm == win_out_shape[0].shape[0] * (win_out_shape[0].shape[1] // win_out_specs[0].block_shape[1])
        body = functools.partial(_out_window_kernel, d_ff, FF_CHUNK, win_body, len(args), len(win_args))
        args, in_specs = args + win_args, in_specs + win_in_specs
        out_specs, out_shape = out_specs + win_out_specs, out_shape + win_out_shape
    outs = pl.pallas_call(
        body,
        grid=(N // tm,),
        in_specs=in_specs,
        out_specs=out_specs,
        out_shape=out_shape,
        scratch_shapes=[pltpu.VMEM((tm, d_ff), BF16)],
        compiler_params=_cparams("out_ffn", "arbitrary"),
        name="out_ffn",
    )(*args)
    return outs[0] if window is None else outs


def kernel(x_prompt, x_sample, state_ret, cache_k_win, cache_v_win, rel_bias, w_in, g_ret, w_out,
           g_pre_mix, g_post_mix, g_pre_ffn, g_post_ffn, w_gu, w_down):
    depth = w_in.shape[0]
    assert depth == 1
    B, S, D = x_prompt.shape
    Bs, Ts, _ = x_sample.shape
    H, Dh = N_ATT_HEADS, HEAD_DIM
    l = 0
    w_in_bf = w_in[l].astype(BF16)

    pos_s = PAST_LEN + jnp.arange(Ts, dtype=jnp.int32)
    N_s = Bs * Ts
    outs = _proj(x_sample.reshape(N_s, D), g_pre_mix[l], w_in_bf, jnp.tile(pos_s, Bs))
    srq, srk, srv, srg, saq, sak, sav = [t.reshape(Bs, Ts, RET_WIDTH) for t in outs]
    sro, s_s = _retention_step(srq, srk, srv, srg, g_ret[l], state_ret[l])
    to_t = lambda t: jnp.transpose(t, (0, 2, 3, 1))
    from_t = lambda t: jnp.transpose(t, (0, 3, 1, 2))
    window = _sample_window_operands(saq, sak, sav, to_t(cache_k_win[l]), to_t(cache_v_win[l]), rel_bias,
                                     WINDOW_HEAD_GROUPS)

    ro, s_p, aq, ak, av, wo_bf, wgu_bf, wd_bf = _proj_retention(
        x_prompt, g_pre_mix[l], w_in_bf, g_ret[l], PROJ_TILE, (w_out[l], w_gu[l], w_down[l]))
    ao = _prompt_attention(aq, ak, av, rel_bias)
    y_p, k_sT, v_sT, sao = _out_block(x_prompt.reshape(B * S, D), ro.reshape(B * S, RET_WIDTH),
                                      ao.reshape(B * S, ATT_WIDTH), wo_bf, wgu_bf, wd_bf, g_post_mix[l],
                                      g_pre_ffn[l], g_post_ffn[l], FFN_TILE, window)
    k_s, v_s = from_t(k_sT), from_t(v_sT)

    y_s = _out_block(x_sample.reshape(N_s, D), sro.reshape(N_s, RET_WIDTH), sao.reshape(N_s, ATT_WIDTH),
                     wo_bf, wgu_bf, wd_bf, g_post_mix[l], g_pre_ffn[l], g_post_ffn[l], N_s)

    return (y_p.reshape(B, S, D), y_s.reshape(Bs, Ts, D),
            s_p[None], ak.reshape(1, B, S, H, Dh), av.reshape(1, B, S, H, Dh),
            s_s[None], k_s[None], v_s[None])
```
